```python
import math
import jax
import jax.numpy as jnp
from jax import lax
import numpy as np

D_MODEL = 1024
BATCH = 8
SEQ = 2048
DEPTH = 4

GRID_W = 64
CTX_LEN = 256
HEAD_DIM = 64
ATTN_WIDTH = D_MODEL // 2
N_Q_HEADS = ATTN_WIDTH // HEAD_DIM
N_KV_HEADS = N_Q_HEADS // 4
KV_REP = N_Q_HEADS // N_KV_HEADS
KV_WIDTH = N_KV_HEADS * HEAD_DIM
FOURIER_WIDTH = D_MODEL - ATTN_WIDTH
N_FOURIER_GROUPS = 4
FOURIER_GROUP = FOURIER_WIDTH // N_FOURIER_GROUPS
MIX_WIDTH = ATTN_WIDTH + FOURIER_WIDTH
IN_WIDTH = ATTN_WIDTH + 2 * KV_WIDTH + FOURIER_WIDTH
ROT_PER_AXIS = HEAD_DIM // 2
ROPE_THETA = 10000.0
Q_BLOCK = 128
N_MOD = 6
D_FF_DENSE = ((8 * D_MODEL // 3 + 255) // 256) * 256
N_EXPERTS = 8
TOP_K = 2
D_FF_EXPERT = 7 * D_MODEL // 2
MOE_BLOCK = 256
EPS = 1e-6
N_DENSE_LAYERS = (DEPTH + 1) // 2
N_MOE_LAYERS = DEPTH // 2

kernel_name = "hybrid_gqa_fnet_moe_dit_prefix"


def rms_norm(x, g):
    x32 = x.astype(jnp.float32)
    y = x32 * lax.rsqrt(jnp.mean(x32 * x32, axis=-1, keepdims=True) + EPS)
    return (y * g.astype(jnp.float32)).astype(x.dtype)


def modulate(h, shift, scale):
    return h * (1 + scale) + shift


def rope_tables(n_tokens):
    n_rows = n_tokens // GRID_W
    rows = jnp.repeat(jnp.arange(n_rows), GRID_W).astype(jnp.float32)
    cols = jnp.tile(jnp.arange(GRID_W), n_rows).astype(jnp.float32)
    inv_freq = ROPE_THETA ** (-jnp.arange(0, ROT_PER_AXIS, 2, dtype=jnp.float32) / ROT_PER_AXIS)
    ang_r = (rows[:, None] * inv_freq)[:, None, :]
    ang_c = (cols[:, None] * inv_freq)[:, None, :]
    return (jnp.cos(ang_r), jnp.sin(ang_r), jnp.cos(ang_c), jnp.sin(ang_c))


def rotate(x, cos, sin):
    x1, x2 = jnp.split(x, 2, axis=-1)
    cos = cos.astype(x.dtype)
    sin = sin.astype(x.dtype)
    return jnp.concatenate([x1 * cos - x2 * sin, x2 * cos + x1 * sin], axis=-1)


def rope2d(x, tables):
    cos_r, sin_r, cos_c, sin_c = tables
    return jnp.concatenate([rotate(x[..., :ROT_PER_AXIS], cos_r, sin_r),
                            rotate(x[..., ROT_PER_AXIS:], cos_c, sin_c)], axis=-1)


def project(h, w_in, g_q, g_k):
    b, n, _ = h.shape
    p = h @ w_in
    q = p[..., :ATTN_WIDTH].reshape(b, n, N_Q_HEADS, HEAD_DIM)
    k = p[..., ATTN_WIDTH:ATTN_WIDTH + KV_WIDTH].reshape(b, n, N_KV_HEADS, HEAD_DIM)
    v = p[..., ATTN_WIDTH + KV_WIDTH:ATTN_WIDTH + 2 * KV_WIDTH].reshape(b, n, N_KV_HEADS, HEAD_DIM)
    f = p[..., ATTN_WIDTH + 2 * KV_WIDTH:]
    return rms_norm(q, g_q), rms_norm(k, g_k), v, f


def attend(q, k, v):
    b, nq = q.shape[:2]
    qg = q.reshape(b, nq, N_KV_HEADS, KV_REP, HEAD_DIM)
    s = jnp.einsum('bqgrd,bkgd->bgrqk', qg, k).astype(jnp.float32) * (HEAD_DIM ** -0.5)
    p = jax.nn.softmax(s, axis=-1).astype(v.dtype)
    o = jnp.einsum('bgrqk,bkgd->bqgrd', p, v)
    return o.reshape(b, nq, ATTN_WIDTH)


def block_attention(q, k, v):
    b, n = q.shape[:2]
    nb = n // Q_BLOCK
    qb = q.reshape(b, nb, Q_BLOCK, N_Q_HEADS, HEAD_DIM).swapaxes(0, 1)
    o = lax.map(lambda qq: attend(qq, k, v), qb)
    return o.swapaxes(0, 1).reshape(b, n, ATTN_WIDTH)


def fourier_mix(f, w_four):
    b, n, _ = f.shape
    fg = f.reshape(b, n, N_FOURIER_GROUPS, FOURIER_GROUP).astype(jnp.float32)
    spec = jnp.fft.fftn(fg, axes=(1, 3), norm='ortho').real.astype(f.dtype)
    y = jnp.einsum('bngc,gcd->bngd', spec, w_four)
    return y.reshape(b, n, FOURIER_WIDTH)


def swiglu(h, w_gate, w_up, w_down):
    return (jax.nn.silu(h @ w_gate) * (h @ w_up)) @ w_down


def moe_swiglu(h, w_router, b_router, w_gate, w_up, w_down):
    t = h.shape[0]
    n_assign = t * TOP_K
    logits = h.astype(jnp.float32) @ w_router.astype(jnp.float32) + b_router.astype(jnp.float32)
    top_val, top_idx = lax.top_k(logits, TOP_K)
    gates = jax.nn.softmax(top_val, axis=-1).reshape(-1)
    expert = top_idx.reshape(-1).astype(jnp.int32)
    token = jnp.repeat(jnp.arange(t, dtype=jnp.int32), TOP_K)
    order = jnp.argsort(expert)
    e_sorted, tok_sorted, gate_sorted = expert[order], token[order], gates[order]
    counts = jnp.bincount(expert, length=N_EXPERTS)
    start = jnp.cumsum(counts) - counts
    padded = ((counts + MOE_BLOCK - 1) // MOE_BLOCK) * MOE_BLOCK
    pend = jnp.cumsum(padded)
    pstart = pend - padded
    dest = pstart[e_sorted] + (jnp.arange(n_assign) - start[e_sorted])
    n_blocks = -(-n_assign // MOE_BLOCK) + N_EXPERTS
    n_rows = n_blocks * MOE_BLOCK
    row_tok = jnp.zeros((n_rows,), jnp.int32).at[dest].set(tok_sorted)
    row_gate = jnp.zeros((n_rows,), jnp.float32).at[dest].set(gate_sorted)
    block_expert = jnp.minimum(
        jnp.searchsorted(pend, jnp.arange(n_blocks) * MOE_BLOCK, side='right'), N_EXPERTS - 1)
    xs = h[row_tok].reshape(n_blocks, MOE_BLOCK, h.shape[-1])

    def expert_block(args):
        xb, e = args
        return swiglu(xb, w_gate[e], w_up[e], w_down[e])

    y = lax.map(expert_block, (xs, block_expert)).reshape(n_rows, h.shape[-1])
    y = y * row_gate[:, None].astype(y.dtype)
    return jax.ops.segment_sum(y, row_tok, num_segments=t)


def setup_inputs(seed: int = 0) -> dict:
    key = jax.random.key(seed)
    ks = jax.random.split(key, 21)
    nrm = lambda k, shape, s: jax.random.normal(k, shape, jnp.float32) * s
    d = D_MODEL
    return {
        "x": nrm(ks[0], (BATCH, SEQ, d), 1.0),
        "c": nrm(ks[1], (BATCH, d), 1.0),
        "ctx": nrm(ks[2], (BATCH, CTX_LEN, d), 1.0),
        "c_ctx": nrm(ks[3], (d,), 1.0),
        "w_mod": nrm(ks[4], (DEPTH, d, N_MOD * d), 0.5 * d ** -0.5),
        "b_mod": nrm(ks[5], (DEPTH, N_MOD * d), 0.02),
        "g_mix": 1.0 + nrm(ks[6], (DEPTH, d), 0.02),
        "g_ffn": 1.0 + nrm(ks[7], (DEPTH, d), 0.02),
        "g_q": 1.0 + nrm(ks[8], (DEPTH, HEAD_DIM), 0.02),
        "g_k": 1.0 + nrm(ks[9], (DEPTH, HEAD_DIM), 0.02),
        "w_in": nrm(ks[10], (DEPTH, d, IN_WIDTH), d ** -0.5),
        "w_four": nrm(ks[11], (DEPTH, N_FOURIER_GROUPS, FOURIER_GROUP, FOURIER_GROUP), FOURIER_GROUP ** -0.5),
        "w_o": nrm(ks[12], (DEPTH, MIX_WIDTH, d), MIX_WIDTH ** -0.5),
        "w_gate_dense": nrm(ks[13], (N_DENSE_LAYERS, d, D_FF_DENSE), d ** -0.5),
        "w_up_dense": nrm(ks[14], (N_DENSE_LAYERS, d, D_FF_DENSE), d ** -0.5),
        "w_down_dense": nrm(ks[15], (N_DENSE_LAYERS, D_FF_DENSE, d), D_FF_DENSE ** -0.5),
        "w_router": nrm(ks[16], (N_MOE_LAYERS, d, N_EXPERTS), d ** -0.5),
        "b_router": nrm(ks[17], (N_MOE_LAYERS, N_EXPERTS), 0.01),
        "w_gate_moe": nrm(ks[18], (N_MOE_LAYERS, N_EXPERTS, d, D_FF_EXPERT), d ** -0.5),
        "w_up_moe": nrm(ks[19], (N_MOE_LAYERS, N_EXPERTS, d, D_FF_EXPERT), d ** -0.5),
        "w_down_moe": nrm(ks[20], (N_MOE_LAYERS, N_EXPERTS, D_FF_EXPERT, d), D_FF_EXPERT ** -0.5),
    }


def reference(x, c, ctx, c_ctx, w_mod, b_mod, g_mix, g_ffn, g_q, g_k, w_in, w_four, w_o,
              w_gate_dense, w_up_dense, w_down_dense, w_router, b_router,
              w_gate_moe, w_up_moe, w_down_moe):
    b, s, d = x.shape
    n_ctx = ctx.shape[1]
    tables = rope_tables(s)
    silu_c = jax.nn.silu(c)
    silu_cc = jax.nn.silu(c_ctx)
    xl, xc = x, ctx
    for l in range(DEPTH):
        last = l == DEPTH - 1
        mod_l = (silu_c @ w_mod[l] + b_mod[l]).reshape(b, N_MOD, 1, d)
        mod_c = (silu_cc @ w_mod[l] + b_mod[l]).reshape(N_MOD, 1, 1, d)
        sh_a, sc_a, ga_a, sh_f, sc_f, ga_f = [mod_l[:, i] for i in range(N_MOD)]
        csh_a, csc_a, cga_a, csh_f, csc_f, cga_f = [mod_c[i] for i in range(N_MOD)]

        hl = modulate(rms_norm(xl, g_mix[l]), sh_a, sc_a)
        hc = modulate(rms_norm(xc, g_mix[l]), csh_a, csc_a)
        ql, kl, vl, fl = project(hl, w_in[l], g_q[l], g_k[l])
        qc, kc, vc, fc = project(hc, w_in[l], g_q[l], g_k[l])
        ql = rope2d(ql, tables)
        kl = rope2d(kl, tables)
        k_all = jnp.concatenate([kc, kl], axis=1)
        v_all = jnp.concatenate([vc, vl], axis=1)
        attn_l = block_attention(ql, k_all, v_all)
        mix_l = jnp.concatenate([attn_l, fourier_mix(fl, w_four[l])], axis=-1) @ w_o[l]
        xl = xl + ga_a * mix_l
        if not last:
            attn_c = attend(qc, kc, vc)
            mix_c = jnp.concatenate([attn_c, fourier_mix(fc, w_four[l])], axis=-1) @ w_o[l]
            xc = xc + cga_a * mix_c

        hl = modulate(rms_norm(xl, g_ffn[l]), sh_f, sc_f).reshape(b * s, d)
        if last:
            tokens = hl
        else:
            hc = modulate(rms_norm(xc, g_ffn[l]), csh_f, csc_f).reshape(b * n_ctx, d)
            tokens = jnp.concatenate([hl, hc], axis=0)
        if l % 2 == 0:
            i = l // 2
            y = swiglu(tokens, w_gate_dense[i], w_up_dense[i], w_down_dense[i])
        else:
            i = l // 2
            y = moe_swiglu(tokens, w_router[i], b_router[i], w_gate_moe[i], w_up_moe[i], w_down_moe[i])
        xl = xl + ga_f * y[:b * s].reshape(b, s, d)
        if not last:
            xc = xc + cga_f * y[b * s:].reshape(b, n_ctx, d)
    return xl
```

```python
import functools

import numpy as np
import jax
import jax.numpy as jnp
from jax import lax
from jax.experimental import pallas as pl
from jax.experimental.pallas import tpu as pltpu

D_MODEL = 1024
DEPTH = 4
GRID_W = 64
HEAD_DIM = 64
ATTN_WIDTH = 512
N_Q_HEADS = 8
N_KV_HEADS = 2
KV_REP = 4
KV_WIDTH = 128
FOURIER_WIDTH = 512
N_FOURIER_GROUPS = 4
FOURIER_GROUP = 128
ROT_PER_AXIS = 32
ROPE_THETA = 10000.0
N_MOD = 6
D_FF_DENSE = 2816
N_EXPERTS = 8
TOP_K = 2
D_FF_EXPERT = 3584
EPS = 1e-6

QKV_WIDTH = ATTN_WIDTH + 2 * KV_WIDTH
MOD_ROWS = 16
LOGIT_LANES = 128

TM_MIX = 512
TQ_ATTN = 512
TR_FOUR = 512
TM_DENSE = 1024
FF_CHUNK = 256
TM_MOE = 1024
TF_MOE = 896
VMEM_LIMIT = 56 * 1024 * 1024

F32 = jnp.float32
BF16 = jnp.bfloat16


def _cparams(n_axes):
    return pltpu.CompilerParams(
        dimension_semantics=("arbitrary",) * n_axes, vmem_limit_bytes=VMEM_LIMIT)


def _rope_tables_t(s, tm):
    n_rows = s // GRID_W
    rows = np.repeat(np.arange(n_rows), GRID_W).astype(np.float64)
    cols = np.tile(np.arange(GRID_W), n_rows).astype(np.float64)
    inv_freq = (ROPE_THETA ** (-np.arange(0, ROT_PER_AXIS, 2, dtype=np.float32) / ROT_PER_AXIS)
                ).astype(np.float32).astype(np.float64)
    ang_r = (rows[None, :].astype(np.float32) * inv_freq[:, None].astype(np.float32)).astype(np.float64)
    ang_c = (cols[None, :].astype(np.float32) * inv_freq[:, None].astype(np.float32)).astype(np.float64)
    cos = np.concatenate([np.cos(ang_r), np.cos(ang_r), np.cos(ang_c), np.cos(ang_c)], axis=0)
    sin = np.concatenate([-np.sin(ang_r), np.sin(ang_r), -np.sin(ang_c), np.sin(ang_c)], axis=0)
    cos = np.concatenate([np.ones((HEAD_DIM, tm)), cos], axis=1)
    sin = np.concatenate([np.zeros((HEAD_DIM, tm)), sin], axis=1)
    return jnp.asarray(cos, F32), jnp.asarray(sin, F32)


def _dft_mats(n):
    k = np.arange(n, dtype=np.int64)
    ang = 2.0 * np.pi * ((k[:, None] * k[None, :]) % n).astype(np.float64) / n
    return jnp.asarray(np.cos(ang), BF16), jnp.asarray(np.sin(ang), BF16)


def _chunk_cols(w, chunk):
    *lead, d, n = w.shape
    w = w.astype(BF16).reshape(*lead, d, n // chunk, chunk)
    return jnp.swapaxes(w, -3, -2)


def _chunk_rows(w, chunk):
    *lead, n, d = w.shape
    return w.astype(BF16).reshape(*lead, n // chunk, chunk, d)


def _mod_kernel(c_ref, w_ref, b_ref, o_ref):
    c = c_ref[...]
    s = (c / (1.0 + jnp.exp(-c))).astype(BF16)
    w = w_ref[0].astype(BF16)
    o_ref[0] = jnp.dot(s, w, preferred_element_type=F32) + b_ref[0]


def _modulation(cvec, w_mod, b_mod):
    depth, d, n = w_mod.shape
    tn = 1536
    return pl.pallas_call(
        _mod_kernel,
        grid=(depth, n // tn),
        in_specs=[
            pl.BlockSpec((MOD_ROWS, d), lambda l, j: (0, 0)),
            pl.BlockSpec((1, d, tn), lambda l, j: (l, 0, j)),
            pl.BlockSpec((1, 1, tn), lambda l, j: (l, 0, j)),
        ],
        out_specs=pl.BlockSpec((1, MOD_ROWS, tn), lambda l, j: (l, 0, j)),
        out_shape=jax.ShapeDtypeStruct((depth, MOD_ROWS, n), F32),
        compiler_params=_cparams(2),
        name="adaln_vectors",
    )(cvec, w_mod, b_mod.reshape(depth, 1, n))


def _mod_row(i, n_ctx_tiles, tiles_per_batch):
    lat = jnp.maximum(i - n_ctx_tiles, 0)
    return jnp.where(i < n_ctx_tiles, 0, lat // tiles_per_batch + 1)


def _mod_vec(mod_ref, row, comp):
    return mod_ref[pl.ds(row, 1), comp * D_MODEL:(comp + 1) * D_MODEL]


def _norm_modulate(x, g, shift, scale):
    ms = jnp.mean(x * x, axis=-1, keepdims=True)
    y = x * lax.rsqrt(ms + EPS) * g
    return y * (1.0 + scale) + shift


def _premix_kernel(x_ref, mod_ref, g_ref, wt_ref, wf_ref, gq_ref, gk_ref, cos_ref, sin_ref,
                   q_ref, k_ref, v_ref, f_ref, *, n_ctx_tiles, tiles_per_batch):
    i = pl.program_id(0)
    row = _mod_row(i, n_ctx_tiles, tiles_per_batch)
    h = _norm_modulate(x_ref[...], g_ref[...], _mod_vec(mod_ref, row, 0), _mod_vec(mod_ref, row, 1))
    hb = h.astype(BF16)
    f_ref[...] = jnp.dot(hb, wf_ref[...], preferred_element_type=F32).astype(BF16)
    pt = lax.dot_general(wt_ref[...], hb, (((1,), (1,)), ((), ())), preferred_element_type=F32)
    v_ref[...] = pt[ATTN_WIDTH + KV_WIDTH:, :].astype(BF16)
    cos = cos_ref[...]
    sin = sin_ref[...]

    def norm_rope(xh, gain):
        ms = jnp.mean(xh * xh, axis=0, keepdims=True)
        y = xh * lax.rsqrt(ms + EPS) * gain
        half = ROT_PER_AXIS // 2
        swapped = jnp.concatenate(
            [y[half:2 * half], y[0:half], y[3 * half:4 * half], y[2 * half:3 * half]], axis=0)
        return y * cos + swapped * sin

    gq = gq_ref[...]
    gk = gk_ref[...]
    for hh in range(N_Q_HEADS):
        q_ref[hh * HEAD_DIM:(hh + 1) * HEAD_DIM, :] = norm_rope(
            pt[hh * HEAD_DIM:(hh + 1) * HEAD_DIM, :], gq).astype(BF16)
    for hh in range(N_KV_HEADS):
        lo = ATTN_WIDTH + hh * HEAD_DIM
        k_ref[hh * HEAD_DIM:(hh + 1) * HEAD_DIM, :] = norm_rope(pt[lo:lo + HEAD_DIM, :], gk).astype(BF16)


def _premix(x, mod_l, g_mix, wt, wf, gq, gk, cos_t, sin_t, *, n_ctx, seq):
    t, d = x.shape
    tm = TM_MIX
    n_ctx_tiles = n_ctx // tm
    tpb = seq // tm

    def tab_idx(i):
        lat = jnp.maximum(i - n_ctx_tiles, 0)
        return (0, jnp.where(i < n_ctx_tiles, 0, lax.rem(lat, tpb) + 1))

    full = lambda shape: pl.BlockSpec(shape, lambda i: (0,) * len(shape))
    return pl.pallas_call(
        functools.partial(_premix_kernel, n_ctx_tiles=n_ctx_tiles, tiles_per_batch=tpb),
        grid=(t // tm,),
        in_specs=[
            pl.BlockSpec((tm, d), lambda i: (i, 0)),
            full(mod_l.shape),
            full((1, d)),
            full(wt.shape),
            full(wf.shape),
            full(gq.shape),
            full(gk.shape),
            pl.BlockSpec((HEAD_DIM, tm), tab_idx),
            pl.BlockSpec((HEAD_DIM, tm), tab_idx),
        ],
        out_specs=[
            pl.BlockSpec((ATTN_WIDTH, tm), lambda i: (0, i)),
            pl.BlockSpec((KV_WIDTH, tm), lambda i: (0, i)),
            pl.BlockSpec((KV_WIDTH, tm), lambda i: (0, i)),
            pl.BlockSpec((tm, FOURIER_WIDTH), lambda i: (i, 0)),
        ],
        out_shape=[
            jax.ShapeDtypeStruct((ATTN_WIDTH, t), BF16),
            jax.ShapeDtypeStruct((KV_WIDTH, t), BF16),
            jax.ShapeDtypeStruct((KV_WIDTH, t), BF16),
            jax.ShapeDtypeStruct((t, FOURIER_WIDTH), BF16),
        ],
        compiler_params=_cparams(1),
        name="premix_project",
    )(x, mod_l, g_mix, wt, wf, gq, gk, cos_t, sin_t)


def _attn_kernel(*refs, n_key_blocks, tq):
    q_ref = refs[0]
    k_refs = refs[1:1 + n_key_blocks]
    v_refs = refs[1 + n_key_blocks:1 + 2 * n_key_blocks]
    o_ref = refs[1 + 2 * n_key_blocks]
    kall_ref, vall_ref, ot_ref = refs[2 + 2 * n_key_blocks:]
    g = pl.program_id(1)
    qt = pl.program_id(2)

    @pl.when(jnp.logical_and(g == 0, qt == 0))
    def _():
        off = 0
        for kr, vr in zip(k_refs, v_refs):
            n = kr.shape[1]
            kall_ref[off:off + n, :] = kr[...].astype(F32).T.astype(BF16)
            vall_ref[:, off:off + n] = vr[...]
            off += n

    row_group = lax.broadcasted_iota(jnp.int32, (KV_WIDTH, tq), 0) // HEAD_DIM

    def head(hh, carry):
        lo = pl.multiple_of(hh * HEAD_DIM, HEAD_DIM)
        qh = q_ref[pl.ds(lo, HEAD_DIM), :]
        q2 = jnp.concatenate([qh, qh], axis=0)
        q2 = jnp.where(row_group == g, q2, jnp.zeros_like(q2))
        st = jnp.dot(kall_ref[...], q2, preferred_element_type=F32)
        m = jnp.max(st, axis=0, keepdims=True)
        p = jnp.exp(st - m)
        denom = jnp.sum(p, axis=0, keepdims=True)
        vt = vall_ref[pl.ds(pl.multiple_of(g * HEAD_DIM, HEAD_DIM), HEAD_DIM), :]
        ot = jnp.dot(vt, p.astype(BF16), preferred_element_type=F32)
        ot_ref[pl.ds(lo, HEAD_DIM), :] = ot / denom
        return carry

    lax.fori_loop(0, KV_REP, head, 0)
    o_ref[...] = ot_ref[...].T.astype(BF16)


def _attention(q_t, k_t, v_t, *, q_tile0, n_q_tiles, key_blocks, tq, n_batch, t, name):
    n_keys = sum(c for c, _ in key_blocks)
    nkb = len(key_blocks)
    q_spec = pl.BlockSpec((KV_REP * HEAD_DIM, tq),
                          lambda b, g, i: (g, q_tile0 + b * n_q_tiles + i))
    k_specs = [pl.BlockSpec((KV_WIDTH, c), (lambda f: (lambda b, g, i: (0, f(b))))(f))
               for c, f in key_blocks]
    return pl.pallas_call(
        functools.partial(_attn_kernel, n_key_blocks=nkb, tq=tq),
        grid=(n_batch, N_KV_HEADS, n_q_tiles),
        in_specs=[q_spec] + k_specs + k_specs,
        out_specs=pl.BlockSpec((tq, KV_REP * HEAD_DIM),
                               lambda b, g, i: (q_tile0 + b * n_q_tiles + i, g)),
        out_shape=jax.ShapeDtypeStruct((t, ATTN_WIDTH), BF16),
        scratch_shapes=[pltpu.VMEM((n_keys, KV_WIDTH), BF16), pltpu.VMEM((KV_WIDTH, n_keys), BF16),
                        pltpu.VMEM((KV_REP * HEAD_DIM, tq), F32)],
        compiler_params=_cparams(3),
        name=name,
    )(q_t, *([k_t] * nkb), *([v_t] * nkb))


def _fourier_kernel(c_ref, s_ref, x_ref, cc_ref, sc_ref, w_ref, o_ref, *, norm):
    x = x_ref[...]
    a = jnp.dot(c_ref[...], x, preferred_element_type=F32).astype(BF16)
    b = jnp.dot(s_ref[...], x, preferred_element_type=F32).astype(BF16)
    cc = cc_ref[...]
    sc = sc_ref[...]
    for grp in range(N_FOURIER_GROUPS):
        sl = slice(grp * FOURIER_GROUP, (grp + 1) * FOURIER_GROUP)
        spec = (jnp.dot(a[:, sl], cc, preferred_element_type=F32)
                - jnp.dot(b[:, sl], sc, preferred_element_type=F32)) * norm
        o_ref[:, sl] = jnp.dot(spec.astype(BF16), w_ref[grp], preferred_element_type=F32).astype(BF16)


def _fourier(f, cmat, smat, cc, sc, w_four, *, n, tr, x_block0, out_tile0, n_batch, t, name):
    n_row_tiles = n // tr
    return pl.pallas_call(
        functools.partial(_fourier_kernel, norm=float(1.0 / np.sqrt(n * FOURIER_GROUP))),
        grid=(n_row_tiles, n_batch),
        in_specs=[
            pl.BlockSpec((tr, n), lambda i, b: (i, 0)),
            pl.BlockSpec((tr, n), lambda i, b: (i, 0)),
            pl.BlockSpec((n, FOURIER_WIDTH), lambda i, b: (x_block0 + b, 0)),
            pl.BlockSpec(cc.shape, lambda i, b: (0, 0)),
            pl.BlockSpec(sc.shape, lambda i, b: (0, 0)),
            pl.BlockSpec(w_four.shape, lambda i, b: (0, 0, 0)),
        ],
        out_specs=pl.BlockSpec((tr, FOURIER_WIDTH), lambda i, b: (out_tile0 + b * n_row_tiles + i, 0)),
        out_shape=jax.ShapeDtypeStruct((t, FOURIER_WIDTH), BF16),
        compiler_params=_cparams(2),
        name=name,
    )(cmat, smat, f, cc, sc, w_four)


def _postmix_kernel(x_ref, a_ref, f_ref, mod_ref, g_ref, woa_ref, wof_ref, *rest,
                    n_ctx_tiles, tiles_per_batch, with_router):
    if with_router:
        wr_ref, br_ref, x1_ref, h_ref, lg_ref = rest
    else:
        x1_ref, h_ref = rest
    i = pl.program_id(0)
    row = _mod_row(i, n_ctx_tiles, tiles_per_batch)
    mix = (jnp.dot(a_ref[...], woa_ref[...], preferred_element_type=F32)
           + jnp.dot(f_ref[...], wof_ref[...], preferred_element_type=F32))
    x1 = x_ref[...] + _mod_vec(mod_ref, row, 2) * mix
    x1_ref[...] = x1
    h = _norm_modulate(x1, g_ref[...], _mod_vec(mod_ref, row, 3), _mod_vec(mod_ref, row, 4))
    hb = h.astype(BF16)
    h_ref[...] = hb
    if with_router:
        lg_ref[...] = jnp.dot(hb, wr_ref[...], preferred_element_type=F32) + br_ref[...]


def _postmix(x, attn, four, mod_l, g_ffn, wo_a, wo_f, router, *, n_ctx, seq):
    t, d = x.shape
    tm = TM_MIX
    full = lambda shape: pl.BlockSpec(shape, lambda i: (0,) * len(shape))
    tile = lambda w: pl.BlockSpec((tm, w), lambda i: (i, 0))
    in_specs = [tile(d), tile(ATTN_WIDTH), tile(FOURIER_WIDTH), full(mod_l.shape), full((1, d)),
                full(wo_a.shape), full(wo_f.shape)]
    out_specs = [tile(d), tile(d)]
    out_shape = [jax.ShapeDtypeStruct((t, d), F32), jax.ShapeDtypeStruct((t, d), BF16)]
    args = [x, attn, four, mod_l, g_ffn, wo_a, wo_f]
    if router is not None:
        in_specs += [full(router[0].shape), full(router[1].shape)]
        out_specs.append(tile(LOGIT_LANES))
        out_shape.append(jax.ShapeDtypeStruct((t, LOGIT_LANES), F32))
        args += list(router)
    return pl.pallas_call(
        functools.partial(_postmix_kernel, n_ctx_tiles=n_ctx // tm, tiles_per_batch=seq // tm,
                          with_router=router is not None),
        grid=(t // tm,),
        in_specs=in_specs,
        out_specs=out_specs,
        out_shape=out_shape,
        compiler_params=_cparams(1),
        name="postmix_wo_norm",
    )(*args)


def _silu_mul(gate, up):
    return (gate / (1.0 + jnp.exp(-gate))) * up


def _dense_ffn_kernel(h_ref, x1_ref, mod_ref, wg_ref, wu_ref, wd_ref, o_ref, acc_ref, *,
                      n_ctx_tiles, tiles_per_batch):
    i = pl.program_id(0)
    row = _mod_row(i, n_ctx_tiles, tiles_per_batch)
    acc_ref[...] = jnp.zeros_like(acc_ref)

    def chunk(c, carry):
        h = h_ref[...]
        gate = jnp.dot(h, wg_ref[c], preferred_element_type=F32)
        up = jnp.dot(h, wu_ref[c], preferred_element_type=F32)
        act = _silu_mul(gate, up).astype(BF16)
        acc_ref[...] += jnp.dot(act, wd_ref[c], preferred_element_type=F32)
        return carry

    lax.fori_loop(0, wg_ref.shape[0], chunk, 0)
    o_ref[...] = x1_ref[...] + _mod_vec(mod_ref, row, 5) * acc_ref[...]


def _dense_ffn(h, x1, mod_l, wg, wu, wd, *, n_ctx, seq):
    t, d = x1.shape
    tm = TM_DENSE
    full = lambda shape: pl.BlockSpec(shape, lambda i: (0,) * len(shape))
    tile = lambda: pl.BlockSpec((tm, d), lambda i: (i, 0))
    return pl.pallas_call(
        functools.partial(_dense_ffn_kernel, n_ctx_tiles=n_ctx // tm, tiles_per_batch=seq // tm),
        grid=(t // tm,),
        in_specs=[tile(), tile(), full(mod_l.shape), full(wg.shape), full(wu.shape), full(wd.shape)],
        out_specs=tile(),
        out_shape=jax.ShapeDtypeStruct((t, d), F32),
        scratch_shapes=[pltpu.VMEM((tm, d), F32)],
        compiler_params=_cparams(1),
        name="dense_swiglu",
    )(h, x1, mod_l, wg, wu, wd)


def _moe_ffn_kernel(be_ref, nb_ref, x_ref, wg_ref, wu_ref, wd_ref, o_ref, acc_ref):
    i = pl.program_id(0)
    j = pl.program_id(1)

    @pl.when(i < nb_ref[0])
    def _():
        x = x_ref[...]
        gate = jnp.dot(x, wg_ref[0, 0], preferred_element_type=F32)
        up = jnp.dot(x, wu_ref[0, 0], preferred_element_type=F32)
        act = _silu_mul(gate, up).astype(BF16)
        part = jnp.dot(act, wd_ref[0, 0], preferred_element_type=F32)

        @pl.when(j == 0)
        def _():
            acc_ref[...] = part

        @pl.when(j > 0)
        def _():
            acc_ref[...] += part

    @pl.when(j == pl.num_programs(1) - 1)
    def _():
        o_ref[...] = jnp.where(i < nb_ref[0], acc_ref[...], 0.0)


def _moe_ffn(xs, block_expert, n_used, wg, wu, wd):
    n_rows, d = xs.shape
    tm = TM_MOE
    n_blocks = n_rows // tm
    n_ff, tf = wg.shape[1], wg.shape[3]
    grid_spec = pltpu.PrefetchScalarGridSpec(
        num_scalar_prefetch=2,
        grid=(n_blocks, n_ff),
        in_specs=[
            pl.BlockSpec((tm, d), lambda i, j, be, nb: (i, 0)),
            pl.BlockSpec((1, 1, d, tf), lambda i, j, be, nb: (be[i], j, 0, 0)),
            pl.BlockSpec((1, 1, d, tf), lambda i, j, be, nb: (be[i], j, 0, 0)),
            pl.BlockSpec((1, 1, tf, d), lambda i, j, be, nb: (be[i], j, 0, 0)),
        ],
        out_specs=pl.BlockSpec((tm, d), lambda i, j, be, nb: (i, 0)),
        scratch_shapes=[pltpu.VMEM((tm, d), F32)],
    )
    return pl.pallas_call(
        _moe_ffn_kernel,
        grid_spec=grid_spec,
        out_shape=jax.ShapeDtypeStruct((n_rows, d), F32),
        compiler_params=_cparams(2),
        name="expert_swiglu",
    )(block_expert, n_used, xs, wg, wu, wd)


def _moe_combine_kernel(x1_ref, y0_ref, y1_ref, g_ref, mod_ref, o_ref, *, n_ctx_tiles, tiles_per_batch):
    i = pl.program_id(0)
    row = _mod_row(i, n_ctx_tiles, tiles_per_batch)
    gates = g_ref[...]
    y = y0_ref[...] * gates[:, 0:1] + y1_ref[...] * gates[:, 1:2]
    o_ref[...] = x1_ref[...] + _mod_vec(mod_ref, row, 5) * y


def _moe_combine(x1, y0, y1, gates, mod_l, *, n_ctx, seq):
    t, d = x1.shape
    tm = TM_MIX
    tile = lambda w: pl.BlockSpec((tm, w), lambda i: (i, 0))
    return pl.pallas_call(
        functools.partial(_moe_combine_kernel, n_ctx_tiles=n_ctx // tm, tiles_per_batch=seq // tm),
        grid=(t // tm,),
        in_specs=[tile(d), tile(d), tile(d), tile(LOGIT_LANES),
                  pl.BlockSpec(mod_l.shape, lambda i: (0, 0))],
        out_specs=tile(d),
        out_shape=jax.ShapeDtypeStruct((t, d), F32),
        compiler_params=_cparams(1),
        name="expert_combine",
    )(x1, y0, y1, gates, mod_l)


def _route(logits):
    t = logits.shape[0]
    top_val, top_idx = lax.top_k(logits[:, :N_EXPERTS], TOP_K)
    gates = jax.nn.softmax(top_val, axis=-1)
    expert = top_idx.reshape(-1).astype(jnp.int32)
    n_assign = t * TOP_K
    onehot = (expert[:, None] == jnp.arange(N_EXPERTS, dtype=jnp.int32)[None, :]).astype(jnp.int32)
    csum = jnp.cumsum(onehot, axis=0)
    rank = jnp.sum((csum - onehot) * onehot, axis=1)
    counts = csum[-1]
    padded = ((counts + TM_MOE - 1) // TM_MOE) * TM_MOE
    pend = jnp.cumsum(padded)
    pstart = pend - padded
    dest = pstart[expert] + rank
    n_blocks = -(-n_assign // TM_MOE) + N_EXPERTS
    n_rows = n_blocks * TM_MOE
    token = jnp.repeat(jnp.arange(t, dtype=jnp.int32), TOP_K)
    row_tok = jnp.zeros((n_rows,), jnp.int32).at[dest].set(token)
    block_expert = jnp.minimum(
        jnp.searchsorted(pend, jnp.arange(n_blocks, dtype=jnp.int32) * TM_MOE, side="right"),
        N_EXPERTS - 1).astype(jnp.int32)
    n_used = (pend[-1] // TM_MOE).astype(jnp.int32).reshape(1)
    return gates, dest.reshape(t, TOP_K), row_tok, block_expert, n_used


def kernel(x, c, ctx, c_ctx, w_mod, b_mod, g_mix, g_ffn, g_q, g_k, w_in, w_four, w_o,
           w_gate_dense, w_up_dense, w_down_dense, w_router, b_router,
           w_gate_moe, w_up_moe, w_down_moe):
    b, s, d = x.shape
    n_ctx_len = ctx.shape[1]
    n_ctx = b * n_ctx_len
    t = n_ctx + b * s
    assert d == D_MODEL and b + 1 <= MOD_ROWS
    assert n_ctx % TM_DENSE == 0 and s % TM_DENSE == 0 and n_ctx_len % 128 == 0 and n_ctx % s == 0

    cvec = jnp.zeros((MOD_ROWS, d), F32).at[0].set(c_ctx).at[1:b + 1].set(c)
    mod = _modulation(cvec, w_mod, b_mod)

    cos_t, sin_t = _rope_tables_t(s, TM_MIX)
    c_lat, s_lat = _dft_mats(s)
    c_ctx_m, s_ctx_m = _dft_mats(n_ctx_len)
    c_grp, s_grp = _dft_mats(FOURIER_GROUP)

    xa = jnp.concatenate([ctx.reshape(n_ctx, d), x.reshape(b * s, d)], axis=0)

    for l in range(DEPTH):
        mod_l = mod[l]
        w_in_l = w_in[l]
        wt = w_in_l[:, :QKV_WIDTH].T.astype(BF16)
        wf = w_in_l[:, QKV_WIDTH:].astype(BF16)
        gq = jnp.broadcast_to((g_q[l] * (HEAD_DIM ** -0.5))[:, None], (HEAD_DIM, TM_MIX))
        gk = jnp.broadcast_to(g_k[l][:, None], (HEAD_DIM, TM_MIX))
        q_t, k_t, v_t, f = _premix(xa, mod_l, g_mix[l].reshape(1, d), wt, wf, gq, gk, cos_t, sin_t,
                                   n_ctx=n_ctx, seq=s)

        n_qt = s // TQ_ATTN
        lat_keys = [(n_ctx_len, lambda bb: bb), (s, lambda bb: n_ctx // s + bb)]
        attn_lat = _attention(q_t, k_t, v_t, q_tile0=n_ctx // TQ_ATTN, n_q_tiles=n_qt,
                              key_blocks=lat_keys, tq=TQ_ATTN, n_batch=b, t=t, name="attention_latent")
        attn_ctx = _attention(q_t, k_t, v_t, q_tile0=0, n_q_tiles=1,
                              key_blocks=[(n_ctx_len, lambda bb: bb)], tq=n_ctx_len, n_batch=b, t=t,
                              name="attention_context")
        attn = jnp.concatenate([attn_ctx[:n_ctx], attn_lat[n_ctx:]], axis=0)

        wfour = w_four[l].astype(BF16)
        four_lat = _fourier(f, c_lat, s_lat, c_grp, s_grp, wfour, n=s, tr=TR_FOUR,
                            x_block0=n_ctx // s, out_tile0=n_ctx // TR_FOUR, n_batch=b, t=t,
                            name="fourier_latent")
        four_ctx = _fourier(f, c_ctx_m, s_ctx_m, c_grp, s_grp, wfour, n=n_ctx_len, tr=n_ctx_len,
                            x_block0=0, out_tile0=0, n_batch=b, t=t, name="fourier_context")
        four = jnp.concatenate([four_ctx[:n_ctx], four_lat[n_ctx:]], axis=0)

        wo = w_o[l].astype(BF16)
        is_moe = l % 2 == 1
        li = l // 2
        router = None
        if is_moe:
            wr = jnp.zeros((d, LOGIT_LANES), BF16).at[:, :N_EXPERTS].set(w_router[li].astype(BF16))
            br = jnp.zeros((1, LOGIT_LANES), F32).at[0, :N_EXPERTS].set(b_router[li])
            router = (wr, br)
        res = _postmix(xa, attn, four, mod_l, g_ffn[l].reshape(1, d), wo[:ATTN_WIDTH], wo[ATTN_WIDTH:],
                       router, n_ctx=n_ctx, seq=s)
        if not is_moe:
            x1, h2 = res
            xa = _dense_ffn(h2, x1, mod_l, _chunk_cols(w_gate_dense[li], FF_CHUNK),
                            _chunk_cols(w_up_dense[li], FF_CHUNK), _chunk_rows(w_down_dense[li], FF_CHUNK),
                            n_ctx=n_ctx, seq=s)
        else:
            x1, h2, logits = res
            gates, pos, row_tok, block_expert, n_used = _route(logits)
            xs = jnp.take(h2, row_tok, axis=0)
            ys = _moe_ffn(xs, block_expert, n_used, _chunk_cols(w_gate_moe[li], TF_MOE),
                          _chunk_cols(w_up_moe[li], TF_MOE), _chunk_rows(w_down_moe[li], TF_MOE))
            y0 = jnp.take(ys, pos[:, 0], axis=0)
            y1 = jnp.take(ys, pos[:, 1], axis=0)
            gates_p = jnp.zeros((t, LOGIT_LANES), F32).at[:, :TOP_K].set(gates)
            xa = _moe_combine(x1, y0, y1, gates_p, mod_l, n_ctx=n_ctx, seq=s)

    return xa[n_ctx:].reshape(b, s, d)
```

```python
import functools

import numpy as np
import jax
import jax.numpy as jnp
from jax import lax
from jax.experimental import pallas as pl
from jax.experimental.pallas import tpu as pltpu
from jax.experimental.pallas import tpu_sc as plsc

D_MODEL = 1024
DEPTH = 4
GRID_W = 64
HEAD_DIM = 64
ATTN_WIDTH = 512
N_Q_HEADS = 8
N_KV_HEADS = 2
KV_REP = 4
KV_WIDTH = 128
FOURIER_WIDTH = 512
N_FOURIER_GROUPS = 4
FOURIER_GROUP = 128
ROT_PER_AXIS = 32
ROPE_THETA = 10000.0
N_MOD = 6
D_FF_DENSE = 2816
N_EXPERTS = 8
TOP_K = 2
D_FF_EXPERT = 3584
EPS = 1e-6

QKV_WIDTH = ATTN_WIDTH + 2 * KV_WIDTH
MOD_ROWS = 16
LOGIT_LANES = 128

TM_MIX = 512
TQ_ATTN = 512
TR_FOUR = 512
TM_DENSE = 1024
FF_CHUNK = 256
TM_MOE = 1024
TF_MOE = 896
SC_DISPATCH_ROWS = 64
SC_GATHER_ROWS = 32
VMEM_LIMIT = 56 * 1024 * 1024

F32 = jnp.float32
BF16 = jnp.bfloat16


def _cparams(n_axes):
    return pltpu.CompilerParams(
        dimension_semantics=("arbitrary",) * n_axes, vmem_limit_bytes=VMEM_LIMIT)


def _rope_tables_t(s, tm):
    n_rows = s // GRID_W
    rows = np.repeat(np.arange(n_rows), GRID_W).astype(np.float64)
    cols = np.tile(np.arange(GRID_W), n_rows).astype(np.float64)
    inv_freq = (ROPE_THETA ** (-np.arange(0, ROT_PER_AXIS, 2, dtype=np.float32) / ROT_PER_AXIS)
                ).astype(np.float32).astype(np.float64)
    ang_r = (rows[None, :].astype(np.float32) * inv_freq[:, None].astype(np.float32)).astype(np.float64)
    ang_c = (cols[None, :].astype(np.float32) * inv_freq[:, None].astype(np.float32)).astype(np.float64)
    cos = np.concatenate([np.cos(ang_r), np.cos(ang_r), np.cos(ang_c), np.cos(ang_c)], axis=0)
    sin = np.concatenate([-np.sin(ang_r), np.sin(ang_r), -np.sin(ang_c), np.sin(ang_c)], axis=0)
    cos = np.concatenate([np.ones((HEAD_DIM, tm)), cos], axis=1)
    sin = np.concatenate([np.zeros((HEAD_DIM, tm)), sin], axis=1)
    return jnp.asarray(cos, F32), jnp.asarray(sin, F32)


def _dft_mats(n):
    k = np.arange(n, dtype=np.int64)
    ang = 2.0 * np.pi * ((k[:, None] * k[None, :]) % n).astype(np.float64) / n
    return jnp.asarray(np.cos(ang), BF16), jnp.asarray(np.sin(ang), BF16)


def _chunk_cols(w, chunk):
    *lead, d, n = w.shape
    w = w.astype(BF16).reshape(*lead, d, n // chunk, chunk)
    return jnp.swapaxes(w, -3, -2)


def _chunk_rows(w, chunk):
    *lead, n, d = w.shape
    return w.astype(BF16).reshape(*lead, n // chunk, chunk, d)


def _mod_kernel(c_ref, w_ref, b_ref, o_ref):
    c = c_ref[...]
    s = (c / (1.0 + jnp.exp(-c))).astype(BF16)
    w = w_ref[0].astype(BF16)
    o_ref[0] = jnp.dot(s, w, preferred_element_type=F32) + b_ref[0]


def _modulation(cvec, w_mod, b_mod):
    depth, d, n = w_mod.shape
    tn = 1536
    return pl.pallas_call(
        _mod_kernel,
        grid=(depth, n // tn),
        in_specs=[
            pl.BlockSpec((MOD_ROWS, d), lambda l, j: (0, 0)),
            pl.BlockSpec((1, d, tn), lambda l, j: (l, 0, j)),
            pl.BlockSpec((1, 1, tn), lambda l, j: (l, 0, j)),
        ],
        out_specs=pl.BlockSpec((1, MOD_ROWS, tn), lambda l, j: (l, 0, j)),
        out_shape=jax.ShapeDtypeStruct((depth, MOD_ROWS, n), F32),
        compiler_params=_cparams(2),
        name="adaln_vectors",
    )(cvec, w_mod, b_mod.reshape(depth, 1, n))


def _mod_row(i, n_ctx_tiles, tiles_per_batch):
    lat = jnp.maximum(i - n_ctx_tiles, 0)
    return jnp.where(i < n_ctx_tiles, 0, lat // tiles_per_batch + 1)


def _mod_vec(mod_ref, row, comp):
    return mod_ref[pl.ds(row, 1), comp * D_MODEL:(comp + 1) * D_MODEL]


def _pack_bf16_pairs(xb):
    n = xb.shape[1] // 2
    bits = lax.bitcast_convert_type(xb.astype(F32), jnp.uint32)
    return (bits[:, :n] >> 16) | (bits[:, n:] & jnp.uint32(0xFFFF0000))


def _unpack_bf16_pairs(w):
    lo = lax.bitcast_convert_type(w << 16, F32)
    hi = lax.bitcast_convert_type(w & jnp.uint32(0xFFFF0000), F32)
    return jnp.concatenate([lo, hi], axis=1).astype(BF16)


def _norm_modulate(x, g, shift, scale):
    ms = jnp.mean(x * x, axis=-1, keepdims=True)
    y = x * lax.rsqrt(ms + EPS) * g
    return y * (1.0 + scale) + shift


def _premix_kernel(x_ref, mod_ref, g_ref, wt_ref, wf_ref, gq_ref, gk_ref, cos_ref, sin_ref,
                   q_ref, k_ref, v_ref, f_ref, *, n_ctx_tiles, tiles_per_batch):
    i = pl.program_id(0)
    row = _mod_row(i, n_ctx_tiles, tiles_per_batch)
    h = _norm_modulate(x_ref[...], g_ref[...], _mod_vec(mod_ref, row, 0), _mod_vec(mod_ref, row, 1))
    hb = h.astype(BF16)
    f_ref[...] = jnp.dot(hb, wf_ref[...], preferred_element_type=F32).astype(BF16)
    pt = lax.dot_general(wt_ref[...], hb, (((1,), (1,)), ((), ())), preferred_element_type=F32)
    v_ref[...] = pt[ATTN_WIDTH + KV_WIDTH:, :].astype(BF16)
    cos = cos_ref[...]
    sin = sin_ref[...]

    def norm_rope(xh, gain):
        ms = jnp.mean(xh * xh, axis=0, keepdims=True)
        y = xh * lax.rsqrt(ms + EPS) * gain
        half = ROT_PER_AXIS // 2
        swapped = jnp.concatenate(
            [y[half:2 * half], y[0:half], y[3 * half:4 * half], y[2 * half:3 * half]], axis=0)
        return y * cos + swapped * sin

    gq = gq_ref[...]
    gk = gk_ref[...]
    for hh in range(N_Q_HEADS):
        q_ref[hh * HEAD_DIM:(hh + 1) * HEAD_DIM, :] = norm_rope(
            pt[hh * HEAD_DIM:(hh + 1) * HEAD_DIM, :], gq).astype(BF16)
    for hh in range(N_KV_HEADS):
        lo = ATTN_WIDTH + hh * HEAD_DIM
        k_ref[hh * HEAD_DIM:(hh + 1) * HEAD_DIM, :] = norm_rope(pt[lo:lo + HEAD_DIM, :], gk).astype(BF16)


def _premix(x, mod_l, g_mix, wt, wf, gq, gk, cos_t, sin_t, *, n_ctx, seq):
    t, d = x.shape
    tm = TM_MIX
    n_ctx_tiles = n_ctx // tm
    tpb = seq // tm

    def tab_idx(i):
        lat = jnp.maximum(i - n_ctx_tiles, 0)
        return (0, jnp.where(i < n_ctx_tiles, 0, lax.rem(lat, tpb) + 1))

    full = lambda shape: pl.BlockSpec(shape, lambda i: (0,) * len(shape))
    return pl.pallas_call(
        functools.partial(_premix_kernel, n_ctx_tiles=n_ctx_tiles, tiles_per_batch=tpb),
        grid=(t // tm,),
        in_specs=[
            pl.BlockSpec((tm, d), lambda i: (i, 0)),
            full(mod_l.shape),
            full((1, d)),
            full(wt.shape),
            full(wf.shape),
            full(gq.shape),
            full(gk.shape),
            pl.BlockSpec((HEAD_DIM, tm), tab_idx),
            pl.BlockSpec((HEAD_DIM, tm), tab_idx),
        ],
        out_specs=[
            pl.BlockSpec((ATTN_WIDTH, tm), lambda i: (0, i)),
            pl.BlockSpec((KV_WIDTH, tm), lambda i: (0, i)),
            pl.BlockSpec((KV_WIDTH, tm), lambda i: (0, i)),
            pl.BlockSpec((tm, FOURIER_WIDTH), lambda i: (i, 0)),
        ],
        out_shape=[
            jax.ShapeDtypeStruct((ATTN_WIDTH, t), BF16),
            jax.ShapeDtypeStruct((KV_WIDTH, t), BF16),
            jax.ShapeDtypeStruct((KV_WIDTH, t), BF16),
            jax.ShapeDtypeStruct((t, FOURIER_WIDTH), BF16),
        ],
        compiler_params=_cparams(1),
        name="premix_project",
    )(x, mod_l, g_mix, wt, wf, gq, gk, cos_t, sin_t)


def _attn_kernel(*refs, n_key_blocks, tq):
    q_ref = refs[0]
    k_refs = refs[1:1 + n_key_blocks]
    v_refs = refs[1 + n_key_blocks:1 + 2 * n_key_blocks]
    o_ref = refs[1 + 2 * n_key_blocks]
    kall_ref, vall_ref, ot_ref = refs[2 + 2 * n_key_blocks:]
    g = pl.program_id(1)
    qt = pl.program_id(2)

    @pl.when(jnp.logical_and(g == 0, qt == 0))
    def _():
        off = 0
        for kr, vr in zip(k_refs, v_refs):
            n = kr.shape[1]
            kall_ref[off:off + n, :] = kr[...].astype(F32).T.astype(BF16)
            vall_ref[:, off:off + n] = vr[...]
            off += n

    row_group = lax.broadcasted_iota(jnp.int32, (KV_WIDTH, tq), 0) // HEAD_DIM

    def head(hh, carry):
        lo = pl.multiple_of(hh * HEAD_DIM, HEAD_DIM)
        qh = q_ref[pl.ds(lo, HEAD_DIM), :]
        q2 = jnp.concatenate([qh, qh], axis=0)
        q2 = jnp.where(row_group == g, q2, jnp.zeros_like(q2))
        st = jnp.dot(kall_ref[...], q2, preferred_element_type=F32)
        m = jnp.max(st, axis=0, keepdims=True)
        p = jnp.exp(st - m)
        denom = jnp.sum(p, axis=0, keepdims=True)
        vt = vall_ref[pl.ds(pl.multiple_of(g * HEAD_DIM, HEAD_DIM), HEAD_DIM), :]
        ot = jnp.dot(vt, p.astype(BF16), preferred_element_type=F32)
        ot_ref[pl.ds(lo, HEAD_DIM), :] = ot / denom
        return carry

    lax.fori_loop(0, KV_REP, head, 0)
    o_ref[...] = ot_ref[...].T.astype(BF16)


def _attention(q_t, k_t, v_t, *, q_tile0, n_q_tiles, key_blocks, tq, n_batch, name):
    n_keys = sum(c for c, _ in key_blocks)
    nkb = len(key_blocks)
    q_spec = pl.BlockSpec((KV_REP * HEAD_DIM, tq),
                          lambda b, g, i: (g, q_tile0 + b * n_q_tiles + i))
    k_specs = [pl.BlockSpec((KV_WIDTH, c), (lambda f: (lambda b, g, i: (0, f(b))))(f))
               for c, f in key_blocks]
    return pl.pallas_call(
        functools.partial(_attn_kernel, n_key_blocks=nkb, tq=tq),
        grid=(n_batch, N_KV_HEADS, n_q_tiles),
        in_specs=[q_spec] + k_specs + k_specs,
        out_specs=pl.BlockSpec((tq, KV_REP * HEAD_DIM),
                               lambda b, g, i: (b * n_q_tiles + i, g)),
        out_shape=jax.ShapeDtypeStruct((n_batch * n_q_tiles * tq, ATTN_WIDTH), BF16),
        scratch_shapes=[pltpu.VMEM((n_keys, KV_WIDTH), BF16), pltpu.VMEM((KV_WIDTH, n_keys), BF16),
                        pltpu.VMEM((KV_REP * HEAD_DIM, tq), F32)],
        compiler_params=_cparams(3),
        name=name,
    )(q_t, *([k_t] * nkb), *([v_t] * nkb))


def _fourier_kernel(c_ref, s_ref, x_ref, cc_ref, sc_ref, w_ref, o_ref, *, norm):
    x = x_ref[...]
    a = jnp.dot(c_ref[...], x, preferred_element_type=F32).astype(BF16)
    b = jnp.dot(s_ref[...], x, preferred_element_type=F32).astype(BF16)
    cc = cc_ref[...]
    sc = sc_ref[...]
    for grp in range(N_FOURIER_GROUPS):
        sl = slice(grp * FOURIER_GROUP, (grp + 1) * FOURIER_GROUP)
        spec = (jnp.dot(a[:, sl], cc, preferred_element_type=F32)
                - jnp.dot(b[:, sl], sc, preferred_element_type=F32)) * norm
        o_ref[:, sl] = jnp.dot(spec.astype(BF16), w_ref[grp], preferred_element_type=F32).astype(BF16)


def _fourier(f, cmat, smat, cc, sc, w_four, *, n, tr, x_block0, n_batch, name):
    n_row_tiles = n // tr
    return pl.pallas_call(
        functools.partial(_fourier_kernel, norm=float(1.0 / np.sqrt(n * FOURIER_GROUP))),
        grid=(n_row_tiles, n_batch),
        in_specs=[
            pl.BlockSpec((tr, n), lambda i, b: (i, 0)),
            pl.BlockSpec((tr, n), lambda i, b: (i, 0)),
            pl.BlockSpec((n, FOURIER_WIDTH), lambda i, b: (x_block0 + b, 0)),
            pl.BlockSpec(cc.shape, lambda i, b: (0, 0)),
            pl.BlockSpec(sc.shape, lambda i, b: (0, 0)),
            pl.BlockSpec(w_four.shape, lambda i, b: (0, 0, 0)),
        ],
        out_specs=pl.BlockSpec((tr, FOURIER_WIDTH), lambda i, b: (b * n_row_tiles + i, 0)),
        out_shape=jax.ShapeDtypeStruct((n_batch * n, FOURIER_WIDTH), BF16),
        compiler_params=_cparams(2),
        name=name,
    )(cmat, smat, f, cc, sc, w_four)


def _postmix_kernel(x_ref, ac_ref, al_ref, fc_ref, fl_ref, mod_ref, g_ref, woa_ref, wof_ref, *rest,
                    n_ctx_tiles, tiles_per_batch, with_router):
    if with_router:
        wr_ref, br_ref, x1_ref, h_ref, lg_ref = rest
    else:
        x1_ref, h_ref = rest
    i = pl.program_id(0)
    row = _mod_row(i, n_ctx_tiles, tiles_per_batch)
    is_ctx = i < n_ctx_tiles
    a = jnp.where(is_ctx, ac_ref[...], al_ref[...])
    f = jnp.where(is_ctx, fc_ref[...], fl_ref[...])
    mix = (jnp.dot(a, woa_ref[...], preferred_element_type=F32)
           + jnp.dot(f, wof_ref[...], preferred_element_type=F32))
    x1 = x_ref[...] + _mod_vec(mod_ref, row, 2) * mix
    x1_ref[...] = x1
    h = _norm_modulate(x1, g_ref[...], _mod_vec(mod_ref, row, 3), _mod_vec(mod_ref, row, 4))
    hb = h.astype(BF16)
    if with_router:
        h_ref[...] = _pack_bf16_pairs(hb)
        lg_ref[...] = jnp.dot(hb, wr_ref[...], preferred_element_type=F32) + br_ref[...]
    else:
        h_ref[...] = hb


def _postmix(x, attn_ctx, attn_lat, four_ctx, four_lat, mod_l, g_ffn, wo_a, wo_f, router, *, n_ctx, seq):
    t, d = x.shape
    tm = TM_MIX
    nct = n_ctx // tm
    full = lambda shape: pl.BlockSpec(shape, lambda i: (0,) * len(shape))
    tile = lambda w: pl.BlockSpec((tm, w), lambda i: (i, 0))
    ctx_tile = lambda w: pl.BlockSpec((tm, w), lambda i: (jnp.minimum(i, nct - 1), 0))
    lat_tile = lambda w: pl.BlockSpec((tm, w), lambda i: (jnp.maximum(i - nct, 0), 0))
    in_specs = [tile(d), ctx_tile(ATTN_WIDTH), lat_tile(ATTN_WIDTH), ctx_tile(FOURIER_WIDTH),
                lat_tile(FOURIER_WIDTH), full(mod_l.shape), full((1, d)), full(wo_a.shape), full(wo_f.shape)]
    if router is None:
        out_specs = [tile(d), tile(d)]
        out_shape = [jax.ShapeDtypeStruct((t, d), F32), jax.ShapeDtypeStruct((t, d), BF16)]
    else:
        out_specs = [tile(d), tile(d // 2)]
        out_shape = [jax.ShapeDtypeStruct((t, d), F32), jax.ShapeDtypeStruct((t, d // 2), jnp.uint32)]
    args = [x, attn_ctx, attn_lat, four_ctx, four_lat, mod_l, g_ffn, wo_a, wo_f]
    if router is not None:
        in_specs += [full(router[0].shape), full(router[1].shape)]
        out_specs.append(tile(LOGIT_LANES))
        out_shape.append(jax.ShapeDtypeStruct((t, LOGIT_LANES), F32))
        args += list(router)
    return pl.pallas_call(
        functools.partial(_postmix_kernel, n_ctx_tiles=n_ctx // tm, tiles_per_batch=seq // tm,
                          with_router=router is not None),
        grid=(t // tm,),
        in_specs=in_specs,
        out_specs=out_specs,
        out_shape=out_shape,
        compiler_params=_cparams(1),
        name="postmix_wo_norm",
    )(*args)


def _silu_mul(gate, up):
    return (gate / (1.0 + jnp.exp(-gate))) * up


def _dense_ffn_kernel(h_ref, x1_ref, mod_ref, wg_ref, wu_ref, wd_ref, o_ref, acc_ref, *,
                      n_ctx_tiles, tiles_per_batch):
    i = pl.program_id(0)
    row = _mod_row(i, n_ctx_tiles, tiles_per_batch)
    acc_ref[...] = jnp.zeros_like(acc_ref)

    def chunk(c, carry):
        h = h_ref[...]
        gate = jnp.dot(h, wg_ref[c], preferred_element_type=F32)
        up = jnp.dot(h, wu_ref[c], preferred_element_type=F32)
        act = _silu_mul(gate, up).astype(BF16)
        acc_ref[...] += jnp.dot(act, wd_ref[c], preferred_element_type=F32)
        return carry

    lax.fori_loop(0, wg_ref.shape[0], chunk, 0)
    o_ref[...] = x1_ref[...] + _mod_vec(mod_ref, row, 5) * acc_ref[...]


def _dense_ffn(h, x1, mod_l, wg, wu, wd, *, n_ctx, seq):
    t, d = x1.shape
    tm = TM_DENSE
    full = lambda shape: pl.BlockSpec(shape, lambda i: (0,) * len(shape))
    tile = lambda: pl.BlockSpec((tm, d), lambda i: (i, 0))
    return pl.pallas_call(
        functools.partial(_dense_ffn_kernel, n_ctx_tiles=n_ctx // tm, tiles_per_batch=seq // tm),
        grid=(t // tm,),
        in_specs=[tile(), tile(), full(mod_l.shape), full(wg.shape), full(wu.shape), full(wd.shape)],
        out_specs=tile(),
        out_shape=jax.ShapeDtypeStruct((t, d), F32),
        scratch_shapes=[pltpu.VMEM((tm, d), F32)],
        compiler_params=_cparams(1),
        name="dense_swiglu",
    )(h, x1, mod_l, wg, wu, wd)


def _moe_ffn_kernel(be_ref, nv_ref, x_ref, wg_ref, wu_ref, wd_ref, o_ref, acc_ref, xb_ref):
    i = pl.program_id(0)
    j = pl.program_id(1)
    n_valid = nv_ref[i]

    @pl.when(j == 0)
    def _():
        x = _unpack_bf16_pairs(x_ref[...])
        rows = lax.broadcasted_iota(jnp.int32, x.shape, 0)
        xb_ref[...] = jnp.where(rows < n_valid, x, jnp.zeros_like(x))

    @pl.when(n_valid > 0)
    def _():
        x = xb_ref[...]
        gate = jnp.dot(x, wg_ref[0, 0], preferred_element_type=F32)
        up = jnp.dot(x, wu_ref[0, 0], preferred_element_type=F32)
        act = _silu_mul(gate, up).astype(BF16)
        part = jnp.dot(act, wd_ref[0, 0], preferred_element_type=F32)

        @pl.when(j == 0)
        def _():
            acc_ref[...] = part

        @pl.when(j > 0)
        def _():
            acc_ref[...] += part

    @pl.when(j == pl.num_programs(1) - 1)
    def _():
        o_ref[...] = jnp.where(n_valid > 0, acc_ref[...], 0.0)


def _moe_ffn(xs, block_expert, n_valid, wg, wu, wd):
    n_rows = xs.shape[0]
    d = D_MODEL
    tm = TM_MOE
    n_blocks = n_rows // tm
    n_ff, tf = wg.shape[1], wg.shape[3]
    grid_spec = pltpu.PrefetchScalarGridSpec(
        num_scalar_prefetch=2,
        grid=(n_blocks, n_ff),
        in_specs=[
            pl.BlockSpec((tm, d // 2), lambda i, j, be, nb: (i, 0)),
            pl.BlockSpec((1, 1, d, tf), lambda i, j, be, nb: (be[i], j, 0, 0)),
            pl.BlockSpec((1, 1, d, tf), lambda i, j, be, nb: (be[i], j, 0, 0)),
            pl.BlockSpec((1, 1, tf, d), lambda i, j, be, nb: (be[i], j, 0, 0)),
        ],
        out_specs=pl.BlockSpec((tm, d), lambda i, j, be, nb: (i, 0)),
        scratch_shapes=[pltpu.VMEM((tm, d), F32), pltpu.VMEM((tm, d), BF16)],
    )
    return pl.pallas_call(
        _moe_ffn_kernel,
        grid_spec=grid_spec,
        out_shape=jax.ShapeDtypeStruct((n_rows, d), F32),
        compiler_params=_cparams(2),
        name="expert_swiglu",
    )(block_expert, n_valid, xs, wg, wu, wd)


def _sc_mesh():
    return plsc.VectorSubcoreMesh(core_axis_name="c", subcore_axis_name="s")


def _sc_params():
    return pltpu.CompilerParams(use_tc_tiling_on_sc=True)


def _sc_dispatch(h_packed, dest, n_rows):
    t, w = h_packed.shape
    win = SC_DISPATCH_ROWS
    n_win = t // win
    idx = [dest[:, k].reshape(n_win, 1, win) for k in range(TOP_K)]

    @functools.partial(
        pl.kernel, out_type=jax.ShapeDtypeStruct((n_rows, w), h_packed.dtype), mesh=_sc_mesh(),
        scratch_types=[], compiler_params=_sc_params(), name="expert_dispatch_scatter")
    def run(x_hbm, i0_hbm, i1_hbm, o_hbm):
        def body(x_vmem, i0_vmem, i1_vmem):
            pltpu.sync_copy(x_vmem, o_hbm.at[i0_vmem.at[0, 0]])
            pltpu.sync_copy(x_vmem, o_hbm.at[i1_vmem.at[0, 0]])

        idx_spec = pl.BlockSpec((1, 1, win), lambda i: (i, 0, 0))
        pltpu.emit_pipeline(
            body, grid=(n_win,),
            in_specs=[pl.BlockSpec((win, w), lambda i: (i, 0)), idx_spec, idx_spec],
            out_specs=[], core_axis_name=("c", "s"), dimension_semantics=(pltpu.PARALLEL,),
        )(x_hbm, i0_hbm, i1_hbm)

    return run(h_packed, *idx)


def _sc_gather(ys, idx_flat):
    w = ys.shape[1]
    n = idx_flat.shape[0]
    win = SC_GATHER_ROWS
    n_win = n // win

    @functools.partial(
        pl.kernel, out_type=jax.ShapeDtypeStruct((n, w), ys.dtype), mesh=_sc_mesh(),
        scratch_types=[], compiler_params=_sc_params(), name="expert_combine_gather")
    def run(y_hbm, i_hbm, o_hbm):
        def body(i_vmem, o_vmem):
            pltpu.sync_copy(y_hbm.at[i_vmem.at[0, 0]], o_vmem)

        pltpu.emit_pipeline(
            body, grid=(n_win,),
            in_specs=[pl.BlockSpec((1, 1, win), lambda i: (i, 0, 0))],
            out_specs=[pl.BlockSpec((win, w), lambda i: (i, 0))],
            core_axis_name=("c", "s"), dimension_semantics=(pltpu.PARALLEL,),
        )(i_hbm, o_hbm)

    return run(ys, idx_flat.reshape(n_win, 1, win))


def _moe_combine_kernel(x1_ref, y_ref, g_ref, mod_ref, o_ref, *, n_ctx_tiles, tiles_per_batch):
    i = pl.program_id(0)
    row = _mod_row(i, n_ctx_tiles, tiles_per_batch)
    gates = g_ref[...]
    y = y_ref[:, :D_MODEL] * gates[:, 0:1] + y_ref[:, D_MODEL:] * gates[:, 1:2]
    o_ref[...] = x1_ref[...] + _mod_vec(mod_ref, row, 5) * y


def _moe_combine(x1, y_pairs, gates, mod_l, *, n_ctx, seq):
    t, d = x1.shape
    tm = TM_MIX
    tile = lambda w: pl.BlockSpec((tm, w), lambda i: (i, 0))
    return pl.pallas_call(
        functools.partial(_moe_combine_kernel, n_ctx_tiles=n_ctx // tm, tiles_per_batch=seq // tm),
        grid=(t // tm,),
        in_specs=[tile(d), tile(TOP_K * d), tile(LOGIT_LANES),
                  pl.BlockSpec(mod_l.shape, lambda i: (0, 0))],
        out_specs=tile(d),
        out_shape=jax.ShapeDtypeStruct((t, d), F32),
        compiler_params=_cparams(1),
        name="expert_combine",
    )(x1, y_pairs, gates, mod_l)


def _route(logits):
    t = logits.shape[0]
    top_val, top_idx = lax.top_k(logits[:, :N_EXPERTS], TOP_K)
    gates = jax.nn.softmax(top_val, axis=-1)
    expert = top_idx.reshape(-1).astype(jnp.int32)
    n_assign = t * TOP_K
    onehot = (expert[:, None] == jnp.arange(N_EXPERTS, dtype=jnp.int32)[None, :]).astype(jnp.int32)
    csum = jnp.cumsum(onehot, axis=0)
    rank = jnp.sum((csum - onehot) * onehot, axis=1)
    counts = csum[-1]
    padded = ((counts + TM_MOE - 1) // TM_MOE) * TM_MOE
    pend = jnp.cumsum(padded)
    pstart = pend - padded
    dest = (pstart[expert] + rank).astype(jnp.int32)
    n_blocks = -(-n_assign // TM_MOE) + N_EXPERTS
    block_start = jnp.arange(n_blocks, dtype=jnp.int32) * TM_MOE
    block_expert = jnp.minimum(jnp.searchsorted(pend, block_start, side="right"),
                               N_EXPERTS - 1).astype(jnp.int32)
    n_valid = jnp.clip(counts[block_expert] - (block_start - pstart[block_expert]), 0, TM_MOE)
    n_valid = jnp.where(block_start < pend[-1], n_valid, 0).astype(jnp.int32)
    return gates, dest.reshape(t, TOP_K), block_expert, n_valid, n_blocks * TM_MOE


def kernel(x, c, ctx, c_ctx, w_mod, b_mod, g_mix, g_ffn, g_q, g_k, w_in, w_four, w_o,
           w_gate_dense, w_up_dense, w_down_dense, w_router, b_router,
           w_gate_moe, w_up_moe, w_down_moe):
    b, s, d = x.shape
    n_ctx_len = ctx.shape[1]
    n_ctx = b * n_ctx_len
    t = n_ctx + b * s
    assert d == D_MODEL and b + 1 <= MOD_ROWS
    assert n_ctx % TM_DENSE == 0 and s % TM_DENSE == 0 and n_ctx_len % 128 == 0 and n_ctx % s == 0

    cvec = jnp.zeros((MOD_ROWS, d), F32).at[0].set(c_ctx).at[1:b + 1].set(c)
    mod = _modulation(cvec, w_mod, b_mod)

    cos_t, sin_t = _rope_tables_t(s, TM_MIX)
    c_lat, s_lat = _dft_mats(s)
    c_ctx_m, s_ctx_m = _dft_mats(n_ctx_len)
    c_grp, s_grp = _dft_mats(FOURIER_GROUP)

    xa = jnp.concatenate([ctx.reshape(n_ctx, d), x.reshape(b * s, d)], axis=0)

    for l in range(DEPTH):
        mod_l = mod[l]
        w_in_l = w_in[l]
        wt = w_in_l[:, :QKV_WIDTH].T.astype(BF16)
        wf = w_in_l[:, QKV_WIDTH:].astype(BF16)
        gq = jnp.broadcast_to((g_q[l] * (HEAD_DIM ** -0.5))[:, None], (HEAD_DIM, TM_MIX))
        gk = jnp.broadcast_to(g_k[l][:, None], (HEAD_DIM, TM_MIX))
        q_t, k_t, v_t, f = _premix(xa, mod_l, g_mix[l].reshape(1, d), wt, wf, gq, gk, cos_t, sin_t,
                                   n_ctx=n_ctx, seq=s)

        n_qt = s // TQ_ATTN
        lat_keys = [(n_ctx_len, lambda bb: bb), (s, lambda bb: n_ctx // s + bb)]
        attn_lat = _attention(q_t, k_t, v_t, q_tile0=n_ctx // TQ_ATTN, n_q_tiles=n_qt,
                              key_blocks=lat_keys, tq=TQ_ATTN, n_batch=b, name="attention_latent")
        attn_ctx = _attention(q_t, k_t, v_t, q_tile0=0, n_q_tiles=1,
                              key_blocks=[(n_ctx_len, lambda bb: bb)], tq=n_ctx_len, n_batch=b,
                              name="attention_context")

        wfour = w_four[l].astype(BF16)
        four_lat = _fourier(f, c_lat, s_lat, c_grp, s_grp, wfour, n=s, tr=TR_FOUR,
                            x_block0=n_ctx // s, n_batch=b, name="fourier_latent")
        four_ctx = _fourier(f, c_ctx_m, s_ctx_m, c_grp, s_grp, wfour, n=n_ctx_len, tr=n_ctx_len,
                            x_block0=0, n_batch=b, name="fourier_context")

        wo = w_o[l].astype(BF16)
        is_moe = l % 2 == 1
        li = l // 2
        router = None
        if is_moe:
            wr = jnp.zeros((d, LOGIT_LANES), BF16).at[:, :N_EXPERTS].set(w_router[li].astype(BF16))
            br = jnp.zeros((1, LOGIT_LANES), F32).at[0, :N_EXPERTS].set(b_router[li])
            router = (wr, br)
        res = _postmix(xa, attn_ctx, attn_lat, four_ctx, four_lat, mod_l, g_ffn[l].reshape(1, d),
                       wo[:ATTN_WIDTH], wo[ATTN_WIDTH:], router, n_ctx=n_ctx, seq=s)
        if not is_moe:
            x1, h2 = res
            xa = _dense_ffn(h2, x1, mod_l, _chunk_cols(w_gate_dense[li], FF_CHUNK),
                            _chunk_cols(w_up_dense[li], FF_CHUNK), _chunk_rows(w_down_dense[li], FF_CHUNK),
                            n_ctx=n_ctx, seq=s)
        else:
            x1, h2, logits = res
            gates, dest, block_expert, n_valid, n_rows = _route(logits)
            xs = _sc_dispatch(h2, dest, n_rows)
            ys = _moe_ffn(xs, block_expert, n_valid, _chunk_cols(w_gate_moe[li], TF_MOE),
                          _chunk_cols(w_up_moe[li], TF_MOE), _chunk_rows(w_down_moe[li], TF_MOE))
            y_pairs = _sc_gather(ys, dest.reshape(-1)).reshape(t, TOP_K * d)
            gates_p = jnp.zeros((t, LOGIT_LANES), F32).at[:, :TOP_K].set(gates)
            xa = _moe_combine(x1, y_pairs, gates_p, mod_l, n_ctx=n_ctx, seq=s)

    return xa[n_ctx:].reshape(b, s, d)
```

```python
import functools

import numpy as np
import jax
import jax.numpy as jnp
from jax import lax
from jax.experimental import pallas as pl
from jax.experimental.pallas import tpu as pltpu
from jax.experimental.pallas import tpu_sc as plsc

D_MODEL = 1024
DEPTH = 4
GRID_W = 64
HEAD_DIM = 64
ATTN_WIDTH = 512
N_Q_HEADS = 8
N_KV_HEADS = 2
KV_REP = 4
KV_WIDTH = 128
FOURIER_WIDTH = 512
N_FOURIER_GROUPS = 4
FOURIER_GROUP = 128
ROT_PER_AXIS = 32
ROPE_THETA = 10000.0
N_MOD = 6
D_FF_DENSE = 2816
N_EXPERTS = 8
TOP_K = 2
D_FF_EXPERT = 3584
EPS = 1e-6

QKV_WIDTH = ATTN_WIDTH + 2 * KV_WIDTH
MOD_ROWS = 16
LOGIT_LANES = 128

TM_MIX = 512
TQ_ATTN = 512
TR_FOUR = 512
TM_DENSE = 1024
FF_CHUNK = 256
TM_MOE = 1024
TF_MOE = 512
SC_DISPATCH_ROWS = 64
SC_GATHER_ROWS = 32
VMEM_LIMIT = 56 * 1024 * 1024

F32 = jnp.float32
BF16 = jnp.bfloat16


def _cparams(n_axes):
    return pltpu.CompilerParams(
        dimension_semantics=("arbitrary",) * n_axes, vmem_limit_bytes=VMEM_LIMIT)


def _rope_tables_t(s, tm):
    n_rows = s // GRID_W
    rows = np.repeat(np.arange(n_rows), GRID_W).astype(np.float64)
    cols = np.tile(np.arange(GRID_W), n_rows).astype(np.float64)
    inv_freq = (ROPE_THETA ** (-np.arange(0, ROT_PER_AXIS, 2, dtype=np.float32) / ROT_PER_AXIS)
                ).astype(np.float32).astype(np.float64)
    ang_r = (rows[None, :].astype(np.float32) * inv_freq[:, None].astype(np.float32)).astype(np.float64)
    ang_c = (cols[None, :].astype(np.float32) * inv_freq[:, None].astype(np.float32)).astype(np.float64)
    cos = np.concatenate([np.cos(ang_r), np.cos(ang_r), np.cos(ang_c), np.cos(ang_c)], axis=0)
    sin = np.concatenate([-np.sin(ang_r), np.sin(ang_r), -np.sin(ang_c), np.sin(ang_c)], axis=0)
    cos = np.concatenate([np.ones((HEAD_DIM, tm)), cos], axis=1)
    sin = np.concatenate([np.zeros((HEAD_DIM, tm)), sin], axis=1)
    return jnp.asarray(cos, F32), jnp.asarray(sin, F32)


def _dft_mats(n):
    k = np.arange(n, dtype=np.int64)
    ang = 2.0 * np.pi * ((k[:, None] * k[None, :]) % n).astype(np.float64) / n
    return jnp.asarray(np.cos(ang), BF16), jnp.asarray(np.sin(ang), BF16)


def _chunk_cols(w, chunk):
    *lead, d, n = w.shape
    w = w.astype(BF16).reshape(*lead, d, n // chunk, chunk)
    return jnp.swapaxes(w, -3, -2)


def _chunk_rows(w, chunk):
    *lead, n, d = w.shape
    return w.astype(BF16).reshape(*lead, n // chunk, chunk, d)


def _mod_kernel(c_ref, w_ref, b_ref, o_ref):
    c = c_ref[...]
    s = (c / (1.0 + jnp.exp(-c))).astype(BF16)
    w = w_ref[0].astype(BF16)
    o_ref[0] = jnp.dot(s, w, preferred_element_type=F32) + b_ref[0]


def _modulation(cvec, w_mod, b_mod):
    depth, d, n = w_mod.shape
    tn = 1536
    return pl.pallas_call(
        _mod_kernel,
        grid=(depth, n // tn),
        in_specs=[
            pl.BlockSpec((MOD_ROWS, d), lambda l, j: (0, 0)),
            pl.BlockSpec((1, d, tn), lambda l, j: (l, 0, j)),
            pl.BlockSpec((1, 1, tn), lambda l, j: (l, 0, j)),
        ],
        out_specs=pl.BlockSpec((1, MOD_ROWS, tn), lambda l, j: (l, 0, j)),
        out_shape=jax.ShapeDtypeStruct((depth, MOD_ROWS, n), F32),
        compiler_params=_cparams(2),
        name="adaln_vectors",
    )(cvec, w_mod, b_mod.reshape(depth, 1, n))


def _mod_row(i, n_ctx_tiles, tiles_per_batch):
    lat = jnp.maximum(i - n_ctx_tiles, 0)
    return jnp.where(i < n_ctx_tiles, 0, lat // tiles_per_batch + 1)


def _mod_vec(mod_ref, row, comp):
    return mod_ref[pl.ds(row, 1), comp * D_MODEL:(comp + 1) * D_MODEL]


def _pack_bf16_pairs(xb):
    n = xb.shape[1] // 2
    bits = lax.bitcast_convert_type(xb.astype(F32), jnp.uint32)
    return (bits[:, :n] >> 16) | (bits[:, n:] & jnp.uint32(0xFFFF0000))


def _unpack_bf16_pairs(w):
    lo = lax.bitcast_convert_type(w << 16, F32)
    hi = lax.bitcast_convert_type(w & jnp.uint32(0xFFFF0000), F32)
    return jnp.concatenate([lo, hi], axis=1).astype(BF16)


def _norm_modulate(x, g, shift, scale):
    ms = jnp.mean(x * x, axis=-1, keepdims=True)
    y = x * lax.rsqrt(ms + EPS) * g
    return y * (1.0 + scale) + shift


def _premix_kernel(x_ref, mod_ref, g_ref, wt_ref, wf_ref, gq_ref, gk_ref, cos_ref, sin_ref,
                   q_ref, k_ref, v_ref, f_ref, *, n_ctx_tiles, tiles_per_batch):
    i = pl.program_id(0)
    row = _mod_row(i, n_ctx_tiles, tiles_per_batch)
    h = _norm_modulate(x_ref[...], g_ref[...], _mod_vec(mod_ref, row, 0), _mod_vec(mod_ref, row, 1))
    hb = h.astype(BF16)
    f_ref[...] = jnp.dot(hb, wf_ref[...], preferred_element_type=F32).astype(BF16)
    pt = lax.dot_general(wt_ref[...], hb, (((1,), (1,)), ((), ())), preferred_element_type=F32)
    v_ref[...] = pt[ATTN_WIDTH + KV_WIDTH:, :].astype(BF16)
    cos = cos_ref[...]
    sin = sin_ref[...]

    def norm_rope(xh, gain):
        ms = jnp.mean(xh * xh, axis=0, keepdims=True)
        y = xh * lax.rsqrt(ms + EPS) * gain
        half = ROT_PER_AXIS // 2
        swapped = jnp.concatenate(
            [y[half:2 * half], y[0:half], y[3 * half:4 * half], y[2 * half:3 * half]], axis=0)
        return y * cos + swapped * sin

    gq = gq_ref[...]
    gk = gk_ref[...]
    for hh in range(N_Q_HEADS):
        q_ref[hh * HEAD_DIM:(hh + 1) * HEAD_DIM, :] = norm_rope(
            pt[hh * HEAD_DIM:(hh + 1) * HEAD_DIM, :], gq).astype(BF16)
    for hh in range(N_KV_HEADS):
        lo = ATTN_WIDTH + hh * HEAD_DIM
        k_ref[hh * HEAD_DIM:(hh + 1) * HEAD_DIM, :] = norm_rope(pt[lo:lo + HEAD_DIM, :], gk).astype(BF16)


def _premix(x, mod_l, g_mix, wt, wf, gq, gk, cos_t, sin_t, *, n_ctx, seq):
    t, d = x.shape
    tm = TM_MIX
    n_ctx_tiles = n_ctx // tm
    tpb = seq // tm

    def tab_idx(i):
        lat = jnp.maximum(i - n_ctx_tiles, 0)
        return (0, jnp.where(i < n_ctx_tiles, 0, lax.rem(lat, tpb) + 1))

    full = lambda shape: pl.BlockSpec(shape, lambda i: (0,) * len(shape))
    return pl.pallas_call(
        functools.partial(_premix_kernel, n_ctx_tiles=n_ctx_tiles, tiles_per_batch=tpb),
        grid=(t // tm,),
        in_specs=[
            pl.BlockSpec((tm, d), lambda i: (i, 0)),
            full(mod_l.shape),
            full((1, d)),
            full(wt.shape),
            full(wf.shape),
            full(gq.shape),
            full(gk.shape),
            pl.BlockSpec((HEAD_DIM, tm), tab_idx),
            pl.BlockSpec((HEAD_DIM, tm), tab_idx),
        ],
        out_specs=[
            pl.BlockSpec((ATTN_WIDTH, tm), lambda i: (0, i)),
            pl.BlockSpec((KV_WIDTH, tm), lambda i: (0, i)),
            pl.BlockSpec((KV_WIDTH, tm), lambda i: (0, i)),
            pl.BlockSpec((tm, FOURIER_WIDTH), lambda i: (i, 0)),
        ],
        out_shape=[
            jax.ShapeDtypeStruct((ATTN_WIDTH, t), BF16),
            jax.ShapeDtypeStruct((KV_WIDTH, t), BF16),
            jax.ShapeDtypeStruct((KV_WIDTH, t), BF16),
            jax.ShapeDtypeStruct((t, FOURIER_WIDTH), BF16),
        ],
        compiler_params=_cparams(1),
        name="premix_project",
    )(x, mod_l, g_mix, wt, wf, gq, gk, cos_t, sin_t)


def _attn_kernel(*refs, n_key_blocks, tq):
    q_ref = refs[0]
    k_refs = refs[1:1 + n_key_blocks]
    v_refs = refs[1 + n_key_blocks:1 + 2 * n_key_blocks]
    o_ref = refs[1 + 2 * n_key_blocks]
    kall_ref, vall_ref, ot_ref = refs[2 + 2 * n_key_blocks:]
    g = pl.program_id(1)
    qt = pl.program_id(2)

    @pl.when(jnp.logical_and(g == 0, qt == 0))
    def _():
        off = 0
        for kr, vr in zip(k_refs, v_refs):
            n = kr.shape[1]
            kall_ref[off:off + n, :] = kr[...].astype(F32).T.astype(BF16)
            vall_ref[:, off:off + n] = vr[...]
            off += n

    row_group = lax.broadcasted_iota(jnp.int32, (KV_WIDTH, tq), 0) // HEAD_DIM

    vt = vall_ref[pl.ds(pl.multiple_of(g * HEAD_DIM, HEAD_DIM), HEAD_DIM), :]
    for hh in range(KV_REP):
        lo = hh * HEAD_DIM
        qh = q_ref[pl.ds(lo, HEAD_DIM), :]
        q2 = jnp.concatenate([qh, qh], axis=0)
        q2 = jnp.where(row_group == g, q2, jnp.zeros_like(q2))
        st = jnp.dot(kall_ref[...], q2, preferred_element_type=F32)
        m = jnp.max(st, axis=0, keepdims=True)
        p = jnp.exp(st - m)
        denom = jnp.sum(p, axis=0, keepdims=True)
        ot = jnp.dot(vt, p.astype(BF16), preferred_element_type=F32)
        ot_ref[pl.ds(lo, HEAD_DIM), :] = ot / denom
    o_ref[...] = ot_ref[...].T.astype(BF16)


def _attention(q_t, k_t, v_t, *, q_tile0, n_q_tiles, key_blocks, tq, n_batch, name):
    n_keys = sum(c for c, _ in key_blocks)
    nkb = len(key_blocks)
    q_spec = pl.BlockSpec((KV_REP * HEAD_DIM, tq),
                          lambda b, g, i: (g, q_tile0 + b * n_q_tiles + i))
    k_specs = [pl.BlockSpec((KV_WIDTH, c), (lambda f: (lambda b, g, i: (0, f(b))))(f))
               for c, f in key_blocks]
    return pl.pallas_call(
        functools.partial(_attn_kernel, n_key_blocks=nkb, tq=tq),
        grid=(n_batch, N_KV_HEADS, n_q_tiles),
        in_specs=[q_spec] + k_specs + k_specs,
        out_specs=pl.BlockSpec((tq, KV_REP * HEAD_DIM),
                               lambda b, g, i: (b * n_q_tiles + i, g)),
        out_shape=jax.ShapeDtypeStruct((n_batch * n_q_tiles * tq, ATTN_WIDTH), BF16),
        scratch_shapes=[pltpu.VMEM((n_keys, KV_WIDTH), BF16), pltpu.VMEM((KV_WIDTH, n_keys), BF16),
                        pltpu.VMEM((KV_REP * HEAD_DIM, tq), F32)],
        compiler_params=_cparams(3),
        name=name,
    )(q_t, *([k_t] * nkb), *([v_t] * nkb))


def _fourier_kernel(c_ref, s_ref, x_ref, cc_ref, sc_ref, w_ref, o_ref, *, norm):
    x = x_ref[...]
    a = jnp.dot(c_ref[...], x, preferred_element_type=F32).astype(BF16)
    b = jnp.dot(s_ref[...], x, preferred_element_type=F32).astype(BF16)
    cc = cc_ref[...]
    sc = sc_ref[...]
    for grp in range(N_FOURIER_GROUPS):
        sl = slice(grp * FOURIER_GROUP, (grp + 1) * FOURIER_GROUP)
        spec = (jnp.dot(a[:, sl], cc, preferred_element_type=F32)
                - jnp.dot(b[:, sl], sc, preferred_element_type=F32)) * norm
        o_ref[:, sl] = jnp.dot(spec.astype(BF16), w_ref[grp], preferred_element_type=F32).astype(BF16)


def _fourier(f, cmat, smat, cc, sc, w_four, *, n, tr, x_block0, n_batch, name):
    n_row_tiles = n // tr
    return pl.pallas_call(
        functools.partial(_fourier_kernel, norm=float(1.0 / np.sqrt(n * FOURIER_GROUP))),
        grid=(n_row_tiles, n_batch),
        in_specs=[
            pl.BlockSpec((tr, n), lambda i, b: (i, 0)),
            pl.BlockSpec((tr, n), lambda i, b: (i, 0)),
            pl.BlockSpec((n, FOURIER_WIDTH), lambda i, b: (x_block0 + b, 0)),
            pl.BlockSpec(cc.shape, lambda i, b: (0, 0)),
            pl.BlockSpec(sc.shape, lambda i, b: (0, 0)),
            pl.BlockSpec(w_four.shape, lambda i, b: (0, 0, 0)),
        ],
        out_specs=pl.BlockSpec((tr, FOURIER_WIDTH), lambda i, b: (b * n_row_tiles + i, 0)),
        out_shape=jax.ShapeDtypeStruct((n_batch * n, FOURIER_WIDTH), BF16),
        compiler_params=_cparams(2),
        name=name,
    )(cmat, smat, f, cc, sc, w_four)


def _postmix_kernel(x_ref, ac_ref, al_ref, fc_ref, fl_ref, mod_ref, g_ref, woa_ref, wof_ref, *rest,
                    n_ctx_tiles, tiles_per_batch, with_router):
    if with_router:
        wr_ref, br_ref, x1_ref, h_ref, lg_ref = rest
    else:
        x1_ref, h_ref = rest
    i = pl.program_id(0)
    row = _mod_row(i, n_ctx_tiles, tiles_per_batch)
    is_ctx = i < n_ctx_tiles
    a = jnp.where(is_ctx, ac_ref[...], al_ref[...])
    f = jnp.where(is_ctx, fc_ref[...], fl_ref[...])
    mix = (jnp.dot(a, woa_ref[...], preferred_element_type=F32)
           + jnp.dot(f, wof_ref[...], preferred_element_type=F32))
    x1 = x_ref[...] + _mod_vec(mod_ref, row, 2) * mix
    x1_ref[...] = x1
    h = _norm_modulate(x1, g_ref[...], _mod_vec(mod_ref, row, 3), _mod_vec(mod_ref, row, 4))
    hb = h.astype(BF16)
    if with_router:
        h_ref[...] = _pack_bf16_pairs(hb)
        lg_ref[...] = jnp.dot(hb, wr_ref[...], preferred_element_type=F32) + br_ref[...]
    else:
        h_ref[...] = hb


def _postmix(x, attn_ctx, attn_lat, four_ctx, four_lat, mod_l, g_ffn, wo_a, wo_f, router, *, n_ctx, seq):
    t, d = x.shape
    tm = TM_MIX
    nct = n_ctx // tm
    full = lambda shape: pl.BlockSpec(shape, lambda i: (0,) * len(shape))
    tile = lambda w: pl.BlockSpec((tm, w), lambda i: (i, 0))
    ctx_tile = lambda w: pl.BlockSpec((tm, w), lambda i: (jnp.minimum(i, nct - 1), 0))
    lat_tile = lambda w: pl.BlockSpec((tm, w), lambda i: (jnp.maximum(i - nct, 0), 0))
    in_specs = [tile(d), ctx_tile(ATTN_WIDTH), lat_tile(ATTN_WIDTH), ctx_tile(FOURIER_WIDTH),
                lat_tile(FOURIER_WIDTH), full(mod_l.shape), full((1, d)), full(wo_a.shape), full(wo_f.shape)]
    if router is None:
        out_specs = [tile(d), tile(d)]
        out_shape = [jax.ShapeDtypeStruct((t, d), F32), jax.ShapeDtypeStruct((t, d), BF16)]
    else:
        out_specs = [tile(d), tile(d // 2)]
        out_shape = [jax.ShapeDtypeStruct((t, d), F32), jax.ShapeDtypeStruct((t, d // 2), jnp.uint32)]
    args = [x, attn_ctx, attn_lat, four_ctx, four_lat, mod_l, g_ffn, wo_a, wo_f]
    if router is not None:
        in_specs += [full(router[0].shape), full(router[1].shape)]
        out_specs.append(tile(LOGIT_LANES))
        out_shape.append(jax.ShapeDtypeStruct((t, LOGIT_LANES), F32))
        args += list(router)
    return pl.pallas_call(
        functools.partial(_postmix_kernel, n_ctx_tiles=n_ctx // tm, tiles_per_batch=seq // tm,
                          with_router=router is not None),
        grid=(t // tm,),
        in_specs=in_specs,
        out_specs=out_specs,
        out_shape=out_shape,
        compiler_params=_cparams(1),
        name="postmix_wo_norm",
    )(*args)


def _silu_mul(gate, up):
    return (gate / (1.0 + jnp.exp(-gate))) * up


def _swiglu_chunks(x, n_chunks, wg_of, wu_of, wd_of, acc_ref):
    gate = jnp.dot(x, wg_of(0), preferred_element_type=F32)
    up = jnp.dot(x, wu_of(0), preferred_element_type=F32)
    for c in range(n_chunks):
        act = _silu_mul(gate, up).astype(BF16)
        if c + 1 < n_chunks:
            gate = jnp.dot(x, wg_of(c + 1), preferred_element_type=F32)
            up = jnp.dot(x, wu_of(c + 1), preferred_element_type=F32)
        acc_ref[...] += jnp.dot(act, wd_of(c), preferred_element_type=F32)


def _dense_ffn_kernel(h_ref, x1_ref, mod_ref, wg_ref, wu_ref, wd_ref, o_ref, acc_ref, *,
                      n_ctx_tiles, tiles_per_batch):
    i = pl.program_id(0)
    row = _mod_row(i, n_ctx_tiles, tiles_per_batch)
    acc_ref[...] = jnp.zeros_like(acc_ref)
    _swiglu_chunks(h_ref[...], wg_ref.shape[0], lambda c: wg_ref[c], lambda c: wu_ref[c],
                   lambda c: wd_ref[c], acc_ref)
    o_ref[...] = x1_ref[...] + _mod_vec(mod_ref, row, 5) * acc_ref[...]


def _dense_ffn(h, x1, mod_l, wg, wu, wd, *, n_ctx, seq):
    t, d = x1.shape
    tm = TM_DENSE
    full = lambda shape: pl.BlockSpec(shape, lambda i: (0,) * len(shape))
    tile = lambda: pl.BlockSpec((tm, d), lambda i: (i, 0))
    return pl.pallas_call(
        functools.partial(_dense_ffn_kernel, n_ctx_tiles=n_ctx // tm, tiles_per_batch=seq // tm),
        grid=(t // tm,),
        in_specs=[tile(), tile(), full(mod_l.shape), full(wg.shape), full(wu.shape), full(wd.shape)],
        out_specs=tile(),
        out_shape=jax.ShapeDtypeStruct((t, d), F32),
        scratch_shapes=[pltpu.VMEM((tm, d), F32)],
        compiler_params=_cparams(1),
        name="dense_swiglu",
    )(h, x1, mod_l, wg, wu, wd)


def _moe_ffn_kernel(be_ref, nv_ref, x_ref, wg_ref, wu_ref, wd_ref, o_ref, xb_ref):
    i = pl.program_id(0)
    j = pl.program_id(1)
    n_valid = nv_ref[i]

    @pl.when(j == 0)
    def _():
        x = _unpack_bf16_pairs(x_ref[...])
        rows = lax.broadcasted_iota(jnp.int32, x.shape, 0)
        xb_ref[...] = jnp.where(rows < n_valid, x, jnp.zeros_like(x))
        o_ref[...] = jnp.zeros_like(o_ref)

    @pl.when(n_valid > 0)
    def _():
        sub = lambda c: slice(c * FF_CHUNK, (c + 1) * FF_CHUNK)
        _swiglu_chunks(xb_ref[...], TF_MOE // FF_CHUNK,
                       lambda c: wg_ref[0, 0, :, sub(c)].astype(BF16),
                       lambda c: wu_ref[0, 0, :, sub(c)].astype(BF16),
                       lambda c: wd_ref[0, 0, sub(c), :].astype(BF16), o_ref)


def _moe_ffn(xs, block_expert, n_valid, wg, wu, wd, layer):
    n_rows = xs.shape[0]
    d = D_MODEL
    tm, tf = TM_MOE, TF_MOE
    n_blocks = n_rows // tm
    n_ff = wg.shape[3] // tf

    def ff_idx(j, nv, i):
        return jnp.where(nv[i] > 0, j, n_ff - 1)

    grid_spec = pltpu.PrefetchScalarGridSpec(
        num_scalar_prefetch=2,
        grid=(n_blocks, n_ff),
        in_specs=[
            pl.BlockSpec((tm, d // 2), lambda i, j, be, nv: (i, 0)),
            pl.BlockSpec((1, 1, d, tf), lambda i, j, be, nv: (layer, be[i], 0, ff_idx(j, nv, i))),
            pl.BlockSpec((1, 1, d, tf), lambda i, j, be, nv: (layer, be[i], 0, ff_idx(j, nv, i))),
            pl.BlockSpec((1, 1, tf, d), lambda i, j, be, nv: (layer, be[i], ff_idx(j, nv, i), 0)),
        ],
        out_specs=pl.BlockSpec((tm, d), lambda i, j, be, nv: (i, 0)),
        scratch_shapes=[pltpu.VMEM((tm, d), BF16)],
    )
    return pl.pallas_call(
        _moe_ffn_kernel,
        grid_spec=grid_spec,
        out_shape=jax.ShapeDtypeStruct((n_rows, d), F32),
        compiler_params=_cparams(2),
        name="expert_swiglu",
    )(block_expert, n_valid, xs, wg, wu, wd)


def _sc_mesh():
    return plsc.VectorSubcoreMesh(core_axis_name="c", subcore_axis_name="s")


def _sc_params():
    return pltpu.CompilerParams(use_tc_tiling_on_sc=True)


def _sc_dispatch(h_packed, dest, n_rows):
    t, w = h_packed.shape
    win = SC_DISPATCH_ROWS
    n_win = t // win
    idx = [dest[:, k].reshape(n_win, 1, win) for k in range(TOP_K)]

    @functools.partial(
        pl.kernel, out_type=jax.ShapeDtypeStruct((n_rows, w), h_packed.dtype), mesh=_sc_mesh(),
        scratch_types=[], compiler_params=_sc_params(), name="expert_dispatch_scatter")
    def run(x_hbm, i0_hbm, i1_hbm, o_hbm):
        def body(x_vmem, i0_vmem, i1_vmem):
            pltpu.sync_copy(x_vmem, o_hbm.at[i0_vmem.at[0, 0]])
            pltpu.sync_copy(x_vmem, o_hbm.at[i1_vmem.at[0, 0]])

        idx_spec = pl.BlockSpec((1, 1, win), lambda i: (i, 0, 0))
        pltpu.emit_pipeline(
            body, grid=(n_win,),
            in_specs=[pl.BlockSpec((win, w), lambda i: (i, 0)), idx_spec, idx_spec],
            out_specs=[], core_axis_name=("c", "s"), dimension_semantics=(pltpu.PARALLEL,),
        )(x_hbm, i0_hbm, i1_hbm)

    return run(h_packed, *idx)


def _sc_gather(ys, idx_flat):
    w = ys.shape[1]
    n = idx_flat.shape[0]
    win = SC_GATHER_ROWS
    n_win = n // win

    @functools.partial(
        pl.kernel, out_type=jax.ShapeDtypeStruct((n, w), ys.dtype), mesh=_sc_mesh(),
        scratch_types=[], compiler_params=_sc_params(), name="expert_combine_gather")
    def run(y_hbm, i_hbm, o_hbm):
        def body(i_vmem, o_vmem):
            pltpu.sync_copy(y_hbm.at[i_vmem.at[0, 0]], o_vmem)

        pltpu.emit_pipeline(
            body, grid=(n_win,),
            in_specs=[pl.BlockSpec((1, 1, win), lambda i: (i, 0, 0))],
            out_specs=[pl.BlockSpec((win, w), lambda i: (i, 0))],
            core_axis_name=("c", "s"), dimension_semantics=(pltpu.PARALLEL,),
        )(i_hbm, o_hbm)

    return run(ys, idx_flat.reshape(n_win, 1, win))


def _moe_combine_kernel(x1_ref, y0_ref, y1_ref, g_ref, mod_ref, o_ref, *, n_ctx_tiles, tiles_per_batch):
    i = pl.program_id(0)
    row = _mod_row(i, n_ctx_tiles, tiles_per_batch)
    gates = g_ref[...]
    y = y0_ref[...] * gates[:, 0:1] + y1_ref[...] * gates[:, 1:2]
    o_ref[...] = x1_ref[...] + _mod_vec(mod_ref, row, 5) * y


def _moe_combine(x1, y_sel, gates, mod_l, *, n_ctx, seq):
    t, d = x1.shape
    tm = TM_MIX
    n_tiles = t // tm
    tile = lambda w: pl.BlockSpec((tm, w), lambda i: (i, 0))
    return pl.pallas_call(
        functools.partial(_moe_combine_kernel, n_ctx_tiles=n_ctx // tm, tiles_per_batch=seq // tm),
        grid=(n_tiles,),
        in_specs=[tile(d), tile(d), pl.BlockSpec((tm, d), lambda i: (n_tiles + i, 0)), tile(LOGIT_LANES),
                  pl.BlockSpec(mod_l.shape, lambda i: (0, 0))],
        out_specs=tile(d),
        out_shape=jax.ShapeDtypeStruct((t, d), F32),
        compiler_params=_cparams(1),
        name="expert_combine",
    )(x1, y_sel, y_sel, gates, mod_l)


def _route(logits):
    t = logits.shape[0]
    top_val, top_idx = lax.top_k(logits[:, :N_EXPERTS], TOP_K)
    gates = jax.nn.softmax(top_val, axis=-1)
    expert = top_idx.reshape(-1).astype(jnp.int32)
    n_assign = t * TOP_K
    onehot = (expert[:, None] == jnp.arange(N_EXPERTS, dtype=jnp.int32)[None, :]).astype(jnp.int32)
    csum = jnp.cumsum(onehot, axis=0)
    rank = jnp.sum((csum - onehot) * onehot, axis=1)
    counts = csum[-1]
    padded = ((counts + TM_MOE - 1) // TM_MOE) * TM_MOE
    pend = jnp.cumsum(padded)
    pstart = pend - padded
    dest = (pstart[expert] + rank).astype(jnp.int32)
    n_blocks = -(-n_assign // TM_MOE) + N_EXPERTS
    block_start = jnp.arange(n_blocks, dtype=jnp.int32) * TM_MOE
    block_expert = jnp.minimum(jnp.searchsorted(pend, block_start, side="right"),
                               N_EXPERTS - 1).astype(jnp.int32)
    n_valid = jnp.clip(counts[block_expert] - (block_start - pstart[block_expert]), 0, TM_MOE)
    n_valid = jnp.where(block_start < pend[-1], n_valid, 0).astype(jnp.int32)
    return gates, dest.reshape(t, TOP_K), block_expert, n_valid, n_blocks * TM_MOE


def kernel(x, c, ctx, c_ctx, w_mod, b_mod, g_mix, g_ffn, g_q, g_k, w_in, w_four, w_o,
           w_gate_dense, w_up_dense, w_down_dense, w_router, b_router,
           w_gate_moe, w_up_moe, w_down_moe):
    b, s, d = x.shape
    n_ctx_len = ctx.shape[1]
    n_ctx = b * n_ctx_len
    t = n_ctx + b * s
    assert d == D_MODEL and b + 1 <= MOD_ROWS
    assert n_ctx % TM_DENSE == 0 and s % TM_DENSE == 0 and n_ctx_len % 128 == 0 and n_ctx % s == 0

    cvec = jnp.zeros((MOD_ROWS, d), F32).at[0].set(c_ctx).at[1:b + 1].set(c)
    mod = _modulation(cvec, w_mod, b_mod)

    cos_t, sin_t = _rope_tables_t(s, TM_MIX)
    c_lat, s_lat = _dft_mats(s)
    c_ctx_m, s_ctx_m = _dft_mats(n_ctx_len)
    c_grp, s_grp = _dft_mats(FOURIER_GROUP)

    xa = jnp.concatenate([ctx.reshape(n_ctx, d), x.reshape(b * s, d)], axis=0)

    for l in range(DEPTH):
        mod_l = mod[l]
        w_in_l = w_in[l]
        wt = w_in_l[:, :QKV_WIDTH].T.astype(BF16)
        wf = w_in_l[:, QKV_WIDTH:].astype(BF16)
        gq = jnp.broadcast_to((g_q[l] * (HEAD_DIM ** -0.5))[:, None], (HEAD_DIM, TM_MIX))
        gk = jnp.broadcast_to(g_k[l][:, None], (HEAD_DIM, TM_MIX))
        q_t, k_t, v_t, f = _premix(xa, mod_l, g_mix[l].reshape(1, d), wt, wf, gq, gk, cos_t, sin_t,
                                   n_ctx=n_ctx, seq=s)

        n_qt = s // TQ_ATTN
        lat_keys = [(n_ctx_len, lambda bb: bb), (s, lambda bb: n_ctx // s + bb)]
        attn_lat = _attention(q_t, k_t, v_t, q_tile0=n_ctx // TQ_ATTN, n_q_tiles=n_qt,
                              key_blocks=lat_keys, tq=TQ_ATTN, n_batch=b, name="attention_latent")
        attn_ctx = _attention(q_t, k_t, v_t, q_tile0=0, n_q_tiles=1,
                              key_blocks=[(n_ctx_len, lambda bb: bb)], tq=n_ctx_len, n_batch=b,
                              name="attention_context")

        wfour = w_four[l].astype(BF16)
        four_lat = _fourier(f, c_lat, s_lat, c_grp, s_grp, wfour, n=s, tr=TR_FOUR,
                            x_block0=n_ctx // s, n_batch=b, name="fourier_latent")
        four_ctx = _fourier(f, c_ctx_m, s_ctx_m, c_grp, s_grp, wfour, n=n_ctx_len, tr=n_ctx_len,
                            x_block0=0, n_batch=b, name="fourier_context")

        wo = w_o[l].astype(BF16)
        is_moe = l % 2 == 1
        li = l // 2
        router = None
        if is_moe:
            wr = jnp.zeros((d, LOGIT_LANES), BF16).at[:, :N_EXPERTS].set(w_router[li].astype(BF16))
            br = jnp.zeros((1, LOGIT_LANES), F32).at[0, :N_EXPERTS].set(b_router[li])
            router = (wr, br)
        res = _postmix(xa, attn_ctx, attn_lat, four_ctx, four_lat, mod_l, g_ffn[l].reshape(1, d),
                       wo[:ATTN_WIDTH], wo[ATTN_WIDTH:], router, n_ctx=n_ctx, seq=s)
        if not is_moe:
            x1, h2 = res
            xa = _dense_ffn(h2, x1, mod_l, _chunk_cols(w_gate_dense[li], FF_CHUNK),
                            _chunk_cols(w_up_dense[li], FF_CHUNK), _chunk_rows(w_down_dense[li], FF_CHUNK),
                            n_ctx=n_ctx, seq=s)
        else:
            x1, h2, logits = res
            gates, dest, block_expert, n_valid, n_rows = _route(logits)
            xs = _sc_dispatch(h2, dest, n_rows)
            ys = _moe_ffn(xs, block_expert, n_valid, w_gate_moe, w_up_moe, w_down_moe, li)
            y_sel = _sc_gather(ys, dest.T.reshape(-1))
            gates_p = jnp.zeros((t, LOGIT_LANES), F32).at[:, :TOP_K].set(gates)
            xa = _moe_combine(x1, y_sel, gates_p, mod_l, n_ctx=n_ctx, seq=s)

    return xa[n_ctx:].reshape(b, s, d)
```

```python
import functools

import numpy as np
import jax
import jax.numpy as jnp
from jax import lax
from jax.experimental import pallas as pl
from jax.experimental.pallas import tpu as pltpu
from jax.experimental.pallas import tpu_sc as plsc

D_MODEL = 1024
DEPTH = 4
GRID_W = 64
HEAD_DIM = 64
ATTN_WIDTH = 512
N_Q_HEADS = 8
N_KV_HEADS = 2
KV_REP = 4
KV_WIDTH = 128
FOURIER_WIDTH = 512
N_FOURIER_GROUPS = 4
FOURIER_GROUP = 128
ROT_PER_AXIS = 32
ROPE_THETA = 10000.0
N_MOD = 6
D_FF_DENSE = 2816
N_EXPERTS = 8
TOP_K = 2
D_FF_EXPERT = 3584
EPS = 1e-6

QKV_WIDTH = ATTN_WIDTH + 2 * KV_WIDTH
MOD_ROWS = 16
LOGIT_LANES = 128

TM_MIX = 512
TQ_ATTN = 512
KEY_CHUNK = 256
ONES_ROWS = 16
SCORE_LOOKAHEAD = 2
LOG2_E = 1.4426950408889634
TR_FOUR = 512
TM_DENSE = 1024
FF_CHUNK = 256
TM_MOE = 1024
TF_MOE = 512
SC_DISPATCH_ROWS = 64
SC_GATHER_ROWS = 32
VMEM_LIMIT = 56 * 1024 * 1024

F32 = jnp.float32
BF16 = jnp.bfloat16


def _cparams(n_axes, flags=None):
    return pltpu.CompilerParams(
        dimension_semantics=("arbitrary",) * n_axes, vmem_limit_bytes=VMEM_LIMIT, flags=flags)


def _rope_tables_t(s, tm):
    n_rows = s // GRID_W
    rows = np.repeat(np.arange(n_rows), GRID_W).astype(np.float64)
    cols = np.tile(np.arange(GRID_W), n_rows).astype(np.float64)
    inv_freq = (ROPE_THETA ** (-np.arange(0, ROT_PER_AXIS, 2, dtype=np.float32) / ROT_PER_AXIS)
                ).astype(np.float32).astype(np.float64)
    ang_r = (rows[None, :].astype(np.float32) * inv_freq[:, None].astype(np.float32)).astype(np.float64)
    ang_c = (cols[None, :].astype(np.float32) * inv_freq[:, None].astype(np.float32)).astype(np.float64)
    cos = np.concatenate([np.cos(ang_r), np.cos(ang_r), np.cos(ang_c), np.cos(ang_c)], axis=0)
    sin = np.concatenate([-np.sin(ang_r), np.sin(ang_r), -np.sin(ang_c), np.sin(ang_c)], axis=0)
    cos = np.concatenate([np.ones((HEAD_DIM, tm)), cos], axis=1)
    sin = np.concatenate([np.zeros((HEAD_DIM, tm)), sin], axis=1)
    return jnp.asarray(cos, F32), jnp.asarray(sin, F32)


def _dft_mats(n):
    k = np.arange(n, dtype=np.int64)
    ang = 2.0 * np.pi * ((k[:, None] * k[None, :]) % n).astype(np.float64) / n
    return jnp.asarray(np.cos(ang), BF16), jnp.asarray(np.sin(ang), BF16)


def _chunk_cols(w, chunk):
    *lead, d, n = w.shape
    w = w.astype(BF16).reshape(*lead, d, n // chunk, chunk)
    return jnp.swapaxes(w, -3, -2)


def _chunk_rows(w, chunk):
    *lead, n, d = w.shape
    return w.astype(BF16).reshape(*lead, n // chunk, chunk, d)


def _mod_kernel(c_ref, w_ref, b_ref, o_ref):
    c = c_ref[...]
    s = (c / (1.0 + jnp.exp(-c))).astype(BF16)
    w = w_ref[0].astype(BF16)
    o_ref[0] = jnp.dot(s, w, preferred_element_type=F32) + b_ref[0]


def _modulation(cvec, w_mod, b_mod):
    depth, d, n = w_mod.shape
    tn = 1536
    return pl.pallas_call(
        _mod_kernel,
        grid=(depth, n // tn),
        in_specs=[
            pl.BlockSpec((MOD_ROWS, d), lambda l, j: (0, 0)),
            pl.BlockSpec((1, d, tn), lambda l, j: (l, 0, j)),
            pl.BlockSpec((1, 1, tn), lambda l, j: (l, 0, j)),
        ],
        out_specs=pl.BlockSpec((1, MOD_ROWS, tn), lambda l, j: (l, 0, j)),
        out_shape=jax.ShapeDtypeStruct((depth, MOD_ROWS, n), F32),
        compiler_params=_cparams(2),
        name="adaln_vectors",
    )(cvec, w_mod, b_mod.reshape(depth, 1, n))


def _mod_row(i, n_ctx_tiles, tiles_per_batch):
    lat = jnp.maximum(i - n_ctx_tiles, 0)
    return jnp.where(i < n_ctx_tiles, 0, lat // tiles_per_batch + 1)


def _mod_vec(mod_ref, row, comp):
    return mod_ref[pl.ds(row, 1), comp * D_MODEL:(comp + 1) * D_MODEL]


def _pack_bf16_pairs(xb):
    n = xb.shape[1] // 2
    bits = lax.bitcast_convert_type(xb.astype(F32), jnp.uint32)
    return (bits[:, :n] >> 16) | (bits[:, n:] & jnp.uint32(0xFFFF0000))


def _unpack_bf16_pairs(w):
    lo = lax.bitcast_convert_type(w << 16, F32)
    hi = lax.bitcast_convert_type(w & jnp.uint32(0xFFFF0000), F32)
    return jnp.concatenate([lo, hi], axis=1).astype(BF16)


def _norm_modulate(x, g, shift, scale):
    ms = jnp.mean(x * x, axis=-1, keepdims=True)
    y = x * lax.rsqrt(ms + EPS) * g
    return y * (1.0 + scale) + shift


def _premix_kernel(x_ref, mod_ref, g_ref, wt_ref, wf_ref, gq_ref, gk_ref, cos_ref, sin_ref,
                   q_ref, k_ref, v_ref, f_ref, *, n_ctx_tiles, tiles_per_batch):
    i = pl.program_id(0)
    row = _mod_row(i, n_ctx_tiles, tiles_per_batch)
    h = _norm_modulate(x_ref[...], g_ref[...], _mod_vec(mod_ref, row, 0), _mod_vec(mod_ref, row, 1))
    hb = h.astype(BF16)
    f_ref[...] = jnp.dot(hb, wf_ref[...], preferred_element_type=F32).astype(BF16)
    pt = lax.dot_general(wt_ref[...], hb, (((1,), (1,)), ((), ())), preferred_element_type=F32)
    v_ref[...] = pt[ATTN_WIDTH + KV_WIDTH:, :].astype(BF16)
    cos = cos_ref[...]
    sin = sin_ref[...]

    def norm_rope(xh, gain):
        ms = jnp.mean(xh * xh, axis=0, keepdims=True)
        y = xh * lax.rsqrt(ms + EPS) * gain
        half = ROT_PER_AXIS // 2
        swapped = jnp.concatenate(
            [y[half:2 * half], y[0:half], y[3 * half:4 * half], y[2 * half:3 * half]], axis=0)
        return y * cos + swapped * sin

    gq = gq_ref[...]
    gk = gk_ref[...]
    for hh in range(N_Q_HEADS):
        q_ref[hh * HEAD_DIM:(hh + 1) * HEAD_DIM, :] = norm_rope(
            pt[hh * HEAD_DIM:(hh + 1) * HEAD_DIM, :], gq).astype(BF16)
    for hh in range(N_KV_HEADS):
        lo = ATTN_WIDTH + hh * HEAD_DIM
        k_ref[hh * HEAD_DIM:(hh + 1) * HEAD_DIM, :] = norm_rope(pt[lo:lo + HEAD_DIM, :], gk).astype(BF16)


def _premix(x, mod_l, g_mix, wt, wf, gq, gk, cos_t, sin_t, *, n_ctx, seq):
    t, d = x.shape
    tm = TM_MIX
    n_ctx_tiles = n_ctx // tm
    tpb = seq // tm

    def tab_idx(i):
        lat = jnp.maximum(i - n_ctx_tiles, 0)
        return (0, jnp.where(i < n_ctx_tiles, 0, lax.rem(lat, tpb) + 1))

    full = lambda shape: pl.BlockSpec(shape, lambda i: (0,) * len(shape))
    return pl.pallas_call(
        functools.partial(_premix_kernel, n_ctx_tiles=n_ctx_tiles, tiles_per_batch=tpb),
        grid=(t // tm,),
        in_specs=[
            pl.BlockSpec((tm, d), lambda i: (i, 0)),
            full(mod_l.shape),
            full((1, d)),
            full(wt.shape),
            full(wf.shape),
            full(gq.shape),
            full(gk.shape),
            pl.BlockSpec((HEAD_DIM, tm), tab_idx),
            pl.BlockSpec((HEAD_DIM, tm), tab_idx),
        ],
        out_specs=[
            pl.BlockSpec((ATTN_WIDTH, tm), lambda i: (0, i)),
            pl.BlockSpec((KV_WIDTH, tm), lambda i: (0, i)),
            pl.BlockSpec((KV_WIDTH, tm), lambda i: (0, i)),
            pl.BlockSpec((tm, FOURIER_WIDTH), lambda i: (i, 0)),
        ],
        out_shape=[
            jax.ShapeDtypeStruct((ATTN_WIDTH, t), BF16),
            jax.ShapeDtypeStruct((KV_WIDTH, t), BF16),
            jax.ShapeDtypeStruct((KV_WIDTH, t), BF16),
            jax.ShapeDtypeStruct((t, FOURIER_WIDTH), BF16),
        ],
        compiler_params=_cparams(1),
        name="premix_project",
    )(x, mod_l, g_mix, wt, wf, gq, gk, cos_t, sin_t)


def _attn_kernel(*refs, n_key_blocks, tq):
    q_ref = refs[0]
    k_refs = refs[1:1 + n_key_blocks]
    v_refs = refs[1 + n_key_blocks:1 + 2 * n_key_blocks]
    o_ref = refs[1 + 2 * n_key_blocks]
    kall_ref, vg_ref, ot_ref = refs[2 + 2 * n_key_blocks:]
    g = pl.program_id(1)
    qt = pl.program_id(2)
    n_keys = kall_ref.shape[0]
    n_chunks = n_keys // KEY_CHUNK
    slabs = KEY_CHUNK // 8

    @pl.when(jnp.logical_and(g == 0, qt == 0))
    def _():
        off = 0
        for kr in k_refs:
            n = kr.shape[1]
            kall_ref[off:off + n, :] = kr[...].astype(F32).T.astype(BF16)
            off += n

    @pl.when(qt == 0)
    def _():
        g_rows = pl.ds(pl.multiple_of(g * HEAD_DIM, HEAD_DIM), HEAD_DIM)
        off = 0
        for vr in v_refs:
            n = vr.shape[1]
            vg_ref[0:HEAD_DIM, off:off + n] = vr[g_rows, :]
            off += n
        vg_ref[HEAD_DIM:, :] = jnp.ones((ONES_ROWS, n_keys), BF16)

    row_group = lax.broadcasted_iota(jnp.int32, (KV_WIDTH, tq), 0) // HEAD_DIM

    def masked_q(hh):
        qh = q_ref[hh * HEAD_DIM:(hh + 1) * HEAD_DIM, :]
        q2 = jnp.concatenate([qh, qh], axis=0)
        return jnp.where(row_group == g, q2, jnp.zeros_like(q2))

    q2s = [masked_q(hh) for hh in range(KV_REP)]
    items = [(hh, c) for hh in range(KV_REP) for c in range(n_chunks)]

    def score(item):
        hh, c = item
        return jnp.dot(kall_ref[c * KEY_CHUNK:(c + 1) * KEY_CHUNK, :], q2s[hh], preferred_element_type=F32)

    pending = [score(it) for it in items[:SCORE_LOOKAHEAD]]
    m = ot = None
    for i, (hh, c) in enumerate(items):
        if c == 0:
            m = jnp.full((1, tq), -jnp.inf, F32)
            ot = jnp.zeros((HEAD_DIM + ONES_ROWS, tq), F32)
        s = pending.pop(0).reshape(slabs, 8, tq)
        if i + SCORE_LOOKAHEAD < len(items):
            pending.append(score(items[i + SCORE_LOOKAHEAD]))
        m_new = jnp.maximum(m, jnp.max(jnp.max(s, axis=0), axis=0, keepdims=True))
        pb = jnp.exp2(s - m_new[None]).reshape(KEY_CHUNK, tq).astype(BF16)
        rows = slice(c * KEY_CHUNK, (c + 1) * KEY_CHUNK)
        ot = jnp.exp2(m - m_new) * ot + jnp.dot(vg_ref[:, rows], pb, preferred_element_type=F32)
        m = m_new
        if c == n_chunks - 1:
            ot_ref[hh * HEAD_DIM:(hh + 1) * HEAD_DIM, :] = ot[:HEAD_DIM] / ot[HEAD_DIM:HEAD_DIM + 1]
    o_ref[...] = ot_ref[...].T.astype(BF16)


def _attention(q_t, k_t, v_t, *, q_tile0, n_q_tiles, key_blocks, tq, n_batch, name):
    n_keys = sum(c for c, _ in key_blocks)
    nkb = len(key_blocks)
    q_spec = pl.BlockSpec((KV_REP * HEAD_DIM, tq),
                          lambda b, g, i: (g, q_tile0 + b * n_q_tiles + i))
    k_specs = [pl.BlockSpec((KV_WIDTH, c), (lambda f: (lambda b, g, i: (0, f(b))))(f))
               for c, f in key_blocks]
    return pl.pallas_call(
        functools.partial(_attn_kernel, n_key_blocks=nkb, tq=tq),
        grid=(n_batch, N_KV_HEADS, n_q_tiles),
        in_specs=[q_spec] + k_specs + k_specs,
        out_specs=pl.BlockSpec((tq, KV_REP * HEAD_DIM),
                               lambda b, g, i: (b * n_q_tiles + i, g)),
        out_shape=jax.ShapeDtypeStruct((n_batch * n_q_tiles * tq, ATTN_WIDTH), BF16),
        scratch_shapes=[pltpu.VMEM((n_keys, KV_WIDTH), BF16),
                        pltpu.VMEM((HEAD_DIM + ONES_ROWS, n_keys), BF16),
                        pltpu.VMEM((KV_REP * HEAD_DIM, tq), F32)],
        compiler_params=_cparams(3),
        name=name,
    )(q_t, *([k_t] * nkb), *([v_t] * nkb))


def _fourier_kernel(c_ref, s_ref, x_ref, cc_ref, sc_ref, w_ref, o_ref, *, norm):
    x = x_ref[...]
    a = jnp.dot(c_ref[...], x, preferred_element_type=F32).astype(BF16)
    b = jnp.dot(s_ref[...], x, preferred_element_type=F32).astype(BF16)
    cc = cc_ref[...]
    sc = sc_ref[...]
    groups = [slice(grp * FOURIER_GROUP, (grp + 1) * FOURIER_GROUP) for grp in range(N_FOURIER_GROUPS)]
    specs = [(jnp.dot(a[:, sl], cc, preferred_element_type=F32)
              - jnp.dot(b[:, sl], sc, preferred_element_type=F32)) * norm for sl in groups]
    for grp, sl in enumerate(groups):
        o_ref[:, sl] = jnp.dot(specs[grp].astype(BF16), w_ref[grp], preferred_element_type=F32).astype(BF16)


def _fourier(f, cmat, smat, cc, sc, w_four, *, n, tr, x_block0, n_batch, name):
    n_row_tiles = n // tr
    return pl.pallas_call(
        functools.partial(_fourier_kernel, norm=float(1.0 / np.sqrt(n * FOURIER_GROUP))),
        grid=(n_row_tiles, n_batch),
        in_specs=[
            pl.BlockSpec((tr, n), lambda i, b: (i, 0)),
            pl.BlockSpec((tr, n), lambda i, b: (i, 0)),
            pl.BlockSpec((n, FOURIER_WIDTH), lambda i, b: (x_block0 + b, 0)),
            pl.BlockSpec(cc.shape, lambda i, b: (0, 0)),
            pl.BlockSpec(sc.shape, lambda i, b: (0, 0)),
            pl.BlockSpec(w_four.shape, lambda i, b: (0, 0, 0)),
        ],
        out_specs=pl.BlockSpec((tr, FOURIER_WIDTH), lambda i, b: (b * n_row_tiles + i, 0)),
        out_shape=jax.ShapeDtypeStruct((n_batch * n, FOURIER_WIDTH), BF16),
        compiler_params=_cparams(2),
        name=name,
    )(cmat, smat, f, cc, sc, w_four)


def _postmix_kernel(x_ref, ac_ref, al_ref, fc_ref, fl_ref, mod_ref, g_ref, woa_ref, wof_ref, *rest,
                    n_ctx_tiles, tiles_per_batch, with_router):
    if with_router:
        wr_ref, br_ref, x1_ref, h_ref, lg_ref = rest
    else:
        x1_ref, h_ref = rest
    i = pl.program_id(0)
    row = _mod_row(i, n_ctx_tiles, tiles_per_batch)
    is_ctx = i < n_ctx_tiles
    a = jnp.where(is_ctx, ac_ref[...], al_ref[...])
    f = jnp.where(is_ctx, fc_ref[...], fl_ref[...])
    mix = (jnp.dot(a, woa_ref[...], preferred_element_type=F32)
           + jnp.dot(f, wof_ref[...], preferred_element_type=F32))
    x1 = x_ref[...] + _mod_vec(mod_ref, row, 2) * mix
    x1_ref[...] = x1
    h = _norm_modulate(x1, g_ref[...], _mod_vec(mod_ref, row, 3), _mod_vec(mod_ref, row, 4))
    hb = h.astype(BF16)
    if with_router:
        h_ref[...] = _pack_bf16_pairs(hb)
        lg_ref[...] = jnp.dot(hb, wr_ref[...], preferred_element_type=F32) + br_ref[...]
    else:
        h_ref[...] = hb


def _postmix(x, attn_ctx, attn_lat, four_ctx, four_lat, mod_l, g_ffn, wo_a, wo_f, router, *, n_ctx, seq):
    t, d = x.shape
    tm = TM_MIX
    nct = n_ctx // tm
    full = lambda shape: pl.BlockSpec(shape, lambda i: (0,) * len(shape))
    tile = lambda w: pl.BlockSpec((tm, w), lambda i: (i, 0))
    ctx_tile = lambda w: pl.BlockSpec((tm, w), lambda i: (jnp.minimum(i, nct - 1), 0))
    lat_tile = lambda w: pl.BlockSpec((tm, w), lambda i: (jnp.maximum(i - nct, 0), 0))
    in_specs = [tile(d), ctx_tile(ATTN_WIDTH), lat_tile(ATTN_WIDTH), ctx_tile(FOURIER_WIDTH),
                lat_tile(FOURIER_WIDTH), full(mod_l.shape), full((1, d)), full(wo_a.shape), full(wo_f.shape)]
    if router is None:
        out_specs = [tile(d), tile(d)]
        out_shape = [jax.ShapeDtypeStruct((t, d), F32), jax.ShapeDtypeStruct((t, d), BF16)]
    else:
        out_specs = [tile(d), tile(d // 2)]
        out_shape = [jax.ShapeDtypeStruct((t, d), F32), jax.ShapeDtypeStruct((t, d // 2), jnp.uint32)]
    args = [x, attn_ctx, attn_lat, four_ctx, four_lat, mod_l, g_ffn, wo_a, wo_f]
    if router is not None:
        in_specs += [full(router[0].shape), full(router[1].shape)]
        out_specs.append(tile(LOGIT_LANES))
        out_shape.append(jax.ShapeDtypeStruct((t, LOGIT_LANES), F32))
        args += list(router)
    return pl.pallas_call(
        functools.partial(_postmix_kernel, n_ctx_tiles=n_ctx // tm, tiles_per_batch=seq // tm,
                          with_router=router is not None),
        grid=(t // tm,),
        in_specs=in_specs,
        out_specs=out_specs,
        out_shape=out_shape,
        compiler_params=_cparams(1),
        name="postmix_wo_norm",
    )(*args)


def _silu_mul(gate, up):
    return (gate / (1.0 + jnp.exp(-gate))) * up


def _swiglu_chunks(x, n_chunks, wg_of, wu_of, wd_of, acc_ref):
    gate = jnp.dot(x, wg_of(0), preferred_element_type=F32)
    up = jnp.dot(x, wu_of(0), preferred_element_type=F32)
    for c in range(n_chunks):
        act = _silu_mul(gate, up).astype(BF16)
        if c + 1 < n_chunks:
            gate = jnp.dot(x, wg_of(c + 1), preferred_element_type=F32)
            up = jnp.dot(x, wu_of(c + 1), preferred_element_type=F32)
        acc_ref[...] += jnp.dot(act, wd_of(c), preferred_element_type=F32)


def _dense_ffn_kernel(h_ref, x1_ref, mod_ref, wg_ref, wu_ref, wd_ref, o_ref, acc_ref, *,
                      n_ctx_tiles, tiles_per_batch):
    i = pl.program_id(0)
    row = _mod_row(i, n_ctx_tiles, tiles_per_batch)
    acc_ref[...] = jnp.zeros_like(acc_ref)
    _swiglu_chunks(h_ref[...], wg_ref.shape[0], lambda c: wg_ref[c], lambda c: wu_ref[c],
                   lambda c: wd_ref[c], acc_ref)
    o_ref[...] = x1_ref[...] + _mod_vec(mod_ref, row, 5) * acc_ref[...]


def _dense_ffn(h, x1, mod_l, wg, wu, wd, *, n_ctx, seq):
    t, d = x1.shape
    tm = TM_DENSE
    full = lambda shape: pl.BlockSpec(shape, lambda i: (0,) * len(shape))
    tile = lambda: pl.BlockSpec((tm, d), lambda i: (i, 0))
    return pl.pallas_call(
        functools.partial(_dense_ffn_kernel, n_ctx_tiles=n_ctx // tm, tiles_per_batch=seq // tm),
        grid=(t // tm,),
        in_specs=[tile(), tile(), full(mod_l.shape), full(wg.shape), full(wu.shape), full(wd.shape)],
        out_specs=tile(),
        out_shape=jax.ShapeDtypeStruct((t, d), F32),
        scratch_shapes=[pltpu.VMEM((tm, d), F32)],
        compiler_params=_cparams(1),
        name="dense_swiglu",
    )(h, x1, mod_l, wg, wu, wd)


def _moe_ffn_kernel(be_ref, nv_ref, x_ref, wg_ref, wu_ref, wd_ref, o_ref, xb_ref):
    i = pl.program_id(0)
    j = pl.program_id(1)
    n_valid = nv_ref[i]

    @pl.when(j == 0)
    def _():
        x = _unpack_bf16_pairs(x_ref[...])
        rows = lax.broadcasted_iota(jnp.int32, x.shape, 0)
        xb_ref[...] = jnp.where(rows < n_valid, x, jnp.zeros_like(x))
        o_ref[...] = jnp.zeros_like(o_ref)

    @pl.when(n_valid > 0)
    def _():
        sub = lambda c: slice(c * FF_CHUNK, (c + 1) * FF_CHUNK)
        _swiglu_chunks(xb_ref[...], TF_MOE // FF_CHUNK,
                       lambda c: wg_ref[0, 0, :, sub(c)].astype(BF16),
                       lambda c: wu_ref[0, 0, :, sub(c)].astype(BF16),
                       lambda c: wd_ref[0, 0, sub(c), :].astype(BF16), o_ref)


def _moe_ffn(xs, block_expert, n_valid, wg, wu, wd, layer):
    n_rows = xs.shape[0]
    d = D_MODEL
    tm, tf = TM_MOE, TF_MOE
    n_blocks = n_rows // tm
    n_ff = wg.shape[3] // tf

    def ff_idx(j, nv, i):
        return jnp.where(nv[i] > 0, j, n_ff - 1)

    grid_spec = pltpu.PrefetchScalarGridSpec(
        num_scalar_prefetch=2,
        grid=(n_blocks, n_ff),
        in_specs=[
            pl.BlockSpec((tm, d // 2), lambda i, j, be, nv: (i, 0)),
            pl.BlockSpec((1, 1, d, tf), lambda i, j, be, nv: (layer, be[i], 0, ff_idx(j, nv, i))),
            pl.BlockSpec((1, 1, d, tf), lambda i, j, be, nv: (layer, be[i], 0, ff_idx(j, nv, i))),
            pl.BlockSpec((1, 1, tf, d), lambda i, j, be, nv: (layer, be[i], ff_idx(j, nv, i), 0)),
        ],
        out_specs=pl.BlockSpec((tm, d), lambda i, j, be, nv: (i, 0)),
        scratch_shapes=[pltpu.VMEM((tm, d), BF16)],
    )
    return pl.pallas_call(
        _moe_ffn_kernel,
        grid_spec=grid_spec,
        out_shape=jax.ShapeDtypeStruct((n_rows, d), F32),
        compiler_params=_cparams(2),
        name="expert_swiglu",
    )(block_expert, n_valid, xs, wg, wu, wd)


def _sc_mesh():
    return plsc.VectorSubcoreMesh(core_axis_name="c", subcore_axis_name="s")


def _sc_params():
    return pltpu.CompilerParams(use_tc_tiling_on_sc=True)


def _sc_dispatch(h_packed, dest, n_rows):
    t, w = h_packed.shape
    win = SC_DISPATCH_ROWS
    n_win = t // win
    idx = [dest[:, k].reshape(n_win, 1, win) for k in range(TOP_K)]

    @functools.partial(
        pl.kernel, out_type=jax.ShapeDtypeStruct((n_rows, w), h_packed.dtype), mesh=_sc_mesh(),
        scratch_types=[], compiler_params=_sc_params(), name="expert_dispatch_scatter")
    def run(x_hbm, i0_hbm, i1_hbm, o_hbm):
        def body(x_vmem, i0_vmem, i1_vmem):
            pltpu.sync_copy(x_vmem, o_hbm.at[i0_vmem.at[0, 0]])
            pltpu.sync_copy(x_vmem, o_hbm.at[i1_vmem.at[0, 0]])

        idx_spec = pl.BlockSpec((1, 1, win), lambda i: (i, 0, 0))
        pltpu.emit_pipeline(
            body, grid=(n_win,),
            in_specs=[pl.BlockSpec((win, w), lambda i: (i, 0)), idx_spec, idx_spec],
            out_specs=[], core_axis_name=("c", "s"), dimension_semantics=(pltpu.PARALLEL,),
        )(x_hbm, i0_hbm, i1_hbm)

    return run(h_packed, *idx)


def _sc_gather(ys, idx_flat):
    w = ys.shape[1]
    n = idx_flat.shape[0]
    win = SC_GATHER_ROWS
    n_win = n // win

    @functools.partial(
        pl.kernel, out_type=jax.ShapeDtypeStruct((n, w), ys.dtype), mesh=_sc_mesh(),
        scratch_types=[], compiler_params=_sc_params(), name="expert_combine_gather")
    def run(y_hbm, i_hbm, o_hbm):
        def body(i_vmem, o_vmem):
            pltpu.sync_copy(y_hbm.at[i_vmem.at[0, 0]], o_vmem)

        pltpu.emit_pipeline(
            body, grid=(n_win,),
            in_specs=[pl.BlockSpec((1, 1, win), lambda i: (i, 0, 0))],
            out_specs=[pl.BlockSpec((win, w), lambda i: (i, 0))],
            core_axis_name=("c", "s"), dimension_semantics=(pltpu.PARALLEL,),
        )(i_hbm, o_hbm)

    return run(ys, idx_flat.reshape(n_win, 1, win))


def _moe_combine_kernel(x1_ref, y0_ref, y1_ref, g_ref, mod_ref, o_ref, *, n_ctx_tiles, tiles_per_batch):
    i = pl.program_id(0)
    row = _mod_row(i, n_ctx_tiles, tiles_per_batch)
    gates = g_ref[...]
    y = y0_ref[...] * gates[:, 0:1] + y1_ref[...] * gates[:, 1:2]
    o_ref[...] = x1_ref[...] + _mod_vec(mod_ref, row, 5) * y


def _moe_combine(x1, y_sel, gates, mod_l, *, n_ctx, seq):
    t, d = x1.shape
    tm = TM_MIX
    n_tiles = t // tm
    tile = lambda w: pl.BlockSpec((tm, w), lambda i: (i, 0))
    return pl.pallas_call(
        functools.partial(_moe_combine_kernel, n_ctx_tiles=n_ctx // tm, tiles_per_batch=seq // tm),
        grid=(n_tiles,),
        in_specs=[tile(d), tile(d), pl.BlockSpec((tm, d), lambda i: (n_tiles + i, 0)), tile(LOGIT_LANES),
                  pl.BlockSpec(mod_l.shape, lambda i: (0, 0))],
        out_specs=tile(d),
        out_shape=jax.ShapeDtypeStruct((t, d), F32),
        compiler_params=_cparams(1),
        name="expert_combine",
    )(x1, y_sel, y_sel, gates, mod_l)


def _route(logits):
    t = logits.shape[0]
    top_val, top_idx = lax.top_k(logits[:, :N_EXPERTS], TOP_K)
    gates = jax.nn.softmax(top_val, axis=-1)
    expert = top_idx.reshape(-1).astype(jnp.int32)
    n_assign = t * TOP_K
    onehot = (expert[:, None] == jnp.arange(N_EXPERTS, dtype=jnp.int32)[None, :]).astype(jnp.int32)
    csum = jnp.cumsum(onehot, axis=0)
    rank = jnp.sum((csum - onehot) * onehot, axis=1)
    counts = csum[-1]
    padded = ((counts + TM_MOE - 1) // TM_MOE) * TM_MOE
    pend = jnp.cumsum(padded)
    pstart = pend - padded
    dest = (pstart[expert] + rank).astype(jnp.int32)
    n_blocks = -(-n_assign // TM_MOE) + N_EXPERTS
    block_start = jnp.arange(n_blocks, dtype=jnp.int32) * TM_MOE
    block_expert = jnp.minimum(jnp.searchsorted(pend, block_start, side="right"),
                               N_EXPERTS - 1).astype(jnp.int32)
    n_valid = jnp.clip(counts[block_expert] - (block_start - pstart[block_expert]), 0, TM_MOE)
    n_valid = jnp.where(block_start < pend[-1], n_valid, 0).astype(jnp.int32)
    return gates, dest.reshape(t, TOP_K), block_expert, n_valid, n_blocks * TM_MOE


def kernel(x, c, ctx, c_ctx, w_mod, b_mod, g_mix, g_ffn, g_q, g_k, w_in, w_four, w_o,
           w_gate_dense, w_up_dense, w_down_dense, w_router, b_router,
           w_gate_moe, w_up_moe, w_down_moe):
    b, s, d = x.shape
    n_ctx_len = ctx.shape[1]
    n_ctx = b * n_ctx_len
    t = n_ctx + b * s
    assert d == D_MODEL and b + 1 <= MOD_ROWS
    assert n_ctx % TM_DENSE == 0 and s % TM_DENSE == 0 and n_ctx_len % 128 == 0 and n_ctx % s == 0

    cvec = jnp.zeros((MOD_ROWS, d), F32).at[0].set(c_ctx).at[1:b + 1].set(c)
    mod = _modulation(cvec, w_mod, b_mod)

    cos_t, sin_t = _rope_tables_t(s, TM_MIX)
    c_lat, s_lat = _dft_mats(s)
    c_ctx_m, s_ctx_m = _dft_mats(n_ctx_len)
    c_grp, s_grp = _dft_mats(FOURIER_GROUP)

    xa = jnp.concatenate([ctx.reshape(n_ctx, d), x.reshape(b * s, d)], axis=0)

    for l in range(DEPTH):
        mod_l = mod[l]
        w_in_l = w_in[l]
        wt = w_in_l[:, :QKV_WIDTH].T.astype(BF16)
        wf = w_in_l[:, QKV_WIDTH:].astype(BF16)
        gq = jnp.broadcast_to((g_q[l] * (LOG2_E * HEAD_DIM ** -0.5))[:, None], (HEAD_DIM, TM_MIX))
        gk = jnp.broadcast_to(g_k[l][:, None], (HEAD_DIM, TM_MIX))
        q_t, k_t, v_t, f = _premix(xa, mod_l, g_mix[l].reshape(1, d), wt, wf, gq, gk, cos_t, sin_t,
                                   n_ctx=n_ctx, seq=s)

        n_qt = s // TQ_ATTN
        lat_keys = [(n_ctx_len, lambda bb: bb), (s, lambda bb: n_ctx // s + bb)]
        attn_lat = _attention(q_t, k_t, v_t, q_tile0=n_ctx // TQ_ATTN, n_q_tiles=n_qt,
                              key_blocks=lat_keys, tq=TQ_ATTN, n_batch=b, name="attention_latent")
        attn_ctx = _attention(q_t, k_t, v_t, q_tile0=0, n_q_tiles=1,
                              key_blocks=[(n_ctx_len, lambda bb: bb)], tq=n_ctx_len, n_batch=b,
                              name="attention_context")

        wfour = w_four[l].astype(BF16)
        four_lat = _fourier(f, c_lat, s_lat, c_grp, s_grp, wfour, n=s, tr=TR_FOUR,
                            x_block0=n_ctx // s, n_batch=b, name="fourier_latent")
        four_ctx = _fourier(f, c_ctx_m, s_ctx_m, c_grp, s_grp, wfour, n=n_ctx_len, tr=n_ctx_len,
                            x_block0=0, n_batch=b, name="fourier_context")

        wo = w_o[l].astype(BF16)
        is_moe = l % 2 == 1
        li = l // 2
        router = None
        if is_moe:
            wr = jnp.zeros((d, LOGIT_LANES), BF16).at[:, :N_EXPERTS].set(w_router[li].astype(BF16))
            br = jnp.zeros((1, LOGIT_LANES), F32).at[0, :N_EXPERTS].set(b_router[li])
            router = (wr, br)
        res = _postmix(xa, attn_ctx, attn_lat, four_ctx, four_lat, mod_l, g_ffn[l].reshape(1, d),
                       wo[:ATTN_WIDTH], wo[ATTN_WIDTH:], router, n_ctx=n_ctx, seq=s)
        if not is_moe:
            x1, h2 = res
            xa = _dense_ffn(h2, x1, mod_l, _chunk_cols(w_gate_dense[li], FF_CHUNK),
                            _chunk_cols(w_up_dense[li], FF_CHUNK), _chunk_rows(w_down_dense[li], FF_CHUNK),
                            n_ctx=n_ctx, seq=s)
        else:
            x1, h2, logits = res
            gates, dest, block_expert, n_valid, n_rows = _route(logits)
            xs = _sc_dispatch(h2, dest, n_rows)
            ys = _moe_ffn(xs, block_expert, n_valid, w_gate_moe, w_up_moe, w_down_moe, li)
            y_sel = _sc_gather(ys, dest.T.reshape(-1))
            gates_p = jnp.zeros((t, LOGIT_LANES), F32).at[:, :TOP_K].set(gates)
            xa = _moe_combine(x1, y_sel, gates_p, mod_l, n_ctx=n_ctx, seq=s)

    return xa[n_ctx:].reshape(b, s, d)
```

```python
import functools

import numpy as np
import jax
import jax.numpy as jnp
from jax import lax
from jax.experimental import pallas as pl
from jax.experimental.pallas import tpu as pltpu
from jax.experimental.pallas import tpu_sc as plsc

D_MODEL = 1024
DEPTH = 4
GRID_W = 64
HEAD_DIM = 64
ATTN_WIDTH = 512
N_Q_HEADS = 8
N_KV_HEADS = 2
KV_REP = 4
KV_WIDTH = 128
FOURIER_WIDTH = 512
N_FOURIER_GROUPS = 4
FOURIER_GROUP = 128
ROT_PER_AXIS = 32
ROPE_THETA = 10000.0
N_MOD = 6
D_FF_DENSE = 2816
N_EXPERTS = 8
TOP_K = 2
D_FF_EXPERT = 3584
EPS = 1e-6

QKV_WIDTH = ATTN_WIDTH + 2 * KV_WIDTH
MOD_ROWS = 16
LOGIT_LANES = 128

TM_MIX = 512
TQ_ATTN = 512
KEY_CHUNK = 256
ONES_ROWS = 16
SCORE_LOOKAHEAD = 2
LOG2_E = 1.4426950408889634
TR_FOUR = 512
TM_DENSE = 1024
FF_CHUNK = 256
TM_MOE = 2048
TM_PASS = 1024
TF_MOE = 512
SC_DISPATCH_ROWS = 64
SC_GATHER_ROWS = 32
VMEM_LIMIT = 56 * 1024 * 1024

F32 = jnp.float32
BF16 = jnp.bfloat16


def _cparams(n_axes, flags=None):
    return pltpu.CompilerParams(
        dimension_semantics=("arbitrary",) * n_axes, vmem_limit_bytes=VMEM_LIMIT, flags=flags)


def _rope_tables_t(s, tm):
    n_rows = s // GRID_W
    rows = np.repeat(np.arange(n_rows), GRID_W).astype(np.float64)
    cols = np.tile(np.arange(GRID_W), n_rows).astype(np.float64)
    inv_freq = (ROPE_THETA ** (-np.arange(0, ROT_PER_AXIS, 2, dtype=np.float32) / ROT_PER_AXIS)
                ).astype(np.float32).astype(np.float64)
    ang_r = (rows[None, :].astype(np.float32) * inv_freq[:, None].astype(np.float32)).astype(np.float64)
    ang_c = (cols[None, :].astype(np.float32) * inv_freq[:, None].astype(np.float32)).astype(np.float64)
    cos = np.concatenate([np.cos(ang_r), np.cos(ang_r), np.cos(ang_c), np.cos(ang_c)], axis=0)
    sin = np.concatenate([-np.sin(ang_r), np.sin(ang_r), -np.sin(ang_c), np.sin(ang_c)], axis=0)
    cos = np.concatenate([np.ones((HEAD_DIM, tm)), cos], axis=1)
    sin = np.concatenate([np.zeros((HEAD_DIM, tm)), sin], axis=1)
    return jnp.asarray(cos, F32), jnp.asarray(sin, F32)


def _dft_mats(n):
    k = np.arange(n, dtype=np.int64)
    ang = 2.0 * np.pi * ((k[:, None] * k[None, :]) % n).astype(np.float64) / n
    return jnp.asarray(np.cos(ang), BF16), jnp.asarray(np.sin(ang), BF16)


def _chunk_cols(w, chunk):
    *lead, d, n = w.shape
    w = w.astype(BF16).reshape(*lead, d, n // chunk, chunk)
    return jnp.swapaxes(w, -3, -2)


def _chunk_rows(w, chunk):
    *lead, n, d = w.shape
    return w.astype(BF16).reshape(*lead, n // chunk, chunk, d)


def _mod_kernel(c_ref, w_ref, b_ref, o_ref):
    c = c_ref[...]
    s = (c / (1.0 + jnp.exp(-c))).astype(BF16)
    w = w_ref[0].astype(BF16)
    o_ref[0] = jnp.dot(s, w, preferred_element_type=F32) + b_ref[0]


def _modulation(cvec, w_mod, b_mod):
    depth, d, n = w_mod.shape
    tn = 1536
    return pl.pallas_call(
        _mod_kernel,
        grid=(depth, n // tn),
        in_specs=[
            pl.BlockSpec((MOD_ROWS, d), lambda l, j: (0, 0)),
            pl.BlockSpec((1, d, tn), lambda l, j: (l, 0, j)),
            pl.BlockSpec((1, 1, tn), lambda l, j: (l, 0, j)),
        ],
        out_specs=pl.BlockSpec((1, MOD_ROWS, tn), lambda l, j: (l, 0, j)),
        out_shape=jax.ShapeDtypeStruct((depth, MOD_ROWS, n), F32),
        compiler_params=_cparams(2),
        name="adaln_vectors",
    )(cvec, w_mod, b_mod.reshape(depth, 1, n))


def _mod_row(i, n_ctx_tiles, tiles_per_batch):
    lat = jnp.maximum(i - n_ctx_tiles, 0)
    return jnp.where(i < n_ctx_tiles, 0, lat // tiles_per_batch + 1)


def _mod_vec(mod_ref, row, comp):
    return mod_ref[pl.ds(row, 1), comp * D_MODEL:(comp + 1) * D_MODEL]


def _pack_bf16_pairs(xb):
    n = xb.shape[1] // 2
    bits = lax.bitcast_convert_type(xb.astype(F32), jnp.uint32)
    return (bits[:, :n] >> 16) | (bits[:, n:] & jnp.uint32(0xFFFF0000))


def _unpack_bf16_pairs(w):
    lo = lax.bitcast_convert_type(w << 16, F32)
    hi = lax.bitcast_convert_type(w & jnp.uint32(0xFFFF0000), F32)
    return jnp.concatenate([lo, hi], axis=1).astype(BF16)


def _norm_modulate(x, g, shift, scale):
    ms = jnp.mean(x * x, axis=-1, keepdims=True)
    y = x * lax.rsqrt(ms + EPS) * g
    return y * (1.0 + scale) + shift


def _premix_kernel(x_ref, mod_ref, g_ref, wt_ref, wf_ref, gq_ref, gk_ref, cos_ref, sin_ref,
                   q_ref, k_ref, v_ref, f_ref, *, n_ctx_tiles, tiles_per_batch):
    i = pl.program_id(0)
    row = _mod_row(i, n_ctx_tiles, tiles_per_batch)
    h = _norm_modulate(x_ref[...], g_ref[...], _mod_vec(mod_ref, row, 0), _mod_vec(mod_ref, row, 1))
    hb = h.astype(BF16)
    f_ref[...] = jnp.dot(hb, wf_ref[...], preferred_element_type=F32).astype(BF16)
    pt = lax.dot_general(wt_ref[...], hb, (((1,), (1,)), ((), ())), preferred_element_type=F32)
    v_ref[...] = pt[ATTN_WIDTH + KV_WIDTH:, :].astype(BF16)
    cos = cos_ref[...]
    sin = sin_ref[...]

    def norm_rope(xh, gain):
        ms = jnp.mean(xh * xh, axis=0, keepdims=True)
        y = xh * lax.rsqrt(ms + EPS) * gain
        half = ROT_PER_AXIS // 2
        swapped = jnp.concatenate(
            [y[half:2 * half], y[0:half], y[3 * half:4 * half], y[2 * half:3 * half]], axis=0)
        return y * cos + swapped * sin

    gq = gq_ref[...]
    gk = gk_ref[...]
    for hh in range(N_Q_HEADS):
        q_ref[hh * HEAD_DIM:(hh + 1) * HEAD_DIM, :] = norm_rope(
            pt[hh * HEAD_DIM:(hh + 1) * HEAD_DIM, :], gq).astype(BF16)
    for hh in range(N_KV_HEADS):
        lo = ATTN_WIDTH + hh * HEAD_DIM
        k_ref[hh * HEAD_DIM:(hh + 1) * HEAD_DIM, :] = norm_rope(pt[lo:lo + HEAD_DIM, :], gk).astype(BF16)


def _premix(x, mod_l, g_mix, wt, wf, gq, gk, cos_t, sin_t, *, n_ctx, seq):
    t, d = x.shape
    tm = TM_MIX
    n_ctx_tiles = n_ctx // tm
    tpb = seq // tm

    def tab_idx(i):
        lat = jnp.maximum(i - n_ctx_tiles, 0)
        return (0, jnp.where(i < n_ctx_tiles, 0, lax.rem(lat, tpb) + 1))

    full = lambda shape: pl.BlockSpec(shape, lambda i: (0,) * len(shape))
    return pl.pallas_call(
        functools.partial(_premix_kernel, n_ctx_tiles=n_ctx_tiles, tiles_per_batch=tpb),
        grid=(t // tm,),
        in_specs=[
            pl.BlockSpec((tm, d), lambda i: (i, 0)),
            full(mod_l.shape),
            full((1, d)),
            full(wt.shape),
            full(wf.shape),
            full(gq.shape),
            full(gk.shape),
            pl.BlockSpec((HEAD_DIM, tm), tab_idx),
            pl.BlockSpec((HEAD_DIM, tm), tab_idx),
        ],
        out_specs=[
            pl.BlockSpec((ATTN_WIDTH, tm), lambda i: (0, i)),
            pl.BlockSpec((KV_WIDTH, tm), lambda i: (0, i)),
            pl.BlockSpec((KV_WIDTH, tm), lambda i: (0, i)),
            pl.BlockSpec((tm, FOURIER_WIDTH), lambda i: (i, 0)),
        ],
        out_shape=[
            jax.ShapeDtypeStruct((ATTN_WIDTH, t), BF16),
            jax.ShapeDtypeStruct((KV_WIDTH, t), BF16),
            jax.ShapeDtypeStruct((KV_WIDTH, t), BF16),
            jax.ShapeDtypeStruct((t, FOURIER_WIDTH), BF16),
        ],
        compiler_params=_cparams(1),
        name="premix_project",
    )(x, mod_l, g_mix, wt, wf, gq, gk, cos_t, sin_t)


def _attn_kernel(*refs, n_key_blocks, tq):
    q_ref = refs[0]
    k_refs = refs[1:1 + n_key_blocks]
    v_refs = refs[1 + n_key_blocks:1 + 2 * n_key_blocks]
    o_ref = refs[1 + 2 * n_key_blocks]
    kall_ref, vg_ref, ot_ref = refs[2 + 2 * n_key_blocks:]
    g = pl.program_id(1)
    qt = pl.program_id(2)
    n_keys = kall_ref.shape[0]
    n_chunks = n_keys // KEY_CHUNK
    slabs = KEY_CHUNK // 8

    @pl.when(jnp.logical_and(g == 0, qt == 0))
    def _():
        off = 0
        for kr in k_refs:
            n = kr.shape[1]
            kall_ref[off:off + n, :] = kr[...].astype(F32).T.astype(BF16)
            off += n

    @pl.when(qt == 0)
    def _():
        g_rows = pl.ds(pl.multiple_of(g * HEAD_DIM, HEAD_DIM), HEAD_DIM)
        off = 0
        for vr in v_refs:
            n = vr.shape[1]
            vg_ref[0:HEAD_DIM, off:off + n] = vr[g_rows, :]
            off += n
        vg_ref[HEAD_DIM:, :] = jnp.ones((ONES_ROWS, n_keys), BF16)

    row_group = lax.broadcasted_iota(jnp.int32, (KV_WIDTH, tq), 0) // HEAD_DIM

    def masked_q(hh):
        qh = q_ref[hh * HEAD_DIM:(hh + 1) * HEAD_DIM, :]
        q2 = jnp.concatenate([qh, qh], axis=0)
        return jnp.where(row_group == g, q2, jnp.zeros_like(q2))

    q2s = [masked_q(hh) for hh in range(KV_REP)]
    items = [(hh, c) for hh in range(KV_REP) for c in range(n_chunks)]

    def score(item):
        hh, c = item
        return jnp.dot(kall_ref[c * KEY_CHUNK:(c + 1) * KEY_CHUNK, :], q2s[hh], preferred_element_type=F32)

    pending = [score(it) for it in items[:SCORE_LOOKAHEAD]]
    m = ot = None
    for i, (hh, c) in enumerate(items):
        if c == 0:
            m = jnp.full((1, tq), -jnp.inf, F32)
            ot = jnp.zeros((HEAD_DIM + ONES_ROWS, tq), F32)
        s = pending.pop(0).reshape(slabs, 8, tq)
        if i + SCORE_LOOKAHEAD < len(items):
            pending.append(score(items[i + SCORE_LOOKAHEAD]))
        m_new = jnp.maximum(m, jnp.max(jnp.max(s, axis=0), axis=0, keepdims=True))
        pb = jnp.exp2(s - m_new[None]).reshape(KEY_CHUNK, tq).astype(BF16)
        rows = slice(c * KEY_CHUNK, (c + 1) * KEY_CHUNK)
        ot = jnp.exp2(m - m_new) * ot + jnp.dot(vg_ref[:, rows], pb, preferred_element_type=F32)
        m = m_new
        if c == n_chunks - 1:
            ot_ref[hh * HEAD_DIM:(hh + 1) * HEAD_DIM, :] = ot[:HEAD_DIM] / ot[HEAD_DIM:HEAD_DIM + 1]
    o_ref[...] = ot_ref[...].T.astype(BF16)


def _attention(q_t, k_t, v_t, *, q_tile0, n_q_tiles, key_blocks, tq, n_batch, name):
    n_keys = sum(c for c, _ in key_blocks)
    nkb = len(key_blocks)
    q_spec = pl.BlockSpec((KV_REP * HEAD_DIM, tq),
                          lambda b, g, i: (g, q_tile0 + b * n_q_tiles + i))
    k_specs = [pl.BlockSpec((KV_WIDTH, c), (lambda f: (lambda b, g, i: (0, f(b))))(f))
               for c, f in key_blocks]
    return pl.pallas_call(
        functools.partial(_attn_kernel, n_key_blocks=nkb, tq=tq),
        grid=(n_batch, N_KV_HEADS, n_q_tiles),
        in_specs=[q_spec] + k_specs + k_specs,
        out_specs=pl.BlockSpec((tq, KV_REP * HEAD_DIM),
                               lambda b, g, i: (b * n_q_tiles + i, g)),
        out_shape=jax.ShapeDtypeStruct((n_batch * n_q_tiles * tq, ATTN_WIDTH), BF16),
        scratch_shapes=[pltpu.VMEM((n_keys, KV_WIDTH), BF16),
                        pltpu.VMEM((HEAD_DIM + ONES_ROWS, n_keys), BF16),
                        pltpu.VMEM((KV_REP * HEAD_DIM, tq), F32)],
        compiler_params=_cparams(3),
        name=name,
    )(q_t, *([k_t] * nkb), *([v_t] * nkb))


def _fourier_kernel(c_ref, s_ref, x_ref, cc_ref, sc_ref, w_ref, o_ref, *, norm):
    x = x_ref[...]
    a = jnp.dot(c_ref[...], x, preferred_element_type=F32).astype(BF16)
    b = jnp.dot(s_ref[...], x, preferred_element_type=F32).astype(BF16)
    cc = cc_ref[...]
    sc = sc_ref[...]
    groups = [slice(grp * FOURIER_GROUP, (grp + 1) * FOURIER_GROUP) for grp in range(N_FOURIER_GROUPS)]
    specs = [(jnp.dot(a[:, sl], cc, preferred_element_type=F32)
              - jnp.dot(b[:, sl], sc, preferred_element_type=F32)) * norm for sl in groups]
    for grp, sl in enumerate(groups):
        o_ref[:, sl] = jnp.dot(specs[grp].astype(BF16), w_ref[grp], preferred_element_type=F32).astype(BF16)


def _fourier(f, cmat, smat, cc, sc, w_four, *, n, tr, x_block0, n_batch, name):
    n_row_tiles = n // tr
    return pl.pallas_call(
        functools.partial(_fourier_kernel, norm=float(1.0 / np.sqrt(n * FOURIER_GROUP))),
        grid=(n_row_tiles, n_batch),
        in_specs=[
            pl.BlockSpec((tr, n), lambda i, b: (i, 0)),
            pl.BlockSpec((tr, n), lambda i, b: (i, 0)),
            pl.BlockSpec((n, FOURIER_WIDTH), lambda i, b: (x_block0 + b, 0)),
            pl.BlockSpec(cc.shape, lambda i, b: (0, 0)),
            pl.BlockSpec(sc.shape, lambda i, b: (0, 0)),
            pl.BlockSpec(w_four.shape, lambda i, b: (0, 0, 0)),
        ],
        out_specs=pl.BlockSpec((tr, FOURIER_WIDTH), lambda i, b: (b * n_row_tiles + i, 0)),
        out_shape=jax.ShapeDtypeStruct((n_batch * n, FOURIER_WIDTH), BF16),
        compiler_params=_cparams(2),
        name=name,
    )(cmat, smat, f, cc, sc, w_four)


def _postmix_kernel(x_ref, ac_ref, al_ref, fc_ref, fl_ref, mod_ref, g_ref, woa_ref, wof_ref, *rest,
                    n_ctx_tiles, tiles_per_batch, with_router):
    if with_router:
        wr_ref, br_ref, x1_ref, h_ref, lg_ref = rest
    else:
        x1_ref, h_ref = rest
    i = pl.program_id(0)
    row = _mod_row(i, n_ctx_tiles, tiles_per_batch)
    is_ctx = i < n_ctx_tiles
    a = jnp.where(is_ctx, ac_ref[...], al_ref[...])
    f = jnp.where(is_ctx, fc_ref[...], fl_ref[...])
    mix = (jnp.dot(a, woa_ref[...], preferred_element_type=F32)
           + jnp.dot(f, wof_ref[...], preferred_element_type=F32))
    x1 = x_ref[...] + _mod_vec(mod_ref, row, 2) * mix
    x1_ref[...] = x1
    h = _norm_modulate(x1, g_ref[...], _mod_vec(mod_ref, row, 3), _mod_vec(mod_ref, row, 4))
    hb = h.astype(BF16)
    if with_router:
        h_ref[...] = _pack_bf16_pairs(hb)
        lg_ref[...] = jnp.dot(hb, wr_ref[...], preferred_element_type=F32) + br_ref[...]
    else:
        h_ref[...] = hb


def _postmix(x, attn_ctx, attn_lat, four_ctx, four_lat, mod_l, g_ffn, wo_a, wo_f, router, *, n_ctx, seq):
    t, d = x.shape
    tm = TM_MIX
    nct = n_ctx // tm
    full = lambda shape: pl.BlockSpec(shape, lambda i: (0,) * len(shape))
    tile = lambda w: pl.BlockSpec((tm, w), lambda i: (i, 0))
    ctx_tile = lambda w: pl.BlockSpec((tm, w), lambda i: (jnp.minimum(i, nct - 1), 0))
    lat_tile = lambda w: pl.BlockSpec((tm, w), lambda i: (jnp.maximum(i - nct, 0), 0))
    in_specs = [tile(d), ctx_tile(ATTN_WIDTH), lat_tile(ATTN_WIDTH), ctx_tile(FOURIER_WIDTH),
                lat_tile(FOURIER_WIDTH), full(mod_l.shape), full((1, d)), full(wo_a.shape), full(wo_f.shape)]
    if router is None:
        out_specs = [tile(d), tile(d)]
        out_shape = [jax.ShapeDtypeStruct((t, d), F32), jax.ShapeDtypeStruct((t, d), BF16)]
    else:
        out_specs = [tile(d), tile(d // 2)]
        out_shape = [jax.ShapeDtypeStruct((t, d), F32), jax.ShapeDtypeStruct((t, d // 2), jnp.uint32)]
    args = [x, attn_ctx, attn_lat, four_ctx, four_lat, mod_l, g_ffn, wo_a, wo_f]
    if router is not None:
        in_specs += [full(router[0].shape), full(router[1].shape)]
        out_specs.append(tile(LOGIT_LANES))
        out_shape.append(jax.ShapeDtypeStruct((t, LOGIT_LANES), F32))
        args += list(router)
    return pl.pallas_call(
        functools.partial(_postmix_kernel, n_ctx_tiles=n_ctx // tm, tiles_per_batch=seq // tm,
                          with_router=router is not None),
        grid=(t // tm,),
        in_specs=in_specs,
        out_specs=out_specs,
        out_shape=out_shape,
        compiler_params=_cparams(1),
        name="postmix_wo_norm",
    )(*args)


def _silu_mul(gate, up):
    return (gate / (1.0 + jnp.exp(-gate))) * up


def _swiglu_chunks(x, n_chunks, wg_of, wu_of, wd_of, acc_ref):
    gate = jnp.dot(x, wg_of(0), preferred_element_type=F32)
    up = jnp.dot(x, wu_of(0), preferred_element_type=F32)
    for c in range(n_chunks):
        act = _silu_mul(gate, up).astype(BF16)
        if c + 1 < n_chunks:
            gate = jnp.dot(x, wg_of(c + 1), preferred_element_type=F32)
            up = jnp.dot(x, wu_of(c + 1), preferred_element_type=F32)
        acc_ref[...] += jnp.dot(act, wd_of(c), preferred_element_type=F32)


def _dense_ffn_kernel(h_ref, x1_ref, mod_ref, wg_ref, wu_ref, wd_ref, o_ref, acc_ref, *,
                      n_ctx_tiles, tiles_per_batch):
    i = pl.program_id(0)
    row = _mod_row(i, n_ctx_tiles, tiles_per_batch)
    acc_ref[...] = jnp.zeros_like(acc_ref)
    _swiglu_chunks(h_ref[...], wg_ref.shape[0], lambda c: wg_ref[c], lambda c: wu_ref[c],
                   lambda c: wd_ref[c], acc_ref)
    o_ref[...] = x1_ref[...] + _mod_vec(mod_ref, row, 5) * acc_ref[...]


def _dense_ffn(h, x1, mod_l, wg, wu, wd, *, n_ctx, seq):
    t, d = x1.shape
    tm = TM_DENSE
    full = lambda shape: pl.BlockSpec(shape, lambda i: (0,) * len(shape))
    tile = lambda: pl.BlockSpec((tm, d), lambda i: (i, 0))
    return pl.pallas_call(
        functools.partial(_dense_ffn_kernel, n_ctx_tiles=n_ctx // tm, tiles_per_batch=seq // tm),
        grid=(t // tm,),
        in_specs=[tile(), tile(), full(mod_l.shape), full(wg.shape), full(wu.shape), full(wd.shape)],
        out_specs=tile(),
        out_shape=jax.ShapeDtypeStruct((t, d), F32),
        scratch_shapes=[pltpu.VMEM((tm, d), F32)],
        compiler_params=_cparams(1),
        name="dense_swiglu",
    )(h, x1, mod_l, wg, wu, wd)


def _moe_ffn_kernel(be_ref, nv_ref, x_ref, wg_ref, wu_ref, wd_ref, o_ref, xb_ref, wgb_ref, wub_ref, wdb_ref):
    i = pl.program_id(0)
    j = pl.program_id(1)
    n_valid = nv_ref[i]

    @pl.when(j == 0)
    def _():
        x = _unpack_bf16_pairs(x_ref[...])
        rows = lax.broadcasted_iota(jnp.int32, x.shape, 0)
        xb_ref[...] = jnp.where(rows < n_valid, x, jnp.zeros_like(x))
        o_ref[...] = jnp.zeros_like(o_ref)

    @pl.when(n_valid > 0)
    def _():
        wgb_ref[...] = wg_ref[0, 0].astype(BF16)
        wub_ref[...] = wu_ref[0, 0].astype(BF16)
        wdb_ref[...] = wd_ref[0, 0].astype(BF16)

    sub = lambda c: slice(c * FF_CHUNK, (c + 1) * FF_CHUNK)
    for r in range(TM_MOE // TM_PASS):
        rows = slice(r * TM_PASS, (r + 1) * TM_PASS)

        @pl.when(n_valid > r * TM_PASS)
        def _():
            _swiglu_chunks(xb_ref[rows, :], TF_MOE // FF_CHUNK,
                           lambda c: wgb_ref[:, sub(c)], lambda c: wub_ref[:, sub(c)],
                           lambda c: wdb_ref[sub(c), :], o_ref.at[rows, :])


def _moe_ffn(xs, block_expert, n_valid, wg, wu, wd, layer):
    n_rows = xs.shape[0]
    d = D_MODEL
    tm, tf = TM_MOE, TF_MOE
    n_blocks = n_rows // tm
    n_ff = wg.shape[3] // tf

    def ff_idx(j, nv, i):
        return jnp.where(nv[i] > 0, j, n_ff - 1)

    grid_spec = pltpu.PrefetchScalarGridSpec(
        num_scalar_prefetch=2,
        grid=(n_blocks, n_ff),
        in_specs=[
            pl.BlockSpec((tm, d // 2), lambda i, j, be, nv: (i, 0)),
            pl.BlockSpec((1, 1, d, tf), lambda i, j, be, nv: (layer, be[i], 0, ff_idx(j, nv, i))),
            pl.BlockSpec((1, 1, d, tf), lambda i, j, be, nv: (layer, be[i], 0, ff_idx(j, nv, i))),
            pl.BlockSpec((1, 1, tf, d), lambda i, j, be, nv: (layer, be[i], ff_idx(j, nv, i), 0)),
        ],
        out_specs=pl.BlockSpec((tm, d), lambda i, j, be, nv: (i, 0)),
        scratch_shapes=[pltpu.VMEM((tm, d), BF16), pltpu.VMEM((d, tf), BF16), pltpu.VMEM((d, tf), BF16),
                        pltpu.VMEM((tf, d), BF16)],
    )
    return pl.pallas_call(
        _moe_ffn_kernel,
        grid_spec=grid_spec,
        out_shape=jax.ShapeDtypeStruct((n_rows, d), F32),
        compiler_params=_cparams(2),
        name="expert_swiglu",
    )(block_expert, n_valid, xs, wg, wu, wd)


def _sc_mesh():
    return plsc.VectorSubcoreMesh(core_axis_name="c", subcore_axis_name="s")


def _sc_params():
    return pltpu.CompilerParams(use_tc_tiling_on_sc=True)


def _sc_dispatch(h_packed, dest, n_rows):
    t, w = h_packed.shape
    win = SC_DISPATCH_ROWS
    n_win = t // win
    idx = [dest[:, k].reshape(n_win, 1, win) for k in range(TOP_K)]

    @functools.partial(
        pl.kernel, out_type=jax.ShapeDtypeStruct((n_rows, w), h_packed.dtype), mesh=_sc_mesh(),
        scratch_types=[], compiler_params=_sc_params(), name="expert_dispatch_scatter")
    def run(x_hbm, i0_hbm, i1_hbm, o_hbm):
        def body(x_vmem, i0_vmem, i1_vmem):
            pltpu.sync_copy(x_vmem, o_hbm.at[i0_vmem.at[0, 0]])
            pltpu.sync_copy(x_vmem, o_hbm.at[i1_vmem.at[0, 0]])

        idx_spec = pl.BlockSpec((1, 1, win), lambda i: (i, 0, 0))
        pltpu.emit_pipeline(
            body, grid=(n_win,),
            in_specs=[pl.BlockSpec((win, w), lambda i: (i, 0)), idx_spec, idx_spec],
            out_specs=[], core_axis_name=("c", "s"), dimension_semantics=(pltpu.PARALLEL,),
        )(x_hbm, i0_hbm, i1_hbm)

    return run(h_packed, *idx)


def _sc_gather(ys, idx_flat):
    w = ys.shape[1]
    n = idx_flat.shape[0]
    win = SC_GATHER_ROWS
    n_win = n // win

    @functools.partial(
        pl.kernel, out_type=jax.ShapeDtypeStruct((n, w), ys.dtype), mesh=_sc_mesh(),
        scratch_types=[], compiler_params=_sc_params(), name="expert_combine_gather")
    def run(y_hbm, i_hbm, o_hbm):
        def body(i_vmem, o_vmem):
            pltpu.sync_copy(y_hbm.at[i_vmem.at[0, 0]], o_vmem)

        pltpu.emit_pipeline(
            body, grid=(n_win,),
            in_specs=[pl.BlockSpec((1, 1, win), lambda i: (i, 0, 0))],
            out_specs=[pl.BlockSpec((win, w), lambda i: (i, 0))],
            core_axis_name=("c", "s"), dimension_semantics=(pltpu.PARALLEL,),
        )(i_hbm, o_hbm)

    return run(ys, idx_flat.reshape(n_win, 1, win))


def _moe_combine_kernel(x1_ref, y0_ref, y1_ref, g_ref, mod_ref, o_ref, *, n_ctx_tiles, tiles_per_batch):
    i = pl.program_id(0)
    row = _mod_row(i, n_ctx_tiles, tiles_per_batch)
    gates = g_ref[...]
    y = y0_ref[...] * gates[:, 0:1] + y1_ref[...] * gates[:, 1:2]
    o_ref[...] = x1_ref[...] + _mod_vec(mod_ref, row, 5) * y


def _moe_combine(x1, y_sel, gates, mod_l, *, n_ctx, seq):
    t, d = x1.shape
    tm = TM_MIX
    n_tiles = t // tm
    tile = lambda w: pl.BlockSpec((tm, w), lambda i: (i, 0))
    return pl.pallas_call(
        functools.partial(_moe_combine_kernel, n_ctx_tiles=n_ctx // tm, tiles_per_batch=seq // tm),
        grid=(n_tiles,),
        in_specs=[tile(d), tile(d), pl.BlockSpec((tm, d), lambda i: (n_tiles + i, 0)), tile(LOGIT_LANES),
                  pl.BlockSpec(mod_l.shape, lambda i: (0, 0))],
        out_specs=tile(d),
        out_shape=jax.ShapeDtypeStruct((t, d), F32),
        compiler_params=_cparams(1),
        name="expert_combine",
    )(x1, y_sel, y_sel, gates, mod_l)


def _route(logits):
    t = logits.shape[0]
    top_val, top_idx = lax.top_k(logits[:, :N_EXPERTS], TOP_K)
    gates = jax.nn.softmax(top_val, axis=-1)
    expert = top_idx.reshape(-1).astype(jnp.int32)
    n_assign = t * TOP_K
    onehot = (expert[:, None] == jnp.arange(N_EXPERTS, dtype=jnp.int32)[None, :]).astype(jnp.int32)
    csum = jnp.cumsum(onehot, axis=0)
    rank = jnp.sum((csum - onehot) * onehot, axis=1)
    counts = csum[-1]
    padded = ((counts + TM_MOE - 1) // TM_MOE) * TM_MOE
    pend = jnp.cumsum(padded)
    pstart = pend - padded
    dest = (pstart[expert] + rank).astype(jnp.int32)
    n_blocks = -(-n_assign // TM_MOE) + N_EXPERTS
    block_start = jnp.arange(n_blocks, dtype=jnp.int32) * TM_MOE
    block_expert = jnp.minimum(jnp.sum(pend[None, :] <= block_start[:, None], axis=1),
                               N_EXPERTS - 1).astype(jnp.int32)
    n_valid = jnp.clip(counts[block_expert] - (block_start - pstart[block_expert]), 0, TM_MOE)
    n_valid = jnp.where(block_start < pend[-1], n_valid, 0).astype(jnp.int32)
    return gates, dest.reshape(t, TOP_K), block_expert, n_valid, n_blocks * TM_MOE


def kernel(x, c, ctx, c_ctx, w_mod, b_mod, g_mix, g_ffn, g_q, g_k, w_in, w_four, w_o,
           w_gate_dense, w_up_dense, w_down_dense, w_router, b_router,
           w_gate_moe, w_up_moe, w_down_moe):
    b, s, d = x.shape
    n_ctx_len = ctx.shape[1]
    n_ctx = b * n_ctx_len
    t = n_ctx + b * s
    assert d == D_MODEL and b + 1 <= MOD_ROWS
    assert n_ctx % TM_DENSE == 0 and s % TM_DENSE == 0 and n_ctx_len % 128 == 0 and n_ctx % s == 0

    cvec = jnp.zeros((MOD_ROWS, d), F32).at[0].set(c_ctx).at[1:b + 1].set(c)
    mod = _modulation(cvec, w_mod, b_mod)

    cos_t, sin_t = _rope_tables_t(s, TM_MIX)
    c_lat, s_lat = _dft_mats(s)
    c_ctx_m, s_ctx_m = _dft_mats(n_ctx_len)
    c_grp, s_grp = _dft_mats(FOURIER_GROUP)

    xa = jnp.concatenate([ctx.reshape(n_ctx, d), x.reshape(b * s, d)], axis=0)

    for l in range(DEPTH):
        mod_l = mod[l]
        w_in_l = w_in[l]
        wt = w_in_l[:, :QKV_WIDTH].T.astype(BF16)
        wf = w_in_l[:, QKV_WIDTH:].astype(BF16)
        gq = jnp.broadcast_to((g_q[l] * (LOG2_E * HEAD_DIM ** -0.5))[:, None], (HEAD_DIM, TM_MIX))
        gk = jnp.broadcast_to(g_k[l][:, None], (HEAD_DIM, TM_MIX))
        q_t, k_t, v_t, f = _premix(xa, mod_l, g_mix[l].reshape(1, d), wt, wf, gq, gk, cos_t, sin_t,
                                   n_ctx=n_ctx, seq=s)

        n_qt = s // TQ_ATTN
        lat_keys = [(n_ctx_len, lambda bb: bb), (s, lambda bb: n_ctx // s + bb)]
        attn_lat = _attention(q_t, k_t, v_t, q_tile0=n_ctx // TQ_ATTN, n_q_tiles=n_qt,
                              key_blocks=lat_keys, tq=TQ_ATTN, n_batch=b, name="attention_latent")
        attn_ctx = _attention(q_t, k_t, v_t, q_tile0=0, n_q_tiles=1,
                              key_blocks=[(n_ctx_len, lambda bb: bb)], tq=n_ctx_len, n_batch=b,
                              name="attention_context")

        wfour = w_four[l].astype(BF16)
        four_lat = _fourier(f, c_lat, s_lat, c_grp, s_grp, wfour, n=s, tr=TR_FOUR,
                            x_block0=n_ctx // s, n_batch=b, name="fourier_latent")
        four_ctx = _fourier(f, c_ctx_m, s_ctx_m, c_grp, s_grp, wfour, n=n_ctx_len, tr=n_ctx_len,
                            x_block0=0, n_batch=b, name="fourier_context")

        wo = w_o[l].astype(BF16)
        is_moe = l % 2 == 1
        li = l // 2
        router = None
        if is_moe:
            wr = jnp.zeros((d, LOGIT_LANES), BF16).at[:, :N_EXPERTS].set(w_router[li].astype(BF16))
            br = jnp.zeros((1, LOGIT_LANES), F32).at[0, :N_EXPERTS].set(b_router[li])
            router = (wr, br)
        res = _postmix(xa, attn_ctx, attn_lat, four_ctx, four_lat, mod_l, g_ffn[l].reshape(1, d),
                       wo[:ATTN_WIDTH], wo[ATTN_WIDTH:], router, n_ctx=n_ctx, seq=s)
        if not is_moe:
            x1, h2 = res
            xa = _dense_ffn(h2, x1, mod_l, _chunk_cols(w_gate_dense[li], FF_CHUNK),
                            _chunk_cols(w_up_dense[li], FF_CHUNK), _chunk_rows(w_down_dense[li], FF_CHUNK),
                            n_ctx=n_ctx, seq=s)
        else:
            x1, h2, logits = res
            gates, dest, block_expert, n_valid, n_rows = _route(logits)
            xs = _sc_dispatch(h2, dest, n_rows)
            ys = _moe_ffn(xs, block_expert, n_valid, w_gate_moe, w_up_moe, w_down_moe, li)
            y_sel = _sc_gather(ys, dest.T.reshape(-1))
            gates_p = jnp.zeros((t, LOGIT_LANES), F32).at[:, :TOP_K].set(gates)
            xa = _moe_combine(x1, y_sel, gates_p, mod_l, n_ctx=n_ctx, seq=s)

    return xa[n_ctx:].reshape(b, s, d)
```

```python
import functools

import numpy as np
import jax
import jax.numpy as jnp
from jax import lax
from jax.experimental import pallas as pl
from jax.experimental.pallas import tpu as pltpu
from jax.experimental.pallas import tpu_sc as plsc

D_MODEL = 1024
DEPTH = 4
GRID_W = 64
HEAD_DIM = 64
ATTN_WIDTH = 512
N_Q_HEADS = 8
N_KV_HEADS = 2
KV_REP = 4
KV_WIDTH = 128
FOURIER_WIDTH = 512
N_FOURIER_GROUPS = 4
FOURIER_GROUP = 128
ROT_PER_AXIS = 32
ROPE_THETA = 10000.0
N_MOD = 6
D_FF_DENSE = 2816
N_EXPERTS = 8
TOP_K = 2
D_FF_EXPERT = 3584
EPS = 1e-6

QKV_WIDTH = ATTN_WIDTH + 2 * KV_WIDTH
MOD_ROWS = 16
LOGIT_LANES = 128

TM_MIX = 512
TQ_ATTN = 512
KEY_CHUNK = 256
ONES_ROWS = 16
SCORE_LOOKAHEAD = 2
LOG2_E = 1.4426950408889634
TR_FOUR = 512
TM_DENSE = 1024
FF_CHUNK = 256
TM_MOE = 2048
TM_PASS = 1024
TF_MOE = 512
SC_DISPATCH_ROWS = 64
SC_GATHER_ROWS = 32
VMEM_LIMIT = 56 * 1024 * 1024

F32 = jnp.float32
BF16 = jnp.bfloat16


def _cparams(n_axes, flags=None):
    return pltpu.CompilerParams(
        dimension_semantics=("arbitrary",) * n_axes, vmem_limit_bytes=VMEM_LIMIT, flags=flags)


def _rope_tables_t(s, tm):
    n_rows = s // GRID_W
    rows = np.repeat(np.arange(n_rows), GRID_W).astype(np.float64)
    cols = np.tile(np.arange(GRID_W), n_rows).astype(np.float64)
    inv_freq = (ROPE_THETA ** (-np.arange(0, ROT_PER_AXIS, 2, dtype=np.float32) / ROT_PER_AXIS)
                ).astype(np.float32).astype(np.float64)
    ang_r = (rows[None, :].astype(np.float32) * inv_freq[:, None].astype(np.float32)).astype(np.float64)
    ang_c = (cols[None, :].astype(np.float32) * inv_freq[:, None].astype(np.float32)).astype(np.float64)
    cos = np.concatenate([np.cos(ang_r), np.cos(ang_r), np.cos(ang_c), np.cos(ang_c)], axis=0)
    sin = np.concatenate([-np.sin(ang_r), np.sin(ang_r), -np.sin(ang_c), np.sin(ang_c)], axis=0)
    cos = np.concatenate([np.ones((HEAD_DIM, tm)), cos], axis=1)
    sin = np.concatenate([np.zeros((HEAD_DIM, tm)), sin], axis=1)
    return jnp.asarray(cos, F32), jnp.asarray(sin, F32)


def _dft_mats(n):
    k = np.arange(n, dtype=np.int64)
    ang = 2.0 * np.pi * ((k[:, None] * k[None, :]) % n).astype(np.float64) / n
    return jnp.asarray(np.cos(ang), BF16), jnp.asarray(np.sin(ang), BF16)


def _chunk_cols(w, chunk):
    *lead, d, n = w.shape
    w = w.astype(BF16).reshape(*lead, d, n // chunk, chunk)
    return jnp.swapaxes(w, -3, -2)


def _chunk_rows(w, chunk):
    *lead, n, d = w.shape
    return w.astype(BF16).reshape(*lead, n // chunk, chunk, d)


def _mod_kernel(c_ref, w_ref, b_ref, o_ref):
    c = c_ref[...]
    s = (c / (1.0 + jnp.exp(-c))).astype(BF16)
    w = w_ref[0].astype(BF16)
    o_ref[0] = jnp.dot(s, w, preferred_element_type=F32) + b_ref[0]


def _modulation(cvec, w_mod, b_mod):
    depth, d, n = w_mod.shape
    tn = 1536
    return pl.pallas_call(
        _mod_kernel,
        grid=(depth, n // tn),
        in_specs=[
            pl.BlockSpec((MOD_ROWS, d), lambda l, j: (0, 0)),
            pl.BlockSpec((1, d, tn), lambda l, j: (l, 0, j)),
            pl.BlockSpec((1, 1, tn), lambda l, j: (l, 0, j)),
        ],
        out_specs=pl.BlockSpec((1, MOD_ROWS, tn), lambda l, j: (l, 0, j)),
        out_shape=jax.ShapeDtypeStruct((depth, MOD_ROWS, n), F32),
        compiler_params=_cparams(2),
        name="adaln_vectors",
    )(cvec, w_mod, b_mod.reshape(depth, 1, n))


def _mod_row(i, n_ctx_tiles, tiles_per_batch):
    lat = jnp.maximum(i - n_ctx_tiles, 0)
    return jnp.where(i < n_ctx_tiles, 0, lat // tiles_per_batch + 1)


def _mod_vec(mod_ref, row, comp):
    return mod_ref[pl.ds(row, 1), comp * D_MODEL:(comp + 1) * D_MODEL]


def _pack_bf16_pairs(xb):
    n = xb.shape[1] // 2
    bits = lax.bitcast_convert_type(xb.astype(F32), jnp.uint32)
    return (bits[:, :n] >> 16) | (bits[:, n:] & jnp.uint32(0xFFFF0000))


def _unpack_bf16_pairs(w):
    lo = lax.bitcast_convert_type(w << 16, F32)
    hi = lax.bitcast_convert_type(w & jnp.uint32(0xFFFF0000), F32)
    return jnp.concatenate([lo, hi], axis=1).astype(BF16)


def _token_specs(x, tm, n_ctx_tiles, tile0):
    if not isinstance(x, tuple):
        return [pl.BlockSpec((tm, x.shape[1]), lambda i: (i + tile0, 0))], [x]
    ctx, lat = x
    specs, arrays = [], []
    if ctx is not None:
        specs.append(pl.BlockSpec((tm, ctx.shape[1]), lambda i: (jnp.minimum(i + tile0, n_ctx_tiles - 1), 0)))
        arrays.append(ctx)
    else:
        assert tile0 >= n_ctx_tiles
    specs.append(pl.BlockSpec((tm, lat.shape[1]), lambda i: (jnp.maximum(i + tile0 - n_ctx_tiles, 0), 0)))
    arrays.append(lat)
    return specs, arrays


def _load_tokens(refs, is_ctx):
    if len(refs) == 2:
        return jnp.where(is_ctx, refs[0][...], refs[1][...])
    return refs[0][...]


def _norm_modulate(x, g, shift, scale):
    ms = jnp.mean(x * x, axis=-1, keepdims=True)
    y = x * lax.rsqrt(ms + EPS) * g
    return y * (1.0 + scale) + shift


def _premix_kernel(*refs, n_x, n_ctx_tiles, tiles_per_batch):
    x_refs = refs[:n_x]
    (mod_ref, g_ref, wt_ref, wf_ref, gq_ref, gk_ref, cos_ref, sin_ref,
     q_ref, k_ref, v_ref, f_ref) = refs[n_x:]
    i = pl.program_id(0)
    row = _mod_row(i, n_ctx_tiles, tiles_per_batch)
    x = _load_tokens(x_refs, i < n_ctx_tiles)
    h = _norm_modulate(x, g_ref[...], _mod_vec(mod_ref, row, 0), _mod_vec(mod_ref, row, 1))
    hb = h.astype(BF16)
    f_ref[...] = jnp.dot(hb, wf_ref[...], preferred_element_type=F32).astype(BF16)
    pt = lax.dot_general(wt_ref[...], hb, (((1,), (1,)), ((), ())), preferred_element_type=F32)
    v_ref[...] = pt[ATTN_WIDTH + KV_WIDTH:, :].astype(BF16)
    cos = cos_ref[...]
    sin = sin_ref[...]

    def norm_rope(xh, gain):
        ms = jnp.mean(xh * xh, axis=0, keepdims=True)
        y = xh * lax.rsqrt(ms + EPS) * gain
        half = ROT_PER_AXIS // 2
        swapped = jnp.concatenate(
            [y[half:2 * half], y[0:half], y[3 * half:4 * half], y[2 * half:3 * half]], axis=0)
        return y * cos + swapped * sin

    gq = gq_ref[...]
    gk = gk_ref[...]
    for hh in range(N_Q_HEADS):
        q_ref[hh * HEAD_DIM:(hh + 1) * HEAD_DIM, :] = norm_rope(
            pt[hh * HEAD_DIM:(hh + 1) * HEAD_DIM, :], gq).astype(BF16)
    for hh in range(N_KV_HEADS):
        lo = ATTN_WIDTH + hh * HEAD_DIM
        k_ref[hh * HEAD_DIM:(hh + 1) * HEAD_DIM, :] = norm_rope(pt[lo:lo + HEAD_DIM, :], gk).astype(BF16)


def _premix(x, mod_l, g_mix, wt, wf, gq, gk, cos_t, sin_t, *, n_ctx, seq, t):
    d = D_MODEL
    tm = TM_MIX
    n_ctx_tiles = n_ctx // tm
    tpb = seq // tm

    def tab_idx(i):
        lat = jnp.maximum(i - n_ctx_tiles, 0)
        return (0, jnp.where(i < n_ctx_tiles, 0, lax.rem(lat, tpb) + 1))

    full = lambda shape: pl.BlockSpec(shape, lambda i: (0,) * len(shape))
    x_specs, x_arrays = _token_specs(x, tm, n_ctx_tiles, 0)
    return pl.pallas_call(
        functools.partial(_premix_kernel, n_x=len(x_arrays), n_ctx_tiles=n_ctx_tiles, tiles_per_batch=tpb),
        grid=(t // tm,),
        in_specs=x_specs + [
            full(mod_l.shape),
            full((1, d)),
            full(wt.shape),
            full(wf.shape),
            full(gq.shape),
            full(gk.shape),
            pl.BlockSpec((HEAD_DIM, tm), tab_idx),
            pl.BlockSpec((HEAD_DIM, tm), tab_idx),
        ],
        out_specs=[
            pl.BlockSpec((ATTN_WIDTH, tm), lambda i: (0, i)),
            pl.BlockSpec((KV_WIDTH, tm), lambda i: (0, i)),
            pl.BlockSpec((KV_WIDTH, tm), lambda i: (0, i)),
            pl.BlockSpec((tm, FOURIER_WIDTH), lambda i: (i, 0)),
        ],
        out_shape=[
            jax.ShapeDtypeStruct((ATTN_WIDTH, t), BF16),
            jax.ShapeDtypeStruct((KV_WIDTH, t), BF16),
            jax.ShapeDtypeStruct((KV_WIDTH, t), BF16),
            jax.ShapeDtypeStruct((t, FOURIER_WIDTH), BF16),
        ],
        compiler_params=_cparams(1),
        name="premix_project",
    )(*x_arrays, mod_l, g_mix, wt, wf, gq, gk, cos_t, sin_t)


def _attn_kernel(*refs, n_key_blocks, tq):
    q_ref = refs[0]
    k_refs = refs[1:1 + n_key_blocks]
    v_refs = refs[1 + n_key_blocks:1 + 2 * n_key_blocks]
    o_ref = refs[1 + 2 * n_key_blocks]
    kall_ref, vg_ref, ot_ref = refs[2 + 2 * n_key_blocks:]
    g = pl.program_id(1)
    qt = pl.program_id(2)
    n_keys = kall_ref.shape[0]
    kc = KEY_CHUNK if n_keys % KEY_CHUNK == 0 else n_keys
    n_chunks = n_keys // kc
    slabs = kc // 8

    @pl.when(jnp.logical_and(g == 0, qt == 0))
    def _():
        off = 0
        for kr in k_refs:
            n = kr.shape[1]
            kall_ref[off:off + n, :] = kr[...].astype(F32).T.astype(BF16)
            off += n

    @pl.when(qt == 0)
    def _():
        g_rows = pl.ds(pl.multiple_of(g * HEAD_DIM, HEAD_DIM), HEAD_DIM)
        off = 0
        for vr in v_refs:
            n = vr.shape[1]
            vg_ref[0:HEAD_DIM, off:off + n] = vr[g_rows, :]
            off += n
        vg_ref[HEAD_DIM:, :] = jnp.ones((ONES_ROWS, n_keys), BF16)

    row_group = lax.broadcasted_iota(jnp.int32, (KV_WIDTH, tq), 0) // HEAD_DIM

    def masked_q(hh):
        qh = q_ref[hh * HEAD_DIM:(hh + 1) * HEAD_DIM, :]
        q2 = jnp.concatenate([qh, qh], axis=0)
        return jnp.where(row_group == g, q2, jnp.zeros_like(q2))

    q2s = [masked_q(hh) for hh in range(KV_REP)]
    items = [(hh, c) for hh in range(KV_REP) for c in range(n_chunks)]

    def score(item):
        hh, c = item
        return jnp.dot(kall_ref[c * kc:(c + 1) * kc, :], q2s[hh], preferred_element_type=F32)

    pending = [score(it) for it in items[:SCORE_LOOKAHEAD]]
    m = ot = None
    for i, (hh, c) in enumerate(items):
        if c == 0:
            m = jnp.full((1, tq), -jnp.inf, F32)
            ot = jnp.zeros((HEAD_DIM + ONES_ROWS, tq), F32)
        s = pending.pop(0).reshape(slabs, 8, tq)
        if i + SCORE_LOOKAHEAD < len(items):
            pending.append(score(items[i + SCORE_LOOKAHEAD]))
        m_new = jnp.maximum(m, jnp.max(jnp.max(s, axis=0), axis=0, keepdims=True))
        pb = jnp.exp2(s - m_new[None]).reshape(kc, tq).astype(BF16)
        rows = slice(c * kc, (c + 1) * kc)
        ot = jnp.exp2(m - m_new) * ot + jnp.dot(vg_ref[:, rows], pb, preferred_element_type=F32)
        m = m_new
        if c == n_chunks - 1:
            ot_ref[hh * HEAD_DIM:(hh + 1) * HEAD_DIM, :] = ot[:HEAD_DIM] / ot[HEAD_DIM:HEAD_DIM + 1]
    o_ref[...] = ot_ref[...].T.astype(BF16)


def _attention(q_t, k_t, v_t, *, q_tile0, n_q_tiles, key_blocks, tq, n_batch, name):
    n_keys = sum(c for c, _ in key_blocks)
    nkb = len(key_blocks)
    q_spec = pl.BlockSpec((KV_REP * HEAD_DIM, tq),
                          lambda b, g, i: (g, q_tile0 + b * n_q_tiles + i))
    k_specs = [pl.BlockSpec((KV_WIDTH, c), (lambda f: (lambda b, g, i: (0, f(b))))(f))
               for c, f in key_blocks]
    return pl.pallas_call(
        functools.partial(_attn_kernel, n_key_blocks=nkb, tq=tq),
        grid=(n_batch, N_KV_HEADS, n_q_tiles),
        in_specs=[q_spec] + k_specs + k_specs,
        out_specs=pl.BlockSpec((tq, KV_REP * HEAD_DIM),
                               lambda b, g, i: (b * n_q_tiles + i, g)),
        out_shape=jax.ShapeDtypeStruct((n_batch * n_q_tiles * tq, ATTN_WIDTH), BF16),
        scratch_shapes=[pltpu.VMEM((n_keys, KV_WIDTH), BF16),
                        pltpu.VMEM((HEAD_DIM + ONES_ROWS, n_keys), BF16),
                        pltpu.VMEM((KV_REP * HEAD_DIM, tq), F32)],
        compiler_params=_cparams(3),
        name=name,
    )(q_t, *([k_t] * nkb), *([v_t] * nkb))


def _fourier_kernel(c_ref, s_ref, x_ref, cc_ref, sc_ref, w_ref, o_ref, *, norm):
    x = x_ref[...]
    a = jnp.dot(c_ref[...], x, preferred_element_type=F32).astype(BF16)
    b = jnp.dot(s_ref[...], x, preferred_element_type=F32).astype(BF16)
    cc = cc_ref[...]
    sc = sc_ref[...]
    groups = [slice(grp * FOURIER_GROUP, (grp + 1) * FOURIER_GROUP) for grp in range(N_FOURIER_GROUPS)]
    specs = [(jnp.dot(a[:, sl], cc, preferred_element_type=F32)
              - jnp.dot(b[:, sl], sc, preferred_element_type=F32)) * norm for sl in groups]
    for grp, sl in enumerate(groups):
        o_ref[:, sl] = jnp.dot(specs[grp].astype(BF16), w_ref[grp], preferred_element_type=F32).astype(BF16)


def _fourier(f, cmat, smat, cc, sc, w_four, *, n, tr, x_block0, n_batch, name):
    n_row_tiles = n // tr
    return pl.pallas_call(
        functools.partial(_fourier_kernel, norm=float(1.0 / np.sqrt(n * FOURIER_GROUP))),
        grid=(n_row_tiles, n_batch),
        in_specs=[
            pl.BlockSpec((tr, n), lambda i, b: (i, 0)),
            pl.BlockSpec((tr, n), lambda i, b: (i, 0)),
            pl.BlockSpec((n, FOURIER_WIDTH), lambda i, b: (x_block0 + b, 0)),
            pl.BlockSpec(cc.shape, lambda i, b: (0, 0)),
            pl.BlockSpec(sc.shape, lambda i, b: (0, 0)),
            pl.BlockSpec(w_four.shape, lambda i, b: (0, 0, 0)),
        ],
        out_specs=pl.BlockSpec((tr, FOURIER_WIDTH), lambda i, b: (b * n_row_tiles + i, 0)),
        out_shape=jax.ShapeDtypeStruct((n_batch * n, FOURIER_WIDTH), BF16),
        compiler_params=_cparams(2),
        name=name,
    )(cmat, smat, f, cc, sc, w_four)


def _postmix_kernel(*refs, n_x, n_a, n_f, tile0, n_ctx_tiles, tiles_per_batch, with_router):
    x_refs, refs = refs[:n_x], refs[n_x:]
    a_refs, refs = refs[:n_a], refs[n_a:]
    f_refs, refs = refs[:n_f], refs[n_f:]
    mod_ref, g_ref, woa_ref, wof_ref = refs[:4]
    if with_router:
        wr_ref, br_ref, x1_ref, h_ref, rt_ref, cnt_ref = refs[4:]
    else:
        x1_ref, h_ref = refs[4:]
    i = pl.program_id(0) + tile0
    row = _mod_row(i, n_ctx_tiles, tiles_per_batch)
    is_ctx = i < n_ctx_tiles
    a = _load_tokens(a_refs, is_ctx)
    f = _load_tokens(f_refs, is_ctx)
    mix = (jnp.dot(a, woa_ref[...], preferred_element_type=F32)
           + jnp.dot(f, wof_ref[...], preferred_element_type=F32))
    x1 = _load_tokens(x_refs, is_ctx) + _mod_vec(mod_ref, row, 2) * mix
    x1_ref[...] = x1
    h = _norm_modulate(x1, g_ref[...], _mod_vec(mod_ref, row, 3), _mod_vec(mod_ref, row, 4))
    hb = h.astype(BF16)
    if with_router:
        h_ref[...] = _pack_bf16_pairs(hb)
        logits = jnp.dot(hb, wr_ref[...], preferred_element_type=F32) + br_ref[...]
        _top2_route(logits, rt_ref, cnt_ref, first=pl.program_id(0) == 0)
    else:
        h_ref[...] = hb


def _top2_route(logits, rt_ref, cnt_ref, first):
    tm, lanes = logits.shape
    lane = lax.broadcasted_iota(jnp.int32, (tm, lanes), 1)
    neg = jnp.float32(-jnp.inf)
    m1 = jnp.max(logits, axis=1, keepdims=True)
    i1 = jnp.min(jnp.where(logits == m1, lane, lanes), axis=1, keepdims=True)
    rest = jnp.where(lane == i1, neg, logits)
    m2 = jnp.max(rest, axis=1, keepdims=True)
    i2 = jnp.min(jnp.where(rest == m2, lane, lanes), axis=1, keepdims=True)
    e = jnp.exp(m2 - m1)
    g1 = 1.0 / (1.0 + e)
    g2 = e / (1.0 + e)
    pick1 = lane == i1
    pick2 = lane == i2
    picked = jnp.logical_or(pick1, pick2).astype(F32)

    @pl.when(first)
    def _():
        cnt_ref[...] = jnp.zeros_like(cnt_ref)

    r_i = lax.broadcasted_iota(jnp.int32, (tm, tm), 0)
    c_i = lax.broadcasted_iota(jnp.int32, (tm, tm), 1)
    lower = jnp.where(c_i < r_i, 1.0, 0.0).astype(BF16)
    before = jnp.dot(lower, picked.astype(BF16), preferred_element_type=F32) + cnt_ref[0:1, :]
    r1 = jnp.sum(jnp.where(pick1, before, 0.0), axis=1, keepdims=True)
    r2 = jnp.sum(jnp.where(pick2, before, 0.0), axis=1, keepdims=True)
    cnt_ref[...] = cnt_ref[...] + jnp.sum(picked, axis=0, keepdims=True)
    out = jnp.zeros((tm, lanes), F32)
    for k, val in enumerate([i1.astype(F32), i2.astype(F32), g1, g2, r1, r2]):
        out = jnp.where(lane == k, val, out)
    rt_ref[...] = out


def _postmix(x, attn, four, mod_l, g_ffn, wo_a, wo_f, router, *, n_ctx, seq, t, tile0):
    d = D_MODEL
    tm = TM_MIX
    nct = n_ctx // tm
    t = t - tile0 * tm
    full = lambda shape: pl.BlockSpec(shape, lambda i: (0,) * len(shape))
    tile = lambda w: pl.BlockSpec((tm, w), lambda i: (i, 0))
    x_specs, x_arrays = _token_specs(x, tm, nct, tile0)
    a_specs, a_arrays = _token_specs(attn, tm, nct, tile0)
    f_specs, f_arrays = _token_specs(four, tm, nct, tile0)
    in_specs = x_specs + a_specs + f_specs + [full(mod_l.shape), full((1, d)), full(wo_a.shape),
                                              full(wo_f.shape)]
    if router is None:
        out_specs = [tile(d), tile(d)]
        out_shape = [jax.ShapeDtypeStruct((t, d), F32), jax.ShapeDtypeStruct((t, d), BF16)]
    else:
        out_specs = [tile(d), tile(d // 2)]
        out_shape = [jax.ShapeDtypeStruct((t, d), F32), jax.ShapeDtypeStruct((t, d // 2), jnp.uint32)]
    args = x_arrays + a_arrays + f_arrays + [mod_l, g_ffn, wo_a, wo_f]
    if router is not None:
        in_specs += [full(router[0].shape), full(router[1].shape)]
        out_specs += [tile(LOGIT_LANES), pl.BlockSpec((8, LOGIT_LANES), lambda i: (0, 0))]
        out_shape += [jax.ShapeDtypeStruct((t, LOGIT_LANES), F32), jax.ShapeDtypeStruct((8, LOGIT_LANES), F32)]
        args += list(router)
    return pl.pallas_call(
        functools.partial(_postmix_kernel, n_x=len(x_arrays), n_a=len(a_arrays), n_f=len(f_arrays),
                          tile0=tile0, n_ctx_tiles=nct, tiles_per_batch=seq // tm,
                          with_router=router is not None),
        grid=(t // tm,),
        in_specs=in_specs,
        out_specs=out_specs,
        out_shape=out_shape,
        compiler_params=_cparams(1),
        name="postmix_wo_norm",
    )(*args)


def _silu_mul(gate, up):
    return (gate / (1.0 + jnp.exp(-gate))) * up


def _swiglu_chunks(x, n_chunks, wg_of, wu_of, wd_of, acc_ref):
    gate = jnp.dot(x, wg_of(0), preferred_element_type=F32)
    up = jnp.dot(x, wu_of(0), preferred_element_type=F32)
    for c in range(n_chunks):
        act = _silu_mul(gate, up).astype(BF16)
        if c + 1 < n_chunks:
            gate = jnp.dot(x, wg_of(c + 1), preferred_element_type=F32)
            up = jnp.dot(x, wu_of(c + 1), preferred_element_type=F32)
        acc_ref[...] += jnp.dot(act, wd_of(c), preferred_element_type=F32)


def _dense_ffn_kernel(h_ref, x1_ref, mod_ref, wg_ref, wu_ref, wd_ref, o_ref, acc_ref, *,
                      tile0, n_ctx_tiles, tiles_per_batch):
    i = pl.program_id(0) + tile0
    row = _mod_row(i, n_ctx_tiles, tiles_per_batch)
    acc_ref[...] = jnp.zeros_like(acc_ref)
    _swiglu_chunks(h_ref[...], wg_ref.shape[0], lambda c: wg_ref[c], lambda c: wu_ref[c],
                   lambda c: wd_ref[c], acc_ref)
    o_ref[...] = x1_ref[...] + _mod_vec(mod_ref, row, 5) * acc_ref[...]


def _dense_ffn(h, x1, mod_l, wg, wu, wd, *, n_ctx, seq, tile0):
    t, d = x1.shape
    tm = TM_DENSE
    full = lambda shape: pl.BlockSpec(shape, lambda i: (0,) * len(shape))
    tile = lambda: pl.BlockSpec((tm, d), lambda i: (i, 0))
    return pl.pallas_call(
        functools.partial(_dense_ffn_kernel, tile0=tile0 * TM_MIX // tm, n_ctx_tiles=n_ctx // tm,
                          tiles_per_batch=seq // tm),
        grid=(t // tm,),
        in_specs=[tile(), tile(), full(mod_l.shape), full(wg.shape), full(wu.shape), full(wd.shape)],
        out_specs=tile(),
        out_shape=jax.ShapeDtypeStruct((t, d), F32),
        scratch_shapes=[pltpu.VMEM((tm, d), F32)],
        compiler_params=_cparams(1),
        name="dense_swiglu",
    )(h, x1, mod_l, wg, wu, wd)


def _moe_ffn_kernel(be_ref, nv_ref, x_ref, wg_ref, wu_ref, wd_ref, o_ref, xb_ref, wgb_ref, wub_ref, wdb_ref):
    i = pl.program_id(0)
    j = pl.program_id(1)
    n_valid = nv_ref[i]

    @pl.when(j == 0)
    def _():
        x = _unpack_bf16_pairs(x_ref[...])
        rows = lax.broadcasted_iota(jnp.int32, x.shape, 0)
        xb_ref[...] = jnp.where(rows < n_valid, x, jnp.zeros_like(x))
        o_ref[...] = jnp.zeros_like(o_ref)

    @pl.when(n_valid > 0)
    def _():
        wgb_ref[...] = wg_ref[0, 0].astype(BF16)
        wub_ref[...] = wu_ref[0, 0].astype(BF16)
        wdb_ref[...] = wd_ref[0, 0].astype(BF16)

    sub = lambda c: slice(c * FF_CHUNK, (c + 1) * FF_CHUNK)
    for r in range(TM_MOE // TM_PASS):
        rows = slice(r * TM_PASS, (r + 1) * TM_PASS)

        @pl.when(n_valid > r * TM_PASS)
        def _():
            _swiglu_chunks(xb_ref[rows, :], TF_MOE // FF_CHUNK,
                           lambda c: wgb_ref[:, sub(c)], lambda c: wub_ref[:, sub(c)],
                           lambda c: wdb_ref[sub(c), :], o_ref.at[rows, :])


def _moe_ffn(xs, block_expert, n_valid, wg, wu, wd, layer):
    n_rows = xs.shape[0]
    d = D_MODEL
    tm, tf = TM_MOE, TF_MOE
    n_blocks = n_rows // tm
    n_ff = wg.shape[3] // tf

    def ff_idx(j, nv, i):
        return jnp.where(nv[i] > 0, j, n_ff - 1)

    grid_spec = pltpu.PrefetchScalarGridSpec(
        num_scalar_prefetch=2,
        grid=(n_blocks, n_ff),
        in_specs=[
            pl.BlockSpec((tm, d // 2), lambda i, j, be, nv: (i, 0)),
            pl.BlockSpec((1, 1, d, tf), lambda i, j, be, nv: (layer, be[i], 0, ff_idx(j, nv, i))),
            pl.BlockSpec((1, 1, d, tf), lambda i, j, be, nv: (layer, be[i], 0, ff_idx(j, nv, i))),
            pl.BlockSpec((1, 1, tf, d), lambda i, j, be, nv: (layer, be[i], ff_idx(j, nv, i), 0)),
        ],
        out_specs=pl.BlockSpec((tm, d), lambda i, j, be, nv: (i, 0)),
        scratch_shapes=[pltpu.VMEM((tm, d), BF16), pltpu.VMEM((d, tf), BF16), pltpu.VMEM((d, tf), BF16),
                        pltpu.VMEM((tf, d), BF16)],
    )
    return pl.pallas_call(
        _moe_ffn_kernel,
        grid_spec=grid_spec,
        out_shape=jax.ShapeDtypeStruct((n_rows, d), F32),
        compiler_params=_cparams(2),
        name="expert_swiglu",
    )(block_expert, n_valid, xs, wg, wu, wd)


def _sc_mesh():
    return plsc.VectorSubcoreMesh(core_axis_name="c", subcore_axis_name="s")


def _sc_params():
    return pltpu.CompilerParams(use_tc_tiling_on_sc=True)


def _sc_dispatch(h_packed, dest, n_rows):
    t, w = h_packed.shape
    win = SC_DISPATCH_ROWS
    n_win = t // win
    idx = [dest[:, k].reshape(n_win, 1, win) for k in range(TOP_K)]

    @functools.partial(
        pl.kernel, out_type=jax.ShapeDtypeStruct((n_rows, w), h_packed.dtype), mesh=_sc_mesh(),
        scratch_types=[], compiler_params=_sc_params(), name="expert_dispatch_scatter")
    def run(x_hbm, i0_hbm, i1_hbm, o_hbm):
        def body(x_vmem, i0_vmem, i1_vmem):
            pltpu.sync_copy(x_vmem, o_hbm.at[i0_vmem.at[0, 0]])
            pltpu.sync_copy(x_vmem, o_hbm.at[i1_vmem.at[0, 0]])

        idx_spec = pl.BlockSpec((1, 1, win), lambda i: (i, 0, 0))
        pltpu.emit_pipeline(
            body, grid=(n_win,),
            in_specs=[pl.BlockSpec((win, w), lambda i: (i, 0)), idx_spec, idx_spec],
            out_specs=[], core_axis_name=("c", "s"), dimension_semantics=(pltpu.PARALLEL,),
        )(x_hbm, i0_hbm, i1_hbm)

    return run(h_packed, *idx)


def _sc_gather(ys, idx_flat):
    w = ys.shape[1]
    n = idx_flat.shape[0]
    win = SC_GATHER_ROWS
    n_win = n // win

    @functools.partial(
        pl.kernel, out_type=jax.ShapeDtypeStruct((n, w), ys.dtype), mesh=_sc_mesh(),
        scratch_types=[], compiler_params=_sc_params(), name="expert_combine_gather")
    def run(y_hbm, i_hbm, o_hbm):
        def body(i_vmem, o_vmem):
            pltpu.sync_copy(y_hbm.at[i_vmem.at[0, 0]], o_vmem)

        pltpu.emit_pipeline(
            body, grid=(n_win,),
            in_specs=[pl.BlockSpec((1, 1, win), lambda i: (i, 0, 0))],
            out_specs=[pl.BlockSpec((win, w), lambda i: (i, 0))],
            core_axis_name=("c", "s"), dimension_semantics=(pltpu.PARALLEL,),
        )(i_hbm, o_hbm)

    return run(ys, idx_flat.reshape(n_win, 1, win))


def _moe_combine_kernel(x1_ref, y0_ref, y1_ref, g_ref, mod_ref, o_ref, *, tile0, n_ctx_tiles,
                        tiles_per_batch):
    i = pl.program_id(0) + tile0
    row = _mod_row(i, n_ctx_tiles, tiles_per_batch)
    gates = g_ref[...]
    y = y0_ref[...] * gates[:, 2:3] + y1_ref[...] * gates[:, 3:4]
    o_ref[...] = x1_ref[...] + _mod_vec(mod_ref, row, 5) * y


def _moe_combine(x1, y_sel, gates, mod_l, *, n_ctx, seq, tile0):
    t, d = x1.shape
    tm = TM_MIX
    n_tiles = t // tm
    tile = lambda w: pl.BlockSpec((tm, w), lambda i: (i, 0))
    return pl.pallas_call(
        functools.partial(_moe_combine_kernel, tile0=tile0, n_ctx_tiles=n_ctx // tm,
                          tiles_per_batch=seq // tm),
        grid=(n_tiles,),
        in_specs=[tile(d), tile(d), pl.BlockSpec((tm, d), lambda i: (n_tiles + i, 0)), tile(LOGIT_LANES),
                  pl.BlockSpec(mod_l.shape, lambda i: (0, 0))],
        out_specs=tile(d),
        out_shape=jax.ShapeDtypeStruct((t, d), F32),
        compiler_params=_cparams(1),
        name="expert_combine",
    )(x1, y_sel, y_sel, gates, mod_l)


def _route(route, counts):
    t = route.shape[0]
    expert = route[:, 0:TOP_K].astype(jnp.int32)
    rank = route[:, 4:4 + TOP_K].astype(jnp.int32)
    counts = counts[0, :N_EXPERTS].astype(jnp.int32)
    n_assign = t * TOP_K
    padded = ((counts + TM_MOE - 1) // TM_MOE) * TM_MOE
    pend = jnp.cumsum(padded)
    pstart = pend - padded
    onehot = expert[:, :, None] == jnp.arange(N_EXPERTS, dtype=jnp.int32)[None, None, :]
    dest = jnp.sum(jnp.where(onehot, pstart[None, None, :], 0), axis=-1) + rank
    n_blocks = -(-n_assign // TM_MOE) + N_EXPERTS
    block_start = jnp.arange(n_blocks, dtype=jnp.int32) * TM_MOE
    block_expert = jnp.minimum(jnp.sum(pend[None, :] <= block_start[:, None], axis=1),
                               N_EXPERTS - 1).astype(jnp.int32)
    n_valid = jnp.clip(counts[block_expert] - (block_start - pstart[block_expert]), 0, TM_MOE)
    n_valid = jnp.where(block_start < pend[-1], n_valid, 0).astype(jnp.int32)
    return dest.astype(jnp.int32), block_expert, n_valid, n_blocks * TM_MOE


def kernel(x, c, ctx, c_ctx, w_mod, b_mod, g_mix, g_ffn, g_q, g_k, w_in, w_four, w_o,
           w_gate_dense, w_up_dense, w_down_dense, w_router, b_router,
           w_gate_moe, w_up_moe, w_down_moe):
    b, s, d = x.shape
    n_ctx_len = ctx.shape[1]
    n_ctx = b * n_ctx_len
    t = n_ctx + b * s
    assert d == D_MODEL and b + 1 <= MOD_ROWS
    assert n_ctx % TM_DENSE == 0 and s % TM_DENSE == 0 and n_ctx_len % 128 == 0 and n_ctx % s == 0

    cvec = jnp.zeros((MOD_ROWS, d), F32).at[0].set(c_ctx).at[1:b + 1].set(c)
    mod = _modulation(cvec, w_mod, b_mod)

    cos_t, sin_t = _rope_tables_t(s, TM_MIX)
    c_lat, s_lat = _dft_mats(s)
    c_ctx_m, s_ctx_m = _dft_mats(n_ctx_len)
    c_grp, s_grp = _dft_mats(FOURIER_GROUP)

    xa = (ctx.reshape(n_ctx, d), x.reshape(b * s, d))

    for l in range(DEPTH):
        mod_l = mod[l]
        w_in_l = w_in[l]
        wt = w_in_l[:, :QKV_WIDTH].T.astype(BF16)
        wf = w_in_l[:, QKV_WIDTH:].astype(BF16)
        gq = jnp.broadcast_to((g_q[l] * (LOG2_E * HEAD_DIM ** -0.5))[:, None], (HEAD_DIM, TM_MIX))
        gk = jnp.broadcast_to(g_k[l][:, None], (HEAD_DIM, TM_MIX))
        last = l == DEPTH - 1
        tile0 = n_ctx // TM_MIX if last else 0
        q_t, k_t, v_t, f = _premix(xa, mod_l, g_mix[l].reshape(1, d), wt, wf, gq, gk, cos_t, sin_t,
                                   n_ctx=n_ctx, seq=s, t=t)

        n_qt = s // TQ_ATTN
        lat_keys = [(n_ctx_len, lambda bb: bb), (s, lambda bb: n_ctx // s + bb)]
        attn_lat = _attention(q_t, k_t, v_t, q_tile0=n_ctx // TQ_ATTN, n_q_tiles=n_qt,
                              key_blocks=lat_keys, tq=TQ_ATTN, n_batch=b, name="attention_latent")
        attn_ctx = None if last else _attention(
            q_t, k_t, v_t, q_tile0=0, n_q_tiles=1, key_blocks=[(n_ctx_len, lambda bb: bb)],
            tq=n_ctx_len, n_batch=b, name="attention_context")

        wfour = w_four[l].astype(BF16)
        four_lat = _fourier(f, c_lat, s_lat, c_grp, s_grp, wfour, n=s, tr=TR_FOUR,
                            x_block0=n_ctx // s, n_batch=b, name="fourier_latent")
        four_ctx = None if last else _fourier(
            f, c_ctx_m, s_ctx_m, c_grp, s_grp, wfour, n=n_ctx_len, tr=n_ctx_len, x_block0=0, n_batch=b,
            name="fourier_context")

        wo = w_o[l].astype(BF16)
        is_moe = l % 2 == 1
        li = l // 2
        router = None
        if is_moe:
            wr = jnp.zeros((d, LOGIT_LANES), BF16).at[:, :N_EXPERTS].set(w_router[li].astype(BF16))
            br = jnp.full((1, LOGIT_LANES), -jnp.inf, F32).at[0, :N_EXPERTS].set(b_router[li])
            router = (wr, br)
        res = _postmix(xa, (attn_ctx, attn_lat), (four_ctx, four_lat), mod_l, g_ffn[l].reshape(1, d),
                       wo[:ATTN_WIDTH], wo[ATTN_WIDTH:], router, n_ctx=n_ctx, seq=s, t=t, tile0=tile0)
        t_l = t - tile0 * TM_MIX
        if not is_moe:
            x1, h2 = res
            xa = _dense_ffn(h2, x1, mod_l, _chunk_cols(w_gate_dense[li], FF_CHUNK),
                            _chunk_cols(w_up_dense[li], FF_CHUNK), _chunk_rows(w_down_dense[li], FF_CHUNK),
                            n_ctx=n_ctx, seq=s, tile0=tile0)
        else:
            x1, h2, route, counts = res
            dest, block_expert, n_valid, n_rows = _route(route, counts)
            xs = _sc_dispatch(h2, dest, n_rows)
            ys = _moe_ffn(xs, block_expert, n_valid, w_gate_moe, w_up_moe, w_down_moe, li)
            y_sel = _sc_gather(ys, dest.T.reshape(-1))
            xa = _moe_combine(x1, y_sel, route, mod_l, n_ctx=n_ctx, seq=s, tile0=tile0)

    return xa.reshape(b, s, d)
```

```python
import functools

import numpy as np
import jax
import jax.numpy as jnp
from jax import lax
from jax.experimental import pallas as pl
from jax.experimental.pallas import tpu as pltpu
from jax.experimental.pallas import tpu_sc as plsc

D_MODEL = 1024
DEPTH = 4
GRID_W = 64
HEAD_DIM = 64
ATTN_WIDTH = 512
N_Q_HEADS = 8
N_KV_HEADS = 2
KV_REP = 4
KV_WIDTH = 128
FOURIER_WIDTH = 512
N_FOURIER_GROUPS = 4
FOURIER_GROUP = 128
ROT_PER_AXIS = 32
ROPE_THETA = 10000.0
N_MOD = 6
D_FF_DENSE = 2816
N_EXPERTS = 8
TOP_K = 2
D_FF_EXPERT = 3584
EPS = 1e-6

QKV_WIDTH = ATTN_WIDTH + 2 * KV_WIDTH
MOD_ROWS = 16
LOGIT_LANES = 128

TM_MIX = 512
TQ_ATTN = 512
KEY_CHUNK = 256
ONES_ROWS = 16
SCORE_LOOKAHEAD = 2
LOG2_E = 1.4426950408889634
TR_FOUR = 512
TM_DENSE = 1024
FF_CHUNK = 256
TM_MOE = 2048
TM_PASS = 1024
TAIL_ROWS = 256
TF_MOE = 512
SC_DISPATCH_ROWS = 64
SC_GATHER_ROWS = 32
VMEM_LIMIT = 56 * 1024 * 1024

F32 = jnp.float32
BF16 = jnp.bfloat16


def _cparams(n_axes, flags=None):
    return pltpu.CompilerParams(
        dimension_semantics=("arbitrary",) * n_axes, vmem_limit_bytes=VMEM_LIMIT, flags=flags)


def _rope_tables_t(s, tm):
    n_rows = s // GRID_W
    rows = np.repeat(np.arange(n_rows), GRID_W).astype(np.float64)
    cols = np.tile(np.arange(GRID_W), n_rows).astype(np.float64)
    inv_freq = (ROPE_THETA ** (-np.arange(0, ROT_PER_AXIS, 2, dtype=np.float32) / ROT_PER_AXIS)
                ).astype(np.float32).astype(np.float64)
    ang_r = (rows[None, :].astype(np.float32) * inv_freq[:, None].astype(np.float32)).astype(np.float64)
    ang_c = (cols[None, :].astype(np.float32) * inv_freq[:, None].astype(np.float32)).astype(np.float64)
    cos = np.concatenate([np.cos(ang_r), np.cos(ang_r), np.cos(ang_c), np.cos(ang_c)], axis=0)
    sin = np.concatenate([-np.sin(ang_r), np.sin(ang_r), -np.sin(ang_c), np.sin(ang_c)], axis=0)
    cos = np.concatenate([np.ones((HEAD_DIM, tm)), cos], axis=1)
    sin = np.concatenate([np.zeros((HEAD_DIM, tm)), sin], axis=1)
    return jnp.asarray(cos, F32), jnp.asarray(sin, F32)


def _dft_mats(n):
    k = np.arange(n, dtype=np.int64)
    ang = 2.0 * np.pi * ((k[:, None] * k[None, :]) % n).astype(np.float64) / n
    return jnp.asarray(np.cos(ang), BF16), jnp.asarray(np.sin(ang), BF16)


def _chunk_cols(w, chunk):
    *lead, d, n = w.shape
    w = w.astype(BF16).reshape(*lead, d, n // chunk, chunk)
    return jnp.swapaxes(w, -3, -2)


def _chunk_rows(w, chunk):
    *lead, n, d = w.shape
    return w.astype(BF16).reshape(*lead, n // chunk, chunk, d)


def _mod_kernel(c_ref, w_ref, b_ref, o_ref):
    c = c_ref[...]
    s = (c / (1.0 + jnp.exp(-c))).astype(BF16)
    w = w_ref[0].astype(BF16)
    o_ref[0] = jnp.dot(s, w, preferred_element_type=F32) + b_ref[0]


def _modulation(cvec, w_mod, b_mod):
    depth, d, n = w_mod.shape
    tn = 1536
    return pl.pallas_call(
        _mod_kernel,
        grid=(depth, n // tn),
        in_specs=[
            pl.BlockSpec((MOD_ROWS, d), lambda l, j: (0, 0)),
            pl.BlockSpec((1, d, tn), lambda l, j: (l, 0, j)),
            pl.BlockSpec((1, 1, tn), lambda l, j: (l, 0, j)),
        ],
        out_specs=pl.BlockSpec((1, MOD_ROWS, tn), lambda l, j: (l, 0, j)),
        out_shape=jax.ShapeDtypeStruct((depth, MOD_ROWS, n), F32),
        compiler_params=_cparams(2),
        name="adaln_vectors",
    )(cvec, w_mod, b_mod.reshape(depth, 1, n))


def _mod_row(i, n_ctx_tiles, tiles_per_batch):
    lat = jnp.maximum(i - n_ctx_tiles, 0)
    return jnp.where(i < n_ctx_tiles, 0, lat // tiles_per_batch + 1)


def _mod_vec(mod_ref, row, comp):
    return mod_ref[pl.ds(row, 1), comp * D_MODEL:(comp + 1) * D_MODEL]


def _pack_bf16_pairs(xb):
    n = xb.shape[1] // 2
    bits = lax.bitcast_convert_type(xb.astype(F32), jnp.uint32)
    return (bits[:, :n] >> 16) | (bits[:, n:] & jnp.uint32(0xFFFF0000))


def _unpack_bf16_pairs(w):
    lo = lax.bitcast_convert_type(w << 16, F32)
    hi = lax.bitcast_convert_type(w & jnp.uint32(0xFFFF0000), F32)
    return jnp.concatenate([lo, hi], axis=1).astype(BF16)


def _token_specs(x, tm, n_ctx_tiles, tile0):
    if not isinstance(x, tuple):
        return [pl.BlockSpec((tm, x.shape[1]), lambda i: (i + tile0, 0))], [x]
    ctx, lat = x
    specs, arrays = [], []
    if ctx is not None:
        specs.append(pl.BlockSpec((tm, ctx.shape[1]), lambda i: (jnp.minimum(i + tile0, n_ctx_tiles - 1), 0)))
        arrays.append(ctx)
    else:
        assert tile0 >= n_ctx_tiles
    specs.append(pl.BlockSpec((tm, lat.shape[1]), lambda i: (jnp.maximum(i + tile0 - n_ctx_tiles, 0), 0)))
    arrays.append(lat)
    return specs, arrays


def _load_tokens(refs, is_ctx):
    if len(refs) == 2:
        return jnp.where(is_ctx, refs[0][...], refs[1][...])
    return refs[0][...]


def _norm_modulate(x, g, shift, scale):
    ms = jnp.mean(x * x, axis=-1, keepdims=True)
    y = x * lax.rsqrt(ms + EPS) * g
    return y * (1.0 + scale) + shift


def _premix_kernel(*refs, n_x, n_ctx_tiles, tiles_per_batch):
    x_refs = refs[:n_x]
    (mod_ref, g_ref, wt_ref, wf_ref, gq_ref, gk_ref, cos_ref, sin_ref,
     q_ref, k_ref, v_ref, f_ref) = refs[n_x:]
    i = pl.program_id(0)
    row = _mod_row(i, n_ctx_tiles, tiles_per_batch)
    x = _load_tokens(x_refs, i < n_ctx_tiles)
    h = _norm_modulate(x, g_ref[...], _mod_vec(mod_ref, row, 0), _mod_vec(mod_ref, row, 1))
    hb = h.astype(BF16)
    f_ref[...] = jnp.dot(hb, wf_ref[...], preferred_element_type=F32).astype(BF16)
    pt = lax.dot_general(wt_ref[...], hb, (((1,), (1,)), ((), ())), preferred_element_type=F32)
    v_ref[...] = pt[ATTN_WIDTH + KV_WIDTH:, :].astype(BF16)
    cos = cos_ref[...]
    sin = sin_ref[...]

    def norm_rope(xh, gain):
        ms = jnp.mean(xh * xh, axis=0, keepdims=True)
        y = xh * lax.rsqrt(ms + EPS) * gain
        half = ROT_PER_AXIS // 2
        swapped = jnp.concatenate(
            [y[half:2 * half], y[0:half], y[3 * half:4 * half], y[2 * half:3 * half]], axis=0)
        return y * cos + swapped * sin

    gq = gq_ref[...]
    gk = gk_ref[...]
    for hh in range(N_Q_HEADS):
        q_ref[hh * HEAD_DIM:(hh + 1) * HEAD_DIM, :] = norm_rope(
            pt[hh * HEAD_DIM:(hh + 1) * HEAD_DIM, :], gq).astype(BF16)
    for hh in range(N_KV_HEADS):
        lo = ATTN_WIDTH + hh * HEAD_DIM
        k_ref[hh * HEAD_DIM:(hh + 1) * HEAD_DIM, :] = norm_rope(pt[lo:lo + HEAD_DIM, :], gk).astype(BF16)


def _premix(x, mod_l, g_mix, wt, wf, gq, gk, cos_t, sin_t, *, n_ctx, seq, t):
    d = D_MODEL
    tm = TM_MIX
    n_ctx_tiles = n_ctx // tm
    tpb = seq // tm

    def tab_idx(i):
        lat = jnp.maximum(i - n_ctx_tiles, 0)
        return (0, jnp.where(i < n_ctx_tiles, 0, lax.rem(lat, tpb) + 1))

    full = lambda shape: pl.BlockSpec(shape, lambda i: (0,) * len(shape))
    x_specs, x_arrays = _token_specs(x, tm, n_ctx_tiles, 0)
    return pl.pallas_call(
        functools.partial(_premix_kernel, n_x=len(x_arrays), n_ctx_tiles=n_ctx_tiles, tiles_per_batch=tpb),
        grid=(t // tm,),
        in_specs=x_specs + [
            full(mod_l.shape),
            full((1, d)),
            full(wt.shape),
            full(wf.shape),
            full(gq.shape),
            full(gk.shape),
            pl.BlockSpec((HEAD_DIM, tm), tab_idx),
            pl.BlockSpec((HEAD_DIM, tm), tab_idx),
        ],
        out_specs=[
            pl.BlockSpec((ATTN_WIDTH, tm), lambda i: (0, i)),
            pl.BlockSpec((KV_WIDTH, tm), lambda i: (0, i)),
            pl.BlockSpec((KV_WIDTH, tm), lambda i: (0, i)),
            pl.BlockSpec((tm, FOURIER_WIDTH), lambda i: (i, 0)),
        ],
        out_shape=[
            jax.ShapeDtypeStruct((ATTN_WIDTH, t), BF16),
            jax.ShapeDtypeStruct((KV_WIDTH, t), BF16),
            jax.ShapeDtypeStruct((KV_WIDTH, t), BF16),
            jax.ShapeDtypeStruct((t, FOURIER_WIDTH), BF16),
        ],
        compiler_params=_cparams(1),
        name="premix_project",
    )(*x_arrays, mod_l, g_mix, wt, wf, gq, gk, cos_t, sin_t)


def _attn_kernel(*refs, n_key_blocks, tq):
    q_ref = refs[0]
    k_refs = refs[1:1 + n_key_blocks]
    v_refs = refs[1 + n_key_blocks:1 + 2 * n_key_blocks]
    o_ref = refs[1 + 2 * n_key_blocks]
    kall_ref, vg_ref, ot_ref = refs[2 + 2 * n_key_blocks:]
    g = pl.program_id(1)
    qt = pl.program_id(2)
    n_keys = kall_ref.shape[0]
    kc = KEY_CHUNK if n_keys % KEY_CHUNK == 0 else n_keys
    n_chunks = n_keys // kc
    slabs = kc // 8

    @pl.when(jnp.logical_and(g == 0, qt == 0))
    def _():
        off = 0
        for kr in k_refs:
            n = kr.shape[1]
            kall_ref[off:off + n, :] = kr[...].astype(F32).T.astype(BF16)
            off += n

    @pl.when(qt == 0)
    def _():
        g_rows = pl.ds(pl.multiple_of(g * HEAD_DIM, HEAD_DIM), HEAD_DIM)
        off = 0
        for vr in v_refs:
            n = vr.shape[1]
            vg_ref[0:HEAD_DIM, off:off + n] = vr[g_rows, :]
            off += n
        vg_ref[HEAD_DIM:, :] = jnp.ones((ONES_ROWS, n_keys), BF16)

    row_group = lax.broadcasted_iota(jnp.int32, (KV_WIDTH, tq), 0) // HEAD_DIM

    def masked_q(hh):
        qh = q_ref[hh * HEAD_DIM:(hh + 1) * HEAD_DIM, :]
        q2 = jnp.concatenate([qh, qh], axis=0)
        return jnp.where(row_group == g, q2, jnp.zeros_like(q2))

    q2s = [masked_q(hh) for hh in range(KV_REP)]
    items = [(hh, c) for hh in range(KV_REP) for c in range(n_chunks)]

    def score(item):
        hh, c = item
        return jnp.dot(kall_ref[c * kc:(c + 1) * kc, :], q2s[hh], preferred_element_type=F32)

    pending = [score(it) for it in items[:SCORE_LOOKAHEAD]]
    m = ot = None
    for i, (hh, c) in enumerate(items):
        if c == 0:
            m = jnp.full((1, tq), -jnp.inf, F32)
            ot = jnp.zeros((HEAD_DIM + ONES_ROWS, tq), F32)
        s = pending.pop(0).reshape(slabs, 8, tq)
        if i + SCORE_LOOKAHEAD < len(items):
            pending.append(score(items[i + SCORE_LOOKAHEAD]))
        m_new = jnp.maximum(m, jnp.max(jnp.max(s, axis=0), axis=0, keepdims=True))
        pb = jnp.exp2(s - m_new[None]).reshape(kc, tq).astype(BF16)
        rows = slice(c * kc, (c + 1) * kc)
        ot = jnp.exp2(m - m_new) * ot + jnp.dot(vg_ref[:, rows], pb, preferred_element_type=F32)
        m = m_new
        if c == n_chunks - 1:
            ot_ref[hh * HEAD_DIM:(hh + 1) * HEAD_DIM, :] = ot[:HEAD_DIM] / ot[HEAD_DIM:HEAD_DIM + 1]
    o_ref[...] = ot_ref[...].T.astype(BF16)


def _attention(q_t, k_t, v_t, *, q_tile0, n_q_tiles, key_blocks, tq, n_batch, name):
    n_keys = sum(c for c, _ in key_blocks)
    nkb = len(key_blocks)
    q_spec = pl.BlockSpec((KV_REP * HEAD_DIM, tq),
                          lambda b, g, i: (g, q_tile0 + b * n_q_tiles + i))
    k_specs = [pl.BlockSpec((KV_WIDTH, c), (lambda f: (lambda b, g, i: (0, f(b))))(f))
               for c, f in key_blocks]
    return pl.pallas_call(
        functools.partial(_attn_kernel, n_key_blocks=nkb, tq=tq),
        grid=(n_batch, N_KV_HEADS, n_q_tiles),
        in_specs=[q_spec] + k_specs + k_specs,
        out_specs=pl.BlockSpec((tq, KV_REP * HEAD_DIM),
                               lambda b, g, i: (b * n_q_tiles + i, g)),
        out_shape=jax.ShapeDtypeStruct((n_batch * n_q_tiles * tq, ATTN_WIDTH), BF16),
        scratch_shapes=[pltpu.VMEM((n_keys, KV_WIDTH), BF16),
                        pltpu.VMEM((HEAD_DIM + ONES_ROWS, n_keys), BF16),
                        pltpu.VMEM((KV_REP * HEAD_DIM, tq), F32)],
        compiler_params=_cparams(3),
        name=name,
    )(q_t, *([k_t] * nkb), *([v_t] * nkb))


def _fourier_kernel(c_ref, s_ref, x_ref, cc_ref, sc_ref, w_ref, o_ref, *, norm):
    x = x_ref[...]
    a = jnp.dot(c_ref[...], x, preferred_element_type=F32).astype(BF16)
    b = jnp.dot(s_ref[...], x, preferred_element_type=F32).astype(BF16)
    cc = cc_ref[...]
    sc = sc_ref[...]
    groups = [slice(grp * FOURIER_GROUP, (grp + 1) * FOURIER_GROUP) for grp in range(N_FOURIER_GROUPS)]
    specs = [(jnp.dot(a[:, sl], cc, preferred_element_type=F32)
              - jnp.dot(b[:, sl], sc, preferred_element_type=F32)) * norm for sl in groups]
    for grp, sl in enumerate(groups):
        o_ref[:, sl] = jnp.dot(specs[grp].astype(BF16), w_ref[grp], preferred_element_type=F32).astype(BF16)


def _fourier(f, cmat, smat, cc, sc, w_four, *, n, tr, x_block0, n_batch, name):
    n_row_tiles = n // tr
    return pl.pallas_call(
        functools.partial(_fourier_kernel, norm=float(1.0 / np.sqrt(n * FOURIER_GROUP))),
        grid=(n_row_tiles, n_batch),
        in_specs=[
            pl.BlockSpec((tr, n), lambda i, b: (i, 0)),
            pl.BlockSpec((tr, n), lambda i, b: (i, 0)),
            pl.BlockSpec((n, FOURIER_WIDTH), lambda i, b: (x_block0 + b, 0)),
            pl.BlockSpec(cc.shape, lambda i, b: (0, 0)),
            pl.BlockSpec(sc.shape, lambda i, b: (0, 0)),
            pl.BlockSpec(w_four.shape, lambda i, b: (0, 0, 0)),
        ],
        out_specs=pl.BlockSpec((tr, FOURIER_WIDTH), lambda i, b: (b * n_row_tiles + i, 0)),
        out_shape=jax.ShapeDtypeStruct((n_batch * n, FOURIER_WIDTH), BF16),
        compiler_params=_cparams(2),
        name=name,
    )(cmat, smat, f, cc, sc, w_four)


def _postmix_kernel(*refs, n_x, n_a, n_f, tile0, n_ctx_tiles, tiles_per_batch, with_router):
    x_refs, refs = refs[:n_x], refs[n_x:]
    a_refs, refs = refs[:n_a], refs[n_a:]
    f_refs, refs = refs[:n_f], refs[n_f:]
    mod_ref, g_ref, woa_ref, wof_ref = refs[:4]
    if with_router:
        wr_ref, br_ref, x1_ref, h_ref, rt_ref, cnt_ref = refs[4:]
    else:
        x1_ref, h_ref = refs[4:]
    i = pl.program_id(0) + tile0
    row = _mod_row(i, n_ctx_tiles, tiles_per_batch)
    is_ctx = i < n_ctx_tiles
    a = _load_tokens(a_refs, is_ctx)
    f = _load_tokens(f_refs, is_ctx)
    mix = (jnp.dot(a, woa_ref[...], preferred_element_type=F32)
           + jnp.dot(f, wof_ref[...], preferred_element_type=F32))
    x1 = _load_tokens(x_refs, is_ctx) + _mod_vec(mod_ref, row, 2) * mix
    x1_ref[...] = x1
    h = _norm_modulate(x1, g_ref[...], _mod_vec(mod_ref, row, 3), _mod_vec(mod_ref, row, 4))
    hb = h.astype(BF16)
    if with_router:
        h_ref[...] = _pack_bf16_pairs(hb)
        logits = jnp.dot(hb, wr_ref[...], preferred_element_type=F32) + br_ref[...]
        _top2_route(logits, rt_ref, cnt_ref, first=pl.program_id(0) == 0)
    else:
        h_ref[...] = hb


def _top2_route(logits, rt_ref, cnt_ref, first):
    tm, lanes = logits.shape
    lane = lax.broadcasted_iota(jnp.int32, (tm, lanes), 1)
    neg = jnp.float32(-jnp.inf)
    m1 = jnp.max(logits, axis=1, keepdims=True)
    i1 = jnp.min(jnp.where(logits == m1, lane, lanes), axis=1, keepdims=True)
    rest = jnp.where(lane == i1, neg, logits)
    m2 = jnp.max(rest, axis=1, keepdims=True)
    i2 = jnp.min(jnp.where(rest == m2, lane, lanes), axis=1, keepdims=True)
    e = jnp.exp(m2 - m1)
    g1 = 1.0 / (1.0 + e)
    g2 = e / (1.0 + e)
    pick1 = lane == i1
    pick2 = lane == i2
    picked = jnp.logical_or(pick1, pick2).astype(F32)

    @pl.when(first)
    def _():
        cnt_ref[...] = jnp.zeros_like(cnt_ref)

    r_i = lax.broadcasted_iota(jnp.int32, (tm, tm), 0)
    c_i = lax.broadcasted_iota(jnp.int32, (tm, tm), 1)
    lower = jnp.where(c_i < r_i, 1.0, 0.0).astype(BF16)
    before = jnp.dot(lower, picked.astype(BF16), preferred_element_type=F32) + cnt_ref[0:1, :]
    r1 = jnp.sum(jnp.where(pick1, before, 0.0), axis=1, keepdims=True)
    r2 = jnp.sum(jnp.where(pick2, before, 0.0), axis=1, keepdims=True)
    cnt_ref[...] = cnt_ref[...] + jnp.sum(picked, axis=0, keepdims=True)
    out = jnp.zeros((tm, lanes), F32)
    for k, val in enumerate([i1.astype(F32), i2.astype(F32), g1, g2, r1, r2]):
        out = jnp.where(lane == k, val, out)
    rt_ref[...] = out


def _postmix(x, attn, four, mod_l, g_ffn, wo_a, wo_f, router, *, n_ctx, seq, t, tile0):
    d = D_MODEL
    tm = TM_MIX
    nct = n_ctx // tm
    t = t - tile0 * tm
    full = lambda shape: pl.BlockSpec(shape, lambda i: (0,) * len(shape))
    tile = lambda w: pl.BlockSpec((tm, w), lambda i: (i, 0))
    x_specs, x_arrays = _token_specs(x, tm, nct, tile0)
    a_specs, a_arrays = _token_specs(attn, tm, nct, tile0)
    f_specs, f_arrays = _token_specs(four, tm, nct, tile0)
    in_specs = x_specs + a_specs + f_specs + [full(mod_l.shape), full((1, d)), full(wo_a.shape),
                                              full(wo_f.shape)]
    if router is None:
        out_specs = [tile(d), tile(d)]
        out_shape = [jax.ShapeDtypeStruct((t, d), F32), jax.ShapeDtypeStruct((t, d), BF16)]
    else:
        out_specs = [tile(d), tile(d // 2)]
        out_shape = [jax.ShapeDtypeStruct((t, d), F32), jax.ShapeDtypeStruct((t, d // 2), jnp.uint32)]
    args = x_arrays + a_arrays + f_arrays + [mod_l, g_ffn, wo_a, wo_f]
    if router is not None:
        in_specs += [full(router[0].shape), full(router[1].shape)]
        out_specs += [tile(LOGIT_LANES), pl.BlockSpec((8, LOGIT_LANES), lambda i: (0, 0))]
        out_shape += [jax.ShapeDtypeStruct((t, LOGIT_LANES), F32), jax.ShapeDtypeStruct((8, LOGIT_LANES), F32)]
        args += list(router)
    return pl.pallas_call(
        functools.partial(_postmix_kernel, n_x=len(x_arrays), n_a=len(a_arrays), n_f=len(f_arrays),
                          tile0=tile0, n_ctx_tiles=nct, tiles_per_batch=seq // tm,
                          with_router=router is not None),
        grid=(t // tm,),
        in_specs=in_specs,
        out_specs=out_specs,
        out_shape=out_shape,
        compiler_params=_cparams(1),
        name="postmix_wo_norm",
    )(*args)


def _silu_mul(gate, up):
    return (gate / (1.0 + jnp.exp(-gate))) * up


def _swiglu_chunks(x, n_chunks, wg_of, wu_of, wd_of, acc_ref):
    gate = jnp.dot(x, wg_of(0), preferred_element_type=F32)
    up = jnp.dot(x, wu_of(0), preferred_element_type=F32)
    for c in range(n_chunks):
        act = _silu_mul(gate, up).astype(BF16)
        if c + 1 < n_chunks:
            gate = jnp.dot(x, wg_of(c + 1), preferred_element_type=F32)
            up = jnp.dot(x, wu_of(c + 1), preferred_element_type=F32)
        acc_ref[...] += jnp.dot(act, wd_of(c), preferred_element_type=F32)


def _dense_ffn_kernel(h_ref, x1_ref, mod_ref, wg_ref, wu_ref, wd_ref, o_ref, acc_ref, *,
                      tile0, n_ctx_tiles, tiles_per_batch):
    i = pl.program_id(0) + tile0
    row = _mod_row(i, n_ctx_tiles, tiles_per_batch)
    acc_ref[...] = jnp.zeros_like(acc_ref)
    _swiglu_chunks(h_ref[...], wg_ref.shape[0], lambda c: wg_ref[c], lambda c: wu_ref[c],
                   lambda c: wd_ref[c], acc_ref)
    o_ref[...] = x1_ref[...] + _mod_vec(mod_ref, row, 5) * acc_ref[...]


def _dense_ffn(h, x1, mod_l, wg, wu, wd, *, n_ctx, seq, tile0):
    t, d = x1.shape
    tm = TM_DENSE
    full = lambda shape: pl.BlockSpec(shape, lambda i: (0,) * len(shape))
    tile = lambda: pl.BlockSpec((tm, d), lambda i: (i, 0))
    return pl.pallas_call(
        functools.partial(_dense_ffn_kernel, tile0=tile0 * TM_MIX // tm, n_ctx_tiles=n_ctx // tm,
                          tiles_per_batch=seq // tm),
        grid=(t // tm,),
        in_specs=[tile(), tile(), full(mod_l.shape), full(wg.shape), full(wu.shape), full(wd.shape)],
        out_specs=tile(),
        out_shape=jax.ShapeDtypeStruct((t, d), F32),
        scratch_shapes=[pltpu.VMEM((tm, d), F32)],
        compiler_params=_cparams(1),
        name="dense_swiglu",
    )(h, x1, mod_l, wg, wu, wd)


def _moe_ffn_kernel(be_ref, nv_ref, x_ref, wg_ref, wu_ref, wd_ref, o_ref, xb_ref, wgb_ref, wub_ref, wdb_ref):
    i = pl.program_id(0)
    j = pl.program_id(1)
    n_valid = nv_ref[i]

    @pl.when(j == 0)
    def _():
        x = _unpack_bf16_pairs(x_ref[...])
        rows = lax.broadcasted_iota(jnp.int32, x.shape, 0)
        xb_ref[...] = jnp.where(rows < n_valid, x, jnp.zeros_like(x))
        o_ref[...] = jnp.zeros_like(o_ref)

    n_sub = TF_MOE // FF_CHUNK
    sub = lambda c: slice(c * FF_CHUNK, (c + 1) * FF_CHUNK)
    rows_needed = ((n_valid + TAIL_ROWS - 1) // TAIL_ROWS) * TAIL_ROWS
    n_full = rows_needed // TM_PASS
    n_tail = (rows_needed - n_full * TM_PASS) // TAIL_ROWS

    def cast_chunk(src_ref, dst_ref, idx):
        w = src_ref[(0, 0) + idx].astype(BF16)
        dst_ref[idx] = w
        return w

    cast_wg = lambda c: cast_chunk(wg_ref, wgb_ref, (slice(None), sub(c)))
    cast_wu = lambda c: cast_chunk(wu_ref, wub_ref, (slice(None), sub(c)))
    cast_wd = lambda c: cast_chunk(wd_ref, wdb_ref, (sub(c), slice(None)))
    read_wg = lambda c: wgb_ref[:, sub(c)]
    read_wu = lambda c: wub_ref[:, sub(c)]
    read_wd = lambda c: wdb_ref[sub(c), :]

    @pl.when(jnp.logical_and(n_full == 0, n_tail > 0))
    def _():
        for c in range(n_sub):
            cast_wg(c), cast_wu(c), cast_wd(c)

    for r in range(TM_MOE // TM_PASS):
        rows = slice(r * TM_PASS, (r + 1) * TM_PASS)
        getters = (cast_wg, cast_wu, cast_wd) if r == 0 else (read_wg, read_wu, read_wd)

        @pl.when(r < n_full)
        def _():
            _swiglu_chunks(xb_ref[rows, :], n_sub, *getters, o_ref.at[rows, :])

    @pl.when(n_tail > 0)
    def _():
        def group(gi, carry):
            rows = pl.ds(pl.multiple_of(n_full * TM_PASS + gi * TAIL_ROWS, TAIL_ROWS), TAIL_ROWS)
            _swiglu_chunks(xb_ref[rows, :], n_sub, read_wg, read_wu, read_wd, o_ref.at[rows, :])
            return carry

        lax.fori_loop(0, n_tail, group, 0)


def _moe_ffn(xs, block_expert, n_valid, wg, wu, wd, layer):
    n_rows = xs.shape[0]
    d = D_MODEL
    tm, tf = TM_MOE, TF_MOE
    n_blocks = n_rows // tm
    n_ff = wg.shape[3] // tf

    def ff_idx(j, nv, i):
        return jnp.where(nv[i] > 0, j, n_ff - 1)

    grid_spec = pltpu.PrefetchScalarGridSpec(
        num_scalar_prefetch=2,
        grid=(n_blocks, n_ff),
        in_specs=[
            pl.BlockSpec((tm, d // 2), lambda i, j, be, nv: (i, 0)),
            pl.BlockSpec((1, 1, d, tf), lambda i, j, be, nv: (layer, be[i], 0, ff_idx(j, nv, i))),
            pl.BlockSpec((1, 1, d, tf), lambda i, j, be, nv: (layer, be[i], 0, ff_idx(j, nv, i))),
            pl.BlockSpec((1, 1, tf, d), lambda i, j, be, nv: (layer, be[i], ff_idx(j, nv, i), 0)),
        ],
        out_specs=pl.BlockSpec((tm, d), lambda i, j, be, nv: (i, 0)),
        scratch_shapes=[pltpu.VMEM((tm, d), BF16), pltpu.VMEM((d, tf), BF16), pltpu.VMEM((d, tf), BF16),
                        pltpu.VMEM((tf, d), BF16)],
    )
    return pl.pallas_call(
        _moe_ffn_kernel,
        grid_spec=grid_spec,
        out_shape=jax.ShapeDtypeStruct((n_rows, d), F32),
        compiler_params=_cparams(2),
        name="expert_swiglu",
    )(block_expert, n_valid, xs, wg, wu, wd)


def _sc_mesh():
    return plsc.VectorSubcoreMesh(core_axis_name="c", subcore_axis_name="s")


def _sc_params():
    return pltpu.CompilerParams(use_tc_tiling_on_sc=True)


def _sc_dispatch(h_packed, dest, n_rows):
    t, w = h_packed.shape
    win = SC_DISPATCH_ROWS
    n_win = t // win
    idx = [dest[:, k].reshape(n_win, 1, win) for k in range(TOP_K)]

    @functools.partial(
        pl.kernel, out_type=jax.ShapeDtypeStruct((n_rows, w), h_packed.dtype), mesh=_sc_mesh(),
        scratch_types=[], compiler_params=_sc_params(), name="expert_dispatch_scatter")
    def run(x_hbm, i0_hbm, i1_hbm, o_hbm):
        def body(x_vmem, i0_vmem, i1_vmem):
            pltpu.sync_copy(x_vmem, o_hbm.at[i0_vmem.at[0, 0]])
            pltpu.sync_copy(x_vmem, o_hbm.at[i1_vmem.at[0, 0]])

        idx_spec = pl.BlockSpec((1, 1, win), lambda i: (i, 0, 0))
        pltpu.emit_pipeline(
            body, grid=(n_win,),
            in_specs=[pl.BlockSpec((win, w), lambda i: (i, 0)), idx_spec, idx_spec],
            out_specs=[], core_axis_name=("c", "s"), dimension_semantics=(pltpu.PARALLEL,),
        )(x_hbm, i0_hbm, i1_hbm)

    return run(h_packed, *idx)


def _sc_gather(ys, idx_flat):
    w = ys.shape[1]
    n = idx_flat.shape[0]
    win = SC_GATHER_ROWS
    n_win = n // win

    @functools.partial(
        pl.kernel, out_type=jax.ShapeDtypeStruct((n, w), ys.dtype), mesh=_sc_mesh(),
        scratch_types=[], compiler_params=_sc_params(), name="expert_combine_gather")
    def run(y_hbm, i_hbm, o_hbm):
        def body(i_vmem, o_vmem):
            pltpu.sync_copy(y_hbm.at[i_vmem.at[0, 0]], o_vmem)

        pltpu.emit_pipeline(
            body, grid=(n_win,),
            in_specs=[pl.BlockSpec((1, 1, win), lambda i: (i, 0, 0))],
            out_specs=[pl.BlockSpec((win, w), lambda i: (i, 0))],
            core_axis_name=("c", "s"), dimension_semantics=(pltpu.PARALLEL,),
        )(i_hbm, o_hbm)

    return run(ys, idx_flat.reshape(n_win, 1, win))


def _moe_combine_kernel(x1_ref, y0_ref, y1_ref, g_ref, mod_ref, o_ref, *, tile0, n_ctx_tiles,
                        tiles_per_batch):
    i = pl.program_id(0) + tile0
    row = _mod_row(i, n_ctx_tiles, tiles_per_batch)
    gates = g_ref[...]
    y = y0_ref[...] * gates[:, 2:3] + y1_ref[...] * gates[:, 3:4]
    o_ref[...] = x1_ref[...] + _mod_vec(mod_ref, row, 5) * y


def _moe_combine(x1, y_sel, gates, mod_l, *, n_ctx, seq, tile0):
    t, d = x1.shape
    tm = TM_MIX
    n_tiles = t // tm
    tile = lambda w: pl.BlockSpec((tm, w), lambda i: (i, 0))
    return pl.pallas_call(
        functools.partial(_moe_combine_kernel, tile0=tile0, n_ctx_tiles=n_ctx // tm,
                          tiles_per_batch=seq // tm),
        grid=(n_tiles,),
        in_specs=[tile(d), tile(d), pl.BlockSpec((tm, d), lambda i: (n_tiles + i, 0)), tile(LOGIT_LANES),
                  pl.BlockSpec(mod_l.shape, lambda i: (0, 0))],
        out_specs=tile(d),
        out_shape=jax.ShapeDtypeStruct((t, d), F32),
        compiler_params=_cparams(1),
        name="expert_combine",
    )(x1, y_sel, y_sel, gates, mod_l)


def _route(route, counts):
    t = route.shape[0]
    expert = route[:, 0:TOP_K].astype(jnp.int32)
    rank = route[:, 4:4 + TOP_K].astype(jnp.int32)
    counts = counts[0, :N_EXPERTS].astype(jnp.int32)
    n_assign = t * TOP_K
    padded = ((counts + TM_MOE - 1) // TM_MOE) * TM_MOE
    pend = jnp.cumsum(padded)
    pstart = pend - padded
    onehot = expert[:, :, None] == jnp.arange(N_EXPERTS, dtype=jnp.int32)[None, None, :]
    dest = jnp.sum(jnp.where(onehot, pstart[None, None, :], 0), axis=-1) + rank
    n_blocks = -(-n_assign // TM_MOE) + N_EXPERTS
    block_start = jnp.arange(n_blocks, dtype=jnp.int32) * TM_MOE
    block_expert = jnp.minimum(jnp.sum(pend[None, :] <= block_start[:, None], axis=1),
                               N_EXPERTS - 1).astype(jnp.int32)
    n_valid = jnp.clip(counts[block_expert] - (block_start - pstart[block_expert]), 0, TM_MOE)
    n_valid = jnp.where(block_start < pend[-1], n_valid, 0).astype(jnp.int32)
    return dest.astype(jnp.int32), block_expert, n_valid, n_blocks * TM_MOE


def kernel(x, c, ctx, c_ctx, w_mod, b_mod, g_mix, g_ffn, g_q, g_k, w_in, w_four, w_o,
           w_gate_dense, w_up_dense, w_down_dense, w_router, b_router,
           w_gate_moe, w_up_moe, w_down_moe):
    b, s, d = x.shape
    n_ctx_len = ctx.shape[1]
    n_ctx = b * n_ctx_len
    t = n_ctx + b * s
    assert d == D_MODEL and b + 1 <= MOD_ROWS
    assert n_ctx % TM_DENSE == 0 and s % TM_DENSE == 0 and n_ctx_len % 128 == 0 and n_ctx % s == 0

    cvec = jnp.zeros((MOD_ROWS, d), F32).at[0].set(c_ctx).at[1:b + 1].set(c)
    mod = _modulation(cvec, w_mod, b_mod)

    cos_t, sin_t = _rope_tables_t(s, TM_MIX)
    c_lat, s_lat = _dft_mats(s)
    c_ctx_m, s_ctx_m = _dft_mats(n_ctx_len)
    c_grp, s_grp = _dft_mats(FOURIER_GROUP)

    xa = (ctx.reshape(n_ctx, d), x.reshape(b * s, d))

    for l in range(DEPTH):
        mod_l = mod[l]
        w_in_l = w_in[l]
        wt = w_in_l[:, :QKV_WIDTH].T.astype(BF16)
        wf = w_in_l[:, QKV_WIDTH:].astype(BF16)
        gq = jnp.broadcast_to((g_q[l] * (LOG2_E * HEAD_DIM ** -0.5))[:, None], (HEAD_DIM, TM_MIX))
        gk = jnp.broadcast_to(g_k[l][:, None], (HEAD_DIM, TM_MIX))
        last = l == DEPTH - 1
        tile0 = n_ctx // TM_MIX if last else 0
        q_t, k_t, v_t, f = _premix(xa, mod_l, g_mix[l].reshape(1, d), wt, wf, gq, gk, cos_t, sin_t,
                                   n_ctx=n_ctx, seq=s, t=t)

        n_qt = s // TQ_ATTN
        lat_keys = [(n_ctx_len, lambda bb: bb), (s, lambda bb: n_ctx // s + bb)]
        attn_lat = _attention(q_t, k_t, v_t, q_tile0=n_ctx // TQ_ATTN, n_q_tiles=n_qt,
                              key_blocks=lat_keys, tq=TQ_ATTN, n_batch=b, name="attention_latent")
        attn_ctx = None if last else _attention(
            q_t, k_t, v_t, q_tile0=0, n_q_tiles=1, key_blocks=[(n_ctx_len, lambda bb: bb)],
            tq=n_ctx_len, n_batch=b, name="attention_context")

        wfour = w_four[l].astype(BF16)
        four_lat = _fourier(f, c_lat, s_lat, c_grp, s_grp, wfour, n=s, tr=TR_FOUR,
                            x_block0=n_ctx // s, n_batch=b, name="fourier_latent")
        four_ctx = None if last else _fourier(
            f, c_ctx_m, s_ctx_m, c_grp, s_grp, wfour, n=n_ctx_len, tr=n_ctx_len, x_block0=0, n_batch=b,
            name="fourier_context")

        wo = w_o[l].astype(BF16)
        is_moe = l % 2 == 1
        li = l // 2
        router = None
        if is_moe:
            wr = jnp.zeros((d, LOGIT_LANES), BF16).at[:, :N_EXPERTS].set(w_router[li].astype(BF16))
            br = jnp.full((1, LOGIT_LANES), -jnp.inf, F32).at[0, :N_EXPERTS].set(b_router[li])
            router = (wr, br)
        res = _postmix(xa, (attn_ctx, attn_lat), (four_ctx, four_lat), mod_l, g_ffn[l].reshape(1, d),
                       wo[:ATTN_WIDTH], wo[ATTN_WIDTH:], router, n_ctx=n_ctx, seq=s, t=t, tile0=tile0)
        t_l = t - tile0 * TM_MIX
        if not is_moe:
            x1, h2 = res
            xa = _dense_ffn(h2, x1, mod_l, _chunk_cols(w_gate_dense[li], FF_CHUNK),
                            _chunk_cols(w_up_dense[li], FF_CHUNK), _chunk_rows(w_down_dense[li], FF_CHUNK),
                            n_ctx=n_ctx, seq=s, tile0=tile0)
        else:
            x1, h2, route, counts = res
            dest, block_expert, n_valid, n_rows = _route(route, counts)
            xs = _sc_dispatch(h2, dest, n_rows)
            ys = _moe_ffn(xs, block_expert, n_valid, w_gate_moe, w_up_moe, w_down_moe, li)
            y_sel = _sc_gather(ys, dest.T.reshape(-1))
            xa = _moe_combine(x1, y_sel, route, mod_l, n_ctx=n_ctx, seq=s, tile0=tile0)

    return xa.reshape(b, s, d)
```

```python
import functools

import numpy as np
import jax
import jax.numpy as jnp
from jax import lax
from jax.experimental import pallas as pl
from jax.experimental.pallas import tpu as pltpu
from jax.experimental.pallas import tpu_sc as plsc

D_MODEL = 1024
DEPTH = 4
GRID_W = 64
HEAD_DIM = 64
ATTN_WIDTH = 512
N_Q_HEADS = 8
N_KV_HEADS = 2
KV_REP = 4
KV_WIDTH = 128
FOURIER_WIDTH = 512
N_FOURIER_GROUPS = 4
FOURIER_GROUP = 128
ROT_PER_AXIS = 32
ROPE_THETA = 10000.0
N_MOD = 6
D_FF_DENSE = 2816
N_EXPERTS = 8
TOP_K = 2
D_FF_EXPERT = 3584
EPS = 1e-6

QKV_WIDTH = ATTN_WIDTH + 2 * KV_WIDTH
MOD_ROWS = 16
LOGIT_LANES = 128

TM_MIX = 512
TQ_ATTN = 512
KEY_CHUNK = 256
ONES_ROWS = 16
SCORE_LOOKAHEAD = 2
LOG2_E = 1.4426950408889634
TR_FOUR = 512
TM_DENSE = 1024
FF_CHUNK = 256
TM_MOE = 2048
TM_PASS = 1024
TAIL_ROWS = 256
TF_MOE = 512
SC_DISPATCH_ROWS = 64
SC_GATHER_ROWS = 64
VMEM_LIMIT = 56 * 1024 * 1024

F32 = jnp.float32
BF16 = jnp.bfloat16


def _cparams(n_axes, flags=None):
    return pltpu.CompilerParams(
        dimension_semantics=("arbitrary",) * n_axes, vmem_limit_bytes=VMEM_LIMIT, flags=flags)


def _rope_tables_t(s, tm):
    n_rows = s // GRID_W
    rows = np.repeat(np.arange(n_rows), GRID_W).astype(np.float64)
    cols = np.tile(np.arange(GRID_W), n_rows).astype(np.float64)
    inv_freq = (ROPE_THETA ** (-np.arange(0, ROT_PER_AXIS, 2, dtype=np.float32) / ROT_PER_AXIS)
                ).astype(np.float32).astype(np.float64)
    ang_r = (rows[None, :].astype(np.float32) * inv_freq[:, None].astype(np.float32)).astype(np.float64)
    ang_c = (cols[None, :].astype(np.float32) * inv_freq[:, None].astype(np.float32)).astype(np.float64)
    cos = np.concatenate([np.cos(ang_r), np.cos(ang_r), np.cos(ang_c), np.cos(ang_c)], axis=0)
    sin = np.concatenate([-np.sin(ang_r), np.sin(ang_r), -np.sin(ang_c), np.sin(ang_c)], axis=0)
    cos = np.concatenate([np.ones((HEAD_DIM, tm)), cos], axis=1)
    sin = np.concatenate([np.zeros((HEAD_DIM, tm)), sin], axis=1)
    return jnp.asarray(cos, F32), jnp.asarray(sin, F32)


def _dft_mats(n):
    k = np.arange(n, dtype=np.int64)
    ang = 2.0 * np.pi * ((k[:, None] * k[None, :]) % n).astype(np.float64) / n
    return jnp.asarray(np.cos(ang), BF16), jnp.asarray(np.sin(ang), BF16)


def _mod_kernel(c_ref, w_ref, b_ref, o_ref):
    c = c_ref[...]
    s = (c / (1.0 + jnp.exp(-c))).astype(BF16)
    w = w_ref[0].astype(BF16)
    o_ref[0] = jnp.dot(s, w, preferred_element_type=F32) + b_ref[0]


def _modulation(cvec, w_mod, b_mod):
    depth, d, n = w_mod.shape
    tn = 1536
    return pl.pallas_call(
        _mod_kernel,
        grid=(depth, n // tn),
        in_specs=[
            pl.BlockSpec((MOD_ROWS, d), lambda l, j: (0, 0)),
            pl.BlockSpec((1, d, tn), lambda l, j: (l, 0, j)),
            pl.BlockSpec((1, 1, tn), lambda l, j: (l, 0, j)),
        ],
        out_specs=pl.BlockSpec((1, MOD_ROWS, tn), lambda l, j: (l, 0, j)),
        out_shape=jax.ShapeDtypeStruct((depth, MOD_ROWS, n), F32),
        compiler_params=_cparams(2),
        name="adaln_vectors",
    )(cvec, w_mod, b_mod.reshape(depth, 1, n))


def _mod_row(i, n_ctx_tiles, tiles_per_batch):
    lat = jnp.maximum(i - n_ctx_tiles, 0)
    return jnp.where(i < n_ctx_tiles, 0, lat // tiles_per_batch + 1)


def _mod_vec(mod_ref, row, comp):
    return mod_ref[pl.ds(row, 1), comp * D_MODEL:(comp + 1) * D_MODEL]


def _pack_bf16_pairs(xb):
    n = xb.shape[1] // 2
    bits = lax.bitcast_convert_type(xb.astype(F32), jnp.uint32)
    return (bits[:, :n] >> 16) | (bits[:, n:] & jnp.uint32(0xFFFF0000))


def _unpack_bf16_pairs(w):
    lo = lax.bitcast_convert_type(w << 16, F32)
    hi = lax.bitcast_convert_type(w & jnp.uint32(0xFFFF0000), F32)
    return jnp.concatenate([lo, hi], axis=1).astype(BF16)


def _token_specs(x, tm, n_ctx_tiles, tile0):
    if not isinstance(x, tuple):
        return [pl.BlockSpec((tm, x.shape[1]), lambda i: (i + tile0, 0))], [x]
    ctx, lat = x
    specs, arrays = [], []
    if ctx is not None:
        specs.append(pl.BlockSpec((tm, ctx.shape[1]), lambda i: (jnp.minimum(i + tile0, n_ctx_tiles - 1), 0)))
        arrays.append(ctx)
    else:
        assert tile0 >= n_ctx_tiles
    specs.append(pl.BlockSpec((tm, lat.shape[1]), lambda i: (jnp.maximum(i + tile0 - n_ctx_tiles, 0), 0)))
    arrays.append(lat)
    return specs, arrays


def _load_tokens(refs, is_ctx):
    if len(refs) == 2:
        return jnp.where(is_ctx, refs[0][...], refs[1][...])
    return refs[0][...]


def _norm_modulate(x, g, shift, scale):
    ms = jnp.mean(x * x, axis=-1, keepdims=True)
    y = x * lax.rsqrt(ms + EPS) * g
    return y * (1.0 + scale) + shift


def _premix_kernel(*refs, n_x, n_ctx_tiles, tiles_per_batch):
    x_refs = refs[:n_x]
    (mod_ref, g_ref, wt_ref, wf_ref, gq_ref, gk_ref, cos_ref, sin_ref,
     q_ref, k_ref, v_ref, f_ref) = refs[n_x:]
    i = pl.program_id(0)
    row = _mod_row(i, n_ctx_tiles, tiles_per_batch)
    x = _load_tokens(x_refs, i < n_ctx_tiles)
    h = _norm_modulate(x, g_ref[...], _mod_vec(mod_ref, row, 0), _mod_vec(mod_ref, row, 1))
    hb = h.astype(BF16)
    f_ref[...] = jnp.dot(hb, wf_ref[...], preferred_element_type=F32).astype(BF16)
    pt = lax.dot_general(wt_ref[...], hb, (((1,), (1,)), ((), ())), preferred_element_type=F32)
    v_ref[...] = pt[ATTN_WIDTH + KV_WIDTH:, :].astype(BF16)
    cos = cos_ref[...]
    sin = sin_ref[...]

    def norm_rope(xh, gain):
        ms = jnp.mean(xh * xh, axis=0, keepdims=True)
        y = xh * lax.rsqrt(ms + EPS) * gain
        half = ROT_PER_AXIS // 2
        swapped = jnp.concatenate(
            [y[half:2 * half], y[0:half], y[3 * half:4 * half], y[2 * half:3 * half]], axis=0)
        return y * cos + swapped * sin

    gq = gq_ref[...]
    gk = gk_ref[...]
    for hh in range(N_Q_HEADS):
        q_ref[hh * HEAD_DIM:(hh + 1) * HEAD_DIM, :] = norm_rope(
            pt[hh * HEAD_DIM:(hh + 1) * HEAD_DIM, :], gq).astype(BF16)
    for hh in range(N_KV_HEADS):
        lo = ATTN_WIDTH + hh * HEAD_DIM
        k_ref[hh * HEAD_DIM:(hh + 1) * HEAD_DIM, :] = norm_rope(pt[lo:lo + HEAD_DIM, :], gk).astype(BF16)


def _premix(x, mod_l, g_mix, wt, wf, gq, gk, cos_t, sin_t, *, n_ctx, seq, t):
    d = D_MODEL
    tm = TM_MIX
    n_ctx_tiles = n_ctx // tm
    tpb = seq // tm

    def tab_idx(i):
        lat = jnp.maximum(i - n_ctx_tiles, 0)
        return (0, jnp.where(i < n_ctx_tiles, 0, lax.rem(lat, tpb) + 1))

    full = lambda shape: pl.BlockSpec(shape, lambda i: (0,) * len(shape))
    x_specs, x_arrays = _token_specs(x, tm, n_ctx_tiles, 0)
    return pl.pallas_call(
        functools.partial(_premix_kernel, n_x=len(x_arrays), n_ctx_tiles=n_ctx_tiles, tiles_per_batch=tpb),
        grid=(t // tm,),
        in_specs=x_specs + [
            full(mod_l.shape),
            full((1, d)),
            full(wt.shape),
            full(wf.shape),
            full(gq.shape),
            full(gk.shape),
            pl.BlockSpec((HEAD_DIM, tm), tab_idx),
            pl.BlockSpec((HEAD_DIM, tm), tab_idx),
        ],
        out_specs=[
            pl.BlockSpec((ATTN_WIDTH, tm), lambda i: (0, i)),
            pl.BlockSpec((KV_WIDTH, tm), lambda i: (0, i)),
            pl.BlockSpec((KV_WIDTH, tm), lambda i: (0, i)),
            pl.BlockSpec((tm, FOURIER_WIDTH), lambda i: (i, 0)),
        ],
        out_shape=[
            jax.ShapeDtypeStruct((ATTN_WIDTH, t), BF16),
            jax.ShapeDtypeStruct((KV_WIDTH, t), BF16),
            jax.ShapeDtypeStruct((KV_WIDTH, t), BF16),
            jax.ShapeDtypeStruct((t, FOURIER_WIDTH), BF16),
        ],
        compiler_params=_cparams(1),
        name="premix_project",
    )(*x_arrays, mod_l, g_mix, wt, wf, gq, gk, cos_t, sin_t)


def _attn_kernel(*refs, n_key_blocks, tq):
    q_ref = refs[0]
    k_refs = refs[1:1 + n_key_blocks]
    v_refs = refs[1 + n_key_blocks:1 + 2 * n_key_blocks]
    o_ref = refs[1 + 2 * n_key_blocks]
    kall_ref, vg_ref, ot_ref = refs[2 + 2 * n_key_blocks:]
    g = pl.program_id(1)
    qt = pl.program_id(2)
    n_keys = kall_ref.shape[0]
    kc = KEY_CHUNK if n_keys % KEY_CHUNK == 0 else n_keys
    n_chunks = n_keys // kc
    slabs = kc // 8

    @pl.when(jnp.logical_and(g == 0, qt == 0))
    def _():
        off = 0
        for kr in k_refs:
            n = kr.shape[1]
            kall_ref[off:off + n, :] = kr[...].astype(F32).T.astype(BF16)
            off += n

    @pl.when(qt == 0)
    def _():
        g_rows = pl.ds(pl.multiple_of(g * HEAD_DIM, HEAD_DIM), HEAD_DIM)
        off = 0
        for vr in v_refs:
            n = vr.shape[1]
            vg_ref[0:HEAD_DIM, off:off + n] = vr[g_rows, :]
            off += n
        vg_ref[HEAD_DIM:, :] = jnp.ones((ONES_ROWS, n_keys), BF16)

    row_group = lax.broadcasted_iota(jnp.int32, (KV_WIDTH, tq), 0) // HEAD_DIM

    def masked_q(hh):
        qh = q_ref[hh * HEAD_DIM:(hh + 1) * HEAD_DIM, :]
        q2 = jnp.concatenate([qh, qh], axis=0)
        return jnp.where(row_group == g, q2, jnp.zeros_like(q2))

    q2s = [masked_q(hh) for hh in range(KV_REP)]
    items = [(hh, c) for hh in range(KV_REP) for c in range(n_chunks)]

    def score(item):
        hh, c = item
        return jnp.dot(kall_ref[c * kc:(c + 1) * kc, :], q2s[hh], preferred_element_type=F32)

    pending = [score(it) for it in items[:SCORE_LOOKAHEAD]]
    m = ot = None
    for i, (hh, c) in enumerate(items):
        if c == 0:
            m = jnp.full((1, tq), -jnp.inf, F32)
            ot = jnp.zeros((HEAD_DIM + ONES_ROWS, tq), F32)
        s = pending.pop(0).reshape(slabs, 8, tq)
        if i + SCORE_LOOKAHEAD < len(items):
            pending.append(score(items[i + SCORE_LOOKAHEAD]))
        m_new = jnp.maximum(m, jnp.max(jnp.max(s, axis=0), axis=0, keepdims=True))
        pb = jnp.exp2(s - m_new[None]).reshape(kc, tq).astype(BF16)
        rows = slice(c * kc, (c + 1) * kc)
        ot = jnp.exp2(m - m_new) * ot + jnp.dot(vg_ref[:, rows], pb, preferred_element_type=F32)
        m = m_new
        if c == n_chunks - 1:
            ot_ref[hh * HEAD_DIM:(hh + 1) * HEAD_DIM, :] = ot[:HEAD_DIM] / ot[HEAD_DIM:HEAD_DIM + 1]
    o_ref[...] = ot_ref[...].T.astype(BF16)


def _attention(q_t, k_t, v_t, *, q_tile0, n_q_tiles, key_blocks, tq, n_batch, name):
    n_keys = sum(c for c, _ in key_blocks)
    nkb = len(key_blocks)
    q_spec = pl.BlockSpec((KV_REP * HEAD_DIM, tq),
                          lambda b, g, i: (g, q_tile0 + b * n_q_tiles + i))
    k_specs = [pl.BlockSpec((KV_WIDTH, c), (lambda f: (lambda b, g, i: (0, f(b))))(f))
               for c, f in key_blocks]
    return pl.pallas_call(
        functools.partial(_attn_kernel, n_key_blocks=nkb, tq=tq),
        grid=(n_batch, N_KV_HEADS, n_q_tiles),
        in_specs=[q_spec] + k_specs + k_specs,
        out_specs=pl.BlockSpec((tq, KV_REP * HEAD_DIM),
                               lambda b, g, i: (b * n_q_tiles + i, g)),
        out_shape=jax.ShapeDtypeStruct((n_batch * n_q_tiles * tq, ATTN_WIDTH), BF16),
        scratch_shapes=[pltpu.VMEM((n_keys, KV_WIDTH), BF16),
                        pltpu.VMEM((HEAD_DIM + ONES_ROWS, n_keys), BF16),
                        pltpu.VMEM((KV_REP * HEAD_DIM, tq), F32)],
        compiler_params=_cparams(3),
        name=name,
    )(q_t, *([k_t] * nkb), *([v_t] * nkb))


def _fourier_kernel(c_ref, s_ref, x_ref, cc_ref, sc_ref, w_ref, o_ref, *, norm):
    x = x_ref[...]
    a = jnp.dot(c_ref[...], x, preferred_element_type=F32).astype(BF16)
    b = jnp.dot(s_ref[...], x, preferred_element_type=F32).astype(BF16)
    cc = cc_ref[...]
    sc = sc_ref[...]
    groups = [slice(grp * FOURIER_GROUP, (grp + 1) * FOURIER_GROUP) for grp in range(N_FOURIER_GROUPS)]
    specs = [(jnp.dot(a[:, sl], cc, preferred_element_type=F32)
              - jnp.dot(b[:, sl], sc, preferred_element_type=F32)) * norm for sl in groups]
    for grp, sl in enumerate(groups):
        o_ref[:, sl] = jnp.dot(specs[grp].astype(BF16), w_ref[grp], preferred_element_type=F32).astype(BF16)


def _fourier(f, cmat, smat, cc, sc, w_four, *, n, tr, x_block0, n_batch, name):
    n_row_tiles = n // tr
    return pl.pallas_call(
        functools.partial(_fourier_kernel, norm=float(1.0 / np.sqrt(n * FOURIER_GROUP))),
        grid=(n_row_tiles, n_batch),
        in_specs=[
            pl.BlockSpec((tr, n), lambda i, b: (i, 0)),
            pl.BlockSpec((tr, n), lambda i, b: (i, 0)),
            pl.BlockSpec((n, FOURIER_WIDTH), lambda i, b: (x_block0 + b, 0)),
            pl.BlockSpec(cc.shape, lambda i, b: (0, 0)),
            pl.BlockSpec(sc.shape, lambda i, b: (0, 0)),
            pl.BlockSpec(w_four.shape, lambda i, b: (0, 0, 0)),
        ],
        out_specs=pl.BlockSpec((tr, FOURIER_WIDTH), lambda i, b: (b * n_row_tiles + i, 0)),
        out_shape=jax.ShapeDtypeStruct((n_batch * n, FOURIER_WIDTH), BF16),
        compiler_params=_cparams(2),
        name=name,
    )(cmat, smat, f, cc, sc, w_four)


def _postmix_kernel(*refs, n_x, n_a, n_f, tile0, n_ctx_tiles, tiles_per_batch, with_router):
    x_refs, refs = refs[:n_x], refs[n_x:]
    a_refs, refs = refs[:n_a], refs[n_a:]
    f_refs, refs = refs[:n_f], refs[n_f:]
    mod_ref, g_ref, woa_ref, wof_ref = refs[:4]
    if with_router:
        wr_ref, br_ref, x1_ref, h_ref, rt_ref, cnt_ref = refs[4:]
    else:
        x1_ref, h_ref = refs[4:]
    i = pl.program_id(0) + tile0
    row = _mod_row(i, n_ctx_tiles, tiles_per_batch)
    is_ctx = i < n_ctx_tiles
    a = _load_tokens(a_refs, is_ctx)
    f = _load_tokens(f_refs, is_ctx)
    mix = (jnp.dot(a, woa_ref[...], preferred_element_type=F32)
           + jnp.dot(f, wof_ref[...], preferred_element_type=F32))
    x1 = _load_tokens(x_refs, is_ctx) + _mod_vec(mod_ref, row, 2) * mix
    x1_ref[...] = x1
    h = _norm_modulate(x1, g_ref[...], _mod_vec(mod_ref, row, 3), _mod_vec(mod_ref, row, 4))
    hb = h.astype(BF16)
    if with_router:
        h_ref[...] = _pack_bf16_pairs(hb)
        logits = jnp.dot(hb, wr_ref[...], preferred_element_type=F32) + br_ref[...]
        _top2_route(logits, rt_ref, cnt_ref, first=pl.program_id(0) == 0)
    else:
        h_ref[...] = hb


def _top2_route(logits, rt_ref, cnt_ref, first):
    tm, lanes = logits.shape
    lane = lax.broadcasted_iota(jnp.int32, (tm, lanes), 1)
    neg = jnp.float32(-jnp.inf)
    m1 = jnp.max(logits, axis=1, keepdims=True)
    i1 = jnp.min(jnp.where(logits == m1, lane, lanes), axis=1, keepdims=True)
    rest = jnp.where(lane == i1, neg, logits)
    m2 = jnp.max(rest, axis=1, keepdims=True)
    i2 = jnp.min(jnp.where(rest == m2, lane, lanes), axis=1, keepdims=True)
    e = jnp.exp(m2 - m1)
    g1 = 1.0 / (1.0 + e)
    g2 = e / (1.0 + e)
    pick1 = lane == i1
    pick2 = lane == i2
    picked = jnp.logical_or(pick1, pick2).astype(F32)

    @pl.when(first)
    def _():
        cnt_ref[...] = jnp.zeros_like(cnt_ref)

    r_i = lax.broadcasted_iota(jnp.int32, (tm, tm), 0)
    c_i = lax.broadcasted_iota(jnp.int32, (tm, tm), 1)
    lower = jnp.where(c_i < r_i, 1.0, 0.0).astype(BF16)
    before = jnp.dot(lower, picked.astype(BF16), preferred_element_type=F32) + cnt_ref[0:1, :]
    r1 = jnp.sum(jnp.where(pick1, before, 0.0), axis=1, keepdims=True)
    r2 = jnp.sum(jnp.where(pick2, before, 0.0), axis=1, keepdims=True)
    cnt_ref[...] = cnt_ref[...] + jnp.sum(picked, axis=0, keepdims=True)
    out = jnp.zeros((tm, lanes), F32)
    for k, val in enumerate([i1.astype(F32), i2.astype(F32), g1, g2, r1, r2]):
        out = jnp.where(lane == k, val, out)
    rt_ref[...] = out


def _postmix(x, attn, four, mod_l, g_ffn, wo_a, wo_f, router, *, n_ctx, seq, t, tile0):
    d = D_MODEL
    tm = TM_MIX
    nct = n_ctx // tm
    t = t - tile0 * tm
    full = lambda shape: pl.BlockSpec(shape, lambda i: (0,) * len(shape))
    tile = lambda w: pl.BlockSpec((tm, w), lambda i: (i, 0))
    x_specs, x_arrays = _token_specs(x, tm, nct, tile0)
    a_specs, a_arrays = _token_specs(attn, tm, nct, tile0)
    f_specs, f_arrays = _token_specs(four, tm, nct, tile0)
    in_specs = x_specs + a_specs + f_specs + [full(mod_l.shape), full((1, d)), full(wo_a.shape),
                                              full(wo_f.shape)]
    if router is None:
        out_specs = [tile(d), tile(d)]
        out_shape = [jax.ShapeDtypeStruct((t, d), F32), jax.ShapeDtypeStruct((t, d), BF16)]
    else:
        out_specs = [tile(d), tile(d // 2)]
        out_shape = [jax.ShapeDtypeStruct((t, d), F32), jax.ShapeDtypeStruct((t, d // 2), jnp.uint32)]
    args = x_arrays + a_arrays + f_arrays + [mod_l, g_ffn, wo_a, wo_f]
    if router is not None:
        in_specs += [full(router[0].shape), full(router[1].shape)]
        out_specs += [tile(LOGIT_LANES), pl.BlockSpec((8, LOGIT_LANES), lambda i: (0, 0))]
        out_shape += [jax.ShapeDtypeStruct((t, LOGIT_LANES), F32), jax.ShapeDtypeStruct((8, LOGIT_LANES), F32)]
        args += list(router)
    return pl.pallas_call(
        functools.partial(_postmix_kernel, n_x=len(x_arrays), n_a=len(a_arrays), n_f=len(f_arrays),
                          tile0=tile0, n_ctx_tiles=nct, tiles_per_batch=seq // tm,
                          with_router=router is not None),
        grid=(t // tm,),
        in_specs=in_specs,
        out_specs=out_specs,
        out_shape=out_shape,
        compiler_params=_cparams(1),
        name="postmix_wo_norm",
    )(*args)


def _silu_mul(gate, up):
    return (gate / (1.0 + jnp.exp(-gate))) * up


def _swiglu_chunks(x, n_chunks, wg_of, wu_of, wd_of, acc_ref):
    gate = jnp.dot(x, wg_of(0), preferred_element_type=F32)
    up = jnp.dot(x, wu_of(0), preferred_element_type=F32)
    for c in range(n_chunks):
        act = _silu_mul(gate, up).astype(BF16)
        if c + 1 < n_chunks:
            gate = jnp.dot(x, wg_of(c + 1), preferred_element_type=F32)
            up = jnp.dot(x, wu_of(c + 1), preferred_element_type=F32)
        acc_ref[...] += jnp.dot(act, wd_of(c), preferred_element_type=F32)


def _dense_ffn_kernel(h_ref, x1_ref, mod_ref, wg_ref, wu_ref, wd_ref, o_ref, acc_ref, *,
                      tile0, n_ctx_tiles, tiles_per_batch):
    i = pl.program_id(0) + tile0
    row = _mod_row(i, n_ctx_tiles, tiles_per_batch)
    acc_ref[...] = jnp.zeros_like(acc_ref)
    sub = lambda c: slice(c * FF_CHUNK, (c + 1) * FF_CHUNK)
    _swiglu_chunks(h_ref[...], wg_ref.shape[1] // FF_CHUNK, lambda c: wg_ref[:, sub(c)],
                   lambda c: wu_ref[:, sub(c)], lambda c: wd_ref[sub(c), :], acc_ref)
    o_ref[...] = x1_ref[...] + _mod_vec(mod_ref, row, 5) * acc_ref[...]


def _dense_ffn(h, x1, mod_l, wg, wu, wd, *, n_ctx, seq, tile0):
    t, d = x1.shape
    tm = TM_DENSE
    full = lambda shape: pl.BlockSpec(shape, lambda i: (0,) * len(shape))
    tile = lambda: pl.BlockSpec((tm, d), lambda i: (i, 0))
    return pl.pallas_call(
        functools.partial(_dense_ffn_kernel, tile0=tile0 * TM_MIX // tm, n_ctx_tiles=n_ctx // tm,
                          tiles_per_batch=seq // tm),
        grid=(t // tm,),
        in_specs=[tile(), tile(), full(mod_l.shape), full(wg.shape), full(wu.shape), full(wd.shape)],
        out_specs=tile(),
        out_shape=jax.ShapeDtypeStruct((t, d), F32),
        scratch_shapes=[pltpu.VMEM((tm, d), F32)],
        compiler_params=_cparams(1),
        name="dense_swiglu",
    )(h, x1, mod_l, wg, wu, wd)


def _moe_ffn_kernel(be_ref, nv_ref, x_ref, wg_ref, wu_ref, wd_ref, o_ref, xb_ref, acc_ref, wgb_ref, wub_ref,
                    wdb_ref):
    i = pl.program_id(0)
    j = pl.program_id(1)
    n_valid = nv_ref[i]

    @pl.when(j == 0)
    def _():
        x = _unpack_bf16_pairs(x_ref[...])
        rows = lax.broadcasted_iota(jnp.int32, x.shape, 0)
        xb_ref[...] = jnp.where(rows < n_valid, x, jnp.zeros_like(x))
        acc_ref[...] = jnp.zeros_like(acc_ref)

    n_sub = TF_MOE // FF_CHUNK
    sub = lambda c: slice(c * FF_CHUNK, (c + 1) * FF_CHUNK)
    rows_needed = ((n_valid + TAIL_ROWS - 1) // TAIL_ROWS) * TAIL_ROWS
    n_full = rows_needed // TM_PASS
    n_tail = (rows_needed - n_full * TM_PASS) // TAIL_ROWS

    def cast_chunk(src_ref, dst_ref, idx):
        w = src_ref[(0, 0) + idx].astype(BF16)
        dst_ref[idx] = w
        return w

    cast_wg = lambda c: cast_chunk(wg_ref, wgb_ref, (slice(None), sub(c)))
    cast_wu = lambda c: cast_chunk(wu_ref, wub_ref, (slice(None), sub(c)))
    cast_wd = lambda c: cast_chunk(wd_ref, wdb_ref, (sub(c), slice(None)))
    read_wg = lambda c: wgb_ref[:, sub(c)]
    read_wu = lambda c: wub_ref[:, sub(c)]
    read_wd = lambda c: wdb_ref[sub(c), :]

    @pl.when(jnp.logical_and(n_full == 0, n_tail > 0))
    def _():
        for c in range(n_sub):
            cast_wg(c), cast_wu(c), cast_wd(c)

    for r in range(TM_MOE // TM_PASS):
        rows = slice(r * TM_PASS, (r + 1) * TM_PASS)
        getters = (cast_wg, cast_wu, cast_wd) if r == 0 else (read_wg, read_wu, read_wd)

        @pl.when(r < n_full)
        def _():
            _swiglu_chunks(xb_ref[rows, :], n_sub, *getters, acc_ref.at[rows, :])

    @pl.when(n_tail > 0)
    def _():
        def group(gi, carry):
            rows = pl.ds(pl.multiple_of(n_full * TM_PASS + gi * TAIL_ROWS, TAIL_ROWS), TAIL_ROWS)
            _swiglu_chunks(xb_ref[rows, :], n_sub, read_wg, read_wu, read_wd, acc_ref.at[rows, :])
            return carry

        lax.fori_loop(0, n_tail, group, 0)

    @pl.when(j == pl.num_programs(1) - 1)
    def _():
        o_ref[...] = _pack_bf16_pairs(acc_ref[...].astype(BF16))


def _moe_ffn(xs, block_expert, n_valid, wg, wu, wd, layer):
    n_rows = xs.shape[0]
    d = D_MODEL
    tm, tf = TM_MOE, TF_MOE
    n_blocks = n_rows // tm
    n_ff = wg.shape[3] // tf

    def ff_idx(j, nv, i):
        return jnp.where(nv[i] > 0, j, n_ff - 1)

    grid_spec = pltpu.PrefetchScalarGridSpec(
        num_scalar_prefetch=2,
        grid=(n_blocks, n_ff),
        in_specs=[
            pl.BlockSpec((tm, d // 2), lambda i, j, be, nv: (i, 0)),
            pl.BlockSpec((1, 1, d, tf), lambda i, j, be, nv: (layer, be[i], 0, ff_idx(j, nv, i))),
            pl.BlockSpec((1, 1, d, tf), lambda i, j, be, nv: (layer, be[i], 0, ff_idx(j, nv, i))),
            pl.BlockSpec((1, 1, tf, d), lambda i, j, be, nv: (layer, be[i], ff_idx(j, nv, i), 0)),
        ],
        out_specs=pl.BlockSpec((tm, d // 2), lambda i, j, be, nv: (i, 0)),
        scratch_shapes=[pltpu.VMEM((tm, d), BF16), pltpu.VMEM((tm, d), F32), pltpu.VMEM((d, tf), BF16),
                        pltpu.VMEM((d, tf), BF16), pltpu.VMEM((tf, d), BF16)],
    )
    return pl.pallas_call(
        _moe_ffn_kernel,
        grid_spec=grid_spec,
        out_shape=jax.ShapeDtypeStruct((n_rows, d // 2), jnp.uint32),
        compiler_params=_cparams(2),
        name="expert_swiglu",
    )(block_expert, n_valid, xs, wg, wu, wd)


def _sc_mesh():
    return plsc.VectorSubcoreMesh(core_axis_name="c", subcore_axis_name="s")


def _sc_params():
    return pltpu.CompilerParams(use_tc_tiling_on_sc=True)


def _sc_dispatch(h_packed, dest, n_rows):
    t, w = h_packed.shape
    win = SC_DISPATCH_ROWS
    n_win = t // win
    idx = [dest[:, k].reshape(n_win, 1, win) for k in range(TOP_K)]

    @functools.partial(
        pl.kernel, out_type=jax.ShapeDtypeStruct((n_rows, w), h_packed.dtype), mesh=_sc_mesh(),
        scratch_types=[], compiler_params=_sc_params(), name="expert_dispatch_scatter")
    def run(x_hbm, i0_hbm, i1_hbm, o_hbm):
        def body(x_vmem, i0_vmem, i1_vmem):
            pltpu.sync_copy(x_vmem, o_hbm.at[i0_vmem.at[0, 0]])
            pltpu.sync_copy(x_vmem, o_hbm.at[i1_vmem.at[0, 0]])

        idx_spec = pl.BlockSpec((1, 1, win), lambda i: (i, 0, 0))
        pltpu.emit_pipeline(
            body, grid=(n_win,),
            in_specs=[pl.BlockSpec((win, w), lambda i: (i, 0)), idx_spec, idx_spec],
            out_specs=[], core_axis_name=("c", "s"), dimension_semantics=(pltpu.PARALLEL,),
        )(x_hbm, i0_hbm, i1_hbm)

    return run(h_packed, *idx)


def _sc_gather(ys, idx_flat):
    w = ys.shape[1]
    n = idx_flat.shape[0]
    win = SC_GATHER_ROWS
    n_win = n // win

    @functools.partial(
        pl.kernel, out_type=jax.ShapeDtypeStruct((n, w), ys.dtype), mesh=_sc_mesh(),
        scratch_types=[], compiler_params=_sc_params(), name="expert_combine_gather")
    def run(y_hbm, i_hbm, o_hbm):
        def body(i_vmem, o_vmem):
            pltpu.sync_copy(y_hbm.at[i_vmem.at[0, 0]], o_vmem)

        pltpu.emit_pipeline(
            body, grid=(n_win,),
            in_specs=[pl.BlockSpec((1, 1, win), lambda i: (i, 0, 0))],
            out_specs=[pl.BlockSpec((win, w), lambda i: (i, 0))],
            core_axis_name=("c", "s"), dimension_semantics=(pltpu.PARALLEL,),
        )(i_hbm, o_hbm)

    return run(ys, idx_flat.reshape(n_win, 1, win))


def _moe_combine_kernel(x1_ref, y0_ref, y1_ref, g_ref, mod_ref, o_ref, *, tile0, n_ctx_tiles,
                        tiles_per_batch):
    i = pl.program_id(0) + tile0
    row = _mod_row(i, n_ctx_tiles, tiles_per_batch)
    gates = g_ref[...]
    y0 = _unpack_bf16_pairs(y0_ref[...]).astype(F32)
    y1 = _unpack_bf16_pairs(y1_ref[...]).astype(F32)
    y = y0 * gates[:, 2:3] + y1 * gates[:, 3:4]
    o_ref[...] = x1_ref[...] + _mod_vec(mod_ref, row, 5) * y


def _moe_combine(x1, y_sel, gates, mod_l, *, n_ctx, seq, tile0):
    t, d = x1.shape
    tm = TM_MIX
    n_tiles = t // tm
    tile = lambda w: pl.BlockSpec((tm, w), lambda i: (i, 0))
    return pl.pallas_call(
        functools.partial(_moe_combine_kernel, tile0=tile0, n_ctx_tiles=n_ctx // tm,
                          tiles_per_batch=seq // tm),
        grid=(n_tiles,),
        in_specs=[tile(d), tile(d // 2), pl.BlockSpec((tm, d // 2), lambda i: (n_tiles + i, 0)),
                  tile(LOGIT_LANES),
                  pl.BlockSpec(mod_l.shape, lambda i: (0, 0))],
        out_specs=tile(d),
        out_shape=jax.ShapeDtypeStruct((t, d), F32),
        compiler_params=_cparams(1),
        name="expert_combine",
    )(x1, y_sel, y_sel, gates, mod_l)


def _route(route, counts):
    t = route.shape[0]
    expert = route[:, 0:TOP_K].astype(jnp.int32)
    rank = route[:, 4:4 + TOP_K].astype(jnp.int32)
    counts = counts[0, :N_EXPERTS].astype(jnp.int32)
    n_assign = t * TOP_K
    padded = ((counts + TM_MOE - 1) // TM_MOE) * TM_MOE
    pend = jnp.cumsum(padded)
    pstart = pend - padded
    onehot = expert[:, :, None] == jnp.arange(N_EXPERTS, dtype=jnp.int32)[None, None, :]
    dest = jnp.sum(jnp.where(onehot, pstart[None, None, :], 0), axis=-1) + rank
    n_blocks = -(-n_assign // TM_MOE) + N_EXPERTS
    block_start = jnp.arange(n_blocks, dtype=jnp.int32) * TM_MOE
    block_expert = jnp.minimum(jnp.sum(pend[None, :] <= block_start[:, None], axis=1),
                               N_EXPERTS - 1).astype(jnp.int32)
    n_valid = jnp.clip(counts[block_expert] - (block_start - pstart[block_expert]), 0, TM_MOE)
    n_valid = jnp.where(block_start < pend[-1], n_valid, 0).astype(jnp.int32)
    return dest.astype(jnp.int32), block_expert, n_valid, n_blocks * TM_MOE


def kernel(x, c, ctx, c_ctx, w_mod, b_mod, g_mix, g_ffn, g_q, g_k, w_in, w_four, w_o,
           w_gate_dense, w_up_dense, w_down_dense, w_router, b_router,
           w_gate_moe, w_up_moe, w_down_moe):
    b, s, d = x.shape
    n_ctx_len = ctx.shape[1]
    n_ctx = b * n_ctx_len
    t = n_ctx + b * s
    assert d == D_MODEL and b + 1 <= MOD_ROWS
    assert n_ctx % TM_DENSE == 0 and s % TM_DENSE == 0 and n_ctx_len % 128 == 0 and n_ctx % s == 0

    cvec = jnp.zeros((MOD_ROWS, d), F32).at[0].set(c_ctx).at[1:b + 1].set(c)
    mod = _modulation(cvec, w_mod, b_mod)

    cos_t, sin_t = _rope_tables_t(s, TM_MIX)
    c_lat, s_lat = _dft_mats(s)
    c_ctx_m, s_ctx_m = _dft_mats(n_ctx_len)
    c_grp, s_grp = _dft_mats(FOURIER_GROUP)

    xa = (ctx.reshape(n_ctx, d), x.reshape(b * s, d))

    for l in range(DEPTH):
        mod_l = mod[l]
        w_in_l = w_in[l]
        wt = w_in_l[:, :QKV_WIDTH].T.astype(BF16)
        wf = w_in_l[:, QKV_WIDTH:].astype(BF16)
        gq = jnp.broadcast_to((g_q[l] * (LOG2_E * HEAD_DIM ** -0.5))[:, None], (HEAD_DIM, TM_MIX))
        gk = jnp.broadcast_to(g_k[l][:, None], (HEAD_DIM, TM_MIX))
        last = l == DEPTH - 1
        tile0 = n_ctx // TM_MIX if last else 0
        q_t, k_t, v_t, f = _premix(xa, mod_l, g_mix[l].reshape(1, d), wt, wf, gq, gk, cos_t, sin_t,
                                   n_ctx=n_ctx, seq=s, t=t)

        n_qt = s // TQ_ATTN
        lat_keys = [(n_ctx_len, lambda bb: bb), (s, lambda bb: n_ctx // s + bb)]
        attn_lat = _attention(q_t, k_t, v_t, q_tile0=n_ctx // TQ_ATTN, n_q_tiles=n_qt,
                              key_blocks=lat_keys, tq=TQ_ATTN, n_batch=b, name="attention_latent")
        attn_ctx = None if last else _attention(
            q_t, k_t, v_t, q_tile0=0, n_q_tiles=1, key_blocks=[(n_ctx_len, lambda bb: bb)],
            tq=n_ctx_len, n_batch=b, name="attention_context")

        wfour = w_four[l].astype(BF16)
        four_lat = _fourier(f, c_lat, s_lat, c_grp, s_grp, wfour, n=s, tr=TR_FOUR,
                            x_block0=n_ctx // s, n_batch=b, name="fourier_latent")
        four_ctx = None if last else _fourier(
            f, c_ctx_m, s_ctx_m, c_grp, s_grp, wfour, n=n_ctx_len, tr=n_ctx_len, x_block0=0, n_batch=b,
            name="fourier_context")

        wo = w_o[l].astype(BF16)
        is_moe = l % 2 == 1
        li = l // 2
        router = None
        if is_moe:
            wr = jnp.zeros((d, LOGIT_LANES), BF16).at[:, :N_EXPERTS].set(w_router[li].astype(BF16))
            br = jnp.full((1, LOGIT_LANES), -jnp.inf, F32).at[0, :N_EXPERTS].set(b_router[li])
            router = (wr, br)
        res = _postmix(xa, (attn_ctx, attn_lat), (four_ctx, four_lat), mod_l, g_ffn[l].reshape(1, d),
                       wo[:ATTN_WIDTH], wo[ATTN_WIDTH:], router, n_ctx=n_ctx, seq=s, t=t, tile0=tile0)
        t_l = t - tile0 * TM_MIX
        if not is_moe:
            x1, h2 = res
            xa = _dense_ffn(h2, x1, mod_l, w_gate_dense[li].astype(BF16), w_up_dense[li].astype(BF16),
                            w_down_dense[li].astype(BF16), n_ctx=n_ctx, seq=s, tile0=tile0)
        else:
            x1, h2, route, counts = res
            dest, block_expert, n_valid, n_rows = _route(route, counts)
            xs = _sc_dispatch(h2, dest, n_rows)
            ys = _moe_ffn(xs, block_expert, n_valid, w_gate_moe, w_up_moe, w_down_moe, li)
            y_sel = _sc_gather(ys, dest.T.reshape(-1))
            xa = _moe_combine(x1, y_sel, route, mod_l, n_ctx=n_ctx, seq=s, tile0=tile0)

    return xa.reshape(b, s, d)
```

```python
import functools

import numpy as np
import jax
import jax.numpy as jnp
from jax import lax
from jax.experimental import pallas as pl
from jax.experimental.pallas import tpu as pltpu
from jax.experimental.pallas import tpu_sc as plsc

D_MODEL = 1024
DEPTH = 4
GRID_W = 64
HEAD_DIM = 64
ATTN_WIDTH = 512
N_Q_HEADS = 8
N_KV_HEADS = 2
KV_REP = 4
KV_WIDTH = 128
FOURIER_WIDTH = 512
N_FOURIER_GROUPS = 4
FOURIER_GROUP = 128
ROT_PER_AXIS = 32
ROPE_THETA = 10000.0
N_MOD = 6
D_FF_DENSE = 2816
N_EXPERTS = 8
TOP_K = 2
D_FF_EXPERT = 3584
EPS = 1e-6

QKV_WIDTH = ATTN_WIDTH + 2 * KV_WIDTH
MOD_ROWS = 16
LOGIT_LANES = 128

TM_MIX = 512
TQ_ATTN = 512
KEY_CHUNK = 256
ONES_ROWS = 16
SCORE_LOOKAHEAD = 2
LOG2_E = 1.4426950408889634
TR_FOUR = 512
SYM_EXTRA = 16
TM_DENSE = 1024
FF_CHUNK = 256
TM_MOE = 2048
TM_PASS = 1024
TAIL_ROWS = 256
TF_MOE = 512
SC_DISPATCH_ROWS = 64
SC_GATHER_ROWS = 64
VMEM_LIMIT = 56 * 1024 * 1024

F32 = jnp.float32
BF16 = jnp.bfloat16


def _cparams(n_axes, flags=None):
    return pltpu.CompilerParams(
        dimension_semantics=("arbitrary",) * n_axes, vmem_limit_bytes=VMEM_LIMIT, flags=flags)


def _rope_tables_t(s, tm):
    n_rows = s // GRID_W
    rows = np.repeat(np.arange(n_rows), GRID_W).astype(np.float64)
    cols = np.tile(np.arange(GRID_W), n_rows).astype(np.float64)
    inv_freq = (ROPE_THETA ** (-np.arange(0, ROT_PER_AXIS, 2, dtype=np.float32) / ROT_PER_AXIS)
                ).astype(np.float32).astype(np.float64)
    ang_r = (rows[None, :].astype(np.float32) * inv_freq[:, None].astype(np.float32)).astype(np.float64)
    ang_c = (cols[None, :].astype(np.float32) * inv_freq[:, None].astype(np.float32)).astype(np.float64)
    cos = np.concatenate([np.cos(ang_r), np.cos(ang_r), np.cos(ang_c), np.cos(ang_c)], axis=0)
    sin = np.concatenate([-np.sin(ang_r), np.sin(ang_r), -np.sin(ang_c), np.sin(ang_c)], axis=0)
    cos = np.concatenate([np.ones((HEAD_DIM, tm)), cos], axis=1)
    sin = np.concatenate([np.zeros((HEAD_DIM, tm)), sin], axis=1)
    return jnp.asarray(cos, F32), jnp.asarray(sin, F32)


def _dft_mats(n):
    k = np.arange(n, dtype=np.int64)
    ang = 2.0 * np.pi * ((k[:, None] * k[None, :]) % n).astype(np.float64) / n
    return jnp.asarray(np.cos(ang), BF16), jnp.asarray(np.sin(ang), BF16)


def _mod_kernel(c_ref, w_ref, b_ref, o_ref):
    c = c_ref[...]
    s = (c / (1.0 + jnp.exp(-c))).astype(BF16)
    w = w_ref[0].astype(BF16)
    o_ref[0] = jnp.dot(s, w, preferred_element_type=F32) + b_ref[0]


def _modulation(cvec, w_mod, b_mod):
    depth, d, n = w_mod.shape
    tn = 1536
    return pl.pallas_call(
        _mod_kernel,
        grid=(depth, n // tn),
        in_specs=[
            pl.BlockSpec((MOD_ROWS, d), lambda l, j: (0, 0)),
            pl.BlockSpec((1, d, tn), lambda l, j: (l, 0, j)),
            pl.BlockSpec((1, 1, tn), lambda l, j: (l, 0, j)),
        ],
        out_specs=pl.BlockSpec((1, MOD_ROWS, tn), lambda l, j: (l, 0, j)),
        out_shape=jax.ShapeDtypeStruct((depth, MOD_ROWS, n), F32),
        compiler_params=_cparams(2),
        name="adaln_vectors",
    )(cvec, w_mod, b_mod.reshape(depth, 1, n))


def _mod_row(i, n_ctx_tiles, tiles_per_batch):
    lat = jnp.maximum(i - n_ctx_tiles, 0)
    return jnp.where(i < n_ctx_tiles, 0, lat // tiles_per_batch + 1)


def _mod_vec(mod_ref, row, comp):
    return mod_ref[pl.ds(row, 1), comp * D_MODEL:(comp + 1) * D_MODEL]


def _pack_bf16_pairs(xb):
    n = xb.shape[1] // 2
    bits = lax.bitcast_convert_type(xb.astype(F32), jnp.uint32)
    return (bits[:, :n] >> 16) | (bits[:, n:] & jnp.uint32(0xFFFF0000))


def _unpack_bf16_pairs(w):
    lo = lax.bitcast_convert_type(w << 16, F32)
    hi = lax.bitcast_convert_type(w & jnp.uint32(0xFFFF0000), F32)
    return jnp.concatenate([lo, hi], axis=1).astype(BF16)


def _token_specs(x, tm, n_ctx_tiles, tile0):
    if not isinstance(x, tuple):
        return [pl.BlockSpec((tm, x.shape[1]), lambda i: (i + tile0, 0))], [x]
    ctx, lat = x
    specs, arrays = [], []
    if ctx is not None:
        specs.append(pl.BlockSpec((tm, ctx.shape[1]), lambda i: (jnp.minimum(i + tile0, n_ctx_tiles - 1), 0)))
        arrays.append(ctx)
    else:
        assert tile0 >= n_ctx_tiles
    specs.append(pl.BlockSpec((tm, lat.shape[1]), lambda i: (jnp.maximum(i + tile0 - n_ctx_tiles, 0), 0)))
    arrays.append(lat)
    return specs, arrays


def _load_tokens(refs, is_ctx):
    if len(refs) == 2:
        return jnp.where(is_ctx, refs[0][...], refs[1][...])
    return refs[0][...]


def _norm_modulate(x, g, shift, scale):
    ms = jnp.mean(x * x, axis=-1, keepdims=True)
    y = x * lax.rsqrt(ms + EPS) * g
    return y * (1.0 + scale) + shift


def _premix_kernel(*refs, n_x, n_ctx_tiles, tiles_per_batch):
    x_refs = refs[:n_x]
    (mod_ref, g_ref, wt_ref, wf_ref, gq_ref, gk_ref, cos_ref, sin_ref,
     q_ref, k_ref, v_ref, f_ref) = refs[n_x:]
    i = pl.program_id(0)
    row = _mod_row(i, n_ctx_tiles, tiles_per_batch)
    x = _load_tokens(x_refs, i < n_ctx_tiles)
    h = _norm_modulate(x, g_ref[...], _mod_vec(mod_ref, row, 0), _mod_vec(mod_ref, row, 1))
    hb = h.astype(BF16)
    f_ref[...] = jnp.dot(hb, wf_ref[...], preferred_element_type=F32).astype(BF16)
    pt = lax.dot_general(wt_ref[...], hb, (((1,), (1,)), ((), ())), preferred_element_type=F32)
    v_ref[...] = pt[ATTN_WIDTH + KV_WIDTH:, :].astype(BF16)
    cos = cos_ref[...]
    sin = sin_ref[...]

    def norm_rope(xh, gain):
        ms = jnp.mean(xh * xh, axis=0, keepdims=True)
        y = xh * lax.rsqrt(ms + EPS) * gain
        half = ROT_PER_AXIS // 2
        swapped = jnp.concatenate(
            [y[half:2 * half], y[0:half], y[3 * half:4 * half], y[2 * half:3 * half]], axis=0)
        return y * cos + swapped * sin

    gq = gq_ref[...]
    gk = gk_ref[...]
    for hh in range(N_Q_HEADS):
        q_ref[hh * HEAD_DIM:(hh + 1) * HEAD_DIM, :] = norm_rope(
            pt[hh * HEAD_DIM:(hh + 1) * HEAD_DIM, :], gq).astype(BF16)
    for hh in range(N_KV_HEADS):
        lo = ATTN_WIDTH + hh * HEAD_DIM
        k_ref[hh * HEAD_DIM:(hh + 1) * HEAD_DIM, :] = norm_rope(pt[lo:lo + HEAD_DIM, :], gk).astype(BF16)


def _premix(x, mod_l, g_mix, wt, wf, gq, gk, cos_t, sin_t, *, n_ctx, seq, t):
    d = D_MODEL
    tm = TM_MIX
    n_ctx_tiles = n_ctx // tm
    tpb = seq // tm

    def tab_idx(i):
        lat = jnp.maximum(i - n_ctx_tiles, 0)
        return (0, jnp.where(i < n_ctx_tiles, 0, lax.rem(lat, tpb) + 1))

    full = lambda shape: pl.BlockSpec(shape, lambda i: (0,) * len(shape))
    x_specs, x_arrays = _token_specs(x, tm, n_ctx_tiles, 0)
    return pl.pallas_call(
        functools.partial(_premix_kernel, n_x=len(x_arrays), n_ctx_tiles=n_ctx_tiles, tiles_per_batch=tpb),
        grid=(t // tm,),
        in_specs=x_specs + [
            full(mod_l.shape),
            full((1, d)),
            full(wt.shape),
            full(wf.shape),
            full(gq.shape),
            full(gk.shape),
            pl.BlockSpec((HEAD_DIM, tm), tab_idx),
            pl.BlockSpec((HEAD_DIM, tm), tab_idx),
        ],
        out_specs=[
            pl.BlockSpec((ATTN_WIDTH, tm), lambda i: (0, i)),
            pl.BlockSpec((KV_WIDTH, tm), lambda i: (0, i)),
            pl.BlockSpec((KV_WIDTH, tm), lambda i: (0, i)),
            pl.BlockSpec((tm, FOURIER_WIDTH), lambda i: (i, 0)),
        ],
        out_shape=[
            jax.ShapeDtypeStruct((ATTN_WIDTH, t), BF16),
            jax.ShapeDtypeStruct((KV_WIDTH, t), BF16),
            jax.ShapeDtypeStruct((KV_WIDTH, t), BF16),
            jax.ShapeDtypeStruct((t, FOURIER_WIDTH), BF16),
        ],
        compiler_params=_cparams(1),
        name="premix_project",
    )(*x_arrays, mod_l, g_mix, wt, wf, gq, gk, cos_t, sin_t)


def _attn_kernel(*refs, n_key_blocks, tq):
    q_ref = refs[0]
    k_refs = refs[1:1 + n_key_blocks]
    v_refs = refs[1 + n_key_blocks:1 + 2 * n_key_blocks]
    o_ref = refs[1 + 2 * n_key_blocks]
    kall_ref, vg_ref, ot_ref = refs[2 + 2 * n_key_blocks:]
    g = pl.program_id(1)
    qt = pl.program_id(2)
    n_keys = kall_ref.shape[0]
    kc = KEY_CHUNK if n_keys % KEY_CHUNK == 0 else n_keys
    n_chunks = n_keys // kc
    slabs = kc // 8

    @pl.when(jnp.logical_and(g == 0, qt == 0))
    def _():
        off = 0
        for kr in k_refs:
            n = kr.shape[1]
            kall_ref[off:off + n, :] = kr[...].astype(F32).T.astype(BF16)
            off += n

    @pl.when(qt == 0)
    def _():
        g_rows = pl.ds(pl.multiple_of(g * HEAD_DIM, HEAD_DIM), HEAD_DIM)
        off = 0
        for vr in v_refs:
            n = vr.shape[1]
            vg_ref[0:HEAD_DIM, off:off + n] = vr[g_rows, :]
            off += n
        vg_ref[HEAD_DIM:, :] = jnp.ones((ONES_ROWS, n_keys), BF16)

    row_group = lax.broadcasted_iota(jnp.int32, (KV_WIDTH, tq), 0) // HEAD_DIM

    def masked_q(hh):
        qh = q_ref[hh * HEAD_DIM:(hh + 1) * HEAD_DIM, :]
        q2 = jnp.concatenate([qh, qh], axis=0)
        return jnp.where(row_group == g, q2, jnp.zeros_like(q2))

    q2s = [masked_q(hh) for hh in range(KV_REP)]
    items = [(hh, c) for hh in range(KV_REP) for c in range(n_chunks)]

    def score(item):
        hh, c = item
        return jnp.dot(kall_ref[c * kc:(c + 1) * kc, :], q2s[hh], preferred_element_type=F32)

    pending = [score(it) for it in items[:SCORE_LOOKAHEAD]]
    m = ot = None
    for i, (hh, c) in enumerate(items):
        if c == 0:
            m = jnp.full((1, tq), -jnp.inf, F32)
            ot = jnp.zeros((HEAD_DIM + ONES_ROWS, tq), F32)
        s = pending.pop(0).reshape(slabs, 8, tq)
        if i + SCORE_LOOKAHEAD < len(items):
            pending.append(score(items[i + SCORE_LOOKAHEAD]))
        m_new = jnp.maximum(m, jnp.max(jnp.max(s, axis=0), axis=0, keepdims=True))
        pb = jnp.exp2(s - m_new[None]).reshape(kc, tq).astype(BF16)
        rows = slice(c * kc, (c + 1) * kc)
        ot = jnp.exp2(m - m_new) * ot + jnp.dot(vg_ref[:, rows], pb, preferred_element_type=F32)
        m = m_new
        if c == n_chunks - 1:
            ot_ref[hh * HEAD_DIM:(hh + 1) * HEAD_DIM, :] = ot[:HEAD_DIM] / ot[HEAD_DIM:HEAD_DIM + 1]
    o_ref[...] = ot_ref[...].T.astype(BF16)


def _attention(q_t, k_t, v_t, *, q_tile0, n_q_tiles, key_blocks, tq, n_batch, name):
    n_keys = sum(c for c, _ in key_blocks)
    nkb = len(key_blocks)
    q_spec = pl.BlockSpec((KV_REP * HEAD_DIM, tq),
                          lambda b, g, i: (g, q_tile0 + b * n_q_tiles + i))
    k_specs = [pl.BlockSpec((KV_WIDTH, c), (lambda f: (lambda b, g, i: (0, f(b))))(f))
               for c, f in key_blocks]
    return pl.pallas_call(
        functools.partial(_attn_kernel, n_key_blocks=nkb, tq=tq),
        grid=(n_batch, N_KV_HEADS, n_q_tiles),
        in_specs=[q_spec] + k_specs + k_specs,
        out_specs=pl.BlockSpec((tq, KV_REP * HEAD_DIM),
                               lambda b, g, i: (b * n_q_tiles + i, g)),
        out_shape=jax.ShapeDtypeStruct((n_batch * n_q_tiles * tq, ATTN_WIDTH), BF16),
        scratch_shapes=[pltpu.VMEM((n_keys, KV_WIDTH), BF16),
                        pltpu.VMEM((HEAD_DIM + ONES_ROWS, n_keys), BF16),
                        pltpu.VMEM((KV_REP * HEAD_DIM, tq), F32)],
        compiler_params=_cparams(3),
        name=name,
    )(q_t, *([k_t] * nkb), *([v_t] * nkb))


def _fourier_kernel(c_ref, s_ref, x_ref, cc_ref, sc_ref, w_ref, o_ref, *, norm):
    x = x_ref[...]
    a = jnp.dot(c_ref[...], x, preferred_element_type=F32).astype(BF16)
    b = jnp.dot(s_ref[...], x, preferred_element_type=F32).astype(BF16)
    cc = cc_ref[...]
    sc = sc_ref[...]
    groups = [slice(grp * FOURIER_GROUP, (grp + 1) * FOURIER_GROUP) for grp in range(N_FOURIER_GROUPS)]
    specs = [(jnp.dot(a[:, sl], cc, preferred_element_type=F32)
              - jnp.dot(b[:, sl], sc, preferred_element_type=F32)) * norm for sl in groups]
    for grp, sl in enumerate(groups):
        o_ref[:, sl] = jnp.dot(specs[grp].astype(BF16), w_ref[grp], preferred_element_type=F32).astype(BF16)


def _fourier(f, cmat, smat, cc, sc, w_four, *, n, tr, x_block0, n_batch, name):
    n_row_tiles = n // tr
    return pl.pallas_call(
        functools.partial(_fourier_kernel, norm=float(1.0 / np.sqrt(n * FOURIER_GROUP))),
        grid=(n_row_tiles, n_batch),
        in_specs=[
            pl.BlockSpec((tr, n), lambda i, b: (i, 0)),
            pl.BlockSpec((tr, n), lambda i, b: (i, 0)),
            pl.BlockSpec((n, FOURIER_WIDTH), lambda i, b: (x_block0 + b, 0)),
            pl.BlockSpec(cc.shape, lambda i, b: (0, 0)),
            pl.BlockSpec(sc.shape, lambda i, b: (0, 0)),
            pl.BlockSpec(w_four.shape, lambda i, b: (0, 0, 0)),
        ],
        out_specs=pl.BlockSpec((tr, FOURIER_WIDTH), lambda i, b: (b * n_row_tiles + i, 0)),
        out_shape=jax.ShapeDtypeStruct((n_batch * n, FOURIER_WIDTH), BF16),
        compiler_params=_cparams(2),
        name=name,
    )(cmat, smat, f, cc, sc, w_four)


def _dft_half_tiles(n, tr):
    n_tiles = n // 2 // tr
    k = (np.arange(n_tiles, dtype=np.int64)[:, None] * tr
         + np.arange(tr + SYM_EXTRA, dtype=np.int64)[None, :])
    pos = np.arange(n, dtype=np.int64)
    ang = 2.0 * np.pi * ((k[:, :, None] * pos[None, None, :]) % n).astype(np.float64) / n
    rev = np.eye(tr, dtype=np.float32)[::-1]
    return jnp.asarray(np.cos(ang), BF16), jnp.asarray(np.sin(ang), BF16), jnp.asarray(rev, BF16)


def _fourier_sym_kernel(c_ref, s_ref, x_ref, cc_ref, sc_ref, w_ref, rev_ref, o_ref, *, norm, tr, n_tiles):
    i = pl.program_id(1)
    x = x_ref[...]
    a = jnp.dot(c_ref[0], x, preferred_element_type=F32).astype(BF16)
    b = jnp.dot(s_ref[0], x, preferred_element_type=F32).astype(BF16)
    cc = cc_ref[...]
    sc = sc_ref[...]
    groups = [slice(grp * FOURIER_GROUP, (grp + 1) * FOURIER_GROUP) for grp in range(N_FOURIER_GROUPS)]
    ps = [jnp.dot(a[:, sl], cc, preferred_element_type=F32) for sl in groups]
    qs = [jnp.dot(b[:, sl], sc, preferred_element_type=F32) for sl in groups]
    upper = jnp.concatenate([((p + q) * norm)[1:tr + 1] for p, q in zip(ps, qs)], axis=1).astype(BF16)
    upper = jnp.dot(rev_ref[...], upper, preferred_element_type=F32).astype(BF16)
    lo_rows = pl.ds(pl.multiple_of(i * tr, tr), tr)
    hi_rows = pl.ds(pl.multiple_of((2 * n_tiles - 1 - i) * tr, tr), tr)
    for grp, sl in enumerate(groups):
        lower = ((ps[grp] - qs[grp]) * norm)[:tr].astype(BF16)
        o_ref[lo_rows, sl] = jnp.dot(lower, w_ref[grp], preferred_element_type=F32).astype(BF16)
    for grp, sl in enumerate(groups):
        o_ref[hi_rows, sl] = jnp.dot(upper[:, sl], w_ref[grp], preferred_element_type=F32).astype(BF16)


def _fourier_sym(f, c_tiles, s_tiles, rev, cc, sc, w_four, *, n, tr, x_block0, n_batch, name):
    n_tiles = n // 2 // tr
    ext = tr + SYM_EXTRA
    return pl.pallas_call(
        functools.partial(_fourier_sym_kernel, norm=float(1.0 / np.sqrt(n * FOURIER_GROUP)), tr=tr,
                          n_tiles=n_tiles),
        grid=(n_batch, n_tiles),
        in_specs=[
            pl.BlockSpec((1, ext, n), lambda b, i: (i, 0, 0)),
            pl.BlockSpec((1, ext, n), lambda b, i: (i, 0, 0)),
            pl.BlockSpec((n, FOURIER_WIDTH), lambda b, i: (x_block0 + b, 0)),
            pl.BlockSpec(cc.shape, lambda b, i: (0, 0)),
            pl.BlockSpec(sc.shape, lambda b, i: (0, 0)),
            pl.BlockSpec(w_four.shape, lambda b, i: (0, 0, 0)),
            pl.BlockSpec(rev.shape, lambda b, i: (0, 0)),
        ],
        out_specs=pl.BlockSpec((n, FOURIER_WIDTH), lambda b, i: (b, 0)),
        out_shape=jax.ShapeDtypeStruct((n_batch * n, FOURIER_WIDTH), BF16),
        compiler_params=_cparams(2),
        name=name,
    )(c_tiles, s_tiles, f, cc, sc, w_four, rev)


def _postmix_kernel(*refs, n_x, n_a, n_f, tile0, n_ctx_tiles, tiles_per_batch, with_router):
    x_refs, refs = refs[:n_x], refs[n_x:]
    a_refs, refs = refs[:n_a], refs[n_a:]
    f_refs, refs = refs[:n_f], refs[n_f:]
    mod_ref, g_ref, woa_ref, wof_ref = refs[:4]
    if with_router:
        wr_ref, br_ref, x1_ref, h_ref, rt_ref, cnt_ref = refs[4:]
    else:
        x1_ref, h_ref = refs[4:]
    i = pl.program_id(0) + tile0
    row = _mod_row(i, n_ctx_tiles, tiles_per_batch)
    is_ctx = i < n_ctx_tiles
    a = _load_tokens(a_refs, is_ctx)
    f = _load_tokens(f_refs, is_ctx)
    mix = (jnp.dot(a, woa_ref[...], preferred_element_type=F32)
           + jnp.dot(f, wof_ref[...], preferred_element_type=F32))
    x1 = _load_tokens(x_refs, is_ctx) + _mod_vec(mod_ref, row, 2) * mix
    x1_ref[...] = x1
    h = _norm_modulate(x1, g_ref[...], _mod_vec(mod_ref, row, 3), _mod_vec(mod_ref, row, 4))
    hb = h.astype(BF16)
    if with_router:
        h_ref[...] = _pack_bf16_pairs(hb)
        logits = jnp.dot(hb, wr_ref[...], preferred_element_type=F32) + br_ref[...]
        _top2_route(logits, rt_ref, cnt_ref, first=pl.program_id(0) == 0)
    else:
        h_ref[...] = hb


def _top2_route(logits, rt_ref, cnt_ref, first):
    tm, lanes = logits.shape
    lane = lax.broadcasted_iota(jnp.int32, (tm, lanes), 1)
    neg = jnp.float32(-jnp.inf)
    m1 = jnp.max(logits, axis=1, keepdims=True)
    i1 = jnp.min(jnp.where(logits == m1, lane, lanes), axis=1, keepdims=True)
    rest = jnp.where(lane == i1, neg, logits)
    m2 = jnp.max(rest, axis=1, keepdims=True)
    i2 = jnp.min(jnp.where(rest == m2, lane, lanes), axis=1, keepdims=True)
    e = jnp.exp(m2 - m1)
    g1 = 1.0 / (1.0 + e)
    g2 = e / (1.0 + e)
    pick1 = lane == i1
    pick2 = lane == i2
    picked = jnp.logical_or(pick1, pick2).astype(F32)

    @pl.when(first)
    def _():
        cnt_ref[...] = jnp.zeros_like(cnt_ref)

    r_i = lax.broadcasted_iota(jnp.int32, (tm, tm), 0)
    c_i = lax.broadcasted_iota(jnp.int32, (tm, tm), 1)
    lower = jnp.where(c_i < r_i, 1.0, 0.0).astype(BF16)
    before = jnp.dot(lower, picked.astype(BF16), preferred_element_type=F32) + cnt_ref[0:1, :]
    r1 = jnp.sum(jnp.where(pick1, before, 0.0), axis=1, keepdims=True)
    r2 = jnp.sum(jnp.where(pick2, before, 0.0), axis=1, keepdims=True)
    cnt_ref[...] = cnt_ref[...] + jnp.sum(picked, axis=0, keepdims=True)
    out = jnp.zeros((tm, lanes), F32)
    for k, val in enumerate([i1.astype(F32), i2.astype(F32), g1, g2, r1, r2]):
        out = jnp.where(lane == k, val, out)
    rt_ref[...] = out


def _postmix(x, attn, four, mod_l, g_ffn, wo_a, wo_f, router, *, n_ctx, seq, t, tile0):
    d = D_MODEL
    tm = TM_MIX
    nct = n_ctx // tm
    t = t - tile0 * tm
    full = lambda shape: pl.BlockSpec(shape, lambda i: (0,) * len(shape))
    tile = lambda w: pl.BlockSpec((tm, w), lambda i: (i, 0))
    x_specs, x_arrays = _token_specs(x, tm, nct, tile0)
    a_specs, a_arrays = _token_specs(attn, tm, nct, tile0)
    f_specs, f_arrays = _token_specs(four, tm, nct, tile0)
    in_specs = x_specs + a_specs + f_specs + [full(mod_l.shape), full((1, d)), full(wo_a.shape),
                                              full(wo_f.shape)]
    if router is None:
        out_specs = [tile(d), tile(d)]
        out_shape = [jax.ShapeDtypeStruct((t, d), F32), jax.ShapeDtypeStruct((t, d), BF16)]
    else:
        out_specs = [tile(d), tile(d // 2)]
        out_shape = [jax.ShapeDtypeStruct((t, d), F32), jax.ShapeDtypeStruct((t, d // 2), jnp.uint32)]
    args = x_arrays + a_arrays + f_arrays + [mod_l, g_ffn, wo_a, wo_f]
    if router is not None:
        in_specs += [full(router[0].shape), full(router[1].shape)]
        out_specs += [tile(LOGIT_LANES), pl.BlockSpec((8, LOGIT_LANES), lambda i: (0, 0))]
        out_shape += [jax.ShapeDtypeStruct((t, LOGIT_LANES), F32), jax.ShapeDtypeStruct((8, LOGIT_LANES), F32)]
        args += list(router)
    return pl.pallas_call(
        functools.partial(_postmix_kernel, n_x=len(x_arrays), n_a=len(a_arrays), n_f=len(f_arrays),
                          tile0=tile0, n_ctx_tiles=nct, tiles_per_batch=seq // tm,
                          with_router=router is not None),
        grid=(t // tm,),
        in_specs=in_specs,
        out_specs=out_specs,
        out_shape=out_shape,
        compiler_params=_cparams(1),
        name="postmix_wo_norm",
    )(*args)


def _silu_mul(gate, up):
    return (gate / (1.0 + jnp.exp(-gate))) * up


def _swiglu_chunks(x, n_chunks, wg_of, wu_of, wd_of, acc_ref):
    gate = jnp.dot(x, wg_of(0), preferred_element_type=F32)
    up = jnp.dot(x, wu_of(0), preferred_element_type=F32)
    for c in range(n_chunks):
        act = _silu_mul(gate, up).astype(BF16)
        if c + 1 < n_chunks:
            gate = jnp.dot(x, wg_of(c + 1), preferred_element_type=F32)
            up = jnp.dot(x, wu_of(c + 1), preferred_element_type=F32)
        acc_ref[...] += jnp.dot(act, wd_of(c), preferred_element_type=F32)


def _dense_ffn_kernel(h_ref, x1_ref, mod_ref, wg_ref, wu_ref, wd_ref, o_ref, acc_ref, *,
                      tile0, n_ctx_tiles, tiles_per_batch):
    i = pl.program_id(0) + tile0
    row = _mod_row(i, n_ctx_tiles, tiles_per_batch)
    acc_ref[...] = jnp.zeros_like(acc_ref)
    sub = lambda c: slice(c * FF_CHUNK, (c + 1) * FF_CHUNK)
    _swiglu_chunks(h_ref[...], wg_ref.shape[1] // FF_CHUNK, lambda c: wg_ref[:, sub(c)],
                   lambda c: wu_ref[:, sub(c)], lambda c: wd_ref[sub(c), :], acc_ref)
    o_ref[...] = x1_ref[...] + _mod_vec(mod_ref, row, 5) * acc_ref[...]


def _dense_ffn(h, x1, mod_l, wg, wu, wd, *, n_ctx, seq, tile0):
    t, d = x1.shape
    tm = TM_DENSE
    full = lambda shape: pl.BlockSpec(shape, lambda i: (0,) * len(shape))
    tile = lambda: pl.BlockSpec((tm, d), lambda i: (i, 0))
    return pl.pallas_call(
        functools.partial(_dense_ffn_kernel, tile0=tile0 * TM_MIX // tm, n_ctx_tiles=n_ctx // tm,
                          tiles_per_batch=seq // tm),
        grid=(t // tm,),
        in_specs=[tile(), tile(), full(mod_l.shape), full(wg.shape), full(wu.shape), full(wd.shape)],
        out_specs=tile(),
        out_shape=jax.ShapeDtypeStruct((t, d), F32),
        scratch_shapes=[pltpu.VMEM((tm, d), F32)],
        compiler_params=_cparams(1),
        name="dense_swiglu",
    )(h, x1, mod_l, wg, wu, wd)


def _moe_ffn_kernel(be_ref, nv_ref, x_ref, wg_ref, wu_ref, wd_ref, o_ref, xb_ref, acc_ref, wgb_ref, wub_ref,
                    wdb_ref):
    i = pl.program_id(0)
    j = pl.program_id(1)
    n_valid = nv_ref[i]

    @pl.when(j == 0)
    def _():
        x = _unpack_bf16_pairs(x_ref[...])
        rows = lax.broadcasted_iota(jnp.int32, x.shape, 0)
        xb_ref[...] = jnp.where(rows < n_valid, x, jnp.zeros_like(x))
        acc_ref[...] = jnp.zeros_like(acc_ref)

    n_sub = TF_MOE // FF_CHUNK
    sub = lambda c: slice(c * FF_CHUNK, (c + 1) * FF_CHUNK)
    rows_needed = ((n_valid + TAIL_ROWS - 1) // TAIL_ROWS) * TAIL_ROWS
    n_full = rows_needed // TM_PASS
    n_tail = (rows_needed - n_full * TM_PASS) // TAIL_ROWS

    def cast_chunk(src_ref, dst_ref, idx):
        w = src_ref[(0, 0) + idx].astype(BF16)
        dst_ref[idx] = w
        return w

    cast_wg = lambda c: cast_chunk(wg_ref, wgb_ref, (slice(None), sub(c)))
    cast_wu = lambda c: cast_chunk(wu_ref, wub_ref, (slice(None), sub(c)))
    cast_wd = lambda c: cast_chunk(wd_ref, wdb_ref, (sub(c), slice(None)))
    read_wg = lambda c: wgb_ref[:, sub(c)]
    read_wu = lambda c: wub_ref[:, sub(c)]
    read_wd = lambda c: wdb_ref[sub(c), :]

    @pl.when(jnp.logical_and(n_full == 0, n_tail > 0))
    def _():
        for c in range(n_sub):
            cast_wg(c), cast_wu(c), cast_wd(c)

    for r in range(TM_MOE // TM_PASS):
        rows = slice(r * TM_PASS, (r + 1) * TM_PASS)
        getters = (cast_wg, cast_wu, cast_wd) if r == 0 else (read_wg, read_wu, read_wd)

        @pl.when(r < n_full)
        def _():
            _swiglu_chunks(xb_ref[rows, :], n_sub, *getters, acc_ref.at[rows, :])

    @pl.when(n_tail > 0)
    def _():
        def group(gi, carry):
            rows = pl.ds(pl.multiple_of(n_full * TM_PASS + gi * TAIL_ROWS, TAIL_ROWS), TAIL_ROWS)
            _swiglu_chunks(xb_ref[rows, :], n_sub, read_wg, read_wu, read_wd, acc_ref.at[rows, :])
            return carry

        lax.fori_loop(0, n_tail, group, 0)

    @pl.when(j == pl.num_programs(1) - 1)
    def _():
        o_ref[...] = _pack_bf16_pairs(acc_ref[...].astype(BF16))


def _moe_ffn(xs, block_expert, n_valid, wg, wu, wd, layer):
    n_rows = xs.shape[0]
    d = D_MODEL
    tm, tf = TM_MOE, TF_MOE
    n_blocks = n_rows // tm
    n_ff = wg.shape[3] // tf

    def ff_idx(j, nv, i):
        return jnp.where(nv[i] > 0, j, n_ff - 1)

    grid_spec = pltpu.PrefetchScalarGridSpec(
        num_scalar_prefetch=2,
        grid=(n_blocks, n_ff),
        in_specs=[
            pl.BlockSpec((tm, d // 2), lambda i, j, be, nv: (i, 0)),
            pl.BlockSpec((1, 1, d, tf), lambda i, j, be, nv: (layer, be[i], 0, ff_idx(j, nv, i))),
            pl.BlockSpec((1, 1, d, tf), lambda i, j, be, nv: (layer, be[i], 0, ff_idx(j, nv, i))),
            pl.BlockSpec((1, 1, tf, d), lambda i, j, be, nv: (layer, be[i], ff_idx(j, nv, i), 0)),
        ],
        out_specs=pl.BlockSpec((tm, d // 2), lambda i, j, be, nv: (i, 0)),
        scratch_shapes=[pltpu.VMEM((tm, d), BF16), pltpu.VMEM((tm, d), F32), pltpu.VMEM((d, tf), BF16),
                        pltpu.VMEM((d, tf), BF16), pltpu.VMEM((tf, d), BF16)],
    )
    return pl.pallas_call(
        _moe_ffn_kernel,
        grid_spec=grid_spec,
        out_shape=jax.ShapeDtypeStruct((n_rows, d // 2), jnp.uint32),
        compiler_params=_cparams(2),
        name="expert_swiglu",
    )(block_expert, n_valid, xs, wg, wu, wd)


def _sc_mesh():
    return plsc.VectorSubcoreMesh(core_axis_name="c", subcore_axis_name="s")


def _sc_params():
    return pltpu.CompilerParams(use_tc_tiling_on_sc=True)


def _sc_dispatch(h_packed, dest, n_rows):
    t, w = h_packed.shape
    win = SC_DISPATCH_ROWS
    n_win = t // win
    idx = [dest[:, k].reshape(n_win, 1, win) for k in range(TOP_K)]

    @functools.partial(
        pl.kernel, out_type=jax.ShapeDtypeStruct((n_rows, w), h_packed.dtype), mesh=_sc_mesh(),
        scratch_types=[], compiler_params=_sc_params(), name="expert_dispatch_scatter")
    def run(x_hbm, i0_hbm, i1_hbm, o_hbm):
        def body(x_vmem, i0_vmem, i1_vmem):
            pltpu.sync_copy(x_vmem, o_hbm.at[i0_vmem.at[0, 0]])
            pltpu.sync_copy(x_vmem, o_hbm.at[i1_vmem.at[0, 0]])

        idx_spec = pl.BlockSpec((1, 1, win), lambda i: (i, 0, 0))
        pltpu.emit_pipeline(
            body, grid=(n_win,),
            in_specs=[pl.BlockSpec((win, w), lambda i: (i, 0)), idx_spec, idx_spec],
            out_specs=[], core_axis_name=("c", "s"), dimension_semantics=(pltpu.PARALLEL,),
        )(x_hbm, i0_hbm, i1_hbm)

    return run(h_packed, *idx)


def _sc_gather(ys, idx_flat):
    w = ys.shape[1]
    n = idx_flat.shape[0]
    win = SC_GATHER_ROWS
    n_win = n // win

    @functools.partial(
        pl.kernel, out_type=jax.ShapeDtypeStruct((n, w), ys.dtype), mesh=_sc_mesh(),
        scratch_types=[], compiler_params=_sc_params(), name="expert_combine_gather")
    def run(y_hbm, i_hbm, o_hbm):
        def body(i_vmem, o_vmem):
            pltpu.sync_copy(y_hbm.at[i_vmem.at[0, 0]], o_vmem)

        pltpu.emit_pipeline(
            body, grid=(n_win,),
            in_specs=[pl.BlockSpec((1, 1, win), lambda i: (i, 0, 0))],
            out_specs=[pl.BlockSpec((win, w), lambda i: (i, 0))],
            core_axis_name=("c", "s"), dimension_semantics=(pltpu.PARALLEL,),
        )(i_hbm, o_hbm)

    return run(ys, idx_flat.reshape(n_win, 1, win))


def _moe_combine_kernel(x1_ref, y0_ref, y1_ref, g_ref, mod_ref, o_ref, *, tile0, n_ctx_tiles,
                        tiles_per_batch):
    i = pl.program_id(0) + tile0
    row = _mod_row(i, n_ctx_tiles, tiles_per_batch)
    gates = g_ref[...]
    y0 = _unpack_bf16_pairs(y0_ref[...]).astype(F32)
    y1 = _unpack_bf16_pairs(y1_ref[...]).astype(F32)
    y = y0 * gates[:, 2:3] + y1 * gates[:, 3:4]
    o_ref[...] = x1_ref[...] + _mod_vec(mod_ref, row, 5) * y


def _moe_combine(x1, y_sel, gates, mod_l, *, n_ctx, seq, tile0):
    t, d = x1.shape
    tm = TM_MIX
    n_tiles = t // tm
    tile = lambda w: pl.BlockSpec((tm, w), lambda i: (i, 0))
    return pl.pallas_call(
        functools.partial(_moe_combine_kernel, tile0=tile0, n_ctx_tiles=n_ctx // tm,
                          tiles_per_batch=seq // tm),
        grid=(n_tiles,),
        in_specs=[tile(d), tile(d // 2), pl.BlockSpec((tm, d // 2), lambda i: (n_tiles + i, 0)),
                  tile(LOGIT_LANES),
                  pl.BlockSpec(mod_l.shape, lambda i: (0, 0))],
        out_specs=tile(d),
        out_shape=jax.ShapeDtypeStruct((t, d), F32),
        compiler_params=_cparams(1),
        name="expert_combine",
    )(x1, y_sel, y_sel, gates, mod_l)


def _route(route, counts):
    t = route.shape[0]
    expert = route[:, 0:TOP_K].astype(jnp.int32)
    rank = route[:, 4:4 + TOP_K].astype(jnp.int32)
    counts = counts[0, :N_EXPERTS].astype(jnp.int32)
    n_assign = t * TOP_K
    padded = ((counts + TM_MOE - 1) // TM_MOE) * TM_MOE
    pend = jnp.cumsum(padded)
    pstart = pend - padded
    onehot = expert[:, :, None] == jnp.arange(N_EXPERTS, dtype=jnp.int32)[None, None, :]
    dest = jnp.sum(jnp.where(onehot, pstart[None, None, :], 0), axis=-1) + rank
    n_blocks = -(-n_assign // TM_MOE) + N_EXPERTS
    block_start = jnp.arange(n_blocks, dtype=jnp.int32) * TM_MOE
    block_expert = jnp.minimum(jnp.sum(pend[None, :] <= block_start[:, None], axis=1),
                               N_EXPERTS - 1).astype(jnp.int32)
    n_valid = jnp.clip(counts[block_expert] - (block_start - pstart[block_expert]), 0, TM_MOE)
    n_valid = jnp.where(block_start < pend[-1], n_valid, 0).astype(jnp.int32)
    return dest.astype(jnp.int32), block_expert, n_valid, n_blocks * TM_MOE


def kernel(x, c, ctx, c_ctx, w_mod, b_mod, g_mix, g_ffn, g_q, g_k, w_in, w_four, w_o,
           w_gate_dense, w_up_dense, w_down_dense, w_router, b_router,
           w_gate_moe, w_up_moe, w_down_moe):
    b, s, d = x.shape
    n_ctx_len = ctx.shape[1]
    n_ctx = b * n_ctx_len
    t = n_ctx + b * s
    assert d == D_MODEL and b + 1 <= MOD_ROWS
    assert n_ctx % TM_DENSE == 0 and s % TM_DENSE == 0 and n_ctx_len % 128 == 0 and n_ctx % s == 0

    cvec = jnp.zeros((MOD_ROWS, d), F32).at[0].set(c_ctx).at[1:b + 1].set(c)
    mod = _modulation(cvec, w_mod, b_mod)

    cos_t, sin_t = _rope_tables_t(s, TM_MIX)
    c_lat, s_lat, rev_lat = _dft_half_tiles(s, TR_FOUR)
    c_ctx_m, s_ctx_m = _dft_mats(n_ctx_len)
    c_grp, s_grp = _dft_mats(FOURIER_GROUP)

    xa = (ctx.reshape(n_ctx, d), x.reshape(b * s, d))

    for l in range(DEPTH):
        mod_l = mod[l]
        w_in_l = w_in[l]
        wt = w_in_l[:, :QKV_WIDTH].T.astype(BF16)
        wf = w_in_l[:, QKV_WIDTH:].astype(BF16)
        gq = jnp.broadcast_to((g_q[l] * (LOG2_E * HEAD_DIM ** -0.5))[:, None], (HEAD_DIM, TM_MIX))
        gk = jnp.broadcast_to(g_k[l][:, None], (HEAD_DIM, TM_MIX))
        last = l == DEPTH - 1
        tile0 = n_ctx // TM_MIX if last else 0
        q_t, k_t, v_t, f = _premix(xa, mod_l, g_mix[l].reshape(1, d), wt, wf, gq, gk, cos_t, sin_t,
                                   n_ctx=n_ctx, seq=s, t=t)

        n_qt = s // TQ_ATTN
        lat_keys = [(n_ctx_len, lambda bb: bb), (s, lambda bb: n_ctx // s + bb)]
        attn_lat = _attention(q_t, k_t, v_t, q_tile0=n_ctx // TQ_ATTN, n_q_tiles=n_qt,
                              key_blocks=lat_keys, tq=TQ_ATTN, n_batch=b, name="attention_latent")
        attn_ctx = None if last else _attention(
            q_t, k_t, v_t, q_tile0=0, n_q_tiles=1, key_blocks=[(n_ctx_len, lambda bb: bb)],
            tq=n_ctx_len, n_batch=b, name="attention_context")

        wfour = w_four[l].astype(BF16)
        four_lat = _fourier_sym(f, c_lat, s_lat, rev_lat, c_grp, s_grp, wfour, n=s, tr=TR_FOUR,
                                x_block0=n_ctx // s, n_batch=b, name="fourier_latent")
        four_ctx = None if last else _fourier(
            f, c_ctx_m, s_ctx_m, c_grp, s_grp, wfour, n=n_ctx_len, tr=n_ctx_len, x_block0=0, n_batch=b,
            name="fourier_context")

        wo = w_o[l].astype(BF16)
        is_moe = l % 2 == 1
        li = l // 2
        router = None
        if is_moe:
            wr = jnp.zeros((d, LOGIT_LANES), BF16).at[:, :N_EXPERTS].set(w_router[li].astype(BF16))
            br = jnp.full((1, LOGIT_LANES), -jnp.inf, F32).at[0, :N_EXPERTS].set(b_router[li])
            router = (wr, br)
        res = _postmix(xa, (attn_ctx, attn_lat), (four_ctx, four_lat), mod_l, g_ffn[l].reshape(1, d),
                       wo[:ATTN_WIDTH], wo[ATTN_WIDTH:], router, n_ctx=n_ctx, seq=s, t=t, tile0=tile0)
        t_l = t - tile0 * TM_MIX
        if not is_moe:
            x1, h2 = res
            xa = _dense_ffn(h2, x1, mod_l, w_gate_dense[li].astype(BF16), w_up_dense[li].astype(BF16),
                            w_down_dense[li].astype(BF16), n_ctx=n_ctx, seq=s, tile0=tile0)
        else:
            x1, h2, route, counts = res
            dest, block_expert, n_valid, n_rows = _route(route, counts)
            xs = _sc_dispatch(h2, dest, n_rows)
            ys = _moe_ffn(xs, block_expert, n_valid, w_gate_moe, w_up_moe, w_down_moe, li)
            y_sel = _sc_gather(ys, dest.T.reshape(-1))
            xa = _moe_combine(x1, y_sel, route, mod_l, n_ctx=n_ctx, seq=s, tile0=tile0)

    return xa.reshape(b, s, d)
```

```python
import functools

import numpy as np
import jax
import jax.numpy as jnp
from jax import lax
from jax.experimental import pallas as pl
from jax.experimental.pallas import tpu as pltpu
from jax.experimental.pallas import tpu_sc as plsc

D_MODEL = 1024
DEPTH = 4
GRID_W = 64
HEAD_DIM = 64
ATTN_WIDTH = 512
N_Q_HEADS = 8
N_KV_HEADS = 2
KV_REP = 4
KV_WIDTH = 128
FOURIER_WIDTH = 512
N_FOURIER_GROUPS = 4
FOURIER_GROUP = 128
ROT_PER_AXIS = 32
ROPE_THETA = 10000.0
N_MOD = 6
D_FF_DENSE = 2816
N_EXPERTS = 8
TOP_K = 2
D_FF_EXPERT = 3584
EPS = 1e-6

QKV_WIDTH = ATTN_WIDTH + 2 * KV_WIDTH
MOD_ROWS = 16
LOGIT_LANES = 128

TM_MIX = 1024
TQ_ATTN = 512
KEY_CHUNK = 256
ONES_ROWS = 16
SCORE_LOOKAHEAD = 2
LOG2_E = 1.4426950408889634
TR_FOUR = 512
SYM_EXTRA = 16
TM_DENSE = 1024
FF_CHUNK = 256
TM_MOE = 2048
TM_PASS = 1024
TAIL_ROWS = 256
TF_MOE = 512
SC_DISPATCH_ROWS = 64
SC_GATHER_ROWS = 64
VMEM_LIMIT = 56 * 1024 * 1024

F32 = jnp.float32
BF16 = jnp.bfloat16


def _cparams(n_axes, flags=None):
    return pltpu.CompilerParams(
        dimension_semantics=("arbitrary",) * n_axes, vmem_limit_bytes=VMEM_LIMIT, flags=flags)


def _rope_tables_t(s, tm):
    n_rows = s // GRID_W
    rows = np.repeat(np.arange(n_rows), GRID_W).astype(np.float64)
    cols = np.tile(np.arange(GRID_W), n_rows).astype(np.float64)
    inv_freq = (ROPE_THETA ** (-np.arange(0, ROT_PER_AXIS, 2, dtype=np.float32) / ROT_PER_AXIS)
                ).astype(np.float32).astype(np.float64)
    ang_r = (rows[None, :].astype(np.float32) * inv_freq[:, None].astype(np.float32)).astype(np.float64)
    ang_c = (cols[None, :].astype(np.float32) * inv_freq[:, None].astype(np.float32)).astype(np.float64)
    cos = np.concatenate([np.cos(ang_r), np.cos(ang_r), np.cos(ang_c), np.cos(ang_c)], axis=0)
    sin = np.concatenate([-np.sin(ang_r), np.sin(ang_r), -np.sin(ang_c), np.sin(ang_c)], axis=0)
    cos = np.concatenate([np.ones((HEAD_DIM, tm)), cos], axis=1)
    sin = np.concatenate([np.zeros((HEAD_DIM, tm)), sin], axis=1)
    return jnp.asarray(cos, F32), jnp.asarray(sin, F32)


def _dft_mats(n):
    k = np.arange(n, dtype=np.int64)
    ang = 2.0 * np.pi * ((k[:, None] * k[None, :]) % n).astype(np.float64) / n
    return jnp.asarray(np.cos(ang), BF16), jnp.asarray(np.sin(ang), BF16)


def _mod_kernel(c_ref, w_ref, b_ref, o_ref):
    c = c_ref[...]
    s = (c / (1.0 + jnp.exp(-c))).astype(BF16)
    w = w_ref[0].astype(BF16)
    o_ref[0] = jnp.dot(s, w, preferred_element_type=F32) + b_ref[0]


def _modulation(cvec, w_mod, b_mod):
    depth, d, n = w_mod.shape
    tn = 1536
    return pl.pallas_call(
        _mod_kernel,
        grid=(depth, n // tn),
        in_specs=[
            pl.BlockSpec((MOD_ROWS, d), lambda l, j: (0, 0)),
            pl.BlockSpec((1, d, tn), lambda l, j: (l, 0, j)),
            pl.BlockSpec((1, 1, tn), lambda l, j: (l, 0, j)),
        ],
        out_specs=pl.BlockSpec((1, MOD_ROWS, tn), lambda l, j: (l, 0, j)),
        out_shape=jax.ShapeDtypeStruct((depth, MOD_ROWS, n), F32),
        compiler_params=_cparams(2),
        name="adaln_vectors",
    )(cvec, w_mod, b_mod.reshape(depth, 1, n))


def _mod_row(i, n_ctx_tiles, tiles_per_batch):
    lat = jnp.maximum(i - n_ctx_tiles, 0)
    return jnp.where(i < n_ctx_tiles, 0, lat // tiles_per_batch + 1)


def _mod_vec(mod_ref, row, comp):
    return mod_ref[pl.ds(row, 1), comp * D_MODEL:(comp + 1) * D_MODEL]


def _pack_bf16_pairs(xb):
    n = xb.shape[1] // 2
    bits = lax.bitcast_convert_type(xb.astype(F32), jnp.uint32)
    return (bits[:, :n] >> 16) | (bits[:, n:] & jnp.uint32(0xFFFF0000))


def _unpack_bf16_pairs(w):
    lo = lax.bitcast_convert_type(w << 16, F32)
    hi = lax.bitcast_convert_type(w & jnp.uint32(0xFFFF0000), F32)
    return jnp.concatenate([lo, hi], axis=1).astype(BF16)


def _token_specs(x, tm, n_ctx_tiles, tile0):
    if not isinstance(x, tuple):
        return [pl.BlockSpec((tm, x.shape[1]), lambda i: (i + tile0, 0))], [x]
    ctx, lat = x
    specs, arrays = [], []
    if ctx is not None:
        specs.append(pl.BlockSpec((tm, ctx.shape[1]), lambda i: (jnp.minimum(i + tile0, n_ctx_tiles - 1), 0)))
        arrays.append(ctx)
    else:
        assert tile0 >= n_ctx_tiles
    specs.append(pl.BlockSpec((tm, lat.shape[1]), lambda i: (jnp.maximum(i + tile0 - n_ctx_tiles, 0), 0)))
    arrays.append(lat)
    return specs, arrays


def _load_tokens(refs, is_ctx):
    if len(refs) == 2:
        return jnp.where(is_ctx, refs[0][...], refs[1][...])
    return refs[0][...]


def _norm_modulate(x, g, shift, scale):
    ms = jnp.mean(x * x, axis=-1, keepdims=True)
    y = x * lax.rsqrt(ms + EPS) * g
    return y * (1.0 + scale) + shift


def _premix_kernel(*refs, n_x, n_ctx_tiles, tiles_per_batch):
    x_refs = refs[:n_x]
    (mod_ref, g_ref, wt_ref, wf_ref, gq_ref, gk_ref, cos_ref, sin_ref,
     q_ref, k_ref, v_ref, f_ref) = refs[n_x:]
    i = pl.program_id(0)
    row = _mod_row(i, n_ctx_tiles, tiles_per_batch)
    x = _load_tokens(x_refs, i < n_ctx_tiles)
    h = _norm_modulate(x, g_ref[...], _mod_vec(mod_ref, row, 0), _mod_vec(mod_ref, row, 1))
    hb = h.astype(BF16)
    f_ref[...] = jnp.dot(hb, wf_ref[...], preferred_element_type=F32).astype(BF16)
    pt = lax.dot_general(wt_ref[...], hb, (((1,), (1,)), ((), ())), preferred_element_type=F32)
    v_ref[...] = pt[ATTN_WIDTH + KV_WIDTH:, :].astype(BF16)
    cos = cos_ref[...]
    sin = sin_ref[...]

    def norm_rope(xh, gain):
        ms = jnp.mean(xh * xh, axis=0, keepdims=True)
        y = xh * lax.rsqrt(ms + EPS) * gain
        half = ROT_PER_AXIS // 2
        swapped = jnp.concatenate(
            [y[half:2 * half], y[0:half], y[3 * half:4 * half], y[2 * half:3 * half]], axis=0)
        return y * cos + swapped * sin

    gq = gq_ref[...]
    gk = gk_ref[...]
    for hh in range(N_Q_HEADS):
        q_ref[hh * HEAD_DIM:(hh + 1) * HEAD_DIM, :] = norm_rope(
            pt[hh * HEAD_DIM:(hh + 1) * HEAD_DIM, :], gq).astype(BF16)
    for hh in range(N_KV_HEADS):
        lo = ATTN_WIDTH + hh * HEAD_DIM
        k_ref[hh * HEAD_DIM:(hh + 1) * HEAD_DIM, :] = norm_rope(pt[lo:lo + HEAD_DIM, :], gk).astype(BF16)


def _premix(x, mod_l, g_mix, wt, wf, gq, gk, cos_t, sin_t, *, n_ctx, seq, t):
    d = D_MODEL
    tm = TM_MIX
    n_ctx_tiles = n_ctx // tm
    tpb = seq // tm

    def tab_idx(i):
        lat = jnp.maximum(i - n_ctx_tiles, 0)
        return (0, jnp.where(i < n_ctx_tiles, 0, lax.rem(lat, tpb) + 1))

    full = lambda shape: pl.BlockSpec(shape, lambda i: (0,) * len(shape))
    x_specs, x_arrays = _token_specs(x, tm, n_ctx_tiles, 0)
    return pl.pallas_call(
        functools.partial(_premix_kernel, n_x=len(x_arrays), n_ctx_tiles=n_ctx_tiles, tiles_per_batch=tpb),
        grid=(t // tm,),
        in_specs=x_specs + [
            full(mod_l.shape),
            full((1, d)),
            full(wt.shape),
            full(wf.shape),
            full(gq.shape),
            full(gk.shape),
            pl.BlockSpec((HEAD_DIM, tm), tab_idx),
            pl.BlockSpec((HEAD_DIM, tm), tab_idx),
        ],
        out_specs=[
            pl.BlockSpec((ATTN_WIDTH, tm), lambda i: (0, i)),
            pl.BlockSpec((KV_WIDTH, tm), lambda i: (0, i)),
            pl.BlockSpec((KV_WIDTH, tm), lambda i: (0, i)),
            pl.BlockSpec((tm, FOURIER_WIDTH), lambda i: (i, 0)),
        ],
        out_shape=[
            jax.ShapeDtypeStruct((ATTN_WIDTH, t), BF16),
            jax.ShapeDtypeStruct((KV_WIDTH, t), BF16),
            jax.ShapeDtypeStruct((KV_WIDTH, t), BF16),
            jax.ShapeDtypeStruct((t, FOURIER_WIDTH), BF16),
        ],
        compiler_params=_cparams(1),
        name="premix_project",
    )(*x_arrays, mod_l, g_mix, wt, wf, gq, gk, cos_t, sin_t)


def _attn_kernel(*refs, n_key_blocks, tq):
    q_ref = refs[0]
    k_refs = refs[1:1 + n_key_blocks]
    v_refs = refs[1 + n_key_blocks:1 + 2 * n_key_blocks]
    o_ref = refs[1 + 2 * n_key_blocks]
    kall_ref, vg_ref, ot_ref = refs[2 + 2 * n_key_blocks:]
    g = pl.program_id(1)
    qt = pl.program_id(2)
    n_keys = kall_ref.shape[0]
    kc = KEY_CHUNK if n_keys % KEY_CHUNK == 0 else n_keys
    n_chunks = n_keys // kc
    slabs = kc // 8

    @pl.when(jnp.logical_and(g == 0, qt == 0))
    def _():
        off = 0
        for kr in k_refs:
            n = kr.shape[1]
            kall_ref[off:off + n, :] = kr[...].astype(F32).T.astype(BF16)
            off += n

    @pl.when(qt == 0)
    def _():
        g_rows = pl.ds(pl.multiple_of(g * HEAD_DIM, HEAD_DIM), HEAD_DIM)
        off = 0
        for vr in v_refs:
            n = vr.shape[1]
            vg_ref[0:HEAD_DIM, off:off + n] = vr[g_rows, :]
            off += n
        vg_ref[HEAD_DIM:, :] = jnp.ones((ONES_ROWS, n_keys), BF16)

    row_group = lax.broadcasted_iota(jnp.int32, (KV_WIDTH, tq), 0) // HEAD_DIM

    def masked_q(hh):
        qh = q_ref[hh * HEAD_DIM:(hh + 1) * HEAD_DIM, :]
        q2 = jnp.concatenate([qh, qh], axis=0)
        return jnp.where(row_group == g, q2, jnp.zeros_like(q2))

    q2s = [masked_q(hh) for hh in range(KV_REP)]
    items = [(hh, c) for hh in range(KV_REP) for c in range(n_chunks)]

    def score(item):
        hh, c = item
        return jnp.dot(kall_ref[c * kc:(c + 1) * kc, :], q2s[hh], preferred_element_type=F32)

    pending = [score(it) for it in items[:SCORE_LOOKAHEAD]]
    m = ot = None
    for i, (hh, c) in enumerate(items):
        if c == 0:
            m = jnp.full((1, tq), -jnp.inf, F32)
            ot = jnp.zeros((HEAD_DIM + ONES_ROWS, tq), F32)
        s = pending.pop(0).reshape(slabs, 8, tq)
        if i + SCORE_LOOKAHEAD < len(items):
            pending.append(score(items[i + SCORE_LOOKAHEAD]))
        m_new = jnp.maximum(m, jnp.max(jnp.max(s, axis=0), axis=0, keepdims=True))
        pb = jnp.exp2(s - m_new[None]).reshape(kc, tq).astype(BF16)
        rows = slice(c * kc, (c + 1) * kc)
        ot = jnp.exp2(m - m_new) * ot + jnp.dot(vg_ref[:, rows], pb, preferred_element_type=F32)
        m = m_new
        if c == n_chunks - 1:
            ot_ref[hh * HEAD_DIM:(hh + 1) * HEAD_DIM, :] = ot[:HEAD_DIM] / ot[HEAD_DIM:HEAD_DIM + 1]
    o_ref[...] = ot_ref[...].T.astype(BF16)


def _attention(q_t, k_t, v_t, *, q_tile0, n_q_tiles, key_blocks, tq, n_batch, name):
    n_keys = sum(c for c, _ in key_blocks)
    nkb = len(key_blocks)
    q_spec = pl.BlockSpec((KV_REP * HEAD_DIM, tq),
                          lambda b, g, i: (g, q_tile0 + b * n_q_tiles + i))
    k_specs = [pl.BlockSpec((KV_WIDTH, c), (lambda f: (lambda b, g, i: (0, f(b))))(f))
               for c, f in key_blocks]
    return pl.pallas_call(
        functools.partial(_attn_kernel, n_key_blocks=nkb, tq=tq),
        grid=(n_batch, N_KV_HEADS, n_q_tiles),
        in_specs=[q_spec] + k_specs + k_specs,
        out_specs=pl.BlockSpec((tq, KV_REP * HEAD_DIM),
                               lambda b, g, i: (b * n_q_tiles + i, g)),
        out_shape=jax.ShapeDtypeStruct((n_batch * n_q_tiles * tq, ATTN_WIDTH), BF16),
        scratch_shapes=[pltpu.VMEM((n_keys, KV_WIDTH), BF16),
                        pltpu.VMEM((HEAD_DIM + ONES_ROWS, n_keys), BF16),
                        pltpu.VMEM((KV_REP * HEAD_DIM, tq), F32)],
        compiler_params=_cparams(3),
        name=name,
    )(q_t, *([k_t] * nkb), *([v_t] * nkb))


def _fourier_kernel(c_ref, s_ref, x_ref, cc_ref, sc_ref, w_ref, o_ref, *, norm):
    x = x_ref[...]
    a = jnp.dot(c_ref[...], x, preferred_element_type=F32).astype(BF16)
    b = jnp.dot(s_ref[...], x, preferred_element_type=F32).astype(BF16)
    cc = cc_ref[...]
    sc = sc_ref[...]
    groups = [slice(grp * FOURIER_GROUP, (grp + 1) * FOURIER_GROUP) for grp in range(N_FOURIER_GROUPS)]
    specs = [(jnp.dot(a[:, sl], cc, preferred_element_type=F32)
              - jnp.dot(b[:, sl], sc, preferred_element_type=F32)) * norm for sl in groups]
    for grp, sl in enumerate(groups):
        o_ref[:, sl] = jnp.dot(specs[grp].astype(BF16), w_ref[grp], preferred_element_type=F32).astype(BF16)


def _fourier(f, cmat, smat, cc, sc, w_four, *, n, tr, x_block0, n_batch, name):
    n_row_tiles = n // tr
    return pl.pallas_call(
        functools.partial(_fourier_kernel, norm=float(1.0 / np.sqrt(n * FOURIER_GROUP))),
        grid=(n_row_tiles, n_batch),
        in_specs=[
            pl.BlockSpec((tr, n), lambda i, b: (i, 0)),
            pl.BlockSpec((tr, n), lambda i, b: (i, 0)),
            pl.BlockSpec((n, FOURIER_WIDTH), lambda i, b: (x_block0 + b, 0)),
            pl.BlockSpec(cc.shape, lambda i, b: (0, 0)),
            pl.BlockSpec(sc.shape, lambda i, b: (0, 0)),
            pl.BlockSpec(w_four.shape, lambda i, b: (0, 0, 0)),
        ],
        out_specs=pl.BlockSpec((tr, FOURIER_WIDTH), lambda i, b: (b * n_row_tiles + i, 0)),
        out_shape=jax.ShapeDtypeStruct((n_batch * n, FOURIER_WIDTH), BF16),
        compiler_params=_cparams(2),
        name=name,
    )(cmat, smat, f, cc, sc, w_four)


def _dft_half_tiles(n, tr):
    n_tiles = n // 2 // tr
    k = (np.arange(n_tiles, dtype=np.int64)[:, None] * tr
         + np.arange(tr + SYM_EXTRA, dtype=np.int64)[None, :])
    pos = np.arange(n, dtype=np.int64)
    ang = 2.0 * np.pi * ((k[:, :, None] * pos[None, None, :]) % n).astype(np.float64) / n
    rev = np.eye(tr, dtype=np.float32)[::-1]
    return jnp.asarray(np.cos(ang), BF16), jnp.asarray(np.sin(ang), BF16), jnp.asarray(rev, BF16)


def _fourier_sym_kernel(c_ref, s_ref, x_ref, cc_ref, sc_ref, w_ref, rev_ref, o_ref, *, norm, tr, n_tiles):
    i = pl.program_id(1)
    x = x_ref[...]
    a = jnp.dot(c_ref[0], x, preferred_element_type=F32).astype(BF16)
    b = jnp.dot(s_ref[0], x, preferred_element_type=F32).astype(BF16)
    cc = cc_ref[...]
    sc = sc_ref[...]
    groups = [slice(grp * FOURIER_GROUP, (grp + 1) * FOURIER_GROUP) for grp in range(N_FOURIER_GROUPS)]
    ps = [jnp.dot(a[:, sl], cc, preferred_element_type=F32) for sl in groups]
    qs = [jnp.dot(b[:, sl], sc, preferred_element_type=F32) for sl in groups]
    upper = jnp.concatenate([((p + q) * norm)[1:tr + 1] for p, q in zip(ps, qs)], axis=1).astype(BF16)
    upper = jnp.dot(rev_ref[...], upper, preferred_element_type=F32).astype(BF16)
    lo_rows = pl.ds(pl.multiple_of(i * tr, tr), tr)
    hi_rows = pl.ds(pl.multiple_of((2 * n_tiles - 1 - i) * tr, tr), tr)
    for grp, sl in enumerate(groups):
        lower = ((ps[grp] - qs[grp]) * norm)[:tr].astype(BF16)
        o_ref[lo_rows, sl] = jnp.dot(lower, w_ref[grp], preferred_element_type=F32).astype(BF16)
    for grp, sl in enumerate(groups):
        o_ref[hi_rows, sl] = jnp.dot(upper[:, sl], w_ref[grp], preferred_element_type=F32).astype(BF16)


def _fourier_sym(f, c_tiles, s_tiles, rev, cc, sc, w_four, *, n, tr, x_block0, n_batch, name):
    n_tiles = n // 2 // tr
    ext = tr + SYM_EXTRA
    return pl.pallas_call(
        functools.partial(_fourier_sym_kernel, norm=float(1.0 / np.sqrt(n * FOURIER_GROUP)), tr=tr,
                          n_tiles=n_tiles),
        grid=(n_batch, n_tiles),
        in_specs=[
            pl.BlockSpec((1, ext, n), lambda b, i: (i, 0, 0)),
            pl.BlockSpec((1, ext, n), lambda b, i: (i, 0, 0)),
            pl.BlockSpec((n, FOURIER_WIDTH), lambda b, i: (x_block0 + b, 0)),
            pl.BlockSpec(cc.shape, lambda b, i: (0, 0)),
            pl.BlockSpec(sc.shape, lambda b, i: (0, 0)),
            pl.BlockSpec(w_four.shape, lambda b, i: (0, 0, 0)),
            pl.BlockSpec(rev.shape, lambda b, i: (0, 0)),
        ],
        out_specs=pl.BlockSpec((n, FOURIER_WIDTH), lambda b, i: (b, 0)),
        out_shape=jax.ShapeDtypeStruct((n_batch * n, FOURIER_WIDTH), BF16),
        compiler_params=_cparams(2),
        name=name,
    )(c_tiles, s_tiles, f, cc, sc, w_four, rev)


def _postmix_kernel(*refs, n_x, n_a, n_f, tile0, n_ctx_tiles, tiles_per_batch, with_router):
    x_refs, refs = refs[:n_x], refs[n_x:]
    a_refs, refs = refs[:n_a], refs[n_a:]
    f_refs, refs = refs[:n_f], refs[n_f:]
    mod_ref, g_ref, woa_ref, wof_ref = refs[:4]
    if with_router:
        wr_ref, br_ref, x1_ref, h_ref, rt_ref, cnt_ref = refs[4:]
    else:
        x1_ref, h_ref = refs[4:]
    i = pl.program_id(0) + tile0
    row = _mod_row(i, n_ctx_tiles, tiles_per_batch)
    is_ctx = i < n_ctx_tiles
    a = _load_tokens(a_refs, is_ctx)
    f = _load_tokens(f_refs, is_ctx)
    mix = (jnp.dot(a, woa_ref[...], preferred_element_type=F32)
           + jnp.dot(f, wof_ref[...], preferred_element_type=F32))
    x1 = _load_tokens(x_refs, is_ctx) + _mod_vec(mod_ref, row, 2) * mix
    x1_ref[...] = x1
    h = _norm_modulate(x1, g_ref[...], _mod_vec(mod_ref, row, 3), _mod_vec(mod_ref, row, 4))
    hb = h.astype(BF16)
    if with_router:
        h_ref[...] = _pack_bf16_pairs(hb)
        logits = jnp.dot(hb, wr_ref[...], preferred_element_type=F32) + br_ref[...]
        _top2_route(logits, rt_ref, cnt_ref, first=pl.program_id(0) == 0)
    else:
        h_ref[...] = hb


def _top2_route(logits, rt_ref, cnt_ref, first):
    tm, lanes = logits.shape
    lane = lax.broadcasted_iota(jnp.int32, (tm, lanes), 1)
    neg = jnp.float32(-jnp.inf)
    m1 = jnp.max(logits, axis=1, keepdims=True)
    i1 = jnp.min(jnp.where(logits == m1, lane, lanes), axis=1, keepdims=True)
    rest = jnp.where(lane == i1, neg, logits)
    m2 = jnp.max(rest, axis=1, keepdims=True)
    i2 = jnp.min(jnp.where(rest == m2, lane, lanes), axis=1, keepdims=True)
    e = jnp.exp(m2 - m1)
    g1 = 1.0 / (1.0 + e)
    g2 = e / (1.0 + e)
    pick1 = lane == i1
    pick2 = lane == i2
    picked = jnp.logical_or(pick1, pick2).astype(F32)

    @pl.when(first)
    def _():
        cnt_ref[...] = jnp.zeros_like(cnt_ref)

    r_i = lax.broadcasted_iota(jnp.int32, (tm, tm), 0)
    c_i = lax.broadcasted_iota(jnp.int32, (tm, tm), 1)
    lower = jnp.where(c_i < r_i, 1.0, 0.0).astype(BF16)
    before = jnp.dot(lower, picked.astype(BF16), preferred_element_type=F32) + cnt_ref[0:1, :]
    r1 = jnp.sum(jnp.where(pick1, before, 0.0), axis=1, keepdims=True)
    r2 = jnp.sum(jnp.where(pick2, before, 0.0), axis=1, keepdims=True)
    cnt_ref[...] = cnt_ref[...] + jnp.sum(picked, axis=0, keepdims=True)
    out = jnp.zeros((tm, lanes), F32)
    for k, val in enumerate([i1.astype(F32), i2.astype(F32), g1, g2, r1, r2]):
        out = jnp.where(lane == k, val, out)
    rt_ref[...] = out


def _postmix(x, attn, four, mod_l, g_ffn, wo_a, wo_f, router, *, n_ctx, seq, t, tile0):
    d = D_MODEL
    tm = TM_MIX
    nct = n_ctx // tm
    t = t - tile0 * tm
    full = lambda shape: pl.BlockSpec(shape, lambda i: (0,) * len(shape))
    tile = lambda w: pl.BlockSpec((tm, w), lambda i: (i, 0))
    x_specs, x_arrays = _token_specs(x, tm, nct, tile0)
    a_specs, a_arrays = _token_specs(attn, tm, nct, tile0)
    f_specs, f_arrays = _token_specs(four, tm, nct, tile0)
    in_specs = x_specs + a_specs + f_specs + [full(mod_l.shape), full((1, d)), full(wo_a.shape),
                                              full(wo_f.shape)]
    if router is None:
        out_specs = [tile(d), tile(d)]
        out_shape = [jax.ShapeDtypeStruct((t, d), F32), jax.ShapeDtypeStruct((t, d), BF16)]
    else:
        out_specs = [tile(d), tile(d // 2)]
        out_shape = [jax.ShapeDtypeStruct((t, d), F32), jax.ShapeDtypeStruct((t, d // 2), jnp.uint32)]
    args = x_arrays + a_arrays + f_arrays + [mod_l, g_ffn, wo_a, wo_f]
    if router is not None:
        in_specs += [full(router[0].shape), full(router[1].shape)]
        out_specs += [tile(LOGIT_LANES), pl.BlockSpec((8, LOGIT_LANES), lambda i: (0, 0))]
        out_shape += [jax.ShapeDtypeStruct((t, LOGIT_LANES), F32), jax.ShapeDtypeStruct((8, LOGIT_LANES), F32)]
        args += list(router)
    return pl.pallas_call(
        functools.partial(_postmix_kernel, n_x=len(x_arrays), n_a=len(a_arrays), n_f=len(f_arrays),
                          tile0=tile0, n_ctx_tiles=nct, tiles_per_batch=seq // tm,
                          with_router=router is not None),
        grid=(t // tm,),
        in_specs=in_specs,
        out_specs=out_specs,
        out_shape=out_shape,
        compiler_params=_cparams(1),
        name="postmix_wo_norm",
    )(*args)


def _silu_mul(gate, up):
    return (gate / (1.0 + jnp.exp(-gate))) * up


def _swiglu_chunks(x, n_chunks, wg_of, wu_of, wd_of, acc_ref):
    gate = jnp.dot(x, wg_of(0), preferred_element_type=F32)
    up = jnp.dot(x, wu_of(0), preferred_element_type=F32)
    for c in range(n_chunks):
        act = _silu_mul(gate, up).astype(BF16)
        if c + 1 < n_chunks:
            gate = jnp.dot(x, wg_of(c + 1), preferred_element_type=F32)
            up = jnp.dot(x, wu_of(c + 1), preferred_element_type=F32)
        acc_ref[...] += jnp.dot(act, wd_of(c), preferred_element_type=F32)


def _dense_ffn_kernel(h_ref, x1_ref, mod_ref, wg_ref, wu_ref, wd_ref, o_ref, acc_ref, *,
                      tile0, n_ctx_tiles, tiles_per_batch):
    i = pl.program_id(0) + tile0
    row = _mod_row(i, n_ctx_tiles, tiles_per_batch)
    acc_ref[...] = jnp.zeros_like(acc_ref)
    sub = lambda c: slice(c * FF_CHUNK, (c + 1) * FF_CHUNK)
    _swiglu_chunks(h_ref[...], wg_ref.shape[1] // FF_CHUNK, lambda c: wg_ref[:, sub(c)],
                   lambda c: wu_ref[:, sub(c)], lambda c: wd_ref[sub(c), :], acc_ref)
    o_ref[...] = x1_ref[...] + _mod_vec(mod_ref, row, 5) * acc_ref[...]


def _dense_ffn(h, x1, mod_l, wg, wu, wd, *, n_ctx, seq, tile0):
    t, d = x1.shape
    tm = TM_DENSE
    full = lambda shape: pl.BlockSpec(shape, lambda i: (0,) * len(shape))
    tile = lambda: pl.BlockSpec((tm, d), lambda i: (i, 0))
    return pl.pallas_call(
        functools.partial(_dense_ffn_kernel, tile0=tile0 * TM_MIX // tm, n_ctx_tiles=n_ctx // tm,
                          tiles_per_batch=seq // tm),
        grid=(t // tm,),
        in_specs=[tile(), tile(), full(mod_l.shape), full(wg.shape), full(wu.shape), full(wd.shape)],
        out_specs=tile(),
        out_shape=jax.ShapeDtypeStruct((t, d), F32),
        scratch_shapes=[pltpu.VMEM((tm, d), F32)],
        compiler_params=_cparams(1),
        name="dense_swiglu",
    )(h, x1, mod_l, wg, wu, wd)


def _moe_ffn_kernel(be_ref, nv_ref, x_ref, wg_ref, wu_ref, wd_ref, o_ref, xb_ref, acc_ref, wgb_ref, wub_ref,
                    wdb_ref):
    i = pl.program_id(0)
    j = pl.program_id(1)
    n_valid = nv_ref[i]

    @pl.when(j == 0)
    def _():
        x = _unpack_bf16_pairs(x_ref[...])
        rows = lax.broadcasted_iota(jnp.int32, x.shape, 0)
        xb_ref[...] = jnp.where(rows < n_valid, x, jnp.zeros_like(x))
        acc_ref[...] = jnp.zeros_like(acc_ref)

    n_sub = TF_MOE // FF_CHUNK
    sub = lambda c: slice(c * FF_CHUNK, (c + 1) * FF_CHUNK)
    rows_needed = ((n_valid + TAIL_ROWS - 1) // TAIL_ROWS) * TAIL_ROWS
    n_full = rows_needed // TM_PASS
    n_tail = (rows_needed - n_full * TM_PASS) // TAIL_ROWS

    def cast_chunk(src_ref, dst_ref, idx):
        w = src_ref[(0, 0) + idx].astype(BF16)
        dst_ref[idx] = w
        return w

    cast_wg = lambda c: cast_chunk(wg_ref, wgb_ref, (slice(None), sub(c)))
    cast_wu = lambda c: cast_chunk(wu_ref, wub_ref, (slice(None), sub(c)))
    cast_wd = lambda c: cast_chunk(wd_ref, wdb_ref, (sub(c), slice(None)))
    read_wg = lambda c: wgb_ref[:, sub(c)]
    read_wu = lambda c: wub_ref[:, sub(c)]
    read_wd = lambda c: wdb_ref[sub(c), :]

    @pl.when(jnp.logical_and(n_full == 0, n_tail > 0))
    def _():
        for c in range(n_sub):
            cast_wg(c), cast_wu(c), cast_wd(c)

    for r in range(TM_MOE // TM_PASS):
        rows = slice(r * TM_PASS, (r + 1) * TM_PASS)
        getters = (cast_wg, cast_wu, cast_wd) if r == 0 else (read_wg, read_wu, read_wd)

        @pl.when(r < n_full)
        def _():
            _swiglu_chunks(xb_ref[rows, :], n_sub, *getters, acc_ref.at[rows, :])

    @pl.when(n_tail > 0)
    def _():
        def group(gi, carry):
            rows = pl.ds(pl.multiple_of(n_full * TM_PASS + gi * TAIL_ROWS, TAIL_ROWS), TAIL_ROWS)
            _swiglu_chunks(xb_ref[rows, :], n_sub, read_wg, read_wu, read_wd, acc_ref.at[rows, :])
            return carry

        lax.fori_loop(0, n_tail, group, 0)

    @pl.when(j == pl.num_programs(1) - 1)
    def _():
        o_ref[...] = _pack_bf16_pairs(acc_ref[...].astype(BF16))


def _moe_ffn(xs, block_expert, n_valid, wg, wu, wd, layer):
    n_rows = xs.shape[0]
    d = D_MODEL
    tm, tf = TM_MOE, TF_MOE
    n_blocks = n_rows // tm
    n_ff = wg.shape[3] // tf

    def ff_idx(j, nv, i):
        return jnp.where(nv[i] > 0, j, n_ff - 1)

    grid_spec = pltpu.PrefetchScalarGridSpec(
        num_scalar_prefetch=2,
        grid=(n_blocks, n_ff),
        in_specs=[
            pl.BlockSpec((tm, d // 2), lambda i, j, be, nv: (i, 0)),
            pl.BlockSpec((1, 1, d, tf), lambda i, j, be, nv: (layer, be[i], 0, ff_idx(j, nv, i))),
            pl.BlockSpec((1, 1, d, tf), lambda i, j, be, nv: (layer, be[i], 0, ff_idx(j, nv, i))),
            pl.BlockSpec((1, 1, tf, d), lambda i, j, be, nv: (layer, be[i], ff_idx(j, nv, i), 0)),
        ],
        out_specs=pl.BlockSpec((tm, d // 2), lambda i, j, be, nv: (i, 0)),
        scratch_shapes=[pltpu.VMEM((tm, d), BF16), pltpu.VMEM((tm, d), F32), pltpu.VMEM((d, tf), BF16),
                        pltpu.VMEM((d, tf), BF16), pltpu.VMEM((tf, d), BF16)],
    )
    return pl.pallas_call(
        _moe_ffn_kernel,
        grid_spec=grid_spec,
        out_shape=jax.ShapeDtypeStruct((n_rows, d // 2), jnp.uint32),
        compiler_params=_cparams(2),
        name="expert_swiglu",
    )(block_expert, n_valid, xs, wg, wu, wd)


def _sc_mesh():
    return plsc.VectorSubcoreMesh(core_axis_name="c", subcore_axis_name="s")


def _sc_params():
    return pltpu.CompilerParams(use_tc_tiling_on_sc=True)


def _sc_dispatch(h_packed, dest, n_rows):
    t, w = h_packed.shape
    win = SC_DISPATCH_ROWS
    n_win = t // win
    idx = [dest[:, k].reshape(n_win, 1, win) for k in range(TOP_K)]

    @functools.partial(
        pl.kernel, out_type=jax.ShapeDtypeStruct((n_rows, w), h_packed.dtype), mesh=_sc_mesh(),
        scratch_types=[], compiler_params=_sc_params(), name="expert_dispatch_scatter")
    def run(x_hbm, i0_hbm, i1_hbm, o_hbm):
        def body(x_vmem, i0_vmem, i1_vmem):
            pltpu.sync_copy(x_vmem, o_hbm.at[i0_vmem.at[0, 0]])
            pltpu.sync_copy(x_vmem, o_hbm.at[i1_vmem.at[0, 0]])

        idx_spec = pl.BlockSpec((1, 1, win), lambda i: (i, 0, 0))
        pltpu.emit_pipeline(
            body, grid=(n_win,),
            in_specs=[pl.BlockSpec((win, w), lambda i: (i, 0)), idx_spec, idx_spec],
            out_specs=[], core_axis_name=("c", "s"), dimension_semantics=(pltpu.PARALLEL,),
        )(x_hbm, i0_hbm, i1_hbm)

    return run(h_packed, *idx)


def _sc_gather(ys, idx_flat):
    w = ys.shape[1]
    n = idx_flat.shape[0]
    win = SC_GATHER_ROWS
    n_win = n // win

    @functools.partial(
        pl.kernel, out_type=jax.ShapeDtypeStruct((n, w), ys.dtype), mesh=_sc_mesh(),
        scratch_types=[], compiler_params=_sc_params(), name="expert_combine_gather")
    def run(y_hbm, i_hbm, o_hbm):
        def body(i_vmem, o_vmem):
            pltpu.sync_copy(y_hbm.at[i_vmem.at[0, 0]], o_vmem)

        pltpu.emit_pipeline(
            body, grid=(n_win,),
            in_specs=[pl.BlockSpec((1, 1, win), lambda i: (i, 0, 0))],
            out_specs=[pl.BlockSpec((win, w), lambda i: (i, 0))],
            core_axis_name=("c", "s"), dimension_semantics=(pltpu.PARALLEL,),
        )(i_hbm, o_hbm)

    return run(ys, idx_flat.reshape(n_win, 1, win))


def _moe_combine_kernel(x1_ref, y0_ref, y1_ref, g_ref, mod_ref, o_ref, *, tile0, n_ctx_tiles,
                        tiles_per_batch):
    i = pl.program_id(0) + tile0
    row = _mod_row(i, n_ctx_tiles, tiles_per_batch)
    gates = g_ref[...]
    y0 = _unpack_bf16_pairs(y0_ref[...]).astype(F32)
    y1 = _unpack_bf16_pairs(y1_ref[...]).astype(F32)
    y = y0 * gates[:, 2:3] + y1 * gates[:, 3:4]
    o_ref[...] = x1_ref[...] + _mod_vec(mod_ref, row, 5) * y


def _moe_combine(x1, y_sel, gates, mod_l, *, n_ctx, seq, tile0):
    t, d = x1.shape
    tm = TM_MIX
    n_tiles = t // tm
    tile = lambda w: pl.BlockSpec((tm, w), lambda i: (i, 0))
    return pl.pallas_call(
        functools.partial(_moe_combine_kernel, tile0=tile0, n_ctx_tiles=n_ctx // tm,
                          tiles_per_batch=seq // tm),
        grid=(n_tiles,),
        in_specs=[tile(d), tile(d // 2), pl.BlockSpec((tm, d // 2), lambda i: (n_tiles + i, 0)),
                  tile(LOGIT_LANES),
                  pl.BlockSpec(mod_l.shape, lambda i: (0, 0))],
        out_specs=tile(d),
        out_shape=jax.ShapeDtypeStruct((t, d), F32),
        compiler_params=_cparams(1),
        name="expert_combine",
    )(x1, y_sel, y_sel, gates, mod_l)


def _route(route, counts):
    t = route.shape[0]
    expert = route[:, 0:TOP_K].astype(jnp.int32)
    rank = route[:, 4:4 + TOP_K].astype(jnp.int32)
    counts = counts[0, :N_EXPERTS].astype(jnp.int32)
    n_assign = t * TOP_K
    padded = ((counts + TM_MOE - 1) // TM_MOE) * TM_MOE
    pend = jnp.cumsum(padded)
    pstart = pend - padded
    onehot = expert[:, :, None] == jnp.arange(N_EXPERTS, dtype=jnp.int32)[None, None, :]
    dest = jnp.sum(jnp.where(onehot, pstart[None, None, :], 0), axis=-1) + rank
    n_blocks = -(-n_assign // TM_MOE) + N_EXPERTS
    block_start = jnp.arange(n_blocks, dtype=jnp.int32) * TM_MOE
    block_expert = jnp.minimum(jnp.sum(pend[None, :] <= block_start[:, None], axis=1),
                               N_EXPERTS - 1).astype(jnp.int32)
    n_valid = jnp.clip(counts[block_expert] - (block_start - pstart[block_expert]), 0, TM_MOE)
    n_valid = jnp.where(block_start < pend[-1], n_valid, 0).astype(jnp.int32)
    return dest.astype(jnp.int32), block_expert, n_valid, n_blocks * TM_MOE


def kernel(x, c, ctx, c_ctx, w_mod, b_mod, g_mix, g_ffn, g_q, g_k, w_in, w_four, w_o,
           w_gate_dense, w_up_dense, w_down_dense, w_router, b_router,
           w_gate_moe, w_up_moe, w_down_moe):
    b, s, d = x.shape
    n_ctx_len = ctx.shape[1]
    n_ctx = b * n_ctx_len
    t = n_ctx + b * s
    assert d == D_MODEL and b + 1 <= MOD_ROWS
    assert n_ctx % TM_DENSE == 0 and s % TM_DENSE == 0 and n_ctx_len % 128 == 0 and n_ctx % s == 0

    cvec = jnp.zeros((MOD_ROWS, d), F32).at[0].set(c_ctx).at[1:b + 1].set(c)
    mod = _modulation(cvec, w_mod, b_mod)

    cos_t, sin_t = _rope_tables_t(s, TM_MIX)
    c_lat, s_lat, rev_lat = _dft_half_tiles(s, TR_FOUR)
    c_ctx_m, s_ctx_m = _dft_mats(n_ctx_len)
    c_grp, s_grp = _dft_mats(FOURIER_GROUP)

    xa = (ctx.reshape(n_ctx, d), x.reshape(b * s, d))

    for l in range(DEPTH):
        mod_l = mod[l]
        w_in_l = w_in[l]
        wt = w_in_l[:, :QKV_WIDTH].T.astype(BF16)
        wf = w_in_l[:, QKV_WIDTH:].astype(BF16)
        gq = jnp.broadcast_to((g_q[l] * (LOG2_E * HEAD_DIM ** -0.5))[:, None], (HEAD_DIM, TM_MIX))
        gk = jnp.broadcast_to(g_k[l][:, None], (HEAD_DIM, TM_MIX))
        last = l == DEPTH - 1
        tile0 = n_ctx // TM_MIX if last else 0
        q_t, k_t, v_t, f = _premix(xa, mod_l, g_mix[l].reshape(1, d), wt, wf, gq, gk, cos_t, sin_t,
                                   n_ctx=n_ctx, seq=s, t=t)

        n_qt = s // TQ_ATTN
        lat_keys = [(n_ctx_len, lambda bb: bb), (s, lambda bb: n_ctx // s + bb)]
        attn_lat = _attention(q_t, k_t, v_t, q_tile0=n_ctx // TQ_ATTN, n_q_tiles=n_qt,
                              key_blocks=lat_keys, tq=TQ_ATTN, n_batch=b, name="attention_latent")
        attn_ctx = None if last else _attention(
            q_t, k_t, v_t, q_tile0=0, n_q_tiles=1, key_blocks=[(n_ctx_len, lambda bb: bb)],
            tq=n_ctx_len, n_batch=b, name="attention_context")

        wfour = w_four[l].astype(BF16)
        four_lat = _fourier_sym(f, c_lat, s_lat, rev_lat, c_grp, s_grp, wfour, n=s, tr=TR_FOUR,
                                x_block0=n_ctx // s, n_batch=b, name="fourier_latent")
        four_ctx = None if last else _fourier(
            f, c_ctx_m, s_ctx_m, c_grp, s_grp, wfour, n=n_ctx_len, tr=n_ctx_len, x_block0=0, n_batch=b,
            name="fourier_context")

        wo = w_o[l].astype(BF16)
        is_moe = l % 2 == 1
        li = l // 2
        router = None
        if is_moe:
            wr = jnp.zeros((d, LOGIT_LANES), BF16).at[:, :N_EXPERTS].set(w_router[li].astype(BF16))
            br = jnp.full((1, LOGIT_LANES), -jnp.inf, F32).at[0, :N_EXPERTS].set(b_router[li])
            router = (wr, br)
        res = _postmix(xa, (attn_ctx, attn_lat), (four_ctx, four_lat), mod_l, g_ffn[l].reshape(1, d),
                       wo[:ATTN_WIDTH], wo[ATTN_WIDTH:], router, n_ctx=n_ctx, seq=s, t=t, tile0=tile0)
        t_l = t - tile0 * TM_MIX
        if not is_moe:
            x1, h2 = res
            xa = _dense_ffn(h2, x1, mod_l, w_gate_dense[li].astype(BF16), w_up_dense[li].astype(BF16),
                            w_down_dense[li].astype(BF16), n_ctx=n_ctx, seq=s, tile0=tile0)
        else:
            x1, h2, route, counts = res
            dest, block_expert, n_valid, n_rows = _route(route, counts)
            xs = _sc_dispatch(h2, dest, n_rows)
            ys = _moe_ffn(xs, block_expert, n_valid, w_gate_moe, w_up_moe, w_down_moe, li)
            y_sel = _sc_gather(ys, dest.T.reshape(-1))
            xa = _moe_combine(x1, y_sel, route, mod_l, n_ctx=n_ctx, seq=s, tile0=tile0)

    return xa.reshape(b, s, d)
```

```python
import functools

import numpy as np
import jax
import jax.numpy as jnp
from jax import lax
from jax.experimental import pallas as pl
from jax.experimental.pallas import tpu as pltpu
from jax.experimental.pallas import tpu_sc as plsc

D_MODEL = 1024
DEPTH = 4
GRID_W = 64
HEAD_DIM = 64
ATTN_WIDTH = 512
N_Q_HEADS = 8
N_KV_HEADS = 2
KV_REP = 4
KV_WIDTH = 128
FOURIER_WIDTH = 512
N_FOURIER_GROUPS = 4
FOURIER_GROUP = 128
ROT_PER_AXIS = 32
ROPE_THETA = 10000.0
N_MOD = 6
D_FF_DENSE = 2816
N_EXPERTS = 8
TOP_K = 2
D_FF_EXPERT = 3584
EPS = 1e-6

QKV_WIDTH = ATTN_WIDTH + 2 * KV_WIDTH
MOD_ROWS = 16
LOGIT_LANES = 128

TM_MIX = 1024
TQ_ATTN = 512
KEY_CHUNK = 256
ONES_ROWS = 16
SCORE_LOOKAHEAD = 2
LOG2_E = 1.4426950408889634
TR_FOUR = 512
SYM_EXTRA = 16
TM_DENSE = 1024
FF_CHUNK = 256
TM_MOE = 2048
TM_PASS = 1024
TAIL_ROWS = 256
TF_MOE = 512
SC_DISPATCH_ROWS = 64
SC_GATHER_ROWS = 64
VMEM_LIMIT = 56 * 1024 * 1024

F32 = jnp.float32
BF16 = jnp.bfloat16


def _cparams(n_axes, flags=None):
    return pltpu.CompilerParams(
        dimension_semantics=("arbitrary",) * n_axes, vmem_limit_bytes=VMEM_LIMIT, flags=flags)


def _rope_tables_t(s, tm):
    n_rows = s // GRID_W
    rows = np.repeat(np.arange(n_rows), GRID_W).astype(np.float64)
    cols = np.tile(np.arange(GRID_W), n_rows).astype(np.float64)
    inv_freq = (ROPE_THETA ** (-np.arange(0, ROT_PER_AXIS, 2, dtype=np.float32) / ROT_PER_AXIS)
                ).astype(np.float32).astype(np.float64)
    ang_r = (rows[None, :].astype(np.float32) * inv_freq[:, None].astype(np.float32)).astype(np.float64)
    ang_c = (cols[None, :].astype(np.float32) * inv_freq[:, None].astype(np.float32)).astype(np.float64)
    cos = np.concatenate([np.cos(ang_r), np.cos(ang_r), np.cos(ang_c), np.cos(ang_c)], axis=0)
    sin = np.concatenate([-np.sin(ang_r), np.sin(ang_r), -np.sin(ang_c), np.sin(ang_c)], axis=0)
    cos = np.concatenate([np.ones((HEAD_DIM, tm)), cos], axis=1)
    sin = np.concatenate([np.zeros((HEAD_DIM, tm)), sin], axis=1)
    return jnp.asarray(cos, F32), jnp.asarray(sin, F32)


def _dft_mats(n):
    k = np.arange(n, dtype=np.int64)
    ang = 2.0 * np.pi * ((k[:, None] * k[None, :]) % n).astype(np.float64) / n
    return jnp.asarray(np.cos(ang), BF16), jnp.asarray(np.sin(ang), BF16)


def _mod_kernel(c_ref, w_ref, b_ref, o_ref):
    c = c_ref[...]
    s = (c / (1.0 + jnp.exp(-c))).astype(BF16)
    w = w_ref[0].astype(BF16)
    o_ref[0] = jnp.dot(s, w, preferred_element_type=F32) + b_ref[0]


def _modulation(cvec, w_mod, b_mod):
    depth, d, n = w_mod.shape
    tn = 1536
    return pl.pallas_call(
        _mod_kernel,
        grid=(depth, n // tn),
        in_specs=[
            pl.BlockSpec((MOD_ROWS, d), lambda l, j: (0, 0)),
            pl.BlockSpec((1, d, tn), lambda l, j: (l, 0, j)),
            pl.BlockSpec((1, 1, tn), lambda l, j: (l, 0, j)),
        ],
        out_specs=pl.BlockSpec((1, MOD_ROWS, tn), lambda l, j: (l, 0, j)),
        out_shape=jax.ShapeDtypeStruct((depth, MOD_ROWS, n), F32),
        compiler_params=_cparams(2),
        name="adaln_vectors",
    )(cvec, w_mod, b_mod.reshape(depth, 1, n))


def _mod_row(i, n_ctx_tiles, tiles_per_batch):
    lat = jnp.maximum(i - n_ctx_tiles, 0)
    return jnp.where(i < n_ctx_tiles, 0, lat // tiles_per_batch + 1)


def _mod_vec(mod_ref, row, comp):
    return mod_ref[pl.ds(row, 1), comp * D_MODEL:(comp + 1) * D_MODEL]


def _pack_bf16_pairs(xb):
    n = xb.shape[1] // 2
    bits = lax.bitcast_convert_type(xb.astype(F32), jnp.uint32)
    return (bits[:, :n] >> 16) | (bits[:, n:] & jnp.uint32(0xFFFF0000))


def _unpack_bf16_pairs(w):
    lo = lax.bitcast_convert_type(w << 16, F32)
    hi = lax.bitcast_convert_type(w & jnp.uint32(0xFFFF0000), F32)
    return jnp.concatenate([lo, hi], axis=1).astype(BF16)


def _token_specs(x, tm, n_ctx_tiles, tile0):
    if not isinstance(x, tuple):
        return [pl.BlockSpec((tm, x.shape[1]), lambda i: (i + tile0, 0))], [x]
    ctx, lat = x
    specs, arrays = [], []
    if ctx is not None:
        specs.append(pl.BlockSpec((tm, ctx.shape[1]), lambda i: (jnp.minimum(i + tile0, n_ctx_tiles - 1), 0)))
        arrays.append(ctx)
    else:
        assert tile0 >= n_ctx_tiles
    specs.append(pl.BlockSpec((tm, lat.shape[1]), lambda i: (jnp.maximum(i + tile0 - n_ctx_tiles, 0), 0)))
    arrays.append(lat)
    return specs, arrays


def _load_tokens(refs, is_ctx):
    if len(refs) == 2:
        return jnp.where(is_ctx, refs[0][...], refs[1][...])
    return refs[0][...]


def _norm_modulate(x, g, shift, scale):
    ms = jnp.mean(x * x, axis=-1, keepdims=True)
    y = x * lax.rsqrt(ms + EPS) * g
    return y * (1.0 + scale) + shift


def _premix_kernel(*refs, n_x, n_ctx_tiles, tiles_per_batch):
    x_refs = refs[:n_x]
    (mod_ref, g_ref, wt_ref, wf_ref, gq_ref, gk_ref, cos_ref, sin_ref,
     q_ref, k_ref, v_ref, f_ref) = refs[n_x:]
    i = pl.program_id(0)
    row = _mod_row(i, n_ctx_tiles, tiles_per_batch)
    x = _load_tokens(x_refs, i < n_ctx_tiles)
    h = _norm_modulate(x, g_ref[...], _mod_vec(mod_ref, row, 0), _mod_vec(mod_ref, row, 1))
    hb = h.astype(BF16)
    f_ref[...] = jnp.dot(hb, wf_ref[...], preferred_element_type=F32).astype(BF16)
    pt = lax.dot_general(wt_ref[...], hb, (((1,), (1,)), ((), ())), preferred_element_type=F32)
    v_ref[...] = pt[ATTN_WIDTH + KV_WIDTH:, :].astype(BF16)
    cos = cos_ref[...]
    sin = sin_ref[...]

    def norm_rope(xh, gain):
        ms = jnp.mean(xh * xh, axis=0, keepdims=True)
        y = xh * lax.rsqrt(ms + EPS) * gain
        half = ROT_PER_AXIS // 2
        swapped = jnp.concatenate(
            [y[half:2 * half], y[0:half], y[3 * half:4 * half], y[2 * half:3 * half]], axis=0)
        return y * cos + swapped * sin

    gq = gq_ref[...]
    gk = gk_ref[...]
    for hh in range(N_Q_HEADS):
        q_ref[hh * HEAD_DIM:(hh + 1) * HEAD_DIM, :] = norm_rope(
            pt[hh * HEAD_DIM:(hh + 1) * HEAD_DIM, :], gq).astype(BF16)
    for hh in range(N_KV_HEADS):
        lo = ATTN_WIDTH + hh * HEAD_DIM
        k_ref[hh * HEAD_DIM:(hh + 1) * HEAD_DIM, :] = norm_rope(pt[lo:lo + HEAD_DIM, :], gk).astype(BF16)


def _premix(x, mod_l, g_mix, wt, wf, gq, gk, cos_t, sin_t, *, n_ctx, seq, t):
    d = D_MODEL
    tm = TM_MIX
    n_ctx_tiles = n_ctx // tm
    tpb = seq // tm

    def tab_idx(i):
        lat = jnp.maximum(i - n_ctx_tiles, 0)
        return (0, jnp.where(i < n_ctx_tiles, 0, lax.rem(lat, tpb) + 1))

    full = lambda shape: pl.BlockSpec(shape, lambda i: (0,) * len(shape))
    x_specs, x_arrays = _token_specs(x, tm, n_ctx_tiles, 0)
    return pl.pallas_call(
        functools.partial(_premix_kernel, n_x=len(x_arrays), n_ctx_tiles=n_ctx_tiles, tiles_per_batch=tpb),
        grid=(t // tm,),
        in_specs=x_specs + [
            full(mod_l.shape),
            full((1, d)),
            full(wt.shape),
            full(wf.shape),
            full(gq.shape),
            full(gk.shape),
            pl.BlockSpec((HEAD_DIM, tm), tab_idx),
            pl.BlockSpec((HEAD_DIM, tm), tab_idx),
        ],
        out_specs=[
            pl.BlockSpec((ATTN_WIDTH, tm), lambda i: (0, i)),
            pl.BlockSpec((KV_WIDTH, tm), lambda i: (0, i)),
            pl.BlockSpec((KV_WIDTH, tm), lambda i: (0, i)),
            pl.BlockSpec((tm, FOURIER_WIDTH), lambda i: (i, 0)),
        ],
        out_shape=[
            jax.ShapeDtypeStruct((ATTN_WIDTH, t), BF16),
            jax.ShapeDtypeStruct((KV_WIDTH, t), BF16),
            jax.ShapeDtypeStruct((KV_WIDTH, t), BF16),
            jax.ShapeDtypeStruct((t, FOURIER_WIDTH), BF16),
        ],
        compiler_params=_cparams(1),
        name="premix_project",
    )(*x_arrays, mod_l, g_mix, wt, wf, gq, gk, cos_t, sin_t)


def _attn_kernel(*refs, n_key_blocks, tq):
    q_ref = refs[0]
    k_refs = refs[1:1 + n_key_blocks]
    v_refs = refs[1 + n_key_blocks:1 + 2 * n_key_blocks]
    o_ref = refs[1 + 2 * n_key_blocks]
    kall_ref, vg_ref, ot_ref = refs[2 + 2 * n_key_blocks:]
    g = pl.program_id(1)
    qt = pl.program_id(2)
    n_keys = kall_ref.shape[0]
    kc = KEY_CHUNK if n_keys % KEY_CHUNK == 0 else n_keys
    n_chunks = n_keys // kc
    slabs = kc // 8

    @pl.when(jnp.logical_and(g == 0, qt == 0))
    def _():
        off = 0
        for kr in k_refs:
            n = kr.shape[1]
            kall_ref[off:off + n, :] = kr[...].astype(F32).T.astype(BF16)
            off += n

    @pl.when(qt == 0)
    def _():
        g_rows = pl.ds(pl.multiple_of(g * HEAD_DIM, HEAD_DIM), HEAD_DIM)
        off = 0
        for vr in v_refs:
            n = vr.shape[1]
            vg_ref[0:HEAD_DIM, off:off + n] = vr[g_rows, :]
            off += n
        vg_ref[HEAD_DIM:, :] = jnp.ones((ONES_ROWS, n_keys), BF16)

    row_group = lax.broadcasted_iota(jnp.int32, (KV_WIDTH, tq), 0) // HEAD_DIM

    def masked_q(hh):
        qh = q_ref[hh * HEAD_DIM:(hh + 1) * HEAD_DIM, :]
        q2 = jnp.concatenate([qh, qh], axis=0)
        return jnp.where(row_group == g, q2, jnp.zeros_like(q2))

    q2s = [masked_q(hh) for hh in range(KV_REP)]
    items = [(hh, c) for hh in range(KV_REP) for c in range(n_chunks)]

    def score(item):
        hh, c = item
        return jnp.dot(kall_ref[c * kc:(c + 1) * kc, :], q2s[hh], preferred_element_type=F32)

    pending = [score(it) for it in items[:SCORE_LOOKAHEAD]]
    m = ot = None
    for i, (hh, c) in enumerate(items):
        if c == 0:
            m = jnp.full((1, tq), -jnp.inf, F32)
            ot = jnp.zeros((HEAD_DIM + ONES_ROWS, tq), F32)
        s = pending.pop(0).reshape(slabs, 8, tq)
        if i + SCORE_LOOKAHEAD < len(items):
            pending.append(score(items[i + SCORE_LOOKAHEAD]))
        m_new = jnp.maximum(m, jnp.max(jnp.max(s, axis=0), axis=0, keepdims=True))
        pb = jnp.exp2(s - m_new[None]).reshape(kc, tq).astype(BF16)
        rows = slice(c * kc, (c + 1) * kc)
        ot = jnp.exp2(m - m_new) * ot + jnp.dot(vg_ref[:, rows], pb, preferred_element_type=F32)
        m = m_new
        if c == n_chunks - 1:
            ot_ref[hh * HEAD_DIM:(hh + 1) * HEAD_DIM, :] = ot[:HEAD_DIM] / ot[HEAD_DIM:HEAD_DIM + 1]
    o_ref[...] = ot_ref[...].T.astype(BF16)


def _attention(q_t, k_t, v_t, *, q_tile0, n_q_tiles, key_blocks, tq, n_batch, name):
    n_keys = sum(c for c, _ in key_blocks)
    nkb = len(key_blocks)
    q_spec = pl.BlockSpec((KV_REP * HEAD_DIM, tq),
                          lambda b, g, i: (g, q_tile0 + b * n_q_tiles + i))
    k_specs = [pl.BlockSpec((KV_WIDTH, c), (lambda f: (lambda b, g, i: (0, f(b))))(f))
               for c, f in key_blocks]
    return pl.pallas_call(
        functools.partial(_attn_kernel, n_key_blocks=nkb, tq=tq),
        grid=(n_batch, N_KV_HEADS, n_q_tiles),
        in_specs=[q_spec] + k_specs + k_specs,
        out_specs=pl.BlockSpec((tq, KV_REP * HEAD_DIM),
                               lambda b, g, i: (b * n_q_tiles + i, g)),
        out_shape=jax.ShapeDtypeStruct((n_batch * n_q_tiles * tq, ATTN_WIDTH), BF16),
        scratch_shapes=[pltpu.VMEM((n_keys, KV_WIDTH), BF16),
                        pltpu.VMEM((HEAD_DIM + ONES_ROWS, n_keys), BF16),
                        pltpu.VMEM((KV_REP * HEAD_DIM, tq), F32)],
        compiler_params=_cparams(3),
        name=name,
    )(q_t, *([k_t] * nkb), *([v_t] * nkb))


def _fourier_kernel(c_ref, s_ref, x_ref, cc_ref, sc_ref, w_ref, o_ref, *, norm):
    x = x_ref[...]
    a = jnp.dot(c_ref[...], x, preferred_element_type=F32).astype(BF16)
    b = jnp.dot(s_ref[...], x, preferred_element_type=F32).astype(BF16)
    cc = cc_ref[...]
    sc = sc_ref[...]
    groups = [slice(grp * FOURIER_GROUP, (grp + 1) * FOURIER_GROUP) for grp in range(N_FOURIER_GROUPS)]
    specs = [(jnp.dot(a[:, sl], cc, preferred_element_type=F32)
              - jnp.dot(b[:, sl], sc, preferred_element_type=F32)) * norm for sl in groups]
    for grp, sl in enumerate(groups):
        o_ref[:, sl] = jnp.dot(specs[grp].astype(BF16), w_ref[grp], preferred_element_type=F32).astype(BF16)


def _fourier(f, cmat, smat, cc, sc, w_four, *, n, tr, x_block0, n_batch, name):
    n_row_tiles = n // tr
    return pl.pallas_call(
        functools.partial(_fourier_kernel, norm=float(1.0 / np.sqrt(n * FOURIER_GROUP))),
        grid=(n_row_tiles, n_batch),
        in_specs=[
            pl.BlockSpec((tr, n), lambda i, b: (i, 0)),
            pl.BlockSpec((tr, n), lambda i, b: (i, 0)),
            pl.BlockSpec((n, FOURIER_WIDTH), lambda i, b: (x_block0 + b, 0)),
            pl.BlockSpec(cc.shape, lambda i, b: (0, 0)),
            pl.BlockSpec(sc.shape, lambda i, b: (0, 0)),
            pl.BlockSpec(w_four.shape, lambda i, b: (0, 0, 0)),
        ],
        out_specs=pl.BlockSpec((tr, FOURIER_WIDTH), lambda i, b: (b * n_row_tiles + i, 0)),
        out_shape=jax.ShapeDtypeStruct((n_batch * n, FOURIER_WIDTH), BF16),
        compiler_params=_cparams(2),
        name=name,
    )(cmat, smat, f, cc, sc, w_four)


def _dft_half_tiles(n, tr):
    n_tiles = n // 2 // tr
    k = (np.arange(n_tiles, dtype=np.int64)[:, None] * tr
         + np.arange(tr + SYM_EXTRA, dtype=np.int64)[None, :])
    pos = np.arange(n, dtype=np.int64)
    ang = 2.0 * np.pi * ((k[:, :, None] * pos[None, None, :]) % n).astype(np.float64) / n
    rev = np.eye(tr, dtype=np.float32)[::-1]
    return jnp.asarray(np.cos(ang), BF16), jnp.asarray(np.sin(ang), BF16), jnp.asarray(rev, BF16)


def _fourier_sym_kernel(c_ref, s_ref, x_ref, cc_ref, sc_ref, w_ref, rev_ref, o_ref, *, norm, tr, n_tiles):
    i = pl.program_id(1)
    x = x_ref[...]
    a = jnp.dot(c_ref[0], x, preferred_element_type=F32).astype(BF16)
    b = jnp.dot(s_ref[0], x, preferred_element_type=F32).astype(BF16)
    cc = cc_ref[...]
    sc = sc_ref[...]
    groups = [slice(grp * FOURIER_GROUP, (grp + 1) * FOURIER_GROUP) for grp in range(N_FOURIER_GROUPS)]
    ps = [jnp.dot(a[:, sl], cc, preferred_element_type=F32) for sl in groups]
    qs = [jnp.dot(b[:, sl], sc, preferred_element_type=F32) for sl in groups]
    upper = jnp.concatenate([((p + q) * norm)[1:tr + 1] for p, q in zip(ps, qs)], axis=1).astype(BF16)
    upper = jnp.dot(rev_ref[...], upper, preferred_element_type=F32).astype(BF16)
    lo_rows = pl.ds(pl.multiple_of(i * tr, tr), tr)
    hi_rows = pl.ds(pl.multiple_of((2 * n_tiles - 1 - i) * tr, tr), tr)
    for grp, sl in enumerate(groups):
        lower = ((ps[grp] - qs[grp]) * norm)[:tr].astype(BF16)
        o_ref[lo_rows, sl] = jnp.dot(lower, w_ref[grp], preferred_element_type=F32).astype(BF16)
    for grp, sl in enumerate(groups):
        o_ref[hi_rows, sl] = jnp.dot(upper[:, sl], w_ref[grp], preferred_element_type=F32).astype(BF16)


def _fourier_sym(f, c_tiles, s_tiles, rev, cc, sc, w_four, *, n, tr, x_block0, n_batch, name):
    n_tiles = n // 2 // tr
    ext = tr + SYM_EXTRA
    return pl.pallas_call(
        functools.partial(_fourier_sym_kernel, norm=float(1.0 / np.sqrt(n * FOURIER_GROUP)), tr=tr,
                          n_tiles=n_tiles),
        grid=(n_batch, n_tiles),
        in_specs=[
            pl.BlockSpec((1, ext, n), lambda b, i: (i, 0, 0)),
            pl.BlockSpec((1, ext, n), lambda b, i: (i, 0, 0)),
            pl.BlockSpec((n, FOURIER_WIDTH), lambda b, i: (x_block0 + b, 0)),
            pl.BlockSpec(cc.shape, lambda b, i: (0, 0)),
            pl.BlockSpec(sc.shape, lambda b, i: (0, 0)),
            pl.BlockSpec(w_four.shape, lambda b, i: (0, 0, 0)),
            pl.BlockSpec(rev.shape, lambda b, i: (0, 0)),
        ],
        out_specs=pl.BlockSpec((n, FOURIER_WIDTH), lambda b, i: (b, 0)),
        out_shape=jax.ShapeDtypeStruct((n_batch * n, FOURIER_WIDTH), BF16),
        compiler_params=_cparams(2),
        name=name,
    )(c_tiles, s_tiles, f, cc, sc, w_four, rev)


def _postmix_kernel(*refs, n_x, n_a, n_f, tile0, n_ctx_tiles, tiles_per_batch, with_router):
    x_refs, refs = refs[:n_x], refs[n_x:]
    a_refs, refs = refs[:n_a], refs[n_a:]
    f_refs, refs = refs[:n_f], refs[n_f:]
    mod_ref, g_ref, woa_ref, wof_ref = refs[:4]
    if with_router:
        wr_ref, br_ref, x1_ref, h_ref, rt_ref, rtt_ref, cnt_ref = refs[4:]
    else:
        x1_ref, h_ref = refs[4:]
    i = pl.program_id(0) + tile0
    row = _mod_row(i, n_ctx_tiles, tiles_per_batch)
    is_ctx = i < n_ctx_tiles
    a = _load_tokens(a_refs, is_ctx)
    f = _load_tokens(f_refs, is_ctx)
    mix = (jnp.dot(a, woa_ref[...], preferred_element_type=F32)
           + jnp.dot(f, wof_ref[...], preferred_element_type=F32))
    x1 = _load_tokens(x_refs, is_ctx) + _mod_vec(mod_ref, row, 2) * mix
    x1_ref[...] = x1
    h = _norm_modulate(x1, g_ref[...], _mod_vec(mod_ref, row, 3), _mod_vec(mod_ref, row, 4))
    hb = h.astype(BF16)
    if with_router:
        h_ref[...] = _pack_bf16_pairs(hb)
        logits = jnp.dot(hb, wr_ref[...], preferred_element_type=F32) + br_ref[...]
        _top2_route(logits, rt_ref, rtt_ref, cnt_ref, first=pl.program_id(0) == 0)
    else:
        h_ref[...] = hb


def _top2_route(logits, rt_ref, rtt_ref, cnt_ref, first):
    tm, lanes = logits.shape
    lane = lax.broadcasted_iota(jnp.int32, (tm, lanes), 1)
    neg = jnp.float32(-jnp.inf)
    m1 = jnp.max(logits, axis=1, keepdims=True)
    i1 = jnp.min(jnp.where(logits == m1, lane, lanes), axis=1, keepdims=True)
    rest = jnp.where(lane == i1, neg, logits)
    m2 = jnp.max(rest, axis=1, keepdims=True)
    i2 = jnp.min(jnp.where(rest == m2, lane, lanes), axis=1, keepdims=True)
    e = jnp.exp(m2 - m1)
    g1 = 1.0 / (1.0 + e)
    g2 = e / (1.0 + e)
    pick1 = lane == i1
    pick2 = lane == i2
    picked = jnp.logical_or(pick1, pick2).astype(F32)

    @pl.when(first)
    def _():
        cnt_ref[...] = jnp.zeros_like(cnt_ref)

    r_i = lax.broadcasted_iota(jnp.int32, (tm, tm), 0)
    c_i = lax.broadcasted_iota(jnp.int32, (tm, tm), 1)
    lower = jnp.where(c_i < r_i, 1.0, 0.0).astype(BF16)
    before = jnp.dot(lower, picked.astype(BF16), preferred_element_type=F32) + cnt_ref[0:1, :]
    r1 = jnp.sum(jnp.where(pick1, before, 0.0), axis=1, keepdims=True)
    r2 = jnp.sum(jnp.where(pick2, before, 0.0), axis=1, keepdims=True)
    cnt_ref[...] = cnt_ref[...] + jnp.sum(picked, axis=0, keepdims=True)
    out = jnp.zeros((tm, lanes), F32)
    for k, val in enumerate([i1.astype(F32), i2.astype(F32), g1, g2, r1, r2]):
        out = jnp.where(lane == k, val, out)
    rt_ref[...] = out
    rtt_ref[...] = out.T[0:8, :]


def _postmix(x, attn, four, mod_l, g_ffn, wo_a, wo_f, router, *, n_ctx, seq, t, tile0):
    d = D_MODEL
    tm = TM_MIX
    nct = n_ctx // tm
    t = t - tile0 * tm
    full = lambda shape: pl.BlockSpec(shape, lambda i: (0,) * len(shape))
    tile = lambda w: pl.BlockSpec((tm, w), lambda i: (i, 0))
    x_specs, x_arrays = _token_specs(x, tm, nct, tile0)
    a_specs, a_arrays = _token_specs(attn, tm, nct, tile0)
    f_specs, f_arrays = _token_specs(four, tm, nct, tile0)
    in_specs = x_specs + a_specs + f_specs + [full(mod_l.shape), full((1, d)), full(wo_a.shape),
                                              full(wo_f.shape)]
    if router is None:
        out_specs = [tile(d), tile(d)]
        out_shape = [jax.ShapeDtypeStruct((t, d), F32), jax.ShapeDtypeStruct((t, d), BF16)]
    else:
        out_specs = [tile(d), tile(d // 2)]
        out_shape = [jax.ShapeDtypeStruct((t, d), F32), jax.ShapeDtypeStruct((t, d // 2), jnp.uint32)]
    args = x_arrays + a_arrays + f_arrays + [mod_l, g_ffn, wo_a, wo_f]
    if router is not None:
        in_specs += [full(router[0].shape), full(router[1].shape)]
        out_specs += [tile(LOGIT_LANES), pl.BlockSpec((8, tm), lambda i: (0, i)),
                      pl.BlockSpec((8, LOGIT_LANES), lambda i: (0, 0))]
        out_shape += [jax.ShapeDtypeStruct((t, LOGIT_LANES), F32), jax.ShapeDtypeStruct((8, t), F32),
                      jax.ShapeDtypeStruct((8, LOGIT_LANES), F32)]
        args += list(router)
    return pl.pallas_call(
        functools.partial(_postmix_kernel, n_x=len(x_arrays), n_a=len(a_arrays), n_f=len(f_arrays),
                          tile0=tile0, n_ctx_tiles=nct, tiles_per_batch=seq // tm,
                          with_router=router is not None),
        grid=(t // tm,),
        in_specs=in_specs,
        out_specs=out_specs,
        out_shape=out_shape,
        compiler_params=_cparams(1),
        name="postmix_wo_norm",
    )(*args)


def _silu_mul(gate, up):
    return (gate / (1.0 + jnp.exp(-gate))) * up


def _swiglu_chunks(x, n_chunks, wg_of, wu_of, wd_of, acc_ref):
    gate = jnp.dot(x, wg_of(0), preferred_element_type=F32)
    up = jnp.dot(x, wu_of(0), preferred_element_type=F32)
    for c in range(n_chunks):
        act = _silu_mul(gate, up).astype(BF16)
        if c + 1 < n_chunks:
            gate = jnp.dot(x, wg_of(c + 1), preferred_element_type=F32)
            up = jnp.dot(x, wu_of(c + 1), preferred_element_type=F32)
        acc_ref[...] += jnp.dot(act, wd_of(c), preferred_element_type=F32)


def _dense_ffn_kernel(h_ref, x1_ref, mod_ref, wg_ref, wu_ref, wd_ref, o_ref, acc_ref, *,
                      tile0, n_ctx_tiles, tiles_per_batch):
    i = pl.program_id(0) + tile0
    row = _mod_row(i, n_ctx_tiles, tiles_per_batch)
    acc_ref[...] = jnp.zeros_like(acc_ref)
    sub = lambda c: slice(c * FF_CHUNK, (c + 1) * FF_CHUNK)
    _swiglu_chunks(h_ref[...], wg_ref.shape[1] // FF_CHUNK, lambda c: wg_ref[:, sub(c)],
                   lambda c: wu_ref[:, sub(c)], lambda c: wd_ref[sub(c), :], acc_ref)
    o_ref[...] = x1_ref[...] + _mod_vec(mod_ref, row, 5) * acc_ref[...]


def _dense_ffn(h, x1, mod_l, wg, wu, wd, *, n_ctx, seq, tile0):
    t, d = x1.shape
    tm = TM_DENSE
    full = lambda shape: pl.BlockSpec(shape, lambda i: (0,) * len(shape))
    tile = lambda: pl.BlockSpec((tm, d), lambda i: (i, 0))
    return pl.pallas_call(
        functools.partial(_dense_ffn_kernel, tile0=tile0 * TM_MIX // tm, n_ctx_tiles=n_ctx // tm,
                          tiles_per_batch=seq // tm),
        grid=(t // tm,),
        in_specs=[tile(), tile(), full(mod_l.shape), full(wg.shape), full(wu.shape), full(wd.shape)],
        out_specs=tile(),
        out_shape=jax.ShapeDtypeStruct((t, d), F32),
        scratch_shapes=[pltpu.VMEM((tm, d), F32)],
        compiler_params=_cparams(1),
        name="dense_swiglu",
    )(h, x1, mod_l, wg, wu, wd)


def _moe_ffn_kernel(be_ref, nv_ref, x_ref, wg_ref, wu_ref, wd_ref, o_ref, xb_ref, acc_ref, wgb_ref, wub_ref,
                    wdb_ref):
    i = pl.program_id(0)
    j = pl.program_id(1)
    n_valid = nv_ref[i]

    @pl.when(j == 0)
    def _():
        x = _unpack_bf16_pairs(x_ref[...])
        rows = lax.broadcasted_iota(jnp.int32, x.shape, 0)
        xb_ref[...] = jnp.where(rows < n_valid, x, jnp.zeros_like(x))
        acc_ref[...] = jnp.zeros_like(acc_ref)

    n_sub = TF_MOE // FF_CHUNK
    sub = lambda c: slice(c * FF_CHUNK, (c + 1) * FF_CHUNK)
    rows_needed = ((n_valid + TAIL_ROWS - 1) // TAIL_ROWS) * TAIL_ROWS
    n_full = rows_needed // TM_PASS
    n_tail = (rows_needed - n_full * TM_PASS) // TAIL_ROWS

    def cast_chunk(src_ref, dst_ref, idx):
        w = src_ref[(0, 0) + idx].astype(BF16)
        dst_ref[idx] = w
        return w

    cast_wg = lambda c: cast_chunk(wg_ref, wgb_ref, (slice(None), sub(c)))
    cast_wu = lambda c: cast_chunk(wu_ref, wub_ref, (slice(None), sub(c)))
    cast_wd = lambda c: cast_chunk(wd_ref, wdb_ref, (sub(c), slice(None)))
    read_wg = lambda c: wgb_ref[:, sub(c)]
    read_wu = lambda c: wub_ref[:, sub(c)]
    read_wd = lambda c: wdb_ref[sub(c), :]

    @pl.when(jnp.logical_and(n_full == 0, n_tail > 0))
    def _():
        for c in range(n_sub):
            cast_wg(c), cast_wu(c), cast_wd(c)

    for r in range(TM_MOE // TM_PASS):
        rows = slice(r * TM_PASS, (r + 1) * TM_PASS)
        getters = (cast_wg, cast_wu, cast_wd) if r == 0 else (read_wg, read_wu, read_wd)

        @pl.when(r < n_full)
        def _():
            _swiglu_chunks(xb_ref[rows, :], n_sub, *getters, acc_ref.at[rows, :])

    @pl.when(n_tail > 0)
    def _():
        def group(gi, carry):
            rows = pl.ds(pl.multiple_of(n_full * TM_PASS + gi * TAIL_ROWS, TAIL_ROWS), TAIL_ROWS)
            _swiglu_chunks(xb_ref[rows, :], n_sub, read_wg, read_wu, read_wd, acc_ref.at[rows, :])
            return carry

        lax.fori_loop(0, n_tail, group, 0)

    @pl.when(j == pl.num_programs(1) - 1)
    def _():
        o_ref[...] = _pack_bf16_pairs(acc_ref[...].astype(BF16))


def _moe_ffn(xs, block_expert, n_valid, wg, wu, wd, layer):
    n_rows = xs.shape[0]
    d = D_MODEL
    tm, tf = TM_MOE, TF_MOE
    n_blocks = n_rows // tm
    n_ff = wg.shape[3] // tf

    def ff_idx(j, nv, i):
        return jnp.where(nv[i] > 0, j, n_ff - 1)

    grid_spec = pltpu.PrefetchScalarGridSpec(
        num_scalar_prefetch=2,
        grid=(n_blocks, n_ff),
        in_specs=[
            pl.BlockSpec((tm, d // 2), lambda i, j, be, nv: (i, 0)),
            pl.BlockSpec((1, 1, d, tf), lambda i, j, be, nv: (layer, be[i], 0, ff_idx(j, nv, i))),
            pl.BlockSpec((1, 1, d, tf), lambda i, j, be, nv: (layer, be[i], 0, ff_idx(j, nv, i))),
            pl.BlockSpec((1, 1, tf, d), lambda i, j, be, nv: (layer, be[i], ff_idx(j, nv, i), 0)),
        ],
        out_specs=pl.BlockSpec((tm, d // 2), lambda i, j, be, nv: (i, 0)),
        scratch_shapes=[pltpu.VMEM((tm, d), BF16), pltpu.VMEM((tm, d), F32), pltpu.VMEM((d, tf), BF16),
                        pltpu.VMEM((d, tf), BF16), pltpu.VMEM((tf, d), BF16)],
    )
    return pl.pallas_call(
        _moe_ffn_kernel,
        grid_spec=grid_spec,
        out_shape=jax.ShapeDtypeStruct((n_rows, d // 2), jnp.uint32),
        compiler_params=_cparams(2),
        name="expert_swiglu",
    )(block_expert, n_valid, xs, wg, wu, wd)


def _sc_mesh():
    return plsc.VectorSubcoreMesh(core_axis_name="c", subcore_axis_name="s")


def _sc_params():
    return pltpu.CompilerParams(use_tc_tiling_on_sc=True)


def _sc_dispatch(h_packed, dest, n_rows):
    t, w = h_packed.shape
    win = SC_DISPATCH_ROWS
    n_win = t // win
    idx = [dest[k].reshape(n_win, 1, win) for k in range(TOP_K)]

    @functools.partial(
        pl.kernel, out_type=jax.ShapeDtypeStruct((n_rows, w), h_packed.dtype), mesh=_sc_mesh(),
        scratch_types=[], compiler_params=_sc_params(), name="expert_dispatch_scatter")
    def run(x_hbm, i0_hbm, i1_hbm, o_hbm):
        def body(x_vmem, i0_vmem, i1_vmem):
            pltpu.sync_copy(x_vmem, o_hbm.at[i0_vmem.at[0, 0]])
            pltpu.sync_copy(x_vmem, o_hbm.at[i1_vmem.at[0, 0]])

        idx_spec = pl.BlockSpec((1, 1, win), lambda i: (i, 0, 0))
        pltpu.emit_pipeline(
            body, grid=(n_win,),
            in_specs=[pl.BlockSpec((win, w), lambda i: (i, 0)), idx_spec, idx_spec],
            out_specs=[], core_axis_name=("c", "s"), dimension_semantics=(pltpu.PARALLEL,),
        )(x_hbm, i0_hbm, i1_hbm)

    return run(h_packed, *idx)


def _sc_gather(ys, idx_flat):
    w = ys.shape[1]
    n = idx_flat.shape[0]
    win = SC_GATHER_ROWS
    n_win = n // win

    @functools.partial(
        pl.kernel, out_type=jax.ShapeDtypeStruct((n, w), ys.dtype), mesh=_sc_mesh(),
        scratch_types=[], compiler_params=_sc_params(), name="expert_combine_gather")
    def run(y_hbm, i_hbm, o_hbm):
        def body(i_vmem, o_vmem):
            pltpu.sync_copy(y_hbm.at[i_vmem.at[0, 0]], o_vmem)

        pltpu.emit_pipeline(
            body, grid=(n_win,),
            in_specs=[pl.BlockSpec((1, 1, win), lambda i: (i, 0, 0))],
            out_specs=[pl.BlockSpec((win, w), lambda i: (i, 0))],
            core_axis_name=("c", "s"), dimension_semantics=(pltpu.PARALLEL,),
        )(i_hbm, o_hbm)

    return run(ys, idx_flat.reshape(n_win, 1, win))


def _moe_combine_kernel(x1_ref, y0_ref, y1_ref, g_ref, mod_ref, o_ref, *, tile0, n_ctx_tiles,
                        tiles_per_batch):
    i = pl.program_id(0) + tile0
    row = _mod_row(i, n_ctx_tiles, tiles_per_batch)
    gates = g_ref[...]
    y0 = _unpack_bf16_pairs(y0_ref[...]).astype(F32)
    y1 = _unpack_bf16_pairs(y1_ref[...]).astype(F32)
    y = y0 * gates[:, 2:3] + y1 * gates[:, 3:4]
    o_ref[...] = x1_ref[...] + _mod_vec(mod_ref, row, 5) * y


def _moe_combine(x1, y_sel, gates, mod_l, *, n_ctx, seq, tile0):
    t, d = x1.shape
    tm = TM_MIX
    n_tiles = t // tm
    tile = lambda w: pl.BlockSpec((tm, w), lambda i: (i, 0))
    return pl.pallas_call(
        functools.partial(_moe_combine_kernel, tile0=tile0, n_ctx_tiles=n_ctx // tm,
                          tiles_per_batch=seq // tm),
        grid=(n_tiles,),
        in_specs=[tile(d), tile(d // 2), pl.BlockSpec((tm, d // 2), lambda i: (n_tiles + i, 0)),
                  tile(LOGIT_LANES),
                  pl.BlockSpec(mod_l.shape, lambda i: (0, 0))],
        out_specs=tile(d),
        out_shape=jax.ShapeDtypeStruct((t, d), F32),
        compiler_params=_cparams(1),
        name="expert_combine",
    )(x1, y_sel, y_sel, gates, mod_l)


def _route(route_t, counts):
    t = route_t.shape[1]
    expert = route_t[0:TOP_K].astype(jnp.int32)
    rank = route_t[4:4 + TOP_K].astype(jnp.int32)
    counts = counts[0, :N_EXPERTS].astype(jnp.int32)
    n_assign = t * TOP_K
    padded = ((counts + TM_MOE - 1) // TM_MOE) * TM_MOE
    pend = jnp.cumsum(padded)
    pstart = pend - padded
    dest = rank
    for e in range(N_EXPERTS):
        dest = dest + jnp.where(expert == e, pstart[e], 0)
    n_blocks = -(-n_assign // TM_MOE) + N_EXPERTS
    block_start = jnp.arange(n_blocks, dtype=jnp.int32) * TM_MOE
    block_expert = jnp.minimum(jnp.sum(pend[None, :] <= block_start[:, None], axis=1),
                               N_EXPERTS - 1).astype(jnp.int32)
    n_valid = jnp.clip(counts[block_expert] - (block_start - pstart[block_expert]), 0, TM_MOE)
    n_valid = jnp.where(block_start < pend[-1], n_valid, 0).astype(jnp.int32)
    return dest.astype(jnp.int32), block_expert, n_valid, n_blocks * TM_MOE


def kernel(x, c, ctx, c_ctx, w_mod, b_mod, g_mix, g_ffn, g_q, g_k, w_in, w_four, w_o,
           w_gate_dense, w_up_dense, w_down_dense, w_router, b_router,
           w_gate_moe, w_up_moe, w_down_moe):
    b, s, d = x.shape
    n_ctx_len = ctx.shape[1]
    n_ctx = b * n_ctx_len
    t = n_ctx + b * s
    assert d == D_MODEL and b + 1 <= MOD_ROWS
    assert n_ctx % TM_DENSE == 0 and s % TM_DENSE == 0 and n_ctx_len % 128 == 0 and n_ctx % s == 0

    cvec = jnp.zeros((MOD_ROWS, d), F32).at[0].set(c_ctx).at[1:b + 1].set(c)
    mod = _modulation(cvec, w_mod, b_mod)

    cos_t, sin_t = _rope_tables_t(s, TM_MIX)
    c_lat, s_lat, rev_lat = _dft_half_tiles(s, TR_FOUR)
    c_ctx_m, s_ctx_m = _dft_mats(n_ctx_len)
    c_grp, s_grp = _dft_mats(FOURIER_GROUP)

    xa = (ctx.reshape(n_ctx, d), x.reshape(b * s, d))

    for l in range(DEPTH):
        mod_l = mod[l]
        w_in_l = w_in[l]
        wt = w_in_l[:, :QKV_WIDTH].T.astype(BF16)
        wf = w_in_l[:, QKV_WIDTH:].astype(BF16)
        gq = jnp.broadcast_to((g_q[l] * (LOG2_E * HEAD_DIM ** -0.5))[:, None], (HEAD_DIM, TM_MIX))
        gk = jnp.broadcast_to(g_k[l][:, None], (HEAD_DIM, TM_MIX))
        last = l == DEPTH - 1
        tile0 = n_ctx // TM_MIX if last else 0
        q_t, k_t, v_t, f = _premix(xa, mod_l, g_mix[l].reshape(1, d), wt, wf, gq, gk, cos_t, sin_t,
                                   n_ctx=n_ctx, seq=s, t=t)

        n_qt = s // TQ_ATTN
        lat_keys = [(n_ctx_len, lambda bb: bb), (s, lambda bb: n_ctx // s + bb)]
        attn_lat = _attention(q_t, k_t, v_t, q_tile0=n_ctx // TQ_ATTN, n_q_tiles=n_qt,
                              key_blocks=lat_keys, tq=TQ_ATTN, n_batch=b, name="attention_latent")
        attn_ctx = None if last else _attention(
            q_t, k_t, v_t, q_tile0=0, n_q_tiles=1, key_blocks=[(n_ctx_len, lambda bb: bb)],
            tq=n_ctx_len, n_batch=b, name="attention_context")

        wfour = w_four[l].astype(BF16)
        four_lat = _fourier_sym(f, c_lat, s_lat, rev_lat, c_grp, s_grp, wfour, n=s, tr=TR_FOUR,
                                x_block0=n_ctx // s, n_batch=b, name="fourier_latent")
        four_ctx = None if last else _fourier(
            f, c_ctx_m, s_ctx_m, c_grp, s_grp, wfour, n=n_ctx_len, tr=n_ctx_len, x_block0=0, n_batch=b,
            name="fourier_context")

        wo = w_o[l].astype(BF16)
        is_moe = l % 2 == 1
        li = l // 2
        router = None
        if is_moe:
            wr = jnp.zeros((d, LOGIT_LANES), BF16).at[:, :N_EXPERTS].set(w_router[li].astype(BF16))
            br = jnp.full((1, LOGIT_LANES), -jnp.inf, F32).at[0, :N_EXPERTS].set(b_router[li])
            router = (wr, br)
        res = _postmix(xa, (attn_ctx, attn_lat), (four_ctx, four_lat), mod_l, g_ffn[l].reshape(1, d),
                       wo[:ATTN_WIDTH], wo[ATTN_WIDTH:], router, n_ctx=n_ctx, seq=s, t=t, tile0=tile0)
        t_l = t - tile0 * TM_MIX
        if not is_moe:
            x1, h2 = res
            xa = _dense_ffn(h2, x1, mod_l, w_gate_dense[li].astype(BF16), w_up_dense[li].astype(BF16),
                            w_down_dense[li].astype(BF16), n_ctx=n_ctx, seq=s, tile0=tile0)
        else:
            x1, h2, route, route_t, counts = res
            dest, block_expert, n_valid, n_rows = _route(route_t, counts)
            xs = _sc_dispatch(h2, dest, n_rows)
            ys = _moe_ffn(xs, block_expert, n_valid, w_gate_moe, w_up_moe, w_down_moe, li)
            y_sel = _sc_gather(ys, dest.reshape(-1))
            xa = _moe_combine(x1, y_sel, route, mod_l, n_ctx=n_ctx, seq=s, tile0=tile0)

    return xa.reshape(b, s, d)
```

```python
import functools

import numpy as np
import jax
import jax.numpy as jnp
from jax import lax
from jax.experimental import pallas as pl
from jax.experimental.pallas import tpu as pltpu
from jax.experimental.pallas import tpu_sc as plsc

D_MODEL = 1024
DEPTH = 4
GRID_W = 64
HEAD_DIM = 64
ATTN_WIDTH = 512
N_Q_HEADS = 8
N_KV_HEADS = 2
KV_REP = 4
KV_WIDTH = 128
FOURIER_WIDTH = 512
N_FOURIER_GROUPS = 4
FOURIER_GROUP = 128
ROT_PER_AXIS = 32
ROPE_THETA = 10000.0
N_EXPERTS = 8
TOP_K = 2
EPS = 1e-6

QKV_WIDTH = ATTN_WIDTH + 2 * KV_WIDTH
MOD_ROWS = 16
LOGIT_LANES = 128

TM_MIX = 1024
TQ_ATTN = 512
KEY_CHUNK = 256
ONES_ROWS = 16
SCORE_LOOKAHEAD = 2
LOG2_E = 1.4426950408889634
TR_FOUR = 512
SYM_EXTRA = 16
TM_DENSE = 1024
FF_CHUNK = 256
TM_MOE = 2048
TM_PASS = 1024
TAIL_ROWS = 256
TF_MOE = 512
SC_DISPATCH_ROWS = 64
SC_GATHER_ROWS = 64
V7X_VMEM_BYTES = 64 * 1024 * 1024
VMEM_LIMIT = V7X_VMEM_BYTES - 8 * 1024 * 1024

F32 = jnp.float32
BF16 = jnp.bfloat16


def _cparams(n_axes):
    return pltpu.CompilerParams(
        dimension_semantics=("arbitrary",) * n_axes, vmem_limit_bytes=VMEM_LIMIT)


def _rope_tables_t(s, tm):
    n_rows = s // GRID_W
    rows = np.repeat(np.arange(n_rows), GRID_W).astype(np.float64)
    cols = np.tile(np.arange(GRID_W), n_rows).astype(np.float64)
    inv_freq = (ROPE_THETA ** (-np.arange(0, ROT_PER_AXIS, 2, dtype=np.float32) / ROT_PER_AXIS)
                ).astype(np.float32).astype(np.float64)
    ang_r = (rows[None, :].astype(np.float32) * inv_freq[:, None].astype(np.float32)).astype(np.float64)
    ang_c = (cols[None, :].astype(np.float32) * inv_freq[:, None].astype(np.float32)).astype(np.float64)
    cos = np.concatenate([np.cos(ang_r), np.cos(ang_r), np.cos(ang_c), np.cos(ang_c)], axis=0)
    sin = np.concatenate([-np.sin(ang_r), np.sin(ang_r), -np.sin(ang_c), np.sin(ang_c)], axis=0)
    cos = np.concatenate([np.ones((HEAD_DIM, tm)), cos], axis=1)
    sin = np.concatenate([np.zeros((HEAD_DIM, tm)), sin], axis=1)
    return jnp.asarray(cos, F32), jnp.asarray(sin, F32)


def _dft_mats(n):
    k = np.arange(n, dtype=np.int64)
    ang = 2.0 * np.pi * ((k[:, None] * k[None, :]) % n).astype(np.float64) / n
    return jnp.asarray(np.cos(ang), BF16), jnp.asarray(np.sin(ang), BF16)


def _mod_kernel(c_ref, w_ref, b_ref, o_ref):
    c = c_ref[...]
    s = (c / (1.0 + jnp.exp(-c))).astype(BF16)
    w = w_ref[0].astype(BF16)
    o_ref[0] = jnp.dot(s, w, preferred_element_type=F32) + b_ref[0]


def _modulation(cvec, w_mod, b_mod):
    depth, d, n = w_mod.shape
    tn = 1536
    return pl.pallas_call(
        _mod_kernel,
        grid=(depth, n // tn),
        in_specs=[
            pl.BlockSpec((MOD_ROWS, d), lambda l, j: (0, 0)),
            pl.BlockSpec((1, d, tn), lambda l, j: (l, 0, j)),
            pl.BlockSpec((1, 1, tn), lambda l, j: (l, 0, j)),
        ],
        out_specs=pl.BlockSpec((1, MOD_ROWS, tn), lambda l, j: (l, 0, j)),
        out_shape=jax.ShapeDtypeStruct((depth, MOD_ROWS, n), F32),
        compiler_params=_cparams(2),
        name="adaln_vectors",
    )(cvec, w_mod, b_mod.reshape(depth, 1, n))


def _mod_row(i, n_ctx_tiles, tiles_per_batch):
    lat = jnp.maximum(i - n_ctx_tiles, 0)
    return jnp.where(i < n_ctx_tiles, 0, lat // tiles_per_batch + 1)


def _mod_vec(mod_ref, row, comp):
    return mod_ref[pl.ds(row, 1), comp * D_MODEL:(comp + 1) * D_MODEL]


def _pack_bf16_pairs(xb):
    n = xb.shape[1] // 2
    bits = lax.bitcast_convert_type(xb.astype(F32), jnp.uint32)
    return (bits[:, :n] >> 16) | (bits[:, n:] & jnp.uint32(0xFFFF0000))


def _unpack_bf16_pairs(w):
    lo = lax.bitcast_convert_type(w << 16, F32)
    hi = lax.bitcast_convert_type(w & jnp.uint32(0xFFFF0000), F32)
    return jnp.concatenate([lo, hi], axis=1).astype(BF16)


def _token_specs(x, tm, n_ctx_tiles, tile0):
    if not isinstance(x, tuple):
        return [pl.BlockSpec((tm, x.shape[1]), lambda i: (i + tile0, 0))], [x]
    ctx, lat = x
    specs, arrays = [], []
    if ctx is not None:
        specs.append(pl.BlockSpec((tm, ctx.shape[1]), lambda i: (jnp.minimum(i + tile0, n_ctx_tiles - 1), 0)))
        arrays.append(ctx)
    else:
        assert tile0 >= n_ctx_tiles
    specs.append(pl.BlockSpec((tm, lat.shape[1]), lambda i: (jnp.maximum(i + tile0 - n_ctx_tiles, 0), 0)))
    arrays.append(lat)
    return specs, arrays


def _load_tokens(refs, is_ctx):
    if len(refs) == 2:
        return jnp.where(is_ctx, refs[0][...], refs[1][...])
    return refs[0][...]


def _norm_modulate(x, g, shift, scale):
    ms = jnp.mean(x * x, axis=-1, keepdims=True)
    y = x * lax.rsqrt(ms + EPS) * g
    return y * (1.0 + scale) + shift


def _premix_kernel(*refs, n_x, n_ctx_tiles, tiles_per_batch):
    x_refs = refs[:n_x]
    (mod_ref, g_ref, wt_ref, wf_ref, gq_ref, gk_ref, cos_ref, sin_ref,
     q_ref, k_ref, v_ref, f_ref) = refs[n_x:]
    i = pl.program_id(0)
    row = _mod_row(i, n_ctx_tiles, tiles_per_batch)
    x = _load_tokens(x_refs, i < n_ctx_tiles)
    h = _norm_modulate(x, g_ref[...], _mod_vec(mod_ref, row, 0), _mod_vec(mod_ref, row, 1))
    hb = h.astype(BF16)
    f_ref[...] = jnp.dot(hb, wf_ref[...], preferred_element_type=F32).astype(BF16)
    pt = lax.dot_general(wt_ref[...], hb, (((1,), (1,)), ((), ())), preferred_element_type=F32)
    v_ref[...] = pt[ATTN_WIDTH + KV_WIDTH:, :].astype(BF16)
    cos = cos_ref[...]
    sin = sin_ref[...]

    def norm_rope(xh, gain):
        ms = jnp.mean(xh * xh, axis=0, keepdims=True)
        y = xh * lax.rsqrt(ms + EPS) * gain
        half = ROT_PER_AXIS // 2
        swapped = jnp.concatenate(
            [y[half:2 * half], y[0:half], y[3 * half:4 * half], y[2 * half:3 * half]], axis=0)
        return y * cos + swapped * sin

    gq = gq_ref[...]
    gk = gk_ref[...]
    for hh in range(N_Q_HEADS):
        q_ref[hh * HEAD_DIM:(hh + 1) * HEAD_DIM, :] = norm_rope(
            pt[hh * HEAD_DIM:(hh + 1) * HEAD_DIM, :], gq).astype(BF16)
    for hh in range(N_KV_HEADS):
        lo = ATTN_WIDTH + hh * HEAD_DIM
        k_ref[hh * HEAD_DIM:(hh + 1) * HEAD_DIM, :] = norm_rope(pt[lo:lo + HEAD_DIM, :], gk).astype(BF16)


def _premix(x, mod_l, g_mix, wt, wf, gq, gk, cos_t, sin_t, *, n_ctx, seq, t):
    d = D_MODEL
    tm = TM_MIX
    n_ctx_tiles = n_ctx // tm
    tpb = seq // tm

    def tab_idx(i):
        lat = jnp.maximum(i - n_ctx_tiles, 0)
        return (0, jnp.where(i < n_ctx_tiles, 0, lax.rem(lat, tpb) + 1))

    full = lambda shape: pl.BlockSpec(shape, lambda i: (0,) * len(shape))
    x_specs, x_arrays = _token_specs(x, tm, n_ctx_tiles, 0)
    return pl.pallas_call(
        functools.partial(_premix_kernel, n_x=len(x_arrays), n_ctx_tiles=n_ctx_tiles, tiles_per_batch=tpb),
        grid=(t // tm,),
        in_specs=x_specs + [
            full(mod_l.shape),
            full((1, d)),
            full(wt.shape),
            full(wf.shape),
            full(gq.shape),
            full(gk.shape),
            pl.BlockSpec((HEAD_DIM, tm), tab_idx),
            pl.BlockSpec((HEAD_DIM, tm), tab_idx),
        ],
        out_specs=[
            pl.BlockSpec((ATTN_WIDTH, tm), lambda i: (0, i)),
            pl.BlockSpec((KV_WIDTH, tm), lambda i: (0, i)),
            pl.BlockSpec((KV_WIDTH, tm), lambda i: (0, i)),
            pl.BlockSpec((tm, FOURIER_WIDTH), lambda i: (i, 0)),
        ],
        out_shape=[
            jax.ShapeDtypeStruct((ATTN_WIDTH, t), BF16),
            jax.ShapeDtypeStruct((KV_WIDTH, t), BF16),
            jax.ShapeDtypeStruct((KV_WIDTH, t), BF16),
            jax.ShapeDtypeStruct((t, FOURIER_WIDTH), BF16),
        ],
        compiler_params=_cparams(1),
        name="premix_project",
    )(*x_arrays, mod_l, g_mix, wt, wf, gq, gk, cos_t, sin_t)


def _attn_kernel(*refs, n_key_blocks, tq):
    q_ref = refs[0]
    k_refs = refs[1:1 + n_key_blocks]
    v_refs = refs[1 + n_key_blocks:1 + 2 * n_key_blocks]
    o_ref = refs[1 + 2 * n_key_blocks]
    kall_ref, vg_ref, ot_ref = refs[2 + 2 * n_key_blocks:]
    g = pl.program_id(1)
    qt = pl.program_id(2)
    n_keys = kall_ref.shape[0]
    kc = KEY_CHUNK if n_keys % KEY_CHUNK == 0 else n_keys
    n_chunks = n_keys // kc
    slabs = kc // 8

    @pl.when(jnp.logical_and(g == 0, qt == 0))
    def _():
        off = 0
        for kr in k_refs:
            n = kr.shape[1]
            kall_ref[off:off + n, :] = kr[...].astype(F32).T.astype(BF16)
            off += n

    @pl.when(qt == 0)
    def _():
        g_rows = pl.ds(pl.multiple_of(g * HEAD_DIM, HEAD_DIM), HEAD_DIM)
        off = 0
        for vr in v_refs:
            n = vr.shape[1]
            vg_ref[0:HEAD_DIM, off:off + n] = vr[g_rows, :]
            off += n
        vg_ref[HEAD_DIM:, :] = jnp.ones((ONES_ROWS, n_keys), BF16)

    row_group = lax.broadcasted_iota(jnp.int32, (KV_WIDTH, tq), 0) // HEAD_DIM

    def masked_q(hh):
        qh = q_ref[hh * HEAD_DIM:(hh + 1) * HEAD_DIM, :]
        q2 = jnp.concatenate([qh, qh], axis=0)
        return jnp.where(row_group == g, q2, jnp.zeros_like(q2))

    q2s = [masked_q(hh) for hh in range(KV_REP)]
    items = [(hh, c) for hh in range(KV_REP) for c in range(n_chunks)]

    def score(item):
        hh, c = item
        return jnp.dot(kall_ref[c * kc:(c + 1) * kc, :], q2s[hh], preferred_element_type=F32)

    pending = [score(it) for it in items[:SCORE_LOOKAHEAD]]
    m = ot = None
    for i, (hh, c) in enumerate(items):
        if c == 0:
            m = jnp.full((1, tq), -jnp.inf, F32)
            ot = jnp.zeros((HEAD_DIM + ONES_ROWS, tq), F32)
        s = pending.pop(0).reshape(slabs, 8, tq)
        if i + SCORE_LOOKAHEAD < len(items):
            pending.append(score(items[i + SCORE_LOOKAHEAD]))
        m_new = jnp.maximum(m, jnp.max(jnp.max(s, axis=0), axis=0, keepdims=True))
        pb = jnp.exp2(s - m_new[None]).reshape(kc, tq).astype(BF16)
        rows = slice(c * kc, (c + 1) * kc)
        ot = jnp.exp2(m - m_new) * ot + jnp.dot(vg_ref[:, rows], pb, preferred_element_type=F32)
        m = m_new
        if c == n_chunks - 1:
            ot_ref[hh * HEAD_DIM:(hh + 1) * HEAD_DIM, :] = ot[:HEAD_DIM] / ot[HEAD_DIM:HEAD_DIM + 1]
    o_ref[...] = ot_ref[...].T.astype(BF16)


def _attention(q_t, k_t, v_t, *, q_tile0, n_q_tiles, key_blocks, tq, n_batch, name):
    n_keys = sum(c for c, _ in key_blocks)
    nkb = len(key_blocks)
    q_spec = pl.BlockSpec((KV_REP * HEAD_DIM, tq),
                          lambda b, g, i: (g, q_tile0 + b * n_q_tiles + i))
    k_specs = [pl.BlockSpec((KV_WIDTH, c), (lambda f: (lambda b, g, i: (0, f(b))))(f))
               for c, f in key_blocks]
    return pl.pallas_call(
        functools.partial(_attn_kernel, n_key_blocks=nkb, tq=tq),
        grid=(n_batch, N_KV_HEADS, n_q_tiles),
        in_specs=[q_spec] + k_specs + k_specs,
        out_specs=pl.BlockSpec((tq, KV_REP * HEAD_DIM),
                               lambda b, g, i: (b * n_q_tiles + i, g)),
        out_shape=jax.ShapeDtypeStruct((n_batch * n_q_tiles * tq, ATTN_WIDTH), BF16),
        scratch_shapes=[pltpu.VMEM((n_keys, KV_WIDTH), BF16),
                        pltpu.VMEM((HEAD_DIM + ONES_ROWS, n_keys), BF16),
                        pltpu.VMEM((KV_REP * HEAD_DIM, tq), F32)],
        compiler_params=_cparams(3),
        name=name,
    )(q_t, *([k_t] * nkb), *([v_t] * nkb))


def _fourier_kernel(c_ref, s_ref, x_ref, cc_ref, sc_ref, w_ref, o_ref, *, norm):
    x = x_ref[...]
    a = jnp.dot(c_ref[...], x, preferred_element_type=F32).astype(BF16)
    b = jnp.dot(s_ref[...], x, preferred_element_type=F32).astype(BF16)
    cc = cc_ref[...]
    sc = sc_ref[...]
    groups = [slice(grp * FOURIER_GROUP, (grp + 1) * FOURIER_GROUP) for grp in range(N_FOURIER_GROUPS)]
    specs = [(jnp.dot(a[:, sl], cc, preferred_element_type=F32)
              - jnp.dot(b[:, sl], sc, preferred_element_type=F32)) * norm for sl in groups]
    for grp, sl in enumerate(groups):
        o_ref[:, sl] = jnp.dot(specs[grp].astype(BF16), w_ref[grp], preferred_element_type=F32).astype(BF16)


def _fourier(f, cmat, smat, cc, sc, w_four, *, n, tr, x_block0, n_batch, name):
    n_row_tiles = n // tr
    return pl.pallas_call(
        functools.partial(_fourier_kernel, norm=float(1.0 / np.sqrt(n * FOURIER_GROUP))),
        grid=(n_row_tiles, n_batch),
        in_specs=[
            pl.BlockSpec((tr, n), lambda i, b: (i, 0)),
            pl.BlockSpec((tr, n), lambda i, b: (i, 0)),
            pl.BlockSpec((n, FOURIER_WIDTH), lambda i, b: (x_block0 + b, 0)),
            pl.BlockSpec(cc.shape, lambda i, b: (0, 0)),
            pl.BlockSpec(sc.shape, lambda i, b: (0, 0)),
            pl.BlockSpec(w_four.shape, lambda i, b: (0, 0, 0)),
        ],
        out_specs=pl.BlockSpec((tr, FOURIER_WIDTH), lambda i, b: (b * n_row_tiles + i, 0)),
        out_shape=jax.ShapeDtypeStruct((n_batch * n, FOURIER_WIDTH), BF16),
        compiler_params=_cparams(2),
        name=name,
    )(cmat, smat, f, cc, sc, w_four)


def _dft_half_tiles(n, tr):
    n_tiles = n // 2 // tr
    k = (np.arange(n_tiles, dtype=np.int64)[:, None] * tr
         + np.arange(tr + SYM_EXTRA, dtype=np.int64)[None, :])
    pos = np.arange(n, dtype=np.int64)
    ang = 2.0 * np.pi * ((k[:, :, None] * pos[None, None, :]) % n).astype(np.float64) / n
    rev = np.eye(tr, dtype=np.float32)[::-1]
    return jnp.asarray(np.cos(ang), BF16), jnp.asarray(np.sin(ang), BF16), jnp.asarray(rev, BF16)


def _fourier_sym_kernel(c_ref, s_ref, x_ref, cc_ref, sc_ref, w_ref, rev_ref, o_ref, *, norm, tr, n_tiles):
    i = pl.program_id(1)
    x = x_ref[...]
    a = jnp.dot(c_ref[0], x, preferred_element_type=F32).astype(BF16)
    b = jnp.dot(s_ref[0], x, preferred_element_type=F32).astype(BF16)
    cc = cc_ref[...]
    sc = sc_ref[...]
    groups = [slice(grp * FOURIER_GROUP, (grp + 1) * FOURIER_GROUP) for grp in range(N_FOURIER_GROUPS)]
    ps = [jnp.dot(a[:, sl], cc, preferred_element_type=F32) for sl in groups]
    qs = [jnp.dot(b[:, sl], sc, preferred_element_type=F32) for sl in groups]
    upper = jnp.concatenate([((p + q) * norm)[1:tr + 1] for p, q in zip(ps, qs)], axis=1).astype(BF16)
    upper = jnp.dot(rev_ref[...], upper, preferred_element_type=F32).astype(BF16)
    lo_rows = pl.ds(pl.multiple_of(i * tr, tr), tr)
    hi_rows = pl.ds(pl.multiple_of((2 * n_tiles - 1 - i) * tr, tr), tr)
    for grp, sl in enumerate(groups):
        lower = ((ps[grp] - qs[grp]) * norm)[:tr].astype(BF16)
        o_ref[lo_rows, sl] = jnp.dot(lower, w_ref[grp], preferred_element_type=F32).astype(BF16)
    for grp, sl in enumerate(groups):
        o_ref[hi_rows, sl] = jnp.dot(upper[:, sl], w_ref[grp], preferred_element_type=F32).astype(BF16)


def _fourier_sym(f, c_tiles, s_tiles, rev, cc, sc, w_four, *, n, tr, x_block0, n_batch, name):
    n_tiles = n // 2 // tr
    ext = tr + SYM_EXTRA
    return pl.pallas_call(
        functools.partial(_fourier_sym_kernel, norm=float(1.0 / np.sqrt(n * FOURIER_GROUP)), tr=tr,
                          n_tiles=n_tiles),
        grid=(n_batch, n_tiles),
        in_specs=[
            pl.BlockSpec((1, ext, n), lambda b, i: (i, 0, 0)),
            pl.BlockSpec((1, ext, n), lambda b, i: (i, 0, 0)),
            pl.BlockSpec((n, FOURIER_WIDTH), lambda b, i: (x_block0 + b, 0)),
            pl.BlockSpec(cc.shape, lambda b, i: (0, 0)),
            pl.BlockSpec(sc.shape, lambda b, i: (0, 0)),
            pl.BlockSpec(w_four.shape, lambda b, i: (0, 0, 0)),
            pl.BlockSpec(rev.shape, lambda b, i: (0, 0)),
        ],
        out_specs=pl.BlockSpec((n, FOURIER_WIDTH), lambda b, i: (b, 0)),
        out_shape=jax.ShapeDtypeStruct((n_batch * n, FOURIER_WIDTH), BF16),
        compiler_params=_cparams(2),
        name=name,
    )(c_tiles, s_tiles, f, cc, sc, w_four, rev)


def _postmix_kernel(*refs, n_x, n_a, n_f, tile0, n_ctx_tiles, tiles_per_batch, with_router):
    x_refs, refs = refs[:n_x], refs[n_x:]
    a_refs, refs = refs[:n_a], refs[n_a:]
    f_refs, refs = refs[:n_f], refs[n_f:]
    mod_ref, g_ref, woa_ref, wof_ref = refs[:4]
    if with_router:
        wr_ref, br_ref, x1_ref, h_ref, rt_ref, rtt_ref, cnt_ref = refs[4:]
    else:
        x1_ref, h_ref = refs[4:]
    i = pl.program_id(0) + tile0
    row = _mod_row(i, n_ctx_tiles, tiles_per_batch)
    is_ctx = i < n_ctx_tiles
    a = _load_tokens(a_refs, is_ctx)
    f = _load_tokens(f_refs, is_ctx)
    mix = (jnp.dot(a, woa_ref[...], preferred_element_type=F32)
           + jnp.dot(f, wof_ref[...], preferred_element_type=F32))
    x1 = _load_tokens(x_refs, is_ctx) + _mod_vec(mod_ref, row, 2) * mix
    x1_ref[...] = x1
    h = _norm_modulate(x1, g_ref[...], _mod_vec(mod_ref, row, 3), _mod_vec(mod_ref, row, 4))
    hb = h.astype(BF16)
    if with_router:
        h_ref[...] = _pack_bf16_pairs(hb)
        logits = jnp.dot(hb, wr_ref[...], preferred_element_type=F32) + br_ref[...]
        _top2_route(logits, rt_ref, rtt_ref, cnt_ref, first=pl.program_id(0) == 0)
    else:
        h_ref[...] = hb


def _top2_route(logits, rt_ref, rtt_ref, cnt_ref, first):
    tm, lanes = logits.shape
    lane = lax.broadcasted_iota(jnp.int32, (tm, lanes), 1)
    neg = jnp.float32(-jnp.inf)
    m1 = jnp.max(logits, axis=1, keepdims=True)
    i1 = jnp.min(jnp.where(logits == m1, lane, lanes), axis=1, keepdims=True)
    rest = jnp.where(lane == i1, neg, logits)
    m2 = jnp.max(rest, axis=1, keepdims=True)
    i2 = jnp.min(jnp.where(rest == m2, lane, lanes), axis=1, keepdims=True)
    e = jnp.exp(m2 - m1)
    g1 = 1.0 / (1.0 + e)
    g2 = e / (1.0 + e)
    pick1 = lane == i1
    pick2 = lane == i2
    picked = jnp.logical_or(pick1, pick2).astype(F32)

    @pl.when(first)
    def _():
        cnt_ref[...] = jnp.zeros_like(cnt_ref)

    r_i = lax.broadcasted_iota(jnp.int32, (tm, tm), 0)
    c_i = lax.broadcasted_iota(jnp.int32, (tm, tm), 1)
    lower = jnp.where(c_i < r_i, 1.0, 0.0).astype(BF16)
    before = jnp.dot(lower, picked.astype(BF16), preferred_element_type=F32) + cnt_ref[0:1, :]
    r1 = jnp.sum(jnp.where(pick1, before, 0.0), axis=1, keepdims=True)
    r2 = jnp.sum(jnp.where(pick2, before, 0.0), axis=1, keepdims=True)
    cnt_ref[...] = cnt_ref[...] + jnp.sum(picked, axis=0, keepdims=True)
    out = jnp.zeros((tm, lanes), F32)
    for k, val in enumerate([i1.astype(F32), i2.astype(F32), g1, g2, r1, r2]):
        out = jnp.where(lane == k, val, out)
    rt_ref[...] = out
    rtt_ref[...] = out.T[0:8, :]


def _postmix(x, attn, four, mod_l, g_ffn, wo_a, wo_f, router, *, n_ctx, seq, t, tile0):
    d = D_MODEL
    tm = TM_MIX
    nct = n_ctx // tm
    t = t - tile0 * tm
    full = lambda shape: pl.BlockSpec(shape, lambda i: (0,) * len(shape))
    tile = lambda w: pl.BlockSpec((tm, w), lambda i: (i, 0))
    x_specs, x_arrays = _token_specs(x, tm, nct, tile0)
    a_specs, a_arrays = _token_specs(attn, tm, nct, tile0)
    f_specs, f_arrays = _token_specs(four, tm, nct, tile0)
    in_specs = x_specs + a_specs + f_specs + [full(mod_l.shape), full((1, d)), full(wo_a.shape),
                                              full(wo_f.shape)]
    if router is None:
        out_specs = [tile(d), tile(d)]
        out_shape = [jax.ShapeDtypeStruct((t, d), F32), jax.ShapeDtypeStruct((t, d), BF16)]
    else:
        out_specs = [tile(d), tile(d // 2)]
        out_shape = [jax.ShapeDtypeStruct((t, d), F32), jax.ShapeDtypeStruct((t, d // 2), jnp.uint32)]
    args = x_arrays + a_arrays + f_arrays + [mod_l, g_ffn, wo_a, wo_f]
    if router is not None:
        in_specs += [full(router[0].shape), full(router[1].shape)]
        out_specs += [tile(LOGIT_LANES), pl.BlockSpec((8, tm), lambda i: (0, i)),
                      pl.BlockSpec((8, LOGIT_LANES), lambda i: (0, 0))]
        out_shape += [jax.ShapeDtypeStruct((t, LOGIT_LANES), F32), jax.ShapeDtypeStruct((8, t), F32),
                      jax.ShapeDtypeStruct((8, LOGIT_LANES), F32)]
        args += list(router)
    return pl.pallas_call(
        functools.partial(_postmix_kernel, n_x=len(x_arrays), n_a=len(a_arrays), n_f=len(f_arrays),
                          tile0=tile0, n_ctx_tiles=nct, tiles_per_batch=seq // tm,
                          with_router=router is not None),
        grid=(t // tm,),
        in_specs=in_specs,
        out_specs=out_specs,
        out_shape=out_shape,
        compiler_params=_cparams(1),
        name="postmix_wo_norm",
    )(*args)


def _silu_mul(gate, up):
    return (gate / (1.0 + jnp.exp(-gate))) * up


def _swiglu_chunks(x, n_chunks, wg_of, wu_of, wd_of, acc_ref):
    gate = jnp.dot(x, wg_of(0), preferred_element_type=F32)
    up = jnp.dot(x, wu_of(0), preferred_element_type=F32)
    for c in range(n_chunks):
        act = _silu_mul(gate, up).astype(BF16)
        if c + 1 < n_chunks:
            gate = jnp.dot(x, wg_of(c + 1), preferred_element_type=F32)
            up = jnp.dot(x, wu_of(c + 1), preferred_element_type=F32)
        acc_ref[...] += jnp.dot(act, wd_of(c), preferred_element_type=F32)


def _dense_ffn_kernel(h_ref, x1_ref, mod_ref, wg_ref, wu_ref, wd_ref, o_ref, acc_ref, *,
                      tile0, n_ctx_tiles, tiles_per_batch):
    i = pl.program_id(0) + tile0
    row = _mod_row(i, n_ctx_tiles, tiles_per_batch)
    acc_ref[...] = jnp.zeros_like(acc_ref)
    sub = lambda c: slice(c * FF_CHUNK, (c + 1) * FF_CHUNK)
    _swiglu_chunks(h_ref[...], wg_ref.shape[1] // FF_CHUNK, lambda c: wg_ref[:, sub(c)],
                   lambda c: wu_ref[:, sub(c)], lambda c: wd_ref[sub(c), :], acc_ref)
    o_ref[...] = x1_ref[...] + _mod_vec(mod_ref, row, 5) * acc_ref[...]


def _dense_ffn(h, x1, mod_l, wg, wu, wd, *, n_ctx, seq, tile0):
    t, d = x1.shape
    tm = TM_DENSE
    full = lambda shape: pl.BlockSpec(shape, lambda i: (0,) * len(shape))
    tile = lambda: pl.BlockSpec((tm, d), lambda i: (i, 0))
    return pl.pallas_call(
        functools.partial(_dense_ffn_kernel, tile0=tile0 * TM_MIX // tm, n_ctx_tiles=n_ctx // tm,
                          tiles_per_batch=seq // tm),
        grid=(t // tm,),
        in_specs=[tile(), tile(), full(mod_l.shape), full(wg.shape), full(wu.shape), full(wd.shape)],
        out_specs=tile(),
        out_shape=jax.ShapeDtypeStruct((t, d), F32),
        scratch_shapes=[pltpu.VMEM((tm, d), F32)],
        compiler_params=_cparams(1),
        name="dense_swiglu",
    )(h, x1, mod_l, wg, wu, wd)


def _moe_ffn_kernel(be_ref, nv_ref, x_ref, wg_ref, wu_ref, wd_ref, o_ref, xb_ref, acc_ref, wgb_ref, wub_ref,
                    wdb_ref):
    i = pl.program_id(0)
    j = pl.program_id(1)
    n_valid = nv_ref[i]

    @pl.when(j == 0)
    def _():
        x = _unpack_bf16_pairs(x_ref[...])
        rows = lax.broadcasted_iota(jnp.int32, x.shape, 0)
        xb_ref[...] = jnp.where(rows < n_valid, x, jnp.zeros_like(x))
        acc_ref[...] = jnp.zeros_like(acc_ref)

    n_sub = TF_MOE // FF_CHUNK
    sub = lambda c: slice(c * FF_CHUNK, (c + 1) * FF_CHUNK)
    rows_needed = ((n_valid + TAIL_ROWS - 1) // TAIL_ROWS) * TAIL_ROWS
    n_full = rows_needed // TM_PASS
    n_tail = (rows_needed - n_full * TM_PASS) // TAIL_ROWS

    def cast_chunk(src_ref, dst_ref, idx):
        w = src_ref[(0, 0) + idx].astype(BF16)
        dst_ref[idx] = w
        return w

    cast_wg = lambda c: cast_chunk(wg_ref, wgb_ref, (slice(None), sub(c)))
    cast_wu = lambda c: cast_chunk(wu_ref, wub_ref, (slice(None), sub(c)))
    cast_wd = lambda c: cast_chunk(wd_ref, wdb_ref, (sub(c), slice(None)))
    read_wg = lambda c: wgb_ref[:, sub(c)]
    read_wu = lambda c: wub_ref[:, sub(c)]
    read_wd = lambda c: wdb_ref[sub(c), :]

    @pl.when(jnp.logical_and(n_full == 0, n_tail > 0))
    def _():
        for c in range(n_sub):
            cast_wg(c), cast_wu(c), cast_wd(c)

    for r in range(TM_MOE // TM_PASS):
        rows = slice(r * TM_PASS, (r + 1) * TM_PASS)
        getters = (cast_wg, cast_wu, cast_wd) if r == 0 else (read_wg, read_wu, read_wd)

        @pl.when(r < n_full)
        def _():
            _swiglu_chunks(xb_ref[rows, :], n_sub, *getters, acc_ref.at[rows, :])

    @pl.when(n_tail > 0)
    def _():
        def group(gi, carry):
            rows = pl.ds(pl.multiple_of(n_full * TM_PASS + gi * TAIL_ROWS, TAIL_ROWS), TAIL_ROWS)
            _swiglu_chunks(xb_ref[rows, :], n_sub, read_wg, read_wu, read_wd, acc_ref.at[rows, :])
            return carry

        lax.fori_loop(0, n_tail, group, 0)

    @pl.when(j == pl.num_programs(1) - 1)
    def _():
        o_ref[...] = _pack_bf16_pairs(acc_ref[...].astype(BF16))


def _moe_ffn(xs, block_expert, n_valid, wg, wu, wd, layer):
    n_rows = xs.shape[0]
    d = D_MODEL
    tm, tf = TM_MOE, TF_MOE
    n_blocks = n_rows // tm
    n_ff = wg.shape[3] // tf

    def ff_idx(j, nv, i):
        return jnp.where(nv[i] > 0, j, n_ff - 1)

    grid_spec = pltpu.PrefetchScalarGridSpec(
        num_scalar_prefetch=2,
        grid=(n_blocks, n_ff),
        in_specs=[
            pl.BlockSpec((tm, d // 2), lambda i, j, be, nv: (i, 0)),
            pl.BlockSpec((1, 1, d, tf), lambda i, j, be, nv: (layer, be[i], 0, ff_idx(j, nv, i))),
            pl.BlockSpec((1, 1, d, tf), lambda i, j, be, nv: (layer, be[i], 0, ff_idx(j, nv, i))),
            pl.BlockSpec((1, 1, tf, d), lambda i, j, be, nv: (layer, be[i], ff_idx(j, nv, i), 0)),
        ],
        out_specs=pl.BlockSpec((tm, d // 2), lambda i, j, be, nv: (i, 0)),
        scratch_shapes=[pltpu.VMEM((tm, d), BF16), pltpu.VMEM((tm, d), F32), pltpu.VMEM((d, tf), BF16),
                        pltpu.VMEM((d, tf), BF16), pltpu.VMEM((tf, d), BF16)],
    )
    return pl.pallas_call(
        _moe_ffn_kernel,
        grid_spec=grid_spec,
        out_shape=jax.ShapeDtypeStruct((n_rows, d // 2), jnp.uint32),
        compiler_params=_cparams(2),
        name="expert_swiglu",
    )(block_expert, n_valid, xs, wg, wu, wd)


def _sc_mesh():
    return plsc.VectorSubcoreMesh(core_axis_name="c", subcore_axis_name="s")


def _sc_params():
    return pltpu.CompilerParams(use_tc_tiling_on_sc=True)


def _sc_dispatch(h_packed, dest, n_rows):
    t, w = h_packed.shape
    win = SC_DISPATCH_ROWS
    n_win = t // win
    idx = [dest[k].reshape(n_win, 1, win) for k in range(TOP_K)]

    @functools.partial(
        pl.kernel, out_type=jax.ShapeDtypeStruct((n_rows, w), h_packed.dtype), mesh=_sc_mesh(),
        scratch_types=[], compiler_params=_sc_params(), name="expert_dispatch_scatter")
    def run(x_hbm, i0_hbm, i1_hbm, o_hbm):
        def body(x_vmem, i0_vmem, i1_vmem):
            pltpu.sync_copy(x_vmem, o_hbm.at[i0_vmem.at[0, 0]])
            pltpu.sync_copy(x_vmem, o_hbm.at[i1_vmem.at[0, 0]])

        idx_spec = pl.BlockSpec((1, 1, win), lambda i: (i, 0, 0))
        pltpu.emit_pipeline(
            body, grid=(n_win,),
            in_specs=[pl.BlockSpec((win, w), lambda i: (i, 0)), idx_spec, idx_spec],
            out_specs=[], core_axis_name=("c", "s"), dimension_semantics=(pltpu.PARALLEL,),
        )(x_hbm, i0_hbm, i1_hbm)

    return run(h_packed, *idx)


def _sc_gather(ys, idx_flat):
    w = ys.shape[1]
    n = idx_flat.shape[0]
    win = SC_GATHER_ROWS
    n_win = n // win

    @functools.partial(
        pl.kernel, out_type=jax.ShapeDtypeStruct((n, w), ys.dtype), mesh=_sc_mesh(),
        scratch_types=[], compiler_params=_sc_params(), name="expert_combine_gather")
    def run(y_hbm, i_hbm, o_hbm):
        def body(i_vmem, o_vmem):
            pltpu.sync_copy(y_hbm.at[i_vmem.at[0, 0]], o_vmem)

        pltpu.emit_pipeline(
            body, grid=(n_win,),
            in_specs=[pl.BlockSpec((1, 1, win), lambda i: (i, 0, 0))],
            out_specs=[pl.BlockSpec((win, w), lambda i: (i, 0))],
            core_axis_name=("c", "s"), dimension_semantics=(pltpu.PARALLEL,),
        )(i_hbm, o_hbm)

    return run(ys, idx_flat.reshape(n_win, 1, win))


def _moe_combine_kernel(x1_ref, y0_ref, y1_ref, g_ref, mod_ref, o_ref, *, tile0, n_ctx_tiles,
                        tiles_per_batch):
    i = pl.program_id(0) + tile0
    row = _mod_row(i, n_ctx_tiles, tiles_per_batch)
    gates = g_ref[...]
    y0 = _unpack_bf16_pairs(y0_ref[...]).astype(F32)
    y1 = _unpack_bf16_pairs(y1_ref[...]).astype(F32)
    y = y0 * gates[:, 2:3] + y1 * gates[:, 3:4]
    o_ref[...] = x1_ref[...] + _mod_vec(mod_ref, row, 5) * y


def _moe_combine(x1, y_sel, gates, mod_l, *, n_ctx, seq, tile0):
    t, d = x1.shape
    tm = TM_MIX
    n_tiles = t // tm
    tile = lambda w: pl.BlockSpec((tm, w), lambda i: (i, 0))
    return pl.pallas_call(
        functools.partial(_moe_combine_kernel, tile0=tile0, n_ctx_tiles=n_ctx // tm,
                          tiles_per_batch=seq // tm),
        grid=(n_tiles,),
        in_specs=[tile(d), tile(d // 2), pl.BlockSpec((tm, d // 2), lambda i: (n_tiles + i, 0)),
                  tile(LOGIT_LANES),
                  pl.BlockSpec(mod_l.shape, lambda i: (0, 0))],
        out_specs=tile(d),
        out_shape=jax.ShapeDtypeStruct((t, d), F32),
        compiler_params=_cparams(1),
        name="expert_combine",
    )(x1, y_sel, y_sel, gates, mod_l)


def _route(route_t, counts):
    t = route_t.shape[1]
    expert = route_t[0:TOP_K].astype(jnp.int32)
    rank = route_t[4:4 + TOP_K].astype(jnp.int32)
    counts = counts[0, :N_EXPERTS].astype(jnp.int32)
    n_assign = t * TOP_K
    padded = ((counts + TM_MOE - 1) // TM_MOE) * TM_MOE
    pend = jnp.cumsum(padded)
    pstart = pend - padded
    dest = rank
    for e in range(N_EXPERTS):
        dest = dest + jnp.where(expert == e, pstart[e], 0)
    n_blocks = -(-n_assign // TM_MOE) + N_EXPERTS
    block_start = jnp.arange(n_blocks, dtype=jnp.int32) * TM_MOE
    block_expert = jnp.minimum(jnp.sum(pend[None, :] <= block_start[:, None], axis=1),
                               N_EXPERTS - 1).astype(jnp.int32)
    n_valid = jnp.clip(counts[block_expert] - (block_start - pstart[block_expert]), 0, TM_MOE)
    n_valid = jnp.where(block_start < pend[-1], n_valid, 0).astype(jnp.int32)
    return dest.astype(jnp.int32), block_expert, n_valid, n_blocks * TM_MOE


def kernel(x, c, ctx, c_ctx, w_mod, b_mod, g_mix, g_ffn, g_q, g_k, w_in, w_four, w_o,
           w_gate_dense, w_up_dense, w_down_dense, w_router, b_router,
           w_gate_moe, w_up_moe, w_down_moe):
    b, s, d = x.shape
    n_ctx_len = ctx.shape[1]
    n_ctx = b * n_ctx_len
    t = n_ctx + b * s
    assert d == D_MODEL and b + 1 <= MOD_ROWS
    assert n_ctx % TM_DENSE == 0 and s % TM_DENSE == 0 and n_ctx_len % 128 == 0 and n_ctx % s == 0

    cvec = jnp.zeros((MOD_ROWS, d), F32).at[0].set(c_ctx).at[1:b + 1].set(c)
    mod = _modulation(cvec, w_mod, b_mod)

    cos_t, sin_t = _rope_tables_t(s, TM_MIX)
    c_lat, s_lat, rev_lat = _dft_half_tiles(s, TR_FOUR)
    c_ctx_m, s_ctx_m = _dft_mats(n_ctx_len)
    c_grp, s_grp = _dft_mats(FOURIER_GROUP)

    xa = (ctx.reshape(n_ctx, d), x.reshape(b * s, d))

    for l in range(DEPTH):
        mod_l = mod[l]
        w_in_l = w_in[l]
        wt = w_in_l[:, :QKV_WIDTH].T.astype(BF16)
        wf = w_in_l[:, QKV_WIDTH:].astype(BF16)
        gq = jnp.broadcast_to((g_q[l] * (LOG2_E * HEAD_DIM ** -0.5))[:, None], (HEAD_DIM, TM_MIX))
        gk = jnp.broadcast_to(g_k[l][:, None], (HEAD_DIM, TM_MIX))
        last = l == DEPTH - 1
        tile0 = n_ctx // TM_MIX if last else 0
        q_t, k_t, v_t, f = _premix(xa, mod_l, g_mix[l].reshape(1, d), wt, wf, gq, gk, cos_t, sin_t,
                                   n_ctx=n_ctx, seq=s, t=t)

        n_qt = s // TQ_ATTN
        lat_keys = [(n_ctx_len, lambda bb: bb), (s, lambda bb: n_ctx // s + bb)]
        attn_lat = _attention(q_t, k_t, v_t, q_tile0=n_ctx // TQ_ATTN, n_q_tiles=n_qt,
                              key_blocks=lat_keys, tq=TQ_ATTN, n_batch=b, name="attention_latent")
        attn_ctx = None if last else _attention(
            q_t, k_t, v_t, q_tile0=0, n_q_tiles=1, key_blocks=[(n_ctx_len, lambda bb: bb)],
            tq=n_ctx_len, n_batch=b, name="attention_context")

        wfour = w_four[l].astype(BF16)
        four_lat = _fourier_sym(f, c_lat, s_lat, rev_lat, c_grp, s_grp, wfour, n=s, tr=TR_FOUR,
                                x_block0=n_ctx // s, n_batch=b, name="fourier_latent")
        four_ctx = None if last else _fourier(
            f, c_ctx_m, s_ctx_m, c_grp, s_grp, wfour, n=n_ctx_len, tr=n_ctx_len, x_block0=0, n_batch=b,
            name="fourier_context")

        wo = w_o[l].astype(BF16)
        is_moe = l % 2 == 1
        li = l // 2
        router = None
        if is_moe:
            wr = jnp.zeros((d, LOGIT_LANES), BF16).at[:, :N_EXPERTS].set(w_router[li].astype(BF16))
            br = jnp.full((1, LOGIT_LANES), -jnp.inf, F32).at[0, :N_EXPERTS].set(b_router[li])
            router = (wr, br)
        res = _postmix(xa, (attn_ctx, attn_lat), (four_ctx, four_lat), mod_l, g_ffn[l].reshape(1, d),
                       wo[:ATTN_WIDTH], wo[ATTN_WIDTH:], router, n_ctx=n_ctx, seq=s, t=t, tile0=tile0)
        t_l = t - tile0 * TM_MIX
        if not is_moe:
            x1, h2 = res
            xa = _dense_ffn(h2, x1, mod_l, w_gate_dense[li].astype(BF16), w_up_dense[li].astype(BF16),
                            w_down_dense[li].astype(BF16), n_ctx=n_ctx, seq=s, tile0=tile0)
        else:
            x1, h2, route, route_t, counts = res
            dest, block_expert, n_valid, n_rows = _route(route_t, counts)
            xs = _sc_dispatch(h2, dest, n_rows)
            ys = _moe_ffn(xs, block_expert, n_valid, w_gate_moe, w_up_moe, w_down_moe, li)
            y_sel = _sc_gather(ys, dest.reshape(-1))
            xa = _moe_combine(x1, y_sel, route, mod_l, n_ctx=n_ctx, seq=s, tile0=tile0)

    return xa.reshape(b, s, d)
```

```python
import functools

import numpy as np
import jax
import jax.numpy as jnp
from jax import lax
from jax.experimental import pallas as pl
from jax.experimental.pallas import tpu as pltpu
from jax.experimental.pallas import tpu_sc as plsc

D_MODEL = 1024
DEPTH = 4
GRID_W = 64
HEAD_DIM = 64
ATTN_WIDTH = 512
N_Q_HEADS = 8
N_KV_HEADS = 2
KV_REP = 4
KV_WIDTH = 128
FOURIER_WIDTH = 512
N_FOURIER_GROUPS = 4
FOURIER_GROUP = 128
ROT_PER_AXIS = 32
ROPE_THETA = 10000.0
N_EXPERTS = 8
TOP_K = 2
EPS = 1e-6

QKV_WIDTH = ATTN_WIDTH + 2 * KV_WIDTH
MOD_ROWS = 16
LOGIT_LANES = 128

TM_MIX = 1024
TQ_ATTN = 512
KEY_CHUNK = 256
ONES_ROWS = 16
SCORE_LOOKAHEAD = 2
LOG2_E = 1.4426950408889634
TR_FOUR = 512
SYM_EXTRA = 16
TM_DENSE = 1024
FF_CHUNK = 256
TM_MOE = 2048
TM_PASS = 1024
TAIL_ROWS = 256
TF_MOE = 512
SC_DISPATCH_ROWS = 64
SC_GATHER_ROWS = 64
V7X_VMEM_BYTES = 64 * 1024 * 1024
VMEM_LIMIT = V7X_VMEM_BYTES - 8 * 1024 * 1024

F32 = jnp.float32
BF16 = jnp.bfloat16


def _cparams(n_axes):
    return pltpu.CompilerParams(
        dimension_semantics=("arbitrary",) * n_axes, vmem_limit_bytes=VMEM_LIMIT)


def _rope_tables_t(s, tm):
    n_rows = s // GRID_W
    rows = np.repeat(np.arange(n_rows), GRID_W).astype(np.float64)
    cols = np.tile(np.arange(GRID_W), n_rows).astype(np.float64)
    inv_freq = (ROPE_THETA ** (-np.arange(0, ROT_PER_AXIS, 2, dtype=np.float32) / ROT_PER_AXIS)
                ).astype(np.float32).astype(np.float64)
    ang_r = (rows[None, :].astype(np.float32) * inv_freq[:, None].astype(np.float32)).astype(np.float64)
    ang_c = (cols[None, :].astype(np.float32) * inv_freq[:, None].astype(np.float32)).astype(np.float64)
    cos = np.concatenate([np.cos(ang_r), np.cos(ang_r), np.cos(ang_c), np.cos(ang_c)], axis=0)
    sin = np.concatenate([-np.sin(ang_r), np.sin(ang_r), -np.sin(ang_c), np.sin(ang_c)], axis=0)
    cos = np.concatenate([np.ones((HEAD_DIM, tm)), cos], axis=1)
    sin = np.concatenate([np.zeros((HEAD_DIM, tm)), sin], axis=1)
    return jnp.asarray(cos, F32), jnp.asarray(sin, F32)


def _dft_mats(n):
    k = np.arange(n, dtype=np.int64)
    ang = 2.0 * np.pi * ((k[:, None] * k[None, :]) % n).astype(np.float64) / n
    return jnp.asarray(np.cos(ang), BF16), jnp.asarray(np.sin(ang), BF16)


def _mod_kernel(c_ref, w_ref, b_ref, o_ref):
    c = c_ref[...]
    s = (c / (1.0 + jnp.exp(-c))).astype(BF16)
    w = w_ref[0].astype(BF16)
    o_ref[0] = jnp.dot(s, w, preferred_element_type=F32) + b_ref[0]


def _modulation(cvec, w_mod, b_mod):
    depth, d, n = w_mod.shape
    tn = 1536
    return pl.pallas_call(
        _mod_kernel,
        grid=(depth, n // tn),
        in_specs=[
            pl.BlockSpec((MOD_ROWS, d), lambda l, j: (0, 0)),
            pl.BlockSpec((1, d, tn), lambda l, j: (l, 0, j)),
            pl.BlockSpec((1, 1, tn), lambda l, j: (l, 0, j)),
        ],
        out_specs=pl.BlockSpec((1, MOD_ROWS, tn), lambda l, j: (l, 0, j)),
        out_shape=jax.ShapeDtypeStruct((depth, MOD_ROWS, n), F32),
        compiler_params=_cparams(2),
        name="adaln_vectors",
    )(cvec, w_mod, b_mod.reshape(depth, 1, n))


def _mod_row(i, n_ctx_tiles, tiles_per_batch):
    lat = jnp.maximum(i - n_ctx_tiles, 0)
    return jnp.where(i < n_ctx_tiles, 0, lat // tiles_per_batch + 1)


def _mod_vec(mod_ref, row, comp):
    return mod_ref[pl.ds(row, 1), comp * D_MODEL:(comp + 1) * D_MODEL]


def _pack_bf16_pairs(xb):
    n = xb.shape[1] // 2
    bits = lax.bitcast_convert_type(xb.astype(F32), jnp.uint32)
    return (bits[:, :n] >> 16) | (bits[:, n:] & jnp.uint32(0xFFFF0000))


def _unpack_bf16_pairs(w):
    lo = lax.bitcast_convert_type(w << 16, F32)
    hi = lax.bitcast_convert_type(w & jnp.uint32(0xFFFF0000), F32)
    return jnp.concatenate([lo, hi], axis=1).astype(BF16)


def _token_specs(x, tm, n_ctx_tiles, tile0):
    if not isinstance(x, tuple):
        return [pl.BlockSpec((tm, x.shape[1]), lambda i: (i + tile0, 0))], [x]
    ctx, lat = x
    specs, arrays = [], []
    if ctx is not None:
        specs.append(pl.BlockSpec((tm, ctx.shape[1]), lambda i: (jnp.minimum(i + tile0, n_ctx_tiles - 1), 0)))
        arrays.append(ctx)
    else:
        assert tile0 >= n_ctx_tiles
    specs.append(pl.BlockSpec((tm, lat.shape[1]), lambda i: (jnp.maximum(i + tile0 - n_ctx_tiles, 0), 0)))
    arrays.append(lat)
    return specs, arrays


def _load_tokens(refs, is_ctx):
    if len(refs) == 2:
        return jnp.where(is_ctx, refs[0][...], refs[1][...])
    return refs[0][...]


def _norm_modulate(x, g, shift, scale):
    ms = jnp.mean(x * x, axis=-1, keepdims=True)
    y = x * lax.rsqrt(ms + EPS) * g
    return y * (1.0 + scale) + shift


def _premix_kernel(*refs, n_x, n_ctx_tiles, tiles_per_batch):
    x_refs = refs[:n_x]
    (mod_ref, g_ref, wt_ref, wf_ref, gq_ref, gk_ref, cos_ref, sin_ref,
     q_ref, k_ref, v_ref, f_ref) = refs[n_x:]
    i = pl.program_id(0)
    row = _mod_row(i, n_ctx_tiles, tiles_per_batch)
    x = _load_tokens(x_refs, i < n_ctx_tiles)
    h = _norm_modulate(x, g_ref[...], _mod_vec(mod_ref, row, 0), _mod_vec(mod_ref, row, 1))
    hb = h.astype(BF16)
    f_ref[...] = jnp.dot(hb, wf_ref[...], preferred_element_type=F32).astype(BF16)
    pt = lax.dot_general(wt_ref[...], hb, (((1,), (1,)), ((), ())), preferred_element_type=F32)
    v_ref[...] = pt[ATTN_WIDTH + KV_WIDTH:, :].astype(BF16)
    cos = cos_ref[...]
    sin = sin_ref[...]

    def norm_rope(xh, gain):
        ms = jnp.mean(xh * xh, axis=0, keepdims=True)
        y = xh * lax.rsqrt(ms + EPS) * gain
        half = ROT_PER_AXIS // 2
        swapped = jnp.concatenate(
            [y[half:2 * half], y[0:half], y[3 * half:4 * half], y[2 * half:3 * half]], axis=0)
        return y * cos + swapped * sin

    gq = gq_ref[...]
    gk = gk_ref[...]
    for hh in range(N_Q_HEADS):
        q_ref[hh * HEAD_DIM:(hh + 1) * HEAD_DIM, :] = norm_rope(
            pt[hh * HEAD_DIM:(hh + 1) * HEAD_DIM, :], gq).astype(BF16)
    for hh in range(N_KV_HEADS):
        lo = ATTN_WIDTH + hh * HEAD_DIM
        k_ref[hh * HEAD_DIM:(hh + 1) * HEAD_DIM, :] = norm_rope(pt[lo:lo + HEAD_DIM, :], gk).astype(BF16)


def _premix(x, mod_l, g_mix, wt, wf, gq, gk, cos_t, sin_t, *, n_ctx, seq, t):
    d = D_MODEL
    tm = TM_MIX
    n_ctx_tiles = n_ctx // tm
    tpb = seq // tm

    def tab_idx(i):
        lat = jnp.maximum(i - n_ctx_tiles, 0)
        return (0, jnp.where(i < n_ctx_tiles, 0, lax.rem(lat, tpb) + 1))

    full = lambda shape: pl.BlockSpec(shape, lambda i: (0,) * len(shape))
    x_specs, x_arrays = _token_specs(x, tm, n_ctx_tiles, 0)
    return pl.pallas_call(
        functools.partial(_premix_kernel, n_x=len(x_arrays), n_ctx_tiles=n_ctx_tiles, tiles_per_batch=tpb),
        grid=(t // tm,),
        in_specs=x_specs + [
            full(mod_l.shape),
            full((1, d)),
            full(wt.shape),
            full(wf.shape),
            full(gq.shape),
            full(gk.shape),
            pl.BlockSpec((HEAD_DIM, tm), tab_idx),
            pl.BlockSpec((HEAD_DIM, tm), tab_idx),
        ],
        out_specs=[
            pl.BlockSpec((ATTN_WIDTH, tm), lambda i: (0, i)),
            pl.BlockSpec((KV_WIDTH, tm), lambda i: (0, i)),
            pl.BlockSpec((KV_WIDTH, tm), lambda i: (0, i)),
            pl.BlockSpec((tm, FOURIER_WIDTH), lambda i: (i, 0)),
        ],
        out_shape=[
            jax.ShapeDtypeStruct((ATTN_WIDTH, t), BF16),
            jax.ShapeDtypeStruct((KV_WIDTH, t), BF16),
            jax.ShapeDtypeStruct((KV_WIDTH, t), BF16),
            jax.ShapeDtypeStruct((t, FOURIER_WIDTH), BF16),
        ],
        compiler_params=_cparams(1),
        name="premix_project",
    )(*x_arrays, mod_l, g_mix, wt, wf, gq, gk, cos_t, sin_t)


def _attn_kernel(*refs, n_key_blocks, tq):
    q_ref = refs[0]
    k_refs = refs[1:1 + n_key_blocks]
    v_refs = refs[1 + n_key_blocks:1 + 2 * n_key_blocks]
    o_ref = refs[1 + 2 * n_key_blocks]
    kall_ref, vg_ref, ot_ref = refs[2 + 2 * n_key_blocks:]
    g = pl.program_id(1)
    qt = pl.program_id(2)
    n_keys = kall_ref.shape[0]
    kc = KEY_CHUNK if n_keys % KEY_CHUNK == 0 else n_keys
    n_chunks = n_keys // kc
    slabs = kc // 8

    @pl.when(jnp.logical_and(g == 0, qt == 0))
    def _():
        off = 0
        for kr in k_refs:
            n = kr.shape[1]
            kall_ref[off:off + n, :] = kr[...].astype(F32).T.astype(BF16)
            off += n

    @pl.when(qt == 0)
    def _():
        g_rows = pl.ds(pl.multiple_of(g * HEAD_DIM, HEAD_DIM), HEAD_DIM)
        off = 0
        for vr in v_refs:
            n = vr.shape[1]
            vg_ref[0:HEAD_DIM, off:off + n] = vr[g_rows, :]
            off += n
        vg_ref[HEAD_DIM:, :] = jnp.ones((ONES_ROWS, n_keys), BF16)

    row_group = lax.broadcasted_iota(jnp.int32, (KV_WIDTH, tq), 0) // HEAD_DIM

    def masked_q(hh):
        qh = q_ref[hh * HEAD_DIM:(hh + 1) * HEAD_DIM, :]
        q2 = jnp.concatenate([qh, qh], axis=0)
        return jnp.where(row_group == g, q2, jnp.zeros_like(q2))

    q2s = [masked_q(hh) for hh in range(KV_REP)]
    items = [(hh, c) for hh in range(KV_REP) for c in range(n_chunks)]

    def score(item):
        hh, c = item
        return jnp.dot(kall_ref[c * kc:(c + 1) * kc, :], q2s[hh], preferred_element_type=F32)

    pending = [score(it) for it in items[:SCORE_LOOKAHEAD]]
    m = ot = None
    for i, (hh, c) in enumerate(items):
        if c == 0:
            m = jnp.full((1, tq), -jnp.inf, F32)
            ot = jnp.zeros((HEAD_DIM + ONES_ROWS, tq), F32)
        s = pending.pop(0).reshape(slabs, 8, tq)
        if i + SCORE_LOOKAHEAD < len(items):
            pending.append(score(items[i + SCORE_LOOKAHEAD]))
        m_new = jnp.maximum(m, jnp.max(jnp.max(s, axis=0), axis=0, keepdims=True))
        pb = jnp.exp2(s - m_new[None]).reshape(kc, tq).astype(BF16)
        rows = slice(c * kc, (c + 1) * kc)
        ot = jnp.exp2(m - m_new) * ot + jnp.dot(vg_ref[:, rows], pb, preferred_element_type=F32)
        m = m_new
        if c == n_chunks - 1:
            ot_ref[hh * HEAD_DIM:(hh + 1) * HEAD_DIM, :] = ot[:HEAD_DIM] / ot[HEAD_DIM:HEAD_DIM + 1]
    o_ref[...] = ot_ref[...].T.astype(BF16)


def _attention(q_t, k_t, v_t, *, q_tile0, n_q_tiles, key_blocks, tq, n_batch, name):
    n_keys = sum(c for c, _ in key_blocks)
    nkb = len(key_blocks)
    q_spec = pl.BlockSpec((KV_REP * HEAD_DIM, tq),
                          lambda b, g, i: (g, q_tile0 + b * n_q_tiles + i))
    k_specs = [pl.BlockSpec((KV_WIDTH, c), (lambda f: (lambda b, g, i: (0, f(b))))(f))
               for c, f in key_blocks]
    return pl.pallas_call(
        functools.partial(_attn_kernel, n_key_blocks=nkb, tq=tq),
        grid=(n_batch, N_KV_HEADS, n_q_tiles),
        in_specs=[q_spec] + k_specs + k_specs,
        out_specs=pl.BlockSpec((tq, KV_REP * HEAD_DIM),
                               lambda b, g, i: (b * n_q_tiles + i, g)),
        out_shape=jax.ShapeDtypeStruct((n_batch * n_q_tiles * tq, ATTN_WIDTH), BF16),
        scratch_shapes=[pltpu.VMEM((n_keys, KV_WIDTH), BF16),
                        pltpu.VMEM((HEAD_DIM + ONES_ROWS, n_keys), BF16),
                        pltpu.VMEM((KV_REP * HEAD_DIM, tq), F32)],
        compiler_params=_cparams(3),
        name=name,
    )(q_t, *([k_t] * nkb), *([v_t] * nkb))


def _fourier_kernel(c_ref, s_ref, x_ref, cc_ref, sc_ref, w_ref, o_ref, *, norm):
    x = x_ref[...]
    a = jnp.dot(c_ref[...], x, preferred_element_type=F32).astype(BF16)
    b = jnp.dot(s_ref[...], x, preferred_element_type=F32).astype(BF16)
    cc = cc_ref[...]
    sc = sc_ref[...]
    groups = [slice(grp * FOURIER_GROUP, (grp + 1) * FOURIER_GROUP) for grp in range(N_FOURIER_GROUPS)]
    specs = [(jnp.dot(a[:, sl], cc, preferred_element_type=F32)
              - jnp.dot(b[:, sl], sc, preferred_element_type=F32)) * norm for sl in groups]
    for grp, sl in enumerate(groups):
        o_ref[:, sl] = jnp.dot(specs[grp].astype(BF16), w_ref[grp], preferred_element_type=F32).astype(BF16)


def _fourier(f, cmat, smat, cc, sc, w_four, *, n, tr, x_block0, n_batch, name):
    n_row_tiles = n // tr
    return pl.pallas_call(
        functools.partial(_fourier_kernel, norm=float(1.0 / np.sqrt(n * FOURIER_GROUP))),
        grid=(n_row_tiles, n_batch),
        in_specs=[
            pl.BlockSpec((tr, n), lambda i, b: (i, 0)),
            pl.BlockSpec((tr, n), lambda i, b: (i, 0)),
            pl.BlockSpec((n, FOURIER_WIDTH), lambda i, b: (x_block0 + b, 0)),
            pl.BlockSpec(cc.shape, lambda i, b: (0, 0)),
            pl.BlockSpec(sc.shape, lambda i, b: (0, 0)),
            pl.BlockSpec(w_four.shape, lambda i, b: (0, 0, 0)),
        ],
        out_specs=pl.BlockSpec((tr, FOURIER_WIDTH), lambda i, b: (b * n_row_tiles + i, 0)),
        out_shape=jax.ShapeDtypeStruct((n_batch * n, FOURIER_WIDTH), BF16),
        compiler_params=_cparams(2),
        name=name,
    )(cmat, smat, f, cc, sc, w_four)


def _dft_half_tiles(n, tr):
    n_tiles = n // 2 // tr
    k = (np.arange(n_tiles, dtype=np.int64)[:, None] * tr
         + np.arange(tr + SYM_EXTRA, dtype=np.int64)[None, :])
    pos = np.arange(n, dtype=np.int64)
    ang = 2.0 * np.pi * ((k[:, :, None] * pos[None, None, :]) % n).astype(np.float64) / n
    rev = np.eye(tr, dtype=np.float32)[::-1]
    return jnp.asarray(np.cos(ang), BF16), jnp.asarray(np.sin(ang), BF16), jnp.asarray(rev, BF16)


def _fourier_sym_kernel(c_ref, s_ref, x_ref, cc_ref, sc_ref, w_ref, rev_ref, o_ref, *, norm, tr, n_tiles):
    i = pl.program_id(1)
    x = x_ref[...]
    a = jnp.dot(c_ref[0], x, preferred_element_type=F32).astype(BF16)
    b = jnp.dot(s_ref[0], x, preferred_element_type=F32).astype(BF16)
    cc = cc_ref[...]
    sc = sc_ref[...]
    groups = [slice(grp * FOURIER_GROUP, (grp + 1) * FOURIER_GROUP) for grp in range(N_FOURIER_GROUPS)]
    ps = [jnp.dot(a[:, sl], cc, preferred_element_type=F32) for sl in groups]
    qs = [jnp.dot(b[:, sl], sc, preferred_element_type=F32) for sl in groups]
    upper = jnp.concatenate([((p + q) * norm)[1:tr + 1] for p, q in zip(ps, qs)], axis=1).astype(BF16)
    upper = jnp.dot(rev_ref[...], upper, preferred_element_type=F32).astype(BF16)
    lo_rows = pl.ds(pl.multiple_of(i * tr, tr), tr)
    hi_rows = pl.ds(pl.multiple_of((2 * n_tiles - 1 - i) * tr, tr), tr)
    for grp, sl in enumerate(groups):
        lower = ((ps[grp] - qs[grp]) * norm)[:tr].astype(BF16)
        o_ref[lo_rows, sl] = jnp.dot(lower, w_ref[grp], preferred_element_type=F32).astype(BF16)
    for grp, sl in enumerate(groups):
        o_ref[hi_rows, sl] = jnp.dot(upper[:, sl], w_ref[grp], preferred_element_type=F32).astype(BF16)


def _fourier_sym(f, c_tiles, s_tiles, rev, cc, sc, w_four, *, n, tr, x_block0, n_batch, name):
    n_tiles = n // 2 // tr
    ext = tr + SYM_EXTRA
    return pl.pallas_call(
        functools.partial(_fourier_sym_kernel, norm=float(1.0 / np.sqrt(n * FOURIER_GROUP)), tr=tr,
                          n_tiles=n_tiles),
        grid=(n_batch, n_tiles),
        in_specs=[
            pl.BlockSpec((1, ext, n), lambda b, i: (i, 0, 0)),
            pl.BlockSpec((1, ext, n), lambda b, i: (i, 0, 0)),
            pl.BlockSpec((n, FOURIER_WIDTH), lambda b, i: (x_block0 + b, 0)),
            pl.BlockSpec(cc.shape, lambda b, i: (0, 0)),
            pl.BlockSpec(sc.shape, lambda b, i: (0, 0)),
            pl.BlockSpec(w_four.shape, lambda b, i: (0, 0, 0)),
            pl.BlockSpec(rev.shape, lambda b, i: (0, 0)),
        ],
        out_specs=pl.BlockSpec((n, FOURIER_WIDTH), lambda b, i: (b, 0)),
        out_shape=jax.ShapeDtypeStruct((n_batch * n, FOURIER_WIDTH), BF16),
        compiler_params=_cparams(2),
        name=name,
    )(c_tiles, s_tiles, f, cc, sc, w_four, rev)


def _postmix_kernel(*refs, n_x, n_a, n_f, tile0, n_ctx_tiles, tiles_per_batch, with_router):
    x_refs, refs = refs[:n_x], refs[n_x:]
    a_refs, refs = refs[:n_a], refs[n_a:]
    f_refs, refs = refs[:n_f], refs[n_f:]
    mod_ref, g_ref, woa_ref, wof_ref = refs[:4]
    if with_router:
        wr_ref, br_ref, x1_ref, h_ref, rt_ref, rtt_ref, cnt_ref = refs[4:]
    else:
        x1_ref, h_ref = refs[4:]
    i = pl.program_id(0) + tile0
    row = _mod_row(i, n_ctx_tiles, tiles_per_batch)
    is_ctx = i < n_ctx_tiles
    a = _load_tokens(a_refs, is_ctx)
    f = _load_tokens(f_refs, is_ctx)
    mix = (jnp.dot(a, woa_ref[...], preferred_element_type=F32)
           + jnp.dot(f, wof_ref[...], preferred_element_type=F32))
    x1 = _load_tokens(x_refs, is_ctx) + _mod_vec(mod_ref, row, 2) * mix
    x1_ref[...] = x1
    h = _norm_modulate(x1, g_ref[...], _mod_vec(mod_ref, row, 3), _mod_vec(mod_ref, row, 4))
    hb = h.astype(BF16)
    if with_router:
        h_ref[...] = _pack_bf16_pairs(hb)
        logits = jnp.dot(hb, wr_ref[...], preferred_element_type=F32) + br_ref[...]
        _top2_route(logits, rt_ref, rtt_ref, cnt_ref, first=pl.program_id(0) == 0)
    else:
        h_ref[...] = hb


def _top2_route(logits, rt_ref, rtt_ref, cnt_ref, first):
    tm, lanes = logits.shape
    lane = lax.broadcasted_iota(jnp.int32, (tm, lanes), 1)
    neg = jnp.float32(-jnp.inf)
    m1 = jnp.max(logits, axis=1, keepdims=True)
    i1 = jnp.min(jnp.where(logits == m1, lane, lanes), axis=1, keepdims=True)
    rest = jnp.where(lane == i1, neg, logits)
    m2 = jnp.max(rest, axis=1, keepdims=True)
    i2 = jnp.min(jnp.where(rest == m2, lane, lanes), axis=1, keepdims=True)
    e = jnp.exp(m2 - m1)
    g1 = 1.0 / (1.0 + e)
    g2 = e / (1.0 + e)
    pick1 = lane == i1
    pick2 = lane == i2
    picked = jnp.logical_or(pick1, pick2).astype(F32)

    @pl.when(first)
    def _():
        cnt_ref[...] = jnp.zeros_like(cnt_ref)

    r_i = lax.broadcasted_iota(jnp.int32, (tm, tm), 0)
    c_i = lax.broadcasted_iota(jnp.int32, (tm, tm), 1)
    lower = jnp.where(c_i < r_i, 1.0, 0.0).astype(BF16)
    before = jnp.dot(lower, picked.astype(BF16), preferred_element_type=F32) + cnt_ref[0:1, :]
    r1 = jnp.sum(jnp.where(pick1, before, 0.0), axis=1, keepdims=True)
    r2 = jnp.sum(jnp.where(pick2, before, 0.0), axis=1, keepdims=True)
    cnt_ref[...] = cnt_ref[...] + jnp.sum(picked, axis=0, keepdims=True)
    out = jnp.zeros((tm, lanes), F32)
    for k, val in enumerate([i1.astype(F32), i2.astype(F32), g1, g2, r1, r2]):
        out = jnp.where(lane == k, val, out)
    rt_ref[...] = out
    rtt_ref[...] = out.T[0:8, :]


def _postmix(x, attn, four, mod_l, g_ffn, wo_a, wo_f, router, *, n_ctx, seq, t, tile0):
    d = D_MODEL
    tm = TM_MIX
    nct = n_ctx // tm
    t = t - tile0 * tm
    full = lambda shape: pl.BlockSpec(shape, lambda i: (0,) * len(shape))
    tile = lambda w: pl.BlockSpec((tm, w), lambda i: (i, 0))
    x_specs, x_arrays = _token_specs(x, tm, nct, tile0)
    a_specs, a_arrays = _token_specs(attn, tm, nct, tile0)
    f_specs, f_arrays = _token_specs(four, tm, nct, tile0)
    in_specs = x_specs + a_specs + f_specs + [full(mod_l.shape), full((1, d)), full(wo_a.shape),
                                              full(wo_f.shape)]
    if router is None:
        out_specs = [tile(d), tile(d)]
        out_shape = [jax.ShapeDtypeStruct((t, d), F32), jax.ShapeDtypeStruct((t, d), BF16)]
    else:
        out_specs = [tile(d), tile(d // 2)]
        out_shape = [jax.ShapeDtypeStruct((t, d), F32), jax.ShapeDtypeStruct((t, d // 2), jnp.uint32)]
    args = x_arrays + a_arrays + f_arrays + [mod_l, g_ffn, wo_a, wo_f]
    if router is not None:
        in_specs += [full(router[0].shape), full(router[1].shape)]
        out_specs += [tile(LOGIT_LANES), pl.BlockSpec((8, tm), lambda i: (0, i)),
                      pl.BlockSpec((8, LOGIT_LANES), lambda i: (0, 0))]
        out_shape += [jax.ShapeDtypeStruct((t, LOGIT_LANES), F32), jax.ShapeDtypeStruct((8, t), F32),
                      jax.ShapeDtypeStruct((8, LOGIT_LANES), F32)]
        args += list(router)
    return pl.pallas_call(
        functools.partial(_postmix_kernel, n_x=len(x_arrays), n_a=len(a_arrays), n_f=len(f_arrays),
                          tile0=tile0, n_ctx_tiles=nct, tiles_per_batch=seq // tm,
                          with_router=router is not None),
        grid=(t // tm,),
        in_specs=in_specs,
        out_specs=out_specs,
        out_shape=out_shape,
        compiler_params=_cparams(1),
        name="postmix_wo_norm",
    )(*args)


def _silu_mul(gate, up):
    return (gate / (1.0 + jnp.exp(-gate))) * up


def _swiglu_chunks(xs, n_chunks, wg_of, wu_of, wd_of, acc_refs):
    if not isinstance(xs, (list, tuple)):
        xs, acc_refs = [xs], [acc_refs]
    items = [(c, r) for c in range(n_chunks) for r in range(len(xs))]
    fetched = {}

    def weight(getter, c):
        if (getter, c) not in fetched:
            fetched[(getter, c)] = getter(c)
        return fetched[(getter, c)]

    def gate_up(c, r):
        return (jnp.dot(xs[r], weight(wg_of, c), preferred_element_type=F32),
                jnp.dot(xs[r], weight(wu_of, c), preferred_element_type=F32))

    gate, up = gate_up(*items[0])
    for idx, (c, r) in enumerate(items):
        act = _silu_mul(gate, up).astype(BF16)
        if idx + 1 < len(items):
            gate, up = gate_up(*items[idx + 1])
        acc_refs[r][...] += jnp.dot(act, weight(wd_of, c), preferred_element_type=F32)


def _dense_ffn_kernel(h_ref, x1_ref, mod_ref, wg_ref, wu_ref, wd_ref, o_ref, acc_ref, *,
                      tile0, n_ctx_tiles, tiles_per_batch):
    i = pl.program_id(0) + tile0
    row = _mod_row(i, n_ctx_tiles, tiles_per_batch)
    acc_ref[...] = jnp.zeros_like(acc_ref)
    sub = lambda c: slice(c * FF_CHUNK, (c + 1) * FF_CHUNK)
    _swiglu_chunks(h_ref[...], wg_ref.shape[1] // FF_CHUNK, lambda c: wg_ref[:, sub(c)],
                   lambda c: wu_ref[:, sub(c)], lambda c: wd_ref[sub(c), :], acc_ref)
    o_ref[...] = x1_ref[...] + _mod_vec(mod_ref, row, 5) * acc_ref[...]


def _dense_ffn(h, x1, mod_l, wg, wu, wd, *, n_ctx, seq, tile0):
    t, d = x1.shape
    tm = TM_DENSE
    full = lambda shape: pl.BlockSpec(shape, lambda i: (0,) * len(shape))
    tile = lambda: pl.BlockSpec((tm, d), lambda i: (i, 0))
    return pl.pallas_call(
        functools.partial(_dense_ffn_kernel, tile0=tile0 * TM_MIX // tm, n_ctx_tiles=n_ctx // tm,
                          tiles_per_batch=seq // tm),
        grid=(t // tm,),
        in_specs=[tile(), tile(), full(mod_l.shape), full(wg.shape), full(wu.shape), full(wd.shape)],
        out_specs=tile(),
        out_shape=jax.ShapeDtypeStruct((t, d), F32),
        scratch_shapes=[pltpu.VMEM((tm, d), F32)],
        compiler_params=_cparams(1),
        name="dense_swiglu",
    )(h, x1, mod_l, wg, wu, wd)


def _moe_ffn_kernel(be_ref, nv_ref, x_ref, wg_ref, wu_ref, wd_ref, o_ref, xb_ref, acc_ref, wgb_ref, wub_ref,
                    wdb_ref):
    i = pl.program_id(0)
    j = pl.program_id(1)
    n_valid = nv_ref[i]

    @pl.when(j == 0)
    def _():
        x = _unpack_bf16_pairs(x_ref[...])
        rows = lax.broadcasted_iota(jnp.int32, x.shape, 0)
        xb_ref[...] = jnp.where(rows < n_valid, x, jnp.zeros_like(x))
        acc_ref[...] = jnp.zeros_like(acc_ref)

    n_sub = TF_MOE // FF_CHUNK
    sub = lambda c: slice(c * FF_CHUNK, (c + 1) * FF_CHUNK)
    rows_needed = ((n_valid + TAIL_ROWS - 1) // TAIL_ROWS) * TAIL_ROWS
    n_full = rows_needed // TM_PASS
    n_tail = (rows_needed - n_full * TM_PASS) // TAIL_ROWS

    def cast_chunk(src_ref, dst_ref, idx):
        w = src_ref[(0, 0) + idx].astype(BF16)
        dst_ref[idx] = w
        return w

    cast_wg = lambda c: cast_chunk(wg_ref, wgb_ref, (slice(None), sub(c)))
    cast_wu = lambda c: cast_chunk(wu_ref, wub_ref, (slice(None), sub(c)))
    cast_wd = lambda c: cast_chunk(wd_ref, wdb_ref, (sub(c), slice(None)))
    read_wg = lambda c: wgb_ref[:, sub(c)]
    read_wu = lambda c: wub_ref[:, sub(c)]
    read_wd = lambda c: wdb_ref[sub(c), :]

    @pl.when(jnp.logical_and(n_full == 0, n_tail > 0))
    def _():
        for c in range(n_sub):
            cast_wg(c), cast_wu(c), cast_wd(c)

    for n_groups in range(1, TM_MOE // TM_PASS + 1):
        groups = [slice(r * TM_PASS, (r + 1) * TM_PASS) for r in range(n_groups)]

        @pl.when(n_full == n_groups)
        def _():
            _swiglu_chunks([xb_ref[rows, :] for rows in groups], n_sub, cast_wg, cast_wu, cast_wd,
                           [acc_ref.at[rows, :] for rows in groups])

    @pl.when(n_tail > 0)
    def _():
        def group(gi, carry):
            rows = pl.ds(pl.multiple_of(n_full * TM_PASS + gi * TAIL_ROWS, TAIL_ROWS), TAIL_ROWS)
            _swiglu_chunks(xb_ref[rows, :], n_sub, read_wg, read_wu, read_wd, acc_ref.at[rows, :])
            return carry

        lax.fori_loop(0, n_tail, group, 0)

    @pl.when(j == pl.num_programs(1) - 1)
    def _():
        o_ref[...] = _pack_bf16_pairs(acc_ref[...].astype(BF16))


def _moe_ffn(xs, block_expert, n_valid, wg, wu, wd, layer):
    n_rows = xs.shape[0]
    d = D_MODEL
    tm, tf = TM_MOE, TF_MOE
    n_blocks = n_rows // tm
    n_ff = wg.shape[3] // tf

    def ff_idx(j, nv, i):
        return jnp.where(nv[i] > 0, j, n_ff - 1)

    grid_spec = pltpu.PrefetchScalarGridSpec(
        num_scalar_prefetch=2,
        grid=(n_blocks, n_ff),
        in_specs=[
            pl.BlockSpec((tm, d // 2), lambda i, j, be, nv: (i, 0)),
            pl.BlockSpec((1, 1, d, tf), lambda i, j, be, nv: (layer, be[i], 0, ff_idx(j, nv, i))),
            pl.BlockSpec((1, 1, d, tf), lambda i, j, be, nv: (layer, be[i], 0, ff_idx(j, nv, i))),
            pl.BlockSpec((1, 1, tf, d), lambda i, j, be, nv: (layer, be[i], ff_idx(j, nv, i), 0)),
        ],
        out_specs=pl.BlockSpec((tm, d // 2), lambda i, j, be, nv: (i, 0)),
        scratch_shapes=[pltpu.VMEM((tm, d), BF16), pltpu.VMEM((tm, d), F32), pltpu.VMEM((d, tf), BF16),
                        pltpu.VMEM((d, tf), BF16), pltpu.VMEM((tf, d), BF16)],
    )
    return pl.pallas_call(
        _moe_ffn_kernel,
        grid_spec=grid_spec,
        out_shape=jax.ShapeDtypeStruct((n_rows, d // 2), jnp.uint32),
        compiler_params=_cparams(2),
        name="expert_swiglu",
    )(block_expert, n_valid, xs, wg, wu, wd)


def _sc_mesh():
    return plsc.VectorSubcoreMesh(core_axis_name="c", subcore_axis_name="s")


def _sc_params():
    return pltpu.CompilerParams(use_tc_tiling_on_sc=True)


def _sc_dispatch(h_packed, dest, n_rows):
    t, w = h_packed.shape
    win = SC_DISPATCH_ROWS
    n_win = t // win
    idx = [dest[k].reshape(n_win, 1, win) for k in range(TOP_K)]

    @functools.partial(
        pl.kernel, out_type=jax.ShapeDtypeStruct((n_rows, w), h_packed.dtype), mesh=_sc_mesh(),
        scratch_types=[], compiler_params=_sc_params(), name="expert_dispatch_scatter")
    def run(x_hbm, i0_hbm, i1_hbm, o_hbm):
        def body(x_vmem, i0_vmem, i1_vmem):
            pltpu.sync_copy(x_vmem, o_hbm.at[i0_vmem.at[0, 0]])
            pltpu.sync_copy(x_vmem, o_hbm.at[i1_vmem.at[0, 0]])

        idx_spec = pl.BlockSpec((1, 1, win), lambda i: (i, 0, 0))
        pltpu.emit_pipeline(
            body, grid=(n_win,),
            in_specs=[pl.BlockSpec((win, w), lambda i: (i, 0)), idx_spec, idx_spec],
            out_specs=[], core_axis_name=("c", "s"), dimension_semantics=(pltpu.PARALLEL,),
        )(x_hbm, i0_hbm, i1_hbm)

    return run(h_packed, *idx)


def _sc_gather(ys, idx_flat):
    w = ys.shape[1]
    n = idx_flat.shape[0]
    win = SC_GATHER_ROWS
    n_win = n // win

    @functools.partial(
        pl.kernel, out_type=jax.ShapeDtypeStruct((n, w), ys.dtype), mesh=_sc_mesh(),
        scratch_types=[], compiler_params=_sc_params(), name="expert_combine_gather")
    def run(y_hbm, i_hbm, o_hbm):
        def body(i_vmem, o_vmem):
            pltpu.sync_copy(y_hbm.at[i_vmem.at[0, 0]], o_vmem)

        pltpu.emit_pipeline(
            body, grid=(n_win,),
            in_specs=[pl.BlockSpec((1, 1, win), lambda i: (i, 0, 0))],
            out_specs=[pl.BlockSpec((win, w), lambda i: (i, 0))],
            core_axis_name=("c", "s"), dimension_semantics=(pltpu.PARALLEL,),
        )(i_hbm, o_hbm)

    return run(ys, idx_flat.reshape(n_win, 1, win))


def _moe_combine_kernel(x1_ref, y0_ref, y1_ref, g_ref, mod_ref, o_ref, *, tile0, n_ctx_tiles,
                        tiles_per_batch):
    i = pl.program_id(0) + tile0
    row = _mod_row(i, n_ctx_tiles, tiles_per_batch)
    gates = g_ref[...]
    y0 = _unpack_bf16_pairs(y0_ref[...]).astype(F32)
    y1 = _unpack_bf16_pairs(y1_ref[...]).astype(F32)
    y = y0 * gates[:, 2:3] + y1 * gates[:, 3:4]
    o_ref[...] = x1_ref[...] + _mod_vec(mod_ref, row, 5) * y


def _moe_combine(x1, y_sel, gates, mod_l, *, n_ctx, seq, tile0):
    t, d = x1.shape
    tm = TM_MIX
    n_tiles = t // tm
    tile = lambda w: pl.BlockSpec((tm, w), lambda i: (i, 0))
    return pl.pallas_call(
        functools.partial(_moe_combine_kernel, tile0=tile0, n_ctx_tiles=n_ctx // tm,
                          tiles_per_batch=seq // tm),
        grid=(n_tiles,),
        in_specs=[tile(d), tile(d // 2), pl.BlockSpec((tm, d // 2), lambda i: (n_tiles + i, 0)),
                  tile(LOGIT_LANES),
                  pl.BlockSpec(mod_l.shape, lambda i: (0, 0))],
        out_specs=tile(d),
        out_shape=jax.ShapeDtypeStruct((t, d), F32),
        compiler_params=_cparams(1),
        name="expert_combine",
    )(x1, y_sel, y_sel, gates, mod_l)


def _route(route_t, counts):
    t = route_t.shape[1]
    expert = route_t[0:TOP_K].astype(jnp.int32)
    rank = route_t[4:4 + TOP_K].astype(jnp.int32)
    counts = counts[0, :N_EXPERTS].astype(jnp.int32)
    n_assign = t * TOP_K
    padded = ((counts + TM_MOE - 1) // TM_MOE) * TM_MOE
    pend = jnp.cumsum(padded)
    pstart = pend - padded
    dest = rank
    for e in range(N_EXPERTS):
        dest = dest + jnp.where(expert == e, pstart[e], 0)
    n_blocks = -(-n_assign // TM_MOE) + N_EXPERTS
    block_start = jnp.arange(n_blocks, dtype=jnp.int32) * TM_MOE
    block_expert = jnp.minimum(jnp.sum(pend[None, :] <= block_start[:, None], axis=1),
                               N_EXPERTS - 1).astype(jnp.int32)
    n_valid = jnp.clip(counts[block_expert] - (block_start - pstart[block_expert]), 0, TM_MOE)
    n_valid = jnp.where(block_start < pend[-1], n_valid, 0).astype(jnp.int32)
    return dest.astype(jnp.int32), block_expert, n_valid, n_blocks * TM_MOE


def kernel(x, c, ctx, c_ctx, w_mod, b_mod, g_mix, g_ffn, g_q, g_k, w_in, w_four, w_o,
           w_gate_dense, w_up_dense, w_down_dense, w_router, b_router,
           w_gate_moe, w_up_moe, w_down_moe):
    b, s, d = x.shape
    n_ctx_len = ctx.shape[1]
    n_ctx = b * n_ctx_len
    t = n_ctx + b * s
    assert d == D_MODEL and b + 1 <= MOD_ROWS
    assert n_ctx % TM_DENSE == 0 and s % TM_DENSE == 0 and n_ctx_len % 128 == 0 and n_ctx % s == 0

    cvec = jnp.zeros((MOD_ROWS, d), F32).at[0].set(c_ctx).at[1:b + 1].set(c)
    mod = _modulation(cvec, w_mod, b_mod)

    cos_t, sin_t = _rope_tables_t(s, TM_MIX)
    c_lat, s_lat, rev_lat = _dft_half_tiles(s, TR_FOUR)
    c_ctx_m, s_ctx_m = _dft_mats(n_ctx_len)
    c_grp, s_grp = _dft_mats(FOURIER_GROUP)

    xa = (ctx.reshape(n_ctx, d), x.reshape(b * s, d))

    for l in range(DEPTH):
        mod_l = mod[l]
        w_in_l = w_in[l]
        wt = w_in_l[:, :QKV_WIDTH].T.astype(BF16)
        wf = w_in_l[:, QKV_WIDTH:].astype(BF16)
        gq = jnp.broadcast_to((g_q[l] * (LOG2_E * HEAD_DIM ** -0.5))[:, None], (HEAD_DIM, TM_MIX))
        gk = jnp.broadcast_to(g_k[l][:, None], (HEAD_DIM, TM_MIX))
        last = l == DEPTH - 1
        tile0 = n_ctx // TM_MIX if last else 0
        q_t, k_t, v_t, f = _premix(xa, mod_l, g_mix[l].reshape(1, d), wt, wf, gq, gk, cos_t, sin_t,
                                   n_ctx=n_ctx, seq=s, t=t)

        n_qt = s // TQ_ATTN
        lat_keys = [(n_ctx_len, lambda bb: bb), (s, lambda bb: n_ctx // s + bb)]
        attn_lat = _attention(q_t, k_t, v_t, q_tile0=n_ctx // TQ_ATTN, n_q_tiles=n_qt,
                              key_blocks=lat_keys, tq=TQ_ATTN, n_batch=b, name="attention_latent")
        attn_ctx = None if last else _attention(
            q_t, k_t, v_t, q_tile0=0, n_q_tiles=1, key_blocks=[(n_ctx_len, lambda bb: bb)],
            tq=n_ctx_len, n_batch=b, name="attention_context")

        wfour = w_four[l].astype(BF16)
        four_lat = _fourier_sym(f, c_lat, s_lat, rev_lat, c_grp, s_grp, wfour, n=s, tr=TR_FOUR,
                                x_block0=n_ctx // s, n_batch=b, name="fourier_latent")
        four_ctx = None if last else _fourier(
            f, c_ctx_m, s_ctx_m, c_grp, s_grp, wfour, n=n_ctx_len, tr=n_ctx_len, x_block0=0, n_batch=b,
            name="fourier_context")

        wo = w_o[l].astype(BF16)
        is_moe = l % 2 == 1
        li = l // 2
        router = None
        if is_moe:
            wr = jnp.zeros((d, LOGIT_LANES), BF16).at[:, :N_EXPERTS].set(w_router[li].astype(BF16))
            br = jnp.full((1, LOGIT_LANES), -jnp.inf, F32).at[0, :N_EXPERTS].set(b_router[li])
            router = (wr, br)
        res = _postmix(xa, (attn_ctx, attn_lat), (four_ctx, four_lat), mod_l, g_ffn[l].reshape(1, d),
                       wo[:ATTN_WIDTH], wo[ATTN_WIDTH:], router, n_ctx=n_ctx, seq=s, t=t, tile0=tile0)
        t_l = t - tile0 * TM_MIX
        if not is_moe:
            x1, h2 = res
            xa = _dense_ffn(h2, x1, mod_l, w_gate_dense[li].astype(BF16), w_up_dense[li].astype(BF16),
                            w_down_dense[li].astype(BF16), n_ctx=n_ctx, seq=s, tile0=tile0)
        else:
            x1, h2, route, route_t, counts = res
            dest, block_expert, n_valid, n_rows = _route(route_t, counts)
            xs = _sc_dispatch(h2, dest, n_rows)
            ys = _moe_ffn(xs, block_expert, n_valid, w_gate_moe, w_up_moe, w_down_moe, li)
            y_sel = _sc_gather(ys, dest.reshape(-1))
            xa = _moe_combine(x1, y_sel, route, mod_l, n_ctx=n_ctx, seq=s, tile0=tile0)

    return xa.reshape(b, s, d)
```

```python
import functools

import numpy as np
import jax
import jax.numpy as jnp
from jax import lax
from jax.experimental import pallas as pl
from jax.experimental.pallas import tpu as pltpu
from jax.experimental.pallas import tpu_sc as plsc

D_MODEL = 1024
DEPTH = 4
GRID_W = 64
HEAD_DIM = 64
ATTN_WIDTH = 512
N_Q_HEADS = 8
N_KV_HEADS = 2
KV_REP = 4
KV_WIDTH = 128
FOURIER_WIDTH = 512
N_FOURIER_GROUPS = 4
FOURIER_GROUP = 128
ROT_PER_AXIS = 32
ROPE_THETA = 10000.0
N_EXPERTS = 8
TOP_K = 2
EPS = 1e-6

QKV_WIDTH = ATTN_WIDTH + 2 * KV_WIDTH
MOD_ROWS = 16
LOGIT_LANES = 128
EXPERT_ROWS = 16

TM_MIX = 1024
TQ_ATTN = 512
KEY_CHUNK = 256
ONES_ROWS = 16
SCORE_LOOKAHEAD = 2
LOG2_E = 1.4426950408889634
TR_FOUR = 512
SYM_EXTRA = 16
TM_DENSE = 1024
FF_CHUNK = 256
TM_MOE = 2048
TM_PASS = 1024
TAIL_ROWS = 256
TF_MOE = 512
SC_DISPATCH_ROWS = 64
SC_GATHER_ROWS = 64
V7X_VMEM_BYTES = 64 * 1024 * 1024
VMEM_LIMIT = V7X_VMEM_BYTES - 8 * 1024 * 1024

F32 = jnp.float32
BF16 = jnp.bfloat16


def _cparams(n_axes):
    return pltpu.CompilerParams(
        dimension_semantics=("arbitrary",) * n_axes, vmem_limit_bytes=VMEM_LIMIT)


def _rope_tables_t(s, tm):
    n_rows = s // GRID_W
    rows = np.repeat(np.arange(n_rows), GRID_W).astype(np.float64)
    cols = np.tile(np.arange(GRID_W), n_rows).astype(np.float64)
    inv_freq = (ROPE_THETA ** (-np.arange(0, ROT_PER_AXIS, 2, dtype=np.float32) / ROT_PER_AXIS)
                ).astype(np.float32).astype(np.float64)
    ang_r = (rows[None, :].astype(np.float32) * inv_freq[:, None].astype(np.float32)).astype(np.float64)
    ang_c = (cols[None, :].astype(np.float32) * inv_freq[:, None].astype(np.float32)).astype(np.float64)
    cos = np.concatenate([np.cos(ang_r), np.cos(ang_r), np.cos(ang_c), np.cos(ang_c)], axis=0)
    sin = np.concatenate([-np.sin(ang_r), np.sin(ang_r), -np.sin(ang_c), np.sin(ang_c)], axis=0)
    cos = np.concatenate([np.ones((HEAD_DIM, tm)), cos], axis=1)
    sin = np.concatenate([np.zeros((HEAD_DIM, tm)), sin], axis=1)
    return jnp.asarray(cos, F32), jnp.asarray(sin, F32)


def _dft_mats(n):
    k = np.arange(n, dtype=np.int64)
    ang = 2.0 * np.pi * ((k[:, None] * k[None, :]) % n).astype(np.float64) / n
    return jnp.asarray(np.cos(ang), BF16), jnp.asarray(np.sin(ang), BF16)


def _mod_kernel(c_ref, w_ref, b_ref, o_ref):
    c = c_ref[...]
    s = (c / (1.0 + jnp.exp(-c))).astype(BF16)
    w = w_ref[0].astype(BF16)
    o_ref[0] = jnp.dot(s, w, preferred_element_type=F32) + b_ref[0]


def _modulation(cvec, w_mod, b_mod):
    depth, d, n = w_mod.shape
    tn = 1536
    return pl.pallas_call(
        _mod_kernel,
        grid=(depth, n // tn),
        in_specs=[
            pl.BlockSpec((MOD_ROWS, d), lambda l, j: (0, 0)),
            pl.BlockSpec((1, d, tn), lambda l, j: (l, 0, j)),
            pl.BlockSpec((1, 1, tn), lambda l, j: (l, 0, j)),
        ],
        out_specs=pl.BlockSpec((1, MOD_ROWS, tn), lambda l, j: (l, 0, j)),
        out_shape=jax.ShapeDtypeStruct((depth, MOD_ROWS, n), F32),
        compiler_params=_cparams(2),
        name="adaln_vectors",
    )(cvec, w_mod, b_mod.reshape(depth, 1, n))


def _mod_row(i, n_ctx_tiles, tiles_per_batch):
    lat = jnp.maximum(i - n_ctx_tiles, 0)
    return jnp.where(i < n_ctx_tiles, 0, lat // tiles_per_batch + 1)


def _mod_vec(mod_ref, row, comp):
    return mod_ref[pl.ds(row, 1), comp * D_MODEL:(comp + 1) * D_MODEL]


def _pack_bf16_pairs(xb):
    n = xb.shape[1] // 2
    bits = lax.bitcast_convert_type(xb.astype(F32), jnp.uint32)
    return (bits[:, :n] >> 16) | (bits[:, n:] & jnp.uint32(0xFFFF0000))


def _unpack_bf16_pairs(w):
    lo = lax.bitcast_convert_type(w << 16, F32)
    hi = lax.bitcast_convert_type(w & jnp.uint32(0xFFFF0000), F32)
    return jnp.concatenate([lo, hi], axis=1).astype(BF16)


def _token_specs(x, tm, n_ctx_tiles, tile0):
    if not isinstance(x, tuple):
        return [pl.BlockSpec((tm, x.shape[1]), lambda i: (i + tile0, 0))], [x]
    ctx, lat = x
    specs, arrays = [], []
    if ctx is not None:
        specs.append(pl.BlockSpec((tm, ctx.shape[1]), lambda i: (jnp.minimum(i + tile0, n_ctx_tiles - 1), 0)))
        arrays.append(ctx)
    else:
        assert tile0 >= n_ctx_tiles
    specs.append(pl.BlockSpec((tm, lat.shape[1]), lambda i: (jnp.maximum(i + tile0 - n_ctx_tiles, 0), 0)))
    arrays.append(lat)
    return specs, arrays


def _load_tokens(refs, is_ctx):
    if len(refs) == 2:
        return jnp.where(is_ctx, refs[0][...], refs[1][...])
    return refs[0][...]


def _norm_modulate(x, g, shift, scale):
    ms = jnp.mean(x * x, axis=-1, keepdims=True)
    y = x * lax.rsqrt(ms + EPS) * g
    return y * (1.0 + scale) + shift


def _premix_kernel(*refs, n_x, n_ctx_tiles, tiles_per_batch):
    x_refs = refs[:n_x]
    (mod_ref, g_ref, wt_ref, wf_ref, gq_ref, gk_ref, cos_ref, sin_ref,
     q_ref, k_ref, v_ref, f_ref) = refs[n_x:]
    i = pl.program_id(0)
    row = _mod_row(i, n_ctx_tiles, tiles_per_batch)
    x = _load_tokens(x_refs, i < n_ctx_tiles)
    h = _norm_modulate(x, g_ref[...], _mod_vec(mod_ref, row, 0), _mod_vec(mod_ref, row, 1))
    hb = h.astype(BF16)
    f_ref[...] = jnp.dot(hb, wf_ref[...], preferred_element_type=F32).astype(BF16)
    pt = lax.dot_general(wt_ref[...], hb, (((1,), (1,)), ((), ())), preferred_element_type=F32)
    v_ref[...] = pt[ATTN_WIDTH + KV_WIDTH:, :].astype(BF16)
    cos = cos_ref[...]
    sin = sin_ref[...]

    def norm_rope(xh, gain):
        ms = jnp.mean(xh * xh, axis=0, keepdims=True)
        y = xh * lax.rsqrt(ms + EPS) * gain
        half = ROT_PER_AXIS // 2
        swapped = jnp.concatenate(
            [y[half:2 * half], y[0:half], y[3 * half:4 * half], y[2 * half:3 * half]], axis=0)
        return y * cos + swapped * sin

    gq = gq_ref[...]
    gk = gk_ref[...]
    for hh in range(N_Q_HEADS):
        q_ref[hh * HEAD_DIM:(hh + 1) * HEAD_DIM, :] = norm_rope(
            pt[hh * HEAD_DIM:(hh + 1) * HEAD_DIM, :], gq).astype(BF16)
    for hh in range(N_KV_HEADS):
        lo = ATTN_WIDTH + hh * HEAD_DIM
        k_ref[hh * HEAD_DIM:(hh + 1) * HEAD_DIM, :] = norm_rope(pt[lo:lo + HEAD_DIM, :], gk).astype(BF16)


def _premix(x, mod_l, g_mix, wt, wf, gq, gk, cos_t, sin_t, *, n_ctx, seq, t):
    d = D_MODEL
    tm = TM_MIX
    n_ctx_tiles = n_ctx // tm
    tpb = seq // tm

    def tab_idx(i):
        lat = jnp.maximum(i - n_ctx_tiles, 0)
        return (0, jnp.where(i < n_ctx_tiles, 0, lax.rem(lat, tpb) + 1))

    full = lambda shape: pl.BlockSpec(shape, lambda i: (0,) * len(shape))
    x_specs, x_arrays = _token_specs(x, tm, n_ctx_tiles, 0)
    return pl.pallas_call(
        functools.partial(_premix_kernel, n_x=len(x_arrays), n_ctx_tiles=n_ctx_tiles, tiles_per_batch=tpb),
        grid=(t // tm,),
        in_specs=x_specs + [
            full(mod_l.shape),
            full((1, d)),
            full(wt.shape),
            full(wf.shape),
            full(gq.shape),
            full(gk.shape),
            pl.BlockSpec((HEAD_DIM, tm), tab_idx),
            pl.BlockSpec((HEAD_DIM, tm), tab_idx),
        ],
        out_specs=[
            pl.BlockSpec((ATTN_WIDTH, tm), lambda i: (0, i)),
            pl.BlockSpec((KV_WIDTH, tm), lambda i: (0, i)),
            pl.BlockSpec((KV_WIDTH, tm), lambda i: (0, i)),
            pl.BlockSpec((tm, FOURIER_WIDTH), lambda i: (i, 0)),
        ],
        out_shape=[
            jax.ShapeDtypeStruct((ATTN_WIDTH, t), BF16),
            jax.ShapeDtypeStruct((KV_WIDTH, t), BF16),
            jax.ShapeDtypeStruct((KV_WIDTH, t), BF16),
            jax.ShapeDtypeStruct((t, FOURIER_WIDTH), BF16),
        ],
        compiler_params=_cparams(1),
        name="premix_project",
    )(*x_arrays, mod_l, g_mix, wt, wf, gq, gk, cos_t, sin_t)


def _attn_kernel(*refs, n_key_blocks, tq):
    q_ref = refs[0]
    k_refs = refs[1:1 + n_key_blocks]
    v_refs = refs[1 + n_key_blocks:1 + 2 * n_key_blocks]
    o_ref = refs[1 + 2 * n_key_blocks]
    kall_ref, vg_ref, ot_ref = refs[2 + 2 * n_key_blocks:]
    g = pl.program_id(1)
    qt = pl.program_id(2)
    n_keys = kall_ref.shape[0]
    kc = KEY_CHUNK if n_keys % KEY_CHUNK == 0 else n_keys
    n_chunks = n_keys // kc
    slabs = kc // 8

    @pl.when(jnp.logical_and(g == 0, qt == 0))
    def _():
        off = 0
        for kr in k_refs:
            n = kr.shape[1]
            kall_ref[off:off + n, :] = kr[...].astype(F32).T.astype(BF16)
            off += n

    @pl.when(qt == 0)
    def _():
        g_rows = pl.ds(pl.multiple_of(g * HEAD_DIM, HEAD_DIM), HEAD_DIM)
        off = 0
        for vr in v_refs:
            n = vr.shape[1]
            vg_ref[0:HEAD_DIM, off:off + n] = vr[g_rows, :]
            off += n
        vg_ref[HEAD_DIM:, :] = jnp.ones((ONES_ROWS, n_keys), BF16)

    row_group = lax.broadcasted_iota(jnp.int32, (KV_WIDTH, tq), 0) // HEAD_DIM

    def masked_q(hh):
        qh = q_ref[hh * HEAD_DIM:(hh + 1) * HEAD_DIM, :]
        q2 = jnp.concatenate([qh, qh], axis=0)
        return jnp.where(row_group == g, q2, jnp.zeros_like(q2))

    q2s = [masked_q(hh) for hh in range(KV_REP)]
    items = [(hh, c) for hh in range(KV_REP) for c in range(n_chunks)]

    def score(item):
        hh, c = item
        return jnp.dot(kall_ref[c * kc:(c + 1) * kc, :], q2s[hh], preferred_element_type=F32)

    pending = [score(it) for it in items[:SCORE_LOOKAHEAD]]
    m = ot = None
    for i, (hh, c) in enumerate(items):
        if c == 0:
            m = jnp.full((1, tq), -jnp.inf, F32)
            ot = jnp.zeros((HEAD_DIM + ONES_ROWS, tq), F32)
        s = pending.pop(0).reshape(slabs, 8, tq)
        if i + SCORE_LOOKAHEAD < len(items):
            pending.append(score(items[i + SCORE_LOOKAHEAD]))
        m_new = jnp.maximum(m, jnp.max(jnp.max(s, axis=0), axis=0, keepdims=True))
        pb = jnp.exp2(s - m_new[None]).reshape(kc, tq).astype(BF16)
        rows = slice(c * kc, (c + 1) * kc)
        ot = jnp.exp2(m - m_new) * ot + jnp.dot(vg_ref[:, rows], pb, preferred_element_type=F32)
        m = m_new
        if c == n_chunks - 1:
            ot_ref[hh * HEAD_DIM:(hh + 1) * HEAD_DIM, :] = ot[:HEAD_DIM] / ot[HEAD_DIM:HEAD_DIM + 1]
    o_ref[...] = ot_ref[...].T.astype(BF16)


def _attention(q_t, k_t, v_t, *, q_tile0, n_q_tiles, key_blocks, tq, n_batch, name):
    n_keys = sum(c for c, _ in key_blocks)
    nkb = len(key_blocks)
    q_spec = pl.BlockSpec((KV_REP * HEAD_DIM, tq),
                          lambda b, g, i: (g, q_tile0 + b * n_q_tiles + i))
    k_specs = [pl.BlockSpec((KV_WIDTH, c), (lambda f: (lambda b, g, i: (0, f(b))))(f))
               for c, f in key_blocks]
    return pl.pallas_call(
        functools.partial(_attn_kernel, n_key_blocks=nkb, tq=tq),
        grid=(n_batch, N_KV_HEADS, n_q_tiles),
        in_specs=[q_spec] + k_specs + k_specs,
        out_specs=pl.BlockSpec((tq, KV_REP * HEAD_DIM),
                               lambda b, g, i: (b * n_q_tiles + i, g)),
        out_shape=jax.ShapeDtypeStruct((n_batch * n_q_tiles * tq, ATTN_WIDTH), BF16),
        scratch_shapes=[pltpu.VMEM((n_keys, KV_WIDTH), BF16),
                        pltpu.VMEM((HEAD_DIM + ONES_ROWS, n_keys), BF16),
                        pltpu.VMEM((KV_REP * HEAD_DIM, tq), F32)],
        compiler_params=_cparams(3),
        name=name,
    )(q_t, *([k_t] * nkb), *([v_t] * nkb))


def _fourier_kernel(c_ref, s_ref, x_ref, cc_ref, sc_ref, w_ref, o_ref, *, norm):
    x = x_ref[...]
    a = jnp.dot(c_ref[...], x, preferred_element_type=F32).astype(BF16)
    b = jnp.dot(s_ref[...], x, preferred_element_type=F32).astype(BF16)
    cc = cc_ref[...]
    sc = sc_ref[...]
    groups = [slice(grp * FOURIER_GROUP, (grp + 1) * FOURIER_GROUP) for grp in range(N_FOURIER_GROUPS)]
    specs = [(jnp.dot(a[:, sl], cc, preferred_element_type=F32)
              - jnp.dot(b[:, sl], sc, preferred_element_type=F32)) * norm for sl in groups]
    for grp, sl in enumerate(groups):
        o_ref[:, sl] = jnp.dot(specs[grp].astype(BF16), w_ref[grp], preferred_element_type=F32).astype(BF16)


def _fourier(f, cmat, smat, cc, sc, w_four, *, n, tr, x_block0, n_batch, name):
    n_row_tiles = n // tr
    return pl.pallas_call(
        functools.partial(_fourier_kernel, norm=float(1.0 / np.sqrt(n * FOURIER_GROUP))),
        grid=(n_row_tiles, n_batch),
        in_specs=[
            pl.BlockSpec((tr, n), lambda i, b: (i, 0)),
            pl.BlockSpec((tr, n), lambda i, b: (i, 0)),
            pl.BlockSpec((n, FOURIER_WIDTH), lambda i, b: (x_block0 + b, 0)),
            pl.BlockSpec(cc.shape, lambda i, b: (0, 0)),
            pl.BlockSpec(sc.shape, lambda i, b: (0, 0)),
            pl.BlockSpec(w_four.shape, lambda i, b: (0, 0, 0)),
        ],
        out_specs=pl.BlockSpec((tr, FOURIER_WIDTH), lambda i, b: (b * n_row_tiles + i, 0)),
        out_shape=jax.ShapeDtypeStruct((n_batch * n, FOURIER_WIDTH), BF16),
        compiler_params=_cparams(2),
        name=name,
    )(cmat, smat, f, cc, sc, w_four)


def _dft_half_tiles(n, tr):
    n_tiles = n // 2 // tr
    k = (np.arange(n_tiles, dtype=np.int64)[:, None] * tr
         + np.arange(tr + SYM_EXTRA, dtype=np.int64)[None, :])
    pos = np.arange(n, dtype=np.int64)
    ang = 2.0 * np.pi * ((k[:, :, None] * pos[None, None, :]) % n).astype(np.float64) / n
    rev = np.eye(tr, dtype=np.float32)[::-1]
    return jnp.asarray(np.cos(ang), BF16), jnp.asarray(np.sin(ang), BF16), jnp.asarray(rev, BF16)


def _fourier_sym_kernel(c_ref, s_ref, x_ref, cc_ref, sc_ref, w_ref, rev_ref, o_ref, *, norm, tr, n_tiles):
    i = pl.program_id(1)
    x = x_ref[...]
    a = jnp.dot(c_ref[0], x, preferred_element_type=F32).astype(BF16)
    b = jnp.dot(s_ref[0], x, preferred_element_type=F32).astype(BF16)
    cc = cc_ref[...]
    sc = sc_ref[...]
    groups = [slice(grp * FOURIER_GROUP, (grp + 1) * FOURIER_GROUP) for grp in range(N_FOURIER_GROUPS)]
    ps = [jnp.dot(a[:, sl], cc, preferred_element_type=F32) for sl in groups]
    qs = [jnp.dot(b[:, sl], sc, preferred_element_type=F32) for sl in groups]
    upper = jnp.concatenate([((p + q) * norm)[1:tr + 1] for p, q in zip(ps, qs)], axis=1).astype(BF16)
    upper = jnp.dot(rev_ref[...], upper, preferred_element_type=F32).astype(BF16)
    lo_rows = pl.ds(pl.multiple_of(i * tr, tr), tr)
    hi_rows = pl.ds(pl.multiple_of((2 * n_tiles - 1 - i) * tr, tr), tr)
    for grp, sl in enumerate(groups):
        lower = ((ps[grp] - qs[grp]) * norm)[:tr].astype(BF16)
        o_ref[lo_rows, sl] = jnp.dot(lower, w_ref[grp], preferred_element_type=F32).astype(BF16)
    for grp, sl in enumerate(groups):
        o_ref[hi_rows, sl] = jnp.dot(upper[:, sl], w_ref[grp], preferred_element_type=F32).astype(BF16)


def _fourier_sym(f, c_tiles, s_tiles, rev, cc, sc, w_four, *, n, tr, x_block0, n_batch, name):
    n_tiles = n // 2 // tr
    ext = tr + SYM_EXTRA
    return pl.pallas_call(
        functools.partial(_fourier_sym_kernel, norm=float(1.0 / np.sqrt(n * FOURIER_GROUP)), tr=tr,
                          n_tiles=n_tiles),
        grid=(n_batch, n_tiles),
        in_specs=[
            pl.BlockSpec((1, ext, n), lambda b, i: (i, 0, 0)),
            pl.BlockSpec((1, ext, n), lambda b, i: (i, 0, 0)),
            pl.BlockSpec((n, FOURIER_WIDTH), lambda b, i: (x_block0 + b, 0)),
            pl.BlockSpec(cc.shape, lambda b, i: (0, 0)),
            pl.BlockSpec(sc.shape, lambda b, i: (0, 0)),
            pl.BlockSpec(w_four.shape, lambda b, i: (0, 0, 0)),
            pl.BlockSpec(rev.shape, lambda b, i: (0, 0)),
        ],
        out_specs=pl.BlockSpec((n, FOURIER_WIDTH), lambda b, i: (b, 0)),
        out_shape=jax.ShapeDtypeStruct((n_batch * n, FOURIER_WIDTH), BF16),
        compiler_params=_cparams(2),
        name=name,
    )(c_tiles, s_tiles, f, cc, sc, w_four, rev)


def _postmix_kernel(*refs, n_x, n_a, n_f, tile0, n_ctx_tiles, tiles_per_batch, with_router):
    x_refs, refs = refs[:n_x], refs[n_x:]
    a_refs, refs = refs[:n_a], refs[n_a:]
    f_refs, refs = refs[:n_f], refs[n_f:]
    mod_ref, g_ref, woa_ref, wof_ref = refs[:4]
    if with_router:
        wr_ref, br_ref, x1_ref, h_ref, rt_ref, rtt_ref, cnt_ref = refs[4:]
    else:
        x1_ref, h_ref = refs[4:]
    i = pl.program_id(0) + tile0
    row = _mod_row(i, n_ctx_tiles, tiles_per_batch)
    is_ctx = i < n_ctx_tiles
    a = _load_tokens(a_refs, is_ctx)
    f = _load_tokens(f_refs, is_ctx)
    mix = (jnp.dot(a, woa_ref[...], preferred_element_type=F32)
           + jnp.dot(f, wof_ref[...], preferred_element_type=F32))
    x1 = _load_tokens(x_refs, is_ctx) + _mod_vec(mod_ref, row, 2) * mix
    x1_ref[...] = x1
    h = _norm_modulate(x1, g_ref[...], _mod_vec(mod_ref, row, 3), _mod_vec(mod_ref, row, 4))
    hb = h.astype(BF16)
    if with_router:
        h_ref[...] = _pack_bf16_pairs(hb)
        logits_t = lax.dot_general(wr_ref[...], hb, (((1,), (1,)), ((), ())),
                                   preferred_element_type=F32) + br_ref[...]
        _top2_route(logits_t, rt_ref, rtt_ref, cnt_ref, first=pl.program_id(0) == 0)
    else:
        h_ref[...] = hb


def _top2_route(logits_t, rt_ref, rtt_ref, cnt_ref, first):
    ne, tm = logits_t.shape
    row = lax.broadcasted_iota(jnp.int32, (ne, tm), 0)
    neg = jnp.float32(-jnp.inf)
    m1 = jnp.max(logits_t, axis=0, keepdims=True)
    i1 = jnp.min(jnp.where(logits_t == m1, row, ne), axis=0, keepdims=True)
    rest = jnp.where(row == i1, neg, logits_t)
    m2 = jnp.max(rest, axis=0, keepdims=True)
    i2 = jnp.min(jnp.where(rest == m2, row, ne), axis=0, keepdims=True)
    e = jnp.exp(m2 - m1)
    g1 = 1.0 / (1.0 + e)
    g2 = e / (1.0 + e)
    pick1 = row == i1
    pick2 = row == i2
    picked = jnp.logical_or(pick1, pick2).astype(F32)

    @pl.when(first)
    def _():
        cnt_ref[...] = jnp.zeros_like(cnt_ref)

    blk = LOGIT_LANES
    r_i = lax.broadcasted_iota(jnp.int32, (blk, blk), 0)
    c_i = lax.broadcasted_iota(jnp.int32, (blk, blk), 1)
    upper = jnp.where(r_i < c_i, 1.0, 0.0).astype(BF16)
    run = cnt_ref[:, 0:1]
    parts = []
    for j in range(tm // blk):
        pj = picked[:, j * blk:(j + 1) * blk]
        parts.append(jnp.dot(pj.astype(BF16), upper, preferred_element_type=F32) + run)
        run = run + jnp.sum(pj, axis=1, keepdims=True)
    before = jnp.concatenate(parts, axis=1)
    cnt_ref[...] = jnp.broadcast_to(run, cnt_ref.shape)
    r1 = jnp.sum(jnp.where(pick1, before, 0.0), axis=0, keepdims=True)
    r2 = jnp.sum(jnp.where(pick2, before, 0.0), axis=0, keepdims=True)
    rec = jnp.concatenate([i1.astype(F32), i2.astype(F32), g1, g2, r1, r2, jnp.zeros((2, tm), F32)], axis=0)
    rtt_ref[...] = rec
    rt_ref[...] = jnp.concatenate([rec, jnp.zeros((LOGIT_LANES - 8, tm), F32)], axis=0).T


def _postmix(x, attn, four, mod_l, g_ffn, wo_a, wo_f, router, *, n_ctx, seq, t, tile0):
    d = D_MODEL
    tm = TM_MIX
    nct = n_ctx // tm
    t = t - tile0 * tm
    full = lambda shape: pl.BlockSpec(shape, lambda i: (0,) * len(shape))
    tile = lambda w: pl.BlockSpec((tm, w), lambda i: (i, 0))
    x_specs, x_arrays = _token_specs(x, tm, nct, tile0)
    a_specs, a_arrays = _token_specs(attn, tm, nct, tile0)
    f_specs, f_arrays = _token_specs(four, tm, nct, tile0)
    in_specs = x_specs + a_specs + f_specs + [full(mod_l.shape), full((1, d)), full(wo_a.shape),
                                              full(wo_f.shape)]
    if router is None:
        out_specs = [tile(d), tile(d)]
        out_shape = [jax.ShapeDtypeStruct((t, d), F32), jax.ShapeDtypeStruct((t, d), BF16)]
    else:
        out_specs = [tile(d), tile(d // 2)]
        out_shape = [jax.ShapeDtypeStruct((t, d), F32), jax.ShapeDtypeStruct((t, d // 2), jnp.uint32)]
    args = x_arrays + a_arrays + f_arrays + [mod_l, g_ffn, wo_a, wo_f]
    if router is not None:
        in_specs += [full(router[0].shape), full(router[1].shape)]
        out_specs += [tile(LOGIT_LANES), pl.BlockSpec((8, tm), lambda i: (0, i)),
                      pl.BlockSpec((EXPERT_ROWS, LOGIT_LANES), lambda i: (0, 0))]
        out_shape += [jax.ShapeDtypeStruct((t, LOGIT_LANES), F32), jax.ShapeDtypeStruct((8, t), F32),
                      jax.ShapeDtypeStruct((EXPERT_ROWS, LOGIT_LANES), F32)]
        args += list(router)
    return pl.pallas_call(
        functools.partial(_postmix_kernel, n_x=len(x_arrays), n_a=len(a_arrays), n_f=len(f_arrays),
                          tile0=tile0, n_ctx_tiles=nct, tiles_per_batch=seq // tm,
                          with_router=router is not None),
        grid=(t // tm,),
        in_specs=in_specs,
        out_specs=out_specs,
        out_shape=out_shape,
        compiler_params=_cparams(1),
        name="postmix_wo_norm",
    )(*args)


def _silu_mul(gate, up):
    return (gate / (1.0 + jnp.exp(-gate))) * up


def _swiglu_chunks(xs, n_chunks, wg_of, wu_of, wd_of, acc_refs):
    if not isinstance(xs, (list, tuple)):
        xs, acc_refs = [xs], [acc_refs]
    items = [(c, r) for c in range(n_chunks) for r in range(len(xs))]
    fetched = {}

    def weight(getter, c):
        if (getter, c) not in fetched:
            fetched[(getter, c)] = getter(c)
        return fetched[(getter, c)]

    def gate_up(c, r):
        return (jnp.dot(xs[r], weight(wg_of, c), preferred_element_type=F32),
                jnp.dot(xs[r], weight(wu_of, c), preferred_element_type=F32))

    gate, up = gate_up(*items[0])
    for idx, (c, r) in enumerate(items):
        act = _silu_mul(gate, up).astype(BF16)
        if idx + 1 < len(items):
            gate, up = gate_up(*items[idx + 1])
        acc_refs[r][...] += jnp.dot(act, weight(wd_of, c), preferred_element_type=F32)


def _dense_ffn_kernel(h_ref, x1_ref, mod_ref, wg_ref, wu_ref, wd_ref, o_ref, acc_ref, *,
                      tile0, n_ctx_tiles, tiles_per_batch):
    i = pl.program_id(0) + tile0
    row = _mod_row(i, n_ctx_tiles, tiles_per_batch)
    acc_ref[...] = jnp.zeros_like(acc_ref)
    sub = lambda c: slice(c * FF_CHUNK, (c + 1) * FF_CHUNK)
    _swiglu_chunks(h_ref[...], wg_ref.shape[1] // FF_CHUNK, lambda c: wg_ref[:, sub(c)],
                   lambda c: wu_ref[:, sub(c)], lambda c: wd_ref[sub(c), :], acc_ref)
    o_ref[...] = x1_ref[...] + _mod_vec(mod_ref, row, 5) * acc_ref[...]


def _dense_ffn(h, x1, mod_l, wg, wu, wd, *, n_ctx, seq, tile0):
    t, d = x1.shape
    tm = TM_DENSE
    full = lambda shape: pl.BlockSpec(shape, lambda i: (0,) * len(shape))
    tile = lambda: pl.BlockSpec((tm, d), lambda i: (i, 0))
    return pl.pallas_call(
        functools.partial(_dense_ffn_kernel, tile0=tile0 * TM_MIX // tm, n_ctx_tiles=n_ctx // tm,
                          tiles_per_batch=seq // tm),
        grid=(t // tm,),
        in_specs=[tile(), tile(), full(mod_l.shape), full(wg.shape), full(wu.shape), full(wd.shape)],
        out_specs=tile(),
        out_shape=jax.ShapeDtypeStruct((t, d), F32),
        scratch_shapes=[pltpu.VMEM((tm, d), F32)],
        compiler_params=_cparams(1),
        name="dense_swiglu",
    )(h, x1, mod_l, wg, wu, wd)


def _moe_ffn_kernel(be_ref, nv_ref, x_ref, wg_ref, wu_ref, wd_ref, o_ref, xb_ref, acc_ref, wgb_ref, wub_ref,
                    wdb_ref):
    i = pl.program_id(0)
    j = pl.program_id(1)
    n_valid = nv_ref[i]

    @pl.when(j == 0)
    def _():
        x = _unpack_bf16_pairs(x_ref[...])
        rows = lax.broadcasted_iota(jnp.int32, x.shape, 0)
        xb_ref[...] = jnp.where(rows < n_valid, x, jnp.zeros_like(x))
        acc_ref[...] = jnp.zeros_like(acc_ref)

    n_sub = TF_MOE // FF_CHUNK
    sub = lambda c: slice(c * FF_CHUNK, (c + 1) * FF_CHUNK)
    rows_needed = ((n_valid + TAIL_ROWS - 1) // TAIL_ROWS) * TAIL_ROWS
    n_full = rows_needed // TM_PASS
    n_tail = (rows_needed - n_full * TM_PASS) // TAIL_ROWS

    def cast_chunk(src_ref, dst_ref, idx):
        w = src_ref[(0, 0) + idx].astype(BF16)
        dst_ref[idx] = w
        return w

    cast_wg = lambda c: cast_chunk(wg_ref, wgb_ref, (slice(None), sub(c)))
    cast_wu = lambda c: cast_chunk(wu_ref, wub_ref, (slice(None), sub(c)))
    cast_wd = lambda c: cast_chunk(wd_ref, wdb_ref, (sub(c), slice(None)))
    read_wg = lambda c: wgb_ref[:, sub(c)]
    read_wu = lambda c: wub_ref[:, sub(c)]
    read_wd = lambda c: wdb_ref[sub(c), :]

    @pl.when(jnp.logical_and(n_full == 0, n_tail > 0))
    def _():
        for c in range(n_sub):
            cast_wg(c), cast_wu(c), cast_wd(c)

    for n_groups in range(1, TM_MOE // TM_PASS + 1):
        groups = [slice(r * TM_PASS, (r + 1) * TM_PASS) for r in range(n_groups)]

        @pl.when(n_full == n_groups)
        def _():
            _swiglu_chunks([xb_ref[rows, :] for rows in groups], n_sub, cast_wg, cast_wu, cast_wd,
                           [acc_ref.at[rows, :] for rows in groups])

    @pl.when(n_tail > 0)
    def _():
        def group(gi, carry):
            rows = pl.ds(pl.multiple_of(n_full * TM_PASS + gi * TAIL_ROWS, TAIL_ROWS), TAIL_ROWS)
            _swiglu_chunks(xb_ref[rows, :], n_sub, read_wg, read_wu, read_wd, acc_ref.at[rows, :])
            return carry

        lax.fori_loop(0, n_tail, group, 0)

    @pl.when(j == pl.num_programs(1) - 1)
    def _():
        o_ref[...] = _pack_bf16_pairs(acc_ref[...].astype(BF16))


def _moe_ffn(xs, block_expert, n_valid, wg, wu, wd, layer):
    n_rows = xs.shape[0]
    d = D_MODEL
    tm, tf = TM_MOE, TF_MOE
    n_blocks = n_rows // tm
    n_ff = wg.shape[3] // tf

    def ff_idx(j, nv, i):
        return jnp.where(nv[i] > 0, j, n_ff - 1)

    grid_spec = pltpu.PrefetchScalarGridSpec(
        num_scalar_prefetch=2,
        grid=(n_blocks, n_ff),
        in_specs=[
            pl.BlockSpec((tm, d // 2), lambda i, j, be, nv: (i, 0)),
            pl.BlockSpec((1, 1, d, tf), lambda i, j, be, nv: (layer, be[i], 0, ff_idx(j, nv, i))),
            pl.BlockSpec((1, 1, d, tf), lambda i, j, be, nv: (layer, be[i], 0, ff_idx(j, nv, i))),
            pl.BlockSpec((1, 1, tf, d), lambda i, j, be, nv: (layer, be[i], ff_idx(j, nv, i), 0)),
        ],
        out_specs=pl.BlockSpec((tm, d // 2), lambda i, j, be, nv: (i, 0)),
        scratch_shapes=[pltpu.VMEM((tm, d), BF16), pltpu.VMEM((tm, d), F32), pltpu.VMEM((d, tf), BF16),
                        pltpu.VMEM((d, tf), BF16), pltpu.VMEM((tf, d), BF16)],
    )
    return pl.pallas_call(
        _moe_ffn_kernel,
        grid_spec=grid_spec,
        out_shape=jax.ShapeDtypeStruct((n_rows, d // 2), jnp.uint32),
        compiler_params=_cparams(2),
        name="expert_swiglu",
    )(block_expert, n_valid, xs, wg, wu, wd)


def _sc_mesh():
    return plsc.VectorSubcoreMesh(core_axis_name="c", subcore_axis_name="s")


def _sc_params():
    return pltpu.CompilerParams(use_tc_tiling_on_sc=True)


def _sc_dispatch(h_packed, dest, n_rows):
    t, w = h_packed.shape
    win = SC_DISPATCH_ROWS
    n_win = t // win
    idx = [dest[k].reshape(n_win, 1, win) for k in range(TOP_K)]

    @functools.partial(
        pl.kernel, out_type=jax.ShapeDtypeStruct((n_rows, w), h_packed.dtype), mesh=_sc_mesh(),
        scratch_types=[], compiler_params=_sc_params(), name="expert_dispatch_scatter")
    def run(x_hbm, i0_hbm, i1_hbm, o_hbm):
        def body(x_vmem, i0_vmem, i1_vmem):
            pltpu.sync_copy(x_vmem, o_hbm.at[i0_vmem.at[0, 0]])
            pltpu.sync_copy(x_vmem, o_hbm.at[i1_vmem.at[0, 0]])

        idx_spec = pl.BlockSpec((1, 1, win), lambda i: (i, 0, 0))
        pltpu.emit_pipeline(
            body, grid=(n_win,),
            in_specs=[pl.BlockSpec((win, w), lambda i: (i, 0)), idx_spec, idx_spec],
            out_specs=[], core_axis_name=("c", "s"), dimension_semantics=(pltpu.PARALLEL,),
        )(x_hbm, i0_hbm, i1_hbm)

    return run(h_packed, *idx)


def _sc_gather(ys, idx_flat):
    w = ys.shape[1]
    n = idx_flat.shape[0]
    win = SC_GATHER_ROWS
    n_win = n // win

    @functools.partial(
        pl.kernel, out_type=jax.ShapeDtypeStruct((n, w), ys.dtype), mesh=_sc_mesh(),
        scratch_types=[], compiler_params=_sc_params(), name="expert_combine_gather")
    def run(y_hbm, i_hbm, o_hbm):
        def body(i_vmem, o_vmem):
            pltpu.sync_copy(y_hbm.at[i_vmem.at[0, 0]], o_vmem)

        pltpu.emit_pipeline(
            body, grid=(n_win,),
            in_specs=[pl.BlockSpec((1, 1, win), lambda i: (i, 0, 0))],
            out_specs=[pl.BlockSpec((win, w), lambda i: (i, 0))],
            core_axis_name=("c", "s"), dimension_semantics=(pltpu.PARALLEL,),
        )(i_hbm, o_hbm)

    return run(ys, idx_flat.reshape(n_win, 1, win))


def _moe_combine_kernel(x1_ref, y0_ref, y1_ref, g_ref, mod_ref, o_ref, *, tile0, n_ctx_tiles,
                        tiles_per_batch):
    i = pl.program_id(0) + tile0
    row = _mod_row(i, n_ctx_tiles, tiles_per_batch)
    gates = g_ref[...]
    y0 = _unpack_bf16_pairs(y0_ref[...]).astype(F32)
    y1 = _unpack_bf16_pairs(y1_ref[...]).astype(F32)
    y = y0 * gates[:, 2:3] + y1 * gates[:, 3:4]
    o_ref[...] = x1_ref[...] + _mod_vec(mod_ref, row, 5) * y


def _moe_combine(x1, y_sel, gates, mod_l, *, n_ctx, seq, tile0):
    t, d = x1.shape
    tm = TM_MIX
    n_tiles = t // tm
    tile = lambda w: pl.BlockSpec((tm, w), lambda i: (i, 0))
    return pl.pallas_call(
        functools.partial(_moe_combine_kernel, tile0=tile0, n_ctx_tiles=n_ctx // tm,
                          tiles_per_batch=seq // tm),
        grid=(n_tiles,),
        in_specs=[tile(d), tile(d // 2), pl.BlockSpec((tm, d // 2), lambda i: (n_tiles + i, 0)),
                  tile(LOGIT_LANES),
                  pl.BlockSpec(mod_l.shape, lambda i: (0, 0))],
        out_specs=tile(d),
        out_shape=jax.ShapeDtypeStruct((t, d), F32),
        compiler_params=_cparams(1),
        name="expert_combine",
    )(x1, y_sel, y_sel, gates, mod_l)


def _route(route_t, counts):
    t = route_t.shape[1]
    expert = route_t[0:TOP_K].astype(jnp.int32)
    rank = route_t[4:4 + TOP_K].astype(jnp.int32)
    counts = counts[:N_EXPERTS, 0].astype(jnp.int32)
    n_assign = t * TOP_K
    padded = ((counts + TM_MOE - 1) // TM_MOE) * TM_MOE
    pend = jnp.cumsum(padded)
    pstart = pend - padded
    dest = rank
    for e in range(N_EXPERTS):
        dest = dest + jnp.where(expert == e, pstart[e], 0)
    n_blocks = -(-n_assign // TM_MOE) + N_EXPERTS
    block_start = jnp.arange(n_blocks, dtype=jnp.int32) * TM_MOE
    block_expert = jnp.minimum(jnp.sum(pend[None, :] <= block_start[:, None], axis=1),
                               N_EXPERTS - 1).astype(jnp.int32)
    n_valid = jnp.clip(counts[block_expert] - (block_start - pstart[block_expert]), 0, TM_MOE)
    n_valid = jnp.where(block_start < pend[-1], n_valid, 0).astype(jnp.int32)
    return dest.astype(jnp.int32), block_expert, n_valid, n_blocks * TM_MOE


def kernel(x, c, ctx, c_ctx, w_mod, b_mod, g_mix, g_ffn, g_q, g_k, w_in, w_four, w_o,
           w_gate_dense, w_up_dense, w_down_dense, w_router, b_router,
           w_gate_moe, w_up_moe, w_down_moe):
    b, s, d = x.shape
    n_ctx_len = ctx.shape[1]
    n_ctx = b * n_ctx_len
    t = n_ctx + b * s
    assert d == D_MODEL and b + 1 <= MOD_ROWS
    assert n_ctx % TM_DENSE == 0 and s % TM_DENSE == 0 and n_ctx_len % 128 == 0 and n_ctx % s == 0

    cvec = jnp.zeros((MOD_ROWS, d), F32).at[0].set(c_ctx).at[1:b + 1].set(c)
    mod = _modulation(cvec, w_mod, b_mod)

    cos_t, sin_t = _rope_tables_t(s, TM_MIX)
    c_lat, s_lat, rev_lat = _dft_half_tiles(s, TR_FOUR)
    c_ctx_m, s_ctx_m = _dft_mats(n_ctx_len)
    c_grp, s_grp = _dft_mats(FOURIER_GROUP)

    xa = (ctx.reshape(n_ctx, d), x.reshape(b * s, d))

    for l in range(DEPTH):
        mod_l = mod[l]
        w_in_l = w_in[l]
        wt = w_in_l[:, :QKV_WIDTH].T.astype(BF16)
        wf = w_in_l[:, QKV_WIDTH:].astype(BF16)
        gq = jnp.broadcast_to((g_q[l] * (LOG2_E * HEAD_DIM ** -0.5))[:, None], (HEAD_DIM, TM_MIX))
        gk = jnp.broadcast_to(g_k[l][:, None], (HEAD_DIM, TM_MIX))
        last = l == DEPTH - 1
        tile0 = n_ctx // TM_MIX if last else 0
        q_t, k_t, v_t, f = _premix(xa, mod_l, g_mix[l].reshape(1, d), wt, wf, gq, gk, cos_t, sin_t,
                                   n_ctx=n_ctx, seq=s, t=t)

        n_qt = s // TQ_ATTN
        lat_keys = [(n_ctx_len, lambda bb: bb), (s, lambda bb: n_ctx // s + bb)]
        attn_lat = _attention(q_t, k_t, v_t, q_tile0=n_ctx // TQ_ATTN, n_q_tiles=n_qt,
                              key_blocks=lat_keys, tq=TQ_ATTN, n_batch=b, name="attention_latent")
        attn_ctx = None if last else _attention(
            q_t, k_t, v_t, q_tile0=0, n_q_tiles=1, key_blocks=[(n_ctx_len, lambda bb: bb)],
            tq=n_ctx_len, n_batch=b, name="attention_context")

        wfour = w_four[l].astype(BF16)
        four_lat = _fourier_sym(f, c_lat, s_lat, rev_lat, c_grp, s_grp, wfour, n=s, tr=TR_FOUR,
                                x_block0=n_ctx // s, n_batch=b, name="fourier_latent")
        four_ctx = None if last else _fourier(
            f, c_ctx_m, s_ctx_m, c_grp, s_grp, wfour, n=n_ctx_len, tr=n_ctx_len, x_block0=0, n_batch=b,
            name="fourier_context")

        wo = w_o[l].astype(BF16)
        is_moe = l % 2 == 1
        li = l // 2
        router = None
        if is_moe:
            wr = jnp.zeros((EXPERT_ROWS, d), BF16).at[:N_EXPERTS].set(w_router[li].T.astype(BF16))
            br = jnp.full((EXPERT_ROWS,), -jnp.inf, F32).at[:N_EXPERTS].set(b_router[li])
            br = jnp.broadcast_to(br[:, None], (EXPERT_ROWS, TM_MIX))
            router = (wr, br)
        res = _postmix(xa, (attn_ctx, attn_lat), (four_ctx, four_lat), mod_l, g_ffn[l].reshape(1, d),
                       wo[:ATTN_WIDTH], wo[ATTN_WIDTH:], router, n_ctx=n_ctx, seq=s, t=t, tile0=tile0)
        t_l = t - tile0 * TM_MIX
        if not is_moe:
            x1, h2 = res
            xa = _dense_ffn(h2, x1, mod_l, w_gate_dense[li].astype(BF16), w_up_dense[li].astype(BF16),
                            w_down_dense[li].astype(BF16), n_ctx=n_ctx, seq=s, tile0=tile0)
        else:
            x1, h2, route, route_t, counts = res
            dest, block_expert, n_valid, n_rows = _route(route_t, counts)
            xs = _sc_dispatch(h2, dest, n_rows)
            ys = _moe_ffn(xs, block_expert, n_valid, w_gate_moe, w_up_moe, w_down_moe, li)
            y_sel = _sc_gather(ys, dest.reshape(-1))
            xa = _moe_combine(x1, y_sel, route, mod_l, n_ctx=n_ctx, seq=s, tile0=tile0)

    return xa.reshape(b, s, d)
```

```python
import functools

import numpy as np
import jax
import jax.numpy as jnp
from jax import lax
from jax.experimental import pallas as pl
from jax.experimental.pallas import tpu as pltpu
from jax.experimental.pallas import tpu_sc as plsc

D_MODEL = 1024
DEPTH = 4
GRID_W = 64
HEAD_DIM = 64
ATTN_WIDTH = 512
N_Q_HEADS = 8
N_KV_HEADS = 2
KV_REP = 4
KV_WIDTH = 128
FOURIER_WIDTH = 512
N_FOURIER_GROUPS = 4
FOURIER_GROUP = 128
ROT_PER_AXIS = 32
ROPE_THETA = 10000.0
N_EXPERTS = 8
TOP_K = 2
EPS = 1e-6

QKV_WIDTH = ATTN_WIDTH + 2 * KV_WIDTH
MOD_ROWS = 16
LOGIT_LANES = 128
EXPERT_ROWS = 16

TM_MIX = 1024
TQ_ATTN = 512
KEY_CHUNK = 256
ONES_ROWS = 16
SCORE_LOOKAHEAD = 2
LOG2_E = 1.4426950408889634
TR_FOUR = 512
SYM_EXTRA = 16
TM_DENSE = 1024
FF_CHUNK = 256
TM_MOE = 2048
TM_PASS = 1024
TAIL_ROWS = 256
TF_MOE = 512
SC_DISPATCH_ROWS = 64
SC_GATHER_ROWS = 64
V7X_VMEM_BYTES = 64 * 1024 * 1024
VMEM_LIMIT = V7X_VMEM_BYTES - 8 * 1024 * 1024

F32 = jnp.float32
BF16 = jnp.bfloat16


def _cparams(n_axes):
    return pltpu.CompilerParams(
        dimension_semantics=("arbitrary",) * n_axes, vmem_limit_bytes=VMEM_LIMIT)


def _rope_tables_t(s, tm):
    n_rows = s // GRID_W
    rows = np.repeat(np.arange(n_rows), GRID_W).astype(np.float64)
    cols = np.tile(np.arange(GRID_W), n_rows).astype(np.float64)
    inv_freq = (ROPE_THETA ** (-np.arange(0, ROT_PER_AXIS, 2, dtype=np.float32) / ROT_PER_AXIS)
                ).astype(np.float32).astype(np.float64)
    ang_r = (rows[None, :].astype(np.float32) * inv_freq[:, None].astype(np.float32)).astype(np.float64)
    ang_c = (cols[None, :].astype(np.float32) * inv_freq[:, None].astype(np.float32)).astype(np.float64)
    cos = np.concatenate([np.cos(ang_r), np.cos(ang_r), np.cos(ang_c), np.cos(ang_c)], axis=0)
    sin = np.concatenate([-np.sin(ang_r), np.sin(ang_r), -np.sin(ang_c), np.sin(ang_c)], axis=0)
    cos = np.concatenate([np.ones((HEAD_DIM, tm)), cos], axis=1)
    sin = np.concatenate([np.zeros((HEAD_DIM, tm)), sin], axis=1)
    return jnp.asarray(cos, F32), jnp.asarray(sin, F32)


def _dft_mats(n):
    k = np.arange(n, dtype=np.int64)
    ang = 2.0 * np.pi * ((k[:, None] * k[None, :]) % n).astype(np.float64) / n
    return jnp.asarray(np.cos(ang), BF16), jnp.asarray(np.sin(ang), BF16)


def _mod_kernel(c_ref, w_ref, b_ref, o_ref):
    c = c_ref[...]
    s = (c / (1.0 + jnp.exp(-c))).astype(BF16)
    w = w_ref[0].astype(BF16)
    o_ref[0] = jnp.dot(s, w, preferred_element_type=F32) + b_ref[0]


def _modulation(cvec, w_mod, b_mod):
    depth, d, n = w_mod.shape
    tn = 1536
    return pl.pallas_call(
        _mod_kernel,
        grid=(depth, n // tn),
        in_specs=[
            pl.BlockSpec((MOD_ROWS, d), lambda l, j: (0, 0)),
            pl.BlockSpec((1, d, tn), lambda l, j: (l, 0, j)),
            pl.BlockSpec((1, 1, tn), lambda l, j: (l, 0, j)),
        ],
        out_specs=pl.BlockSpec((1, MOD_ROWS, tn), lambda l, j: (l, 0, j)),
        out_shape=jax.ShapeDtypeStruct((depth, MOD_ROWS, n), F32),
        compiler_params=_cparams(2),
        name="adaln_vectors",
    )(cvec, w_mod, b_mod.reshape(depth, 1, n))


def _mod_row(i, n_ctx_tiles, tiles_per_batch):
    lat = jnp.maximum(i - n_ctx_tiles, 0)
    return jnp.where(i < n_ctx_tiles, 0, lat // tiles_per_batch + 1)


def _mod_vec(mod_ref, row, comp):
    return mod_ref[pl.ds(row, 1), comp * D_MODEL:(comp + 1) * D_MODEL]


def _pack_bf16_pairs(xb):
    n = xb.shape[1] // 2
    bits = lax.bitcast_convert_type(xb.astype(F32), jnp.uint32)
    return (bits[:, :n] >> 16) | (bits[:, n:] & jnp.uint32(0xFFFF0000))


def _unpack_bf16_pairs(w):
    lo = lax.bitcast_convert_type(w << 16, F32)
    hi = lax.bitcast_convert_type(w & jnp.uint32(0xFFFF0000), F32)
    return jnp.concatenate([lo, hi], axis=1).astype(BF16)


def _token_specs(x, tm, n_ctx_tiles, tile0):
    if not isinstance(x, tuple):
        return [pl.BlockSpec((tm, x.shape[1]), lambda i: (i + tile0, 0))], [x]
    ctx, lat = x
    specs, arrays = [], []
    if ctx is not None:
        specs.append(pl.BlockSpec((tm, ctx.shape[1]), lambda i: (jnp.minimum(i + tile0, n_ctx_tiles - 1), 0)))
        arrays.append(ctx)
    else:
        assert tile0 >= n_ctx_tiles
    specs.append(pl.BlockSpec((tm, lat.shape[1]), lambda i: (jnp.maximum(i + tile0 - n_ctx_tiles, 0), 0)))
    arrays.append(lat)
    return specs, arrays


def _load_tokens(refs, is_ctx):
    if len(refs) == 2:
        return jnp.where(is_ctx, refs[0][...], refs[1][...])
    return refs[0][...]


def _norm_modulate(x, g, shift, scale):
    ms = jnp.mean(x * x, axis=-1, keepdims=True)
    y = x * lax.rsqrt(ms + EPS) * g
    return y * (1.0 + scale) + shift


def _premix_kernel(*refs, n_x, n_ctx_tiles, tiles_per_batch):
    x_refs = refs[:n_x]
    (mod_ref, g_ref, wt_ref, wf_ref, gq_ref, gk_ref, cos_ref, sin_ref,
     q_ref, k_ref, v_ref, f_ref) = refs[n_x:]
    i = pl.program_id(0)
    row = _mod_row(i, n_ctx_tiles, tiles_per_batch)
    x = _load_tokens(x_refs, i < n_ctx_tiles)
    h = _norm_modulate(x, g_ref[...], _mod_vec(mod_ref, row, 0), _mod_vec(mod_ref, row, 1))
    hb = h.astype(BF16)
    f_ref[...] = jnp.dot(hb, wf_ref[...], preferred_element_type=F32).astype(BF16)
    pt = lax.dot_general(wt_ref[...], hb, (((1,), (1,)), ((), ())), preferred_element_type=F32)
    v_ref[...] = pt[ATTN_WIDTH + KV_WIDTH:, :].astype(BF16)
    cos = cos_ref[...]
    sin = sin_ref[...]

    def norm_rope(xh, gain):
        ms = jnp.mean(xh * xh, axis=0, keepdims=True)
        y = xh * lax.rsqrt(ms + EPS) * gain
        half = ROT_PER_AXIS // 2
        swapped = jnp.concatenate(
            [y[half:2 * half], y[0:half], y[3 * half:4 * half], y[2 * half:3 * half]], axis=0)
        return y * cos + swapped * sin

    gq = gq_ref[...]
    gk = gk_ref[...]
    for hh in range(N_Q_HEADS):
        q_ref[hh * HEAD_DIM:(hh + 1) * HEAD_DIM, :] = norm_rope(
            pt[hh * HEAD_DIM:(hh + 1) * HEAD_DIM, :], gq).astype(BF16)
    for hh in range(N_KV_HEADS):
        lo = ATTN_WIDTH + hh * HEAD_DIM
        k_ref[hh * HEAD_DIM:(hh + 1) * HEAD_DIM, :] = norm_rope(pt[lo:lo + HEAD_DIM, :], gk).astype(BF16)


def _premix(x, mod_l, g_mix, wt, wf, gq, gk, cos_t, sin_t, *, n_ctx, seq, t):
    d = D_MODEL
    tm = TM_MIX
    n_ctx_tiles = n_ctx // tm
    tpb = seq // tm

    def tab_idx(i):
        lat = jnp.maximum(i - n_ctx_tiles, 0)
        return (0, jnp.where(i < n_ctx_tiles, 0, lax.rem(lat, tpb) + 1))

    full = lambda shape: pl.BlockSpec(shape, lambda i: (0,) * len(shape))
    x_specs, x_arrays = _token_specs(x, tm, n_ctx_tiles, 0)
    return pl.pallas_call(
        functools.partial(_premix_kernel, n_x=len(x_arrays), n_ctx_tiles=n_ctx_tiles, tiles_per_batch=tpb),
        grid=(t // tm,),
        in_specs=x_specs + [
            full(mod_l.shape),
            full((1, d)),
            full(wt.shape),
            full(wf.shape),
            full(gq.shape),
            full(gk.shape),
            pl.BlockSpec((HEAD_DIM, tm), tab_idx),
            pl.BlockSpec((HEAD_DIM, tm), tab_idx),
        ],
        out_specs=[
            pl.BlockSpec((ATTN_WIDTH, tm), lambda i: (0, i)),
            pl.BlockSpec((KV_WIDTH, tm), lambda i: (0, i)),
            pl.BlockSpec((KV_WIDTH, tm), lambda i: (0, i)),
            pl.BlockSpec((tm, FOURIER_WIDTH), lambda i: (i, 0)),
        ],
        out_shape=[
            jax.ShapeDtypeStruct((ATTN_WIDTH, t), BF16),
            jax.ShapeDtypeStruct((KV_WIDTH, t), BF16),
            jax.ShapeDtypeStruct((KV_WIDTH, t), BF16),
            jax.ShapeDtypeStruct((t, FOURIER_WIDTH), BF16),
        ],
        compiler_params=_cparams(1),
        name="premix_project",
    )(*x_arrays, mod_l, g_mix, wt, wf, gq, gk, cos_t, sin_t)


def _attn_kernel(*refs, n_key_blocks, tq):
    q_ref = refs[0]
    k_refs = refs[1:1 + n_key_blocks]
    v_refs = refs[1 + n_key_blocks:1 + 2 * n_key_blocks]
    o_ref = refs[1 + 2 * n_key_blocks]
    kall_ref, vg_ref, ot_ref = refs[2 + 2 * n_key_blocks:]
    g = pl.program_id(1)
    qt = pl.program_id(2)
    n_keys = kall_ref.shape[0]
    kc = KEY_CHUNK if n_keys % KEY_CHUNK == 0 else n_keys
    n_chunks = n_keys // kc
    slabs = kc // 8

    @pl.when(jnp.logical_and(g == 0, qt == 0))
    def _():
        off = 0
        for kr in k_refs:
            n = kr.shape[1]
            kall_ref[off:off + n, :] = kr[...].astype(F32).T.astype(BF16)
            off += n

    @pl.when(qt == 0)
    def _():
        g_rows = pl.ds(pl.multiple_of(g * HEAD_DIM, HEAD_DIM), HEAD_DIM)
        off = 0
        for vr in v_refs:
            n = vr.shape[1]
            vg_ref[0:HEAD_DIM, off:off + n] = vr[g_rows, :]
            off += n
        vg_ref[HEAD_DIM:, :] = jnp.ones((ONES_ROWS, n_keys), BF16)

    row_group = lax.broadcasted_iota(jnp.int32, (KV_WIDTH, tq), 0) // HEAD_DIM

    def masked_q(hh):
        qh = q_ref[hh * HEAD_DIM:(hh + 1) * HEAD_DIM, :]
        q2 = jnp.concatenate([qh, qh], axis=0)
        return jnp.where(row_group == g, q2, jnp.zeros_like(q2))

    q2s = [masked_q(hh) for hh in range(KV_REP)]
    items = [(hh, c) for hh in range(KV_REP) for c in range(n_chunks)]

    def score(item):
        hh, c = item
        return jnp.dot(kall_ref[c * kc:(c + 1) * kc, :], q2s[hh], preferred_element_type=F32)

    pending = [score(it) for it in items[:SCORE_LOOKAHEAD]]
    m = ot = None
    for i, (hh, c) in enumerate(items):
        if c == 0:
            m = jnp.full((1, tq), -jnp.inf, F32)
            ot = jnp.zeros((HEAD_DIM + ONES_ROWS, tq), F32)
        s = pending.pop(0).reshape(slabs, 8, tq)
        if i + SCORE_LOOKAHEAD < len(items):
            pending.append(score(items[i + SCORE_LOOKAHEAD]))
        m_new = jnp.maximum(m, jnp.max(jnp.max(s, axis=0), axis=0, keepdims=True))
        pb = jnp.exp2(s - m_new[None]).reshape(kc, tq).astype(BF16)
        rows = slice(c * kc, (c + 1) * kc)
        ot = jnp.exp2(m - m_new) * ot + jnp.dot(vg_ref[:, rows], pb, preferred_element_type=F32)
        m = m_new
        if c == n_chunks - 1:
            ot_ref[hh * HEAD_DIM:(hh + 1) * HEAD_DIM, :] = ot[:HEAD_DIM] / ot[HEAD_DIM:HEAD_DIM + 1]
    o_ref[...] = ot_ref[...].T.astype(BF16)


def _attention(q_t, k_t, v_t, *, q_tile0, n_q_tiles, key_blocks, tq, n_batch, name):
    n_keys = sum(c for c, _ in key_blocks)
    nkb = len(key_blocks)
    q_spec = pl.BlockSpec((KV_REP * HEAD_DIM, tq),
                          lambda b, g, i: (g, q_tile0 + b * n_q_tiles + i))
    k_specs = [pl.BlockSpec((KV_WIDTH, c), (lambda f: (lambda b, g, i: (0, f(b))))(f))
               for c, f in key_blocks]
    return pl.pallas_call(
        functools.partial(_attn_kernel, n_key_blocks=nkb, tq=tq),
        grid=(n_batch, N_KV_HEADS, n_q_tiles),
        in_specs=[q_spec] + k_specs + k_specs,
        out_specs=pl.BlockSpec((tq, KV_REP * HEAD_DIM),
                               lambda b, g, i: (b * n_q_tiles + i, g)),
        out_shape=jax.ShapeDtypeStruct((n_batch * n_q_tiles * tq, ATTN_WIDTH), BF16),
        scratch_shapes=[pltpu.VMEM((n_keys, KV_WIDTH), BF16),
                        pltpu.VMEM((HEAD_DIM + ONES_ROWS, n_keys), BF16),
                        pltpu.VMEM((KV_REP * HEAD_DIM, tq), F32)],
        compiler_params=_cparams(3),
        name=name,
    )(q_t, *([k_t] * nkb), *([v_t] * nkb))


def _fourier_kernel(c_ref, s_ref, x_ref, cc_ref, sc_ref, w_ref, o_ref, *, norm):
    x = x_ref[...]
    a = jnp.dot(c_ref[...], x, preferred_element_type=F32).astype(BF16)
    b = jnp.dot(s_ref[...], x, preferred_element_type=F32).astype(BF16)
    groups = [slice(grp * FOURIER_GROUP, (grp + 1) * FOURIER_GROUP) for grp in range(N_FOURIER_GROUPS)]
    cws, sws = _channel_maps(cc_ref[...], sc_ref[...], w_ref, norm)
    for grp, sl in enumerate(groups):
        o_ref[:, sl] = (jnp.dot(a[:, sl], cws[grp], preferred_element_type=F32)
                        - jnp.dot(b[:, sl], sws[grp], preferred_element_type=F32)).astype(BF16)


def _channel_maps(cc, sc, w_ref, norm):
    cws = [(jnp.dot(cc, w_ref[grp], preferred_element_type=F32) * norm).astype(BF16)
           for grp in range(N_FOURIER_GROUPS)]
    sws = [(jnp.dot(sc, w_ref[grp], preferred_element_type=F32) * norm).astype(BF16)
           for grp in range(N_FOURIER_GROUPS)]
    return cws, sws


def _fourier(f, cmat, smat, cc, sc, w_four, *, n, tr, x_block0, n_batch, name):
    n_row_tiles = n // tr
    return pl.pallas_call(
        functools.partial(_fourier_kernel, norm=float(1.0 / np.sqrt(n * FOURIER_GROUP))),
        grid=(n_row_tiles, n_batch),
        in_specs=[
            pl.BlockSpec((tr, n), lambda i, b: (i, 0)),
            pl.BlockSpec((tr, n), lambda i, b: (i, 0)),
            pl.BlockSpec((n, FOURIER_WIDTH), lambda i, b: (x_block0 + b, 0)),
            pl.BlockSpec(cc.shape, lambda i, b: (0, 0)),
            pl.BlockSpec(sc.shape, lambda i, b: (0, 0)),
            pl.BlockSpec(w_four.shape, lambda i, b: (0, 0, 0)),
        ],
        out_specs=pl.BlockSpec((tr, FOURIER_WIDTH), lambda i, b: (b * n_row_tiles + i, 0)),
        out_shape=jax.ShapeDtypeStruct((n_batch * n, FOURIER_WIDTH), BF16),
        compiler_params=_cparams(2),
        name=name,
    )(cmat, smat, f, cc, sc, w_four)


def _dft_half_tiles(n, tr):
    n_tiles = n // 2 // tr
    k = (np.arange(n_tiles, dtype=np.int64)[:, None] * tr
         + np.arange(tr + SYM_EXTRA, dtype=np.int64)[None, :])
    pos = np.arange(n, dtype=np.int64)
    ang = 2.0 * np.pi * ((k[:, :, None] * pos[None, None, :]) % n).astype(np.float64) / n
    rev = np.eye(tr, dtype=np.float32)[::-1]
    return jnp.asarray(np.cos(ang), BF16), jnp.asarray(np.sin(ang), BF16), jnp.asarray(rev, BF16)


def _fourier_sym_kernel(c_ref, s_ref, x_ref, cc_ref, sc_ref, w_ref, rev_ref, o_ref, cw_ref, sw_ref, *,
                        norm, tr, n_tiles):
    i = pl.program_id(1)

    @pl.when(jnp.logical_and(pl.program_id(0) == 0, i == 0))
    def _():
        cws, sws = _channel_maps(cc_ref[...], sc_ref[...], w_ref, norm)
        for grp in range(N_FOURIER_GROUPS):
            cw_ref[grp] = cws[grp]
            sw_ref[grp] = sws[grp]

    cws = [cw_ref[grp] for grp in range(N_FOURIER_GROUPS)]
    sws = [sw_ref[grp] for grp in range(N_FOURIER_GROUPS)]
    x = x_ref[...]
    a = jnp.dot(c_ref[0], x, preferred_element_type=F32).astype(BF16)
    b = jnp.dot(s_ref[0], x, preferred_element_type=F32).astype(BF16)
    groups = [slice(grp * FOURIER_GROUP, (grp + 1) * FOURIER_GROUP) for grp in range(N_FOURIER_GROUPS)]
    ps = [jnp.dot(a[:, sl], cws[grp], preferred_element_type=F32) for grp, sl in enumerate(groups)]
    qs = [jnp.dot(b[:, sl], sws[grp], preferred_element_type=F32) for grp, sl in enumerate(groups)]
    upper = jnp.concatenate([(p + q)[1:tr + 1] for p, q in zip(ps, qs)], axis=1).astype(BF16)
    hi_rows = pl.ds(pl.multiple_of((2 * n_tiles - 1 - i) * tr, tr), tr)
    o_ref[hi_rows, :] = jnp.dot(rev_ref[...], upper, preferred_element_type=F32).astype(BF16)
    lo_rows = pl.ds(pl.multiple_of(i * tr, tr), tr)
    for grp, sl in enumerate(groups):
        o_ref[lo_rows, sl] = (ps[grp] - qs[grp])[:tr].astype(BF16)


def _fourier_sym(f, c_tiles, s_tiles, rev, cc, sc, w_four, *, n, tr, x_block0, n_batch, name):
    n_tiles = n // 2 // tr
    ext = tr + SYM_EXTRA
    return pl.pallas_call(
        functools.partial(_fourier_sym_kernel, norm=float(1.0 / np.sqrt(n * FOURIER_GROUP)), tr=tr,
                          n_tiles=n_tiles),
        grid=(n_batch, n_tiles),
        in_specs=[
            pl.BlockSpec((1, ext, n), lambda b, i: (i, 0, 0)),
            pl.BlockSpec((1, ext, n), lambda b, i: (i, 0, 0)),
            pl.BlockSpec((n, FOURIER_WIDTH), lambda b, i: (x_block0 + b, 0)),
            pl.BlockSpec(cc.shape, lambda b, i: (0, 0)),
            pl.BlockSpec(sc.shape, lambda b, i: (0, 0)),
            pl.BlockSpec(w_four.shape, lambda b, i: (0, 0, 0)),
            pl.BlockSpec(rev.shape, lambda b, i: (0, 0)),
        ],
        out_specs=pl.BlockSpec((n, FOURIER_WIDTH), lambda b, i: (b, 0)),
        out_shape=jax.ShapeDtypeStruct((n_batch * n, FOURIER_WIDTH), BF16),
        scratch_shapes=[pltpu.VMEM(w_four.shape, BF16), pltpu.VMEM(w_four.shape, BF16)],
        compiler_params=_cparams(2),
        name=name,
    )(c_tiles, s_tiles, f, cc, sc, w_four, rev)


def _postmix_kernel(*refs, n_x, n_a, n_f, tile0, n_ctx_tiles, tiles_per_batch, with_router):
    x_refs, refs = refs[:n_x], refs[n_x:]
    a_refs, refs = refs[:n_a], refs[n_a:]
    f_refs, refs = refs[:n_f], refs[n_f:]
    mod_ref, g_ref, woa_ref, wof_ref = refs[:4]
    if with_router:
        wr_ref, br_ref, x1_ref, h_ref, rt_ref, rtt_ref, cnt_ref = refs[4:]
    else:
        x1_ref, h_ref = refs[4:]
    i = pl.program_id(0) + tile0
    row = _mod_row(i, n_ctx_tiles, tiles_per_batch)
    is_ctx = i < n_ctx_tiles
    a = _load_tokens(a_refs, is_ctx)
    f = _load_tokens(f_refs, is_ctx)
    mix = (jnp.dot(a, woa_ref[...], preferred_element_type=F32)
           + jnp.dot(f, wof_ref[...], preferred_element_type=F32))
    x1 = _load_tokens(x_refs, is_ctx) + _mod_vec(mod_ref, row, 2) * mix
    x1_ref[...] = x1
    h = _norm_modulate(x1, g_ref[...], _mod_vec(mod_ref, row, 3), _mod_vec(mod_ref, row, 4))
    hb = h.astype(BF16)
    if with_router:
        h_ref[...] = _pack_bf16_pairs(hb)
        logits_t = lax.dot_general(wr_ref[...], hb, (((1,), (1,)), ((), ())),
                                   preferred_element_type=F32) + br_ref[...]
        _top2_route(logits_t, rt_ref, rtt_ref, cnt_ref, first=pl.program_id(0) == 0)
    else:
        h_ref[...] = hb


def _top2_route(logits_t, rt_ref, rtt_ref, cnt_ref, first):
    ne, tm = logits_t.shape
    row = lax.broadcasted_iota(jnp.int32, (ne, tm), 0)
    neg = jnp.float32(-jnp.inf)
    m1 = jnp.max(logits_t, axis=0, keepdims=True)
    i1 = jnp.min(jnp.where(logits_t == m1, row, ne), axis=0, keepdims=True)
    rest = jnp.where(row == i1, neg, logits_t)
    m2 = jnp.max(rest, axis=0, keepdims=True)
    i2 = jnp.min(jnp.where(rest == m2, row, ne), axis=0, keepdims=True)
    e = jnp.exp(m2 - m1)
    g1 = 1.0 / (1.0 + e)
    g2 = e / (1.0 + e)
    pick1 = row == i1
    pick2 = row == i2
    picked = jnp.logical_or(pick1, pick2).astype(F32)

    @pl.when(first)
    def _():
        cnt_ref[...] = jnp.zeros_like(cnt_ref)

    blk = LOGIT_LANES
    r_i = lax.broadcasted_iota(jnp.int32, (blk, blk), 0)
    c_i = lax.broadcasted_iota(jnp.int32, (blk, blk), 1)
    upper = jnp.where(r_i < c_i, 1.0, 0.0).astype(BF16)
    run = cnt_ref[:, 0:1]
    parts = []
    for j in range(tm // blk):
        pj = picked[:, j * blk:(j + 1) * blk]
        parts.append(jnp.dot(pj.astype(BF16), upper, preferred_element_type=F32) + run)
        run = run + jnp.sum(pj, axis=1, keepdims=True)
    before = jnp.concatenate(parts, axis=1)
    cnt_ref[...] = jnp.broadcast_to(run, cnt_ref.shape)
    r1 = jnp.sum(jnp.where(pick1, before, 0.0), axis=0, keepdims=True)
    r2 = jnp.sum(jnp.where(pick2, before, 0.0), axis=0, keepdims=True)
    rec = jnp.concatenate([i1.astype(F32), i2.astype(F32), g1, g2, r1, r2, jnp.zeros((2, tm), F32)], axis=0)
    rtt_ref[...] = rec
    rt_ref[...] = jnp.concatenate([rec, jnp.zeros((LOGIT_LANES - 8, tm), F32)], axis=0).T


def _postmix(x, attn, four, mod_l, g_ffn, wo_a, wo_f, router, *, n_ctx, seq, t, tile0):
    d = D_MODEL
    tm = TM_MIX
    nct = n_ctx // tm
    t = t - tile0 * tm
    full = lambda shape: pl.BlockSpec(shape, lambda i: (0,) * len(shape))
    tile = lambda w: pl.BlockSpec((tm, w), lambda i: (i, 0))
    x_specs, x_arrays = _token_specs(x, tm, nct, tile0)
    a_specs, a_arrays = _token_specs(attn, tm, nct, tile0)
    f_specs, f_arrays = _token_specs(four, tm, nct, tile0)
    in_specs = x_specs + a_specs + f_specs + [full(mod_l.shape), full((1, d)), full(wo_a.shape),
                                              full(wo_f.shape)]
    if router is None:
        out_specs = [tile(d), tile(d)]
        out_shape = [jax.ShapeDtypeStruct((t, d), F32), jax.ShapeDtypeStruct((t, d), BF16)]
    else:
        out_specs = [tile(d), tile(d // 2)]
        out_shape = [jax.ShapeDtypeStruct((t, d), F32), jax.ShapeDtypeStruct((t, d // 2), jnp.uint32)]
    args = x_arrays + a_arrays + f_arrays + [mod_l, g_ffn, wo_a, wo_f]
    if router is not None:
        in_specs += [full(router[0].shape), full(router[1].shape)]
        out_specs += [tile(LOGIT_LANES), pl.BlockSpec((8, tm), lambda i: (0, i)),
                      pl.BlockSpec((EXPERT_ROWS, LOGIT_LANES), lambda i: (0, 0))]
        out_shape += [jax.ShapeDtypeStruct((t, LOGIT_LANES), F32), jax.ShapeDtypeStruct((8, t), F32),
                      jax.ShapeDtypeStruct((EXPERT_ROWS, LOGIT_LANES), F32)]
        args += list(router)
    return pl.pallas_call(
        functools.partial(_postmix_kernel, n_x=len(x_arrays), n_a=len(a_arrays), n_f=len(f_arrays),
                          tile0=tile0, n_ctx_tiles=nct, tiles_per_batch=seq // tm,
                          with_router=router is not None),
        grid=(t // tm,),
        in_specs=in_specs,
        out_specs=out_specs,
        out_shape=out_shape,
        compiler_params=_cparams(1),
        name="postmix_wo_norm",
    )(*args)


def _silu_mul(gate, up):
    return (gate / (1.0 + jnp.exp(-gate))) * up


def _swiglu_chunks(xs, n_chunks, wg_of, wu_of, wd_of, acc_refs):
    if not isinstance(xs, (list, tuple)):
        xs, acc_refs = [xs], [acc_refs]
    items = [(c, r) for c in range(n_chunks) for r in range(len(xs))]
    fetched = {}

    def weight(getter, c):
        if (getter, c) not in fetched:
            fetched[(getter, c)] = getter(c)
        return fetched[(getter, c)]

    def gate_up(c, r):
        return (jnp.dot(xs[r], weight(wg_of, c), preferred_element_type=F32),
                jnp.dot(xs[r], weight(wu_of, c), preferred_element_type=F32))

    gate, up = gate_up(*items[0])
    for idx, (c, r) in enumerate(items):
        act = _silu_mul(gate, up).astype(BF16)
        if idx + 1 < len(items):
            gate, up = gate_up(*items[idx + 1])
        acc_refs[r][...] += jnp.dot(act, weight(wd_of, c), preferred_element_type=F32)


def _dense_ffn_kernel(h_ref, x1_ref, mod_ref, wg_ref, wu_ref, wd_ref, o_ref, acc_ref, *,
                      tile0, n_ctx_tiles, tiles_per_batch):
    i = pl.program_id(0) + tile0
    row = _mod_row(i, n_ctx_tiles, tiles_per_batch)
    acc_ref[...] = jnp.zeros_like(acc_ref)
    sub = lambda c: slice(c * FF_CHUNK, (c + 1) * FF_CHUNK)
    _swiglu_chunks(h_ref[...], wg_ref.shape[1] // FF_CHUNK, lambda c: wg_ref[:, sub(c)],
                   lambda c: wu_ref[:, sub(c)], lambda c: wd_ref[sub(c), :], acc_ref)
    o_ref[...] = x1_ref[...] + _mod_vec(mod_ref, row, 5) * acc_ref[...]


def _dense_ffn(h, x1, mod_l, wg, wu, wd, *, n_ctx, seq, tile0):
    t, d = x1.shape
    tm = TM_DENSE
    full = lambda shape: pl.BlockSpec(shape, lambda i: (0,) * len(shape))
    tile = lambda: pl.BlockSpec((tm, d), lambda i: (i, 0))
    return pl.pallas_call(
        functools.partial(_dense_ffn_kernel, tile0=tile0 * TM_MIX // tm, n_ctx_tiles=n_ctx // tm,
                          tiles_per_batch=seq // tm),
        grid=(t // tm,),
        in_specs=[tile(), tile(), full(mod_l.shape), full(wg.shape), full(wu.shape), full(wd.shape)],
        out_specs=tile(),
        out_shape=jax.ShapeDtypeStruct((t, d), F32),
        scratch_shapes=[pltpu.VMEM((tm, d), F32)],
        compiler_params=_cparams(1),
        name="dense_swiglu",
    )(h, x1, mod_l, wg, wu, wd)


def _moe_ffn_kernel(be_ref, nv_ref, x_ref, wg_ref, wu_ref, wd_ref, o_ref, xb_ref, acc_ref, wgb_ref, wub_ref,
                    wdb_ref):
    i = pl.program_id(0)
    j = pl.program_id(1)
    n_valid = nv_ref[i]

    @pl.when(j == 0)
    def _():
        x = _unpack_bf16_pairs(x_ref[...])
        rows = lax.broadcasted_iota(jnp.int32, x.shape, 0)
        xb_ref[...] = jnp.where(rows < n_valid, x, jnp.zeros_like(x))
        acc_ref[...] = jnp.zeros_like(acc_ref)

    n_sub = TF_MOE // FF_CHUNK
    sub = lambda c: slice(c * FF_CHUNK, (c + 1) * FF_CHUNK)
    rows_needed = ((n_valid + TAIL_ROWS - 1) // TAIL_ROWS) * TAIL_ROWS
    n_full = rows_needed // TM_PASS
    n_tail = (rows_needed - n_full * TM_PASS) // TAIL_ROWS

    def cast_chunk(src_ref, dst_ref, idx):
        w = src_ref[(0, 0) + idx].astype(BF16)
        dst_ref[idx] = w
        return w

    cast_wg = lambda c: cast_chunk(wg_ref, wgb_ref, (slice(None), sub(c)))
    cast_wu = lambda c: cast_chunk(wu_ref, wub_ref, (slice(None), sub(c)))
    cast_wd = lambda c: cast_chunk(wd_ref, wdb_ref, (sub(c), slice(None)))
    read_wg = lambda c: wgb_ref[:, sub(c)]
    read_wu = lambda c: wub_ref[:, sub(c)]
    read_wd = lambda c: wdb_ref[sub(c), :]

    @pl.when(jnp.logical_and(n_full == 0, n_tail > 0))
    def _():
        for c in range(n_sub):
            cast_wg(c), cast_wu(c), cast_wd(c)

    for n_groups in range(1, TM_MOE // TM_PASS + 1):
        groups = [slice(r * TM_PASS, (r + 1) * TM_PASS) for r in range(n_groups)]

        @pl.when(n_full == n_groups)
        def _():
            _swiglu_chunks([xb_ref[rows, :] for rows in groups], n_sub, cast_wg, cast_wu, cast_wd,
                           [acc_ref.at[rows, :] for rows in groups])

    @pl.when(n_tail > 0)
    def _():
        def group(gi, carry):
            rows = pl.ds(pl.multiple_of(n_full * TM_PASS + gi * TAIL_ROWS, TAIL_ROWS), TAIL_ROWS)
            _swiglu_chunks(xb_ref[rows, :], n_sub, read_wg, read_wu, read_wd, acc_ref.at[rows, :])
            return carry

        lax.fori_loop(0, n_tail, group, 0)

    @pl.when(j == pl.num_programs(1) - 1)
    def _():
        o_ref[...] = _pack_bf16_pairs(acc_ref[...].astype(BF16))


def _moe_ffn(xs, block_expert, n_valid, wg, wu, wd, layer):
    n_rows = xs.shape[0]
    d = D_MODEL
    tm, tf = TM_MOE, TF_MOE
    n_blocks = n_rows // tm
    n_ff = wg.shape[3] // tf

    def ff_idx(j, nv, i):
        return jnp.where(nv[i] > 0, j, n_ff - 1)

    grid_spec = pltpu.PrefetchScalarGridSpec(
        num_scalar_prefetch=2,
        grid=(n_blocks, n_ff),
        in_specs=[
            pl.BlockSpec((tm, d // 2), lambda i, j, be, nv: (i, 0)),
            pl.BlockSpec((1, 1, d, tf), lambda i, j, be, nv: (layer, be[i], 0, ff_idx(j, nv, i))),
            pl.BlockSpec((1, 1, d, tf), lambda i, j, be, nv: (layer, be[i], 0, ff_idx(j, nv, i))),
            pl.BlockSpec((1, 1, tf, d), lambda i, j, be, nv: (layer, be[i], ff_idx(j, nv, i), 0)),
        ],
        out_specs=pl.BlockSpec((tm, d // 2), lambda i, j, be, nv: (i, 0)),
        scratch_shapes=[pltpu.VMEM((tm, d), BF16), pltpu.VMEM((tm, d), F32), pltpu.VMEM((d, tf), BF16),
                        pltpu.VMEM((d, tf), BF16), pltpu.VMEM((tf, d), BF16)],
    )
    return pl.pallas_call(
        _moe_ffn_kernel,
        grid_spec=grid_spec,
        out_shape=jax.ShapeDtypeStruct((n_rows, d // 2), jnp.uint32),
        compiler_params=_cparams(2),
        name="expert_swiglu",
    )(block_expert, n_valid, xs, wg, wu, wd)


def _sc_mesh():
    return plsc.VectorSubcoreMesh(core_axis_name="c", subcore_axis_name="s")


def _sc_params():
    return pltpu.CompilerParams(use_tc_tiling_on_sc=True)


def _sc_dispatch(h_packed, dest, n_rows):
    t, w = h_packed.shape
    win = SC_DISPATCH_ROWS
    n_win = t // win
    idx = [dest[k].reshape(n_win, 1, win) for k in range(TOP_K)]

    @functools.partial(
        pl.kernel, out_type=jax.ShapeDtypeStruct((n_rows, w), h_packed.dtype), mesh=_sc_mesh(),
        scratch_types=[], compiler_params=_sc_params(), name="expert_dispatch_scatter")
    def run(x_hbm, i0_hbm, i1_hbm, o_hbm):
        def body(x_vmem, i0_vmem, i1_vmem):
            pltpu.sync_copy(x_vmem, o_hbm.at[i0_vmem.at[0, 0]])
            pltpu.sync_copy(x_vmem, o_hbm.at[i1_vmem.at[0, 0]])

        idx_spec = pl.BlockSpec((1, 1, win), lambda i: (i, 0, 0))
        pltpu.emit_pipeline(
            body, grid=(n_win,),
            in_specs=[pl.BlockSpec((win, w), lambda i: (i, 0)), idx_spec, idx_spec],
            out_specs=[], core_axis_name=("c", "s"), dimension_semantics=(pltpu.PARALLEL,),
        )(x_hbm, i0_hbm, i1_hbm)

    return run(h_packed, *idx)


def _sc_gather(ys, idx_flat):
    w = ys.shape[1]
    n = idx_flat.shape[0]
    win = SC_GATHER_ROWS
    n_win = n // win

    @functools.partial(
        pl.kernel, out_type=jax.ShapeDtypeStruct((n, w), ys.dtype), mesh=_sc_mesh(),
        scratch_types=[], compiler_params=_sc_params(), name="expert_combine_gather")
    def run(y_hbm, i_hbm, o_hbm):
        def body(i_vmem, o_vmem):
            pltpu.sync_copy(y_hbm.at[i_vmem.at[0, 0]], o_vmem)

        pltpu.emit_pipeline(
            body, grid=(n_win,),
            in_specs=[pl.BlockSpec((1, 1, win), lambda i: (i, 0, 0))],
            out_specs=[pl.BlockSpec((win, w), lambda i: (i, 0))],
            core_axis_name=("c", "s"), dimension_semantics=(pltpu.PARALLEL,),
        )(i_hbm, o_hbm)

    return run(ys, idx_flat.reshape(n_win, 1, win))


def _moe_combine_kernel(x1_ref, y0_ref, y1_ref, g_ref, mod_ref, o_ref, *, tile0, n_ctx_tiles,
                        tiles_per_batch):
    i = pl.program_id(0) + tile0
    row = _mod_row(i, n_ctx_tiles, tiles_per_batch)
    gates = g_ref[...]
    y0 = _unpack_bf16_pairs(y0_ref[...]).astype(F32)
    y1 = _unpack_bf16_pairs(y1_ref[...]).astype(F32)
    y = y0 * gates[:, 2:3] + y1 * gates[:, 3:4]
    o_ref[...] = x1_ref[...] + _mod_vec(mod_ref, row, 5) * y


def _moe_combine(x1, y_sel, gates, mod_l, *, n_ctx, seq, tile0):
    t, d = x1.shape
    tm = TM_MIX
    n_tiles = t // tm
    tile = lambda w: pl.BlockSpec((tm, w), lambda i: (i, 0))
    return pl.pallas_call(
        functools.partial(_moe_combine_kernel, tile0=tile0, n_ctx_tiles=n_ctx // tm,
                          tiles_per_batch=seq // tm),
        grid=(n_tiles,),
        in_specs=[tile(d), tile(d // 2), pl.BlockSpec((tm, d // 2), lambda i: (n_tiles + i, 0)),
                  tile(LOGIT_LANES),
                  pl.BlockSpec(mod_l.shape, lambda i: (0, 0))],
        out_specs=tile(d),
        out_shape=jax.ShapeDtypeStruct((t, d), F32),
        compiler_params=_cparams(1),
        name="expert_combine",
    )(x1, y_sel, y_sel, gates, mod_l)


def _route(route_t, counts):
    t = route_t.shape[1]
    expert = route_t[0:TOP_K].astype(jnp.int32)
    rank = route_t[4:4 + TOP_K].astype(jnp.int32)
    counts = counts[:N_EXPERTS, 0].astype(jnp.int32)
    n_assign = t * TOP_K
    padded = ((counts + TM_MOE - 1) // TM_MOE) * TM_MOE
    pend = jnp.cumsum(padded)
    pstart = pend - padded
    dest = rank
    for e in range(N_EXPERTS):
        dest = dest + jnp.where(expert == e, pstart[e], 0)
    n_blocks = -(-n_assign // TM_MOE) + N_EXPERTS
    block_start = jnp.arange(n_blocks, dtype=jnp.int32) * TM_MOE
    block_expert = jnp.minimum(jnp.sum(pend[None, :] <= block_start[:, None], axis=1),
                               N_EXPERTS - 1).astype(jnp.int32)
    n_valid = jnp.clip(counts[block_expert] - (block_start - pstart[block_expert]), 0, TM_MOE)
    n_valid = jnp.where(block_start < pend[-1], n_valid, 0).astype(jnp.int32)
    return dest.astype(jnp.int32), block_expert, n_valid, n_blocks * TM_MOE


def kernel(x, c, ctx, c_ctx, w_mod, b_mod, g_mix, g_ffn, g_q, g_k, w_in, w_four, w_o,
           w_gate_dense, w_up_dense, w_down_dense, w_router, b_router,
           w_gate_moe, w_up_moe, w_down_moe):
    b, s, d = x.shape
    n_ctx_len = ctx.shape[1]
    n_ctx = b * n_ctx_len
    t = n_ctx + b * s
    assert d == D_MODEL and b + 1 <= MOD_ROWS
    assert n_ctx % TM_DENSE == 0 and s % TM_DENSE == 0 and n_ctx_len % 128 == 0 and n_ctx % s == 0

    cvec = jnp.zeros((MOD_ROWS, d), F32).at[0].set(c_ctx).at[1:b + 1].set(c)
    mod = _modulation(cvec, w_mod, b_mod)

    cos_t, sin_t = _rope_tables_t(s, TM_MIX)
    c_lat, s_lat, rev_lat = _dft_half_tiles(s, TR_FOUR)
    c_ctx_m, s_ctx_m = _dft_mats(n_ctx_len)
    c_grp, s_grp = _dft_mats(FOURIER_GROUP)

    xa = (ctx.reshape(n_ctx, d), x.reshape(b * s, d))

    for l in range(DEPTH):
        mod_l = mod[l]
        w_in_l = w_in[l]
        wt = w_in_l[:, :QKV_WIDTH].T.astype(BF16)
        wf = w_in_l[:, QKV_WIDTH:].astype(BF16)
        gq = jnp.broadcast_to((g_q[l] * (LOG2_E * HEAD_DIM ** -0.5))[:, None], (HEAD_DIM, TM_MIX))
        gk = jnp.broadcast_to(g_k[l][:, None], (HEAD_DIM, TM_MIX))
        last = l == DEPTH - 1
        tile0 = n_ctx // TM_MIX if last else 0
        q_t, k_t, v_t, f = _premix(xa, mod_l, g_mix[l].reshape(1, d), wt, wf, gq, gk, cos_t, sin_t,
                                   n_ctx=n_ctx, seq=s, t=t)

        n_qt = s // TQ_ATTN
        lat_keys = [(n_ctx_len, lambda bb: bb), (s, lambda bb: n_ctx // s + bb)]
        attn_lat = _attention(q_t, k_t, v_t, q_tile0=n_ctx // TQ_ATTN, n_q_tiles=n_qt,
                              key_blocks=lat_keys, tq=TQ_ATTN, n_batch=b, name="attention_latent")
        attn_ctx = None if last else _attention(
            q_t, k_t, v_t, q_tile0=0, n_q_tiles=1, key_blocks=[(n_ctx_len, lambda bb: bb)],
            tq=n_ctx_len, n_batch=b, name="attention_context")

        wfour = w_four[l].astype(BF16)
        four_lat = _fourier_sym(f, c_lat, s_lat, rev_lat, c_grp, s_grp, wfour, n=s, tr=TR_FOUR,
                                x_block0=n_ctx // s, n_batch=b, name="fourier_latent")
        four_ctx = None if last else _fourier(
            f, c_ctx_m, s_ctx_m, c_grp, s_grp, wfour, n=n_ctx_len, tr=n_ctx_len, x_block0=0, n_batch=b,
            name="fourier_context")

        wo = w_o[l].astype(BF16)
        is_moe = l % 2 == 1
        li = l // 2
        router = None
        if is_moe:
            wr = jnp.zeros((EXPERT_ROWS, d), BF16).at[:N_EXPERTS].set(w_router[li].T.astype(BF16))
            br = jnp.full((EXPERT_ROWS,), -jnp.inf, F32).at[:N_EXPERTS].set(b_router[li])
            br = jnp.broadcast_to(br[:, None], (EXPERT_ROWS, TM_MIX))
            router = (wr, br)
        res = _postmix(xa, (attn_ctx, attn_lat), (four_ctx, four_lat), mod_l, g_ffn[l].reshape(1, d),
                       wo[:ATTN_WIDTH], wo[ATTN_WIDTH:], router, n_ctx=n_ctx, seq=s, t=t, tile0=tile0)
        t_l = t - tile0 * TM_MIX
        if not is_moe:
            x1, h2 = res
            xa = _dense_ffn(h2, x1, mod_l, w_gate_dense[li].astype(BF16), w_up_dense[li].astype(BF16),
                            w_down_dense[li].astype(BF16), n_ctx=n_ctx, seq=s, tile0=tile0)
        else:
            x1, h2, route, route_t, counts = res
            dest, block_expert, n_valid, n_rows = _route(route_t, counts)
            xs = _sc_dispatch(h2, dest, n_rows)
            ys = _moe_ffn(xs, block_expert, n_valid, w_gate_moe, w_up_moe, w_down_moe, li)
            y_sel = _sc_gather(ys, dest.reshape(-1))
            xa = _moe_combine(x1, y_sel, route, mod_l, n_ctx=n_ctx, seq=s, tile0=tile0)

    return xa.reshape(b, s, d)
```

```python
import functools

import numpy as np
import jax
import jax.numpy as jnp
from jax import lax
from jax.experimental import pallas as pl
from jax.experimental.pallas import tpu as pltpu
from jax.experimental.pallas import tpu_sc as plsc

D_MODEL = 1024
DEPTH = 4
GRID_W = 64
HEAD_DIM = 64
ATTN_WIDTH = 512
N_Q_HEADS = 8
N_KV_HEADS = 2
KV_REP = 4
KV_WIDTH = 128
FOURIER_WIDTH = 512
N_FOURIER_GROUPS = 4
FOURIER_GROUP = 128
ROT_PER_AXIS = 32
ROPE_THETA = 10000.0
N_EXPERTS = 8
TOP_K = 2
EPS = 1e-6

QKV_WIDTH = ATTN_WIDTH + 2 * KV_WIDTH
MOD_ROWS = 16
LOGIT_LANES = 128
EXPERT_ROWS = 16

TM_MIX = 1024
TQ_ATTN = 512
KEY_CHUNK = 256
ONES_ROWS = 16
SCORE_LOOKAHEAD = 2
LOG2_E = 1.4426950408889634
TR_FOUR = 512
SYM_EXTRA = 16
TM_DENSE = 1024
FF_CHUNK = 256
TM_MOE = 2048
TM_PASS = 1024
TAIL_ROWS = 256
TF_MOE = 512
SC_DISPATCH_ROWS = 64
SC_GATHER_ROWS = 64
V7X_VMEM_BYTES = 64 * 1024 * 1024
VMEM_LIMIT = V7X_VMEM_BYTES - 8 * 1024 * 1024

F32 = jnp.float32
BF16 = jnp.bfloat16


def _cparams(n_axes):
    return pltpu.CompilerParams(
        dimension_semantics=("arbitrary",) * n_axes, vmem_limit_bytes=VMEM_LIMIT)


def _rope_tables_t(s, tm):
    n_rows = s // GRID_W
    rows = np.repeat(np.arange(n_rows), GRID_W).astype(np.float64)
    cols = np.tile(np.arange(GRID_W), n_rows).astype(np.float64)
    inv_freq = (ROPE_THETA ** (-np.arange(0, ROT_PER_AXIS, 2, dtype=np.float32) / ROT_PER_AXIS)
                ).astype(np.float32).astype(np.float64)
    ang_r = (rows[None, :].astype(np.float32) * inv_freq[:, None].astype(np.float32)).astype(np.float64)
    ang_c = (cols[None, :].astype(np.float32) * inv_freq[:, None].astype(np.float32)).astype(np.float64)
    cos = np.concatenate([np.cos(ang_r), np.cos(ang_r), np.cos(ang_c), np.cos(ang_c)], axis=0)
    sin = np.concatenate([-np.sin(ang_r), np.sin(ang_r), -np.sin(ang_c), np.sin(ang_c)], axis=0)
    cos = np.concatenate([np.ones((HEAD_DIM, tm)), cos], axis=1)
    sin = np.concatenate([np.zeros((HEAD_DIM, tm)), sin], axis=1)
    return jnp.asarray(cos, F32), jnp.asarray(sin, F32)


def _dft_mats(n):
    k = np.arange(n, dtype=np.int64)
    ang = 2.0 * np.pi * ((k[:, None] * k[None, :]) % n).astype(np.float64) / n
    return jnp.asarray(np.cos(ang), BF16), jnp.asarray(np.sin(ang), BF16)


def _mod_kernel(c_ref, w_ref, b_ref, o_ref):
    c = c_ref[...]
    s = (c / (1.0 + jnp.exp(-c))).astype(BF16)
    w = w_ref[0].astype(BF16)
    o_ref[0] = jnp.dot(s, w, preferred_element_type=F32) + b_ref[0]


def _modulation(cvec, w_mod, b_mod):
    depth, d, n = w_mod.shape
    tn = 1536
    return pl.pallas_call(
        _mod_kernel,
        grid=(depth, n // tn),
        in_specs=[
            pl.BlockSpec((MOD_ROWS, d), lambda l, j: (0, 0)),
            pl.BlockSpec((1, d, tn), lambda l, j: (l, 0, j)),
            pl.BlockSpec((1, 1, tn), lambda l, j: (l, 0, j)),
        ],
        out_specs=pl.BlockSpec((1, MOD_ROWS, tn), lambda l, j: (l, 0, j)),
        out_shape=jax.ShapeDtypeStruct((depth, MOD_ROWS, n), F32),
        compiler_params=_cparams(2),
        name="adaln_vectors",
    )(cvec, w_mod, b_mod.reshape(depth, 1, n))


def _mod_row(i, n_ctx_tiles, tiles_per_batch):
    lat = jnp.maximum(i - n_ctx_tiles, 0)
    return jnp.where(i < n_ctx_tiles, 0, lat // tiles_per_batch + 1)


def _mod_vec(mod_ref, row, comp):
    return mod_ref[pl.ds(row, 1), comp * D_MODEL:(comp + 1) * D_MODEL]


def _pack_bf16_pairs(xb):
    n = xb.shape[1] // 2
    bits = lax.bitcast_convert_type(xb.astype(F32), jnp.uint32)
    return (bits[:, :n] >> 16) | (bits[:, n:] & jnp.uint32(0xFFFF0000))


def _unpack_bf16_pairs(w):
    lo = lax.bitcast_convert_type(w << 16, F32)
    hi = lax.bitcast_convert_type(w & jnp.uint32(0xFFFF0000), F32)
    return jnp.concatenate([lo, hi], axis=1).astype(BF16)


def _token_specs(x, tm, n_ctx_tiles, tile0):
    if not isinstance(x, tuple):
        return [pl.BlockSpec((tm, x.shape[1]), lambda i: (i + tile0, 0))], [x]
    ctx, lat = x
    specs, arrays = [], []
    if ctx is not None:
        specs.append(pl.BlockSpec((tm, ctx.shape[1]), lambda i: (jnp.minimum(i + tile0, n_ctx_tiles - 1), 0)))
        arrays.append(ctx)
    else:
        assert tile0 >= n_ctx_tiles
    specs.append(pl.BlockSpec((tm, lat.shape[1]), lambda i: (jnp.maximum(i + tile0 - n_ctx_tiles, 0), 0)))
    arrays.append(lat)
    return specs, arrays


def _layer_spec(w, layer):
    return pl.BlockSpec((None,) + w.shape[1:], lambda i: (layer, 0, 0), pipeline_mode=pl.Buffered(1))


def _load_tokens(refs, is_ctx):
    if len(refs) == 2:
        return jnp.where(is_ctx, refs[0][...], refs[1][...])
    return refs[0][...]


def _norm_modulate(x, g, shift, scale):
    ms = jnp.mean(x * x, axis=-1, keepdims=True)
    y = x * lax.rsqrt(ms + EPS) * g
    return y * (1.0 + scale) + shift


def _premix_kernel(*refs, n_x, n_ctx_tiles, tiles_per_batch):
    x_refs = refs[:n_x]
    (mod_ref, g_ref, w_ref, gq_ref, gk_ref, cos_ref, sin_ref,
     q_ref, k_ref, v_ref, f_ref, wt_ref, wf_ref) = refs[n_x:]
    i = pl.program_id(0)

    @pl.when(i == 0)
    def _():
        wt_ref[...] = w_ref[:, :QKV_WIDTH].T.astype(BF16)
        wf_ref[...] = w_ref[:, QKV_WIDTH:].astype(BF16)

    row = _mod_row(i, n_ctx_tiles, tiles_per_batch)
    x = _load_tokens(x_refs, i < n_ctx_tiles)
    h = _norm_modulate(x, g_ref[...], _mod_vec(mod_ref, row, 0), _mod_vec(mod_ref, row, 1))
    hb = h.astype(BF16)
    f_ref[...] = jnp.dot(hb, wf_ref[...], preferred_element_type=F32).astype(BF16)
    pt = lax.dot_general(wt_ref[...], hb, (((1,), (1,)), ((), ())), preferred_element_type=F32)
    v_ref[...] = pt[ATTN_WIDTH + KV_WIDTH:, :].astype(BF16)
    cos = cos_ref[...]
    sin = sin_ref[...]

    def norm_rope(xh, gain):
        ms = jnp.mean(xh * xh, axis=0, keepdims=True)
        y = xh * lax.rsqrt(ms + EPS) * gain
        half = ROT_PER_AXIS // 2
        swapped = jnp.concatenate(
            [y[half:2 * half], y[0:half], y[3 * half:4 * half], y[2 * half:3 * half]], axis=0)
        return y * cos + swapped * sin

    gq = gq_ref[...]
    gk = gk_ref[...]
    for hh in range(N_Q_HEADS):
        q_ref[hh * HEAD_DIM:(hh + 1) * HEAD_DIM, :] = norm_rope(
            pt[hh * HEAD_DIM:(hh + 1) * HEAD_DIM, :], gq).astype(BF16)
    for hh in range(N_KV_HEADS):
        lo = ATTN_WIDTH + hh * HEAD_DIM
        k_ref[hh * HEAD_DIM:(hh + 1) * HEAD_DIM, :] = norm_rope(pt[lo:lo + HEAD_DIM, :], gk).astype(BF16)


def _premix(x, mod_l, g_mix, w_in, layer, gq, gk, cos_t, sin_t, *, n_ctx, seq, t):
    d = D_MODEL
    tm = TM_MIX
    n_ctx_tiles = n_ctx // tm
    tpb = seq // tm

    def tab_idx(i):
        lat = jnp.maximum(i - n_ctx_tiles, 0)
        return (0, jnp.where(i < n_ctx_tiles, 0, lax.rem(lat, tpb) + 1))

    full = lambda shape: pl.BlockSpec(shape, lambda i: (0,) * len(shape))
    x_specs, x_arrays = _token_specs(x, tm, n_ctx_tiles, 0)
    return pl.pallas_call(
        functools.partial(_premix_kernel, n_x=len(x_arrays), n_ctx_tiles=n_ctx_tiles, tiles_per_batch=tpb),
        grid=(t // tm,),
        in_specs=x_specs + [
            full(mod_l.shape),
            full((1, d)),
            _layer_spec(w_in, layer),
            full(gq.shape),
            full(gk.shape),
            pl.BlockSpec((HEAD_DIM, tm), tab_idx),
            pl.BlockSpec((HEAD_DIM, tm), tab_idx),
        ],
        out_specs=[
            pl.BlockSpec((ATTN_WIDTH, tm), lambda i: (0, i)),
            pl.BlockSpec((KV_WIDTH, tm), lambda i: (0, i)),
            pl.BlockSpec((KV_WIDTH, tm), lambda i: (0, i)),
            pl.BlockSpec((tm, FOURIER_WIDTH), lambda i: (i, 0)),
        ],
        out_shape=[
            jax.ShapeDtypeStruct((ATTN_WIDTH, t), BF16),
            jax.ShapeDtypeStruct((KV_WIDTH, t), BF16),
            jax.ShapeDtypeStruct((KV_WIDTH, t), BF16),
            jax.ShapeDtypeStruct((t, FOURIER_WIDTH), BF16),
        ],
        scratch_shapes=[pltpu.VMEM((QKV_WIDTH, d), BF16), pltpu.VMEM((d, FOURIER_WIDTH), BF16)],
        compiler_params=_cparams(1),
        name="premix_project",
    )(*x_arrays, mod_l, g_mix, w_in, gq, gk, cos_t, sin_t)


def _attn_kernel(*refs, n_key_blocks, tq):
    q_ref = refs[0]
    k_refs = refs[1:1 + n_key_blocks]
    v_refs = refs[1 + n_key_blocks:1 + 2 * n_key_blocks]
    o_ref = refs[1 + 2 * n_key_blocks]
    kall_ref, vg_ref, ot_ref = refs[2 + 2 * n_key_blocks:]
    g = pl.program_id(1)
    qt = pl.program_id(2)
    n_keys = kall_ref.shape[0]
    kc = KEY_CHUNK if n_keys % KEY_CHUNK == 0 else n_keys
    n_chunks = n_keys // kc
    slabs = kc // 8

    @pl.when(jnp.logical_and(g == 0, qt == 0))
    def _():
        off = 0
        for kr in k_refs:
            n = kr.shape[1]
            kall_ref[off:off + n, :] = kr[...].astype(F32).T.astype(BF16)
            off += n

    @pl.when(qt == 0)
    def _():
        g_rows = pl.ds(pl.multiple_of(g * HEAD_DIM, HEAD_DIM), HEAD_DIM)
        off = 0
        for vr in v_refs:
            n = vr.shape[1]
            vg_ref[0:HEAD_DIM, off:off + n] = vr[g_rows, :]
            off += n
        vg_ref[HEAD_DIM:, :] = jnp.ones((ONES_ROWS, n_keys), BF16)

    row_group = lax.broadcasted_iota(jnp.int32, (KV_WIDTH, tq), 0) // HEAD_DIM

    def masked_q(hh):
        qh = q_ref[hh * HEAD_DIM:(hh + 1) * HEAD_DIM, :]
        q2 = jnp.concatenate([qh, qh], axis=0)
        return jnp.where(row_group == g, q2, jnp.zeros_like(q2))

    q2s = [masked_q(hh) for hh in range(KV_REP)]
    items = [(hh, c) for hh in range(KV_REP) for c in range(n_chunks)]

    def score(item):
        hh, c = item
        return jnp.dot(kall_ref[c * kc:(c + 1) * kc, :], q2s[hh], preferred_element_type=F32)

    pending = [score(it) for it in items[:SCORE_LOOKAHEAD]]
    m = ot = None
    for i, (hh, c) in enumerate(items):
        if c == 0:
            m = jnp.full((1, tq), -jnp.inf, F32)
            ot = jnp.zeros((HEAD_DIM + ONES_ROWS, tq), F32)
        s = pending.pop(0).reshape(slabs, 8, tq)
        if i + SCORE_LOOKAHEAD < len(items):
            pending.append(score(items[i + SCORE_LOOKAHEAD]))
        m_new = jnp.maximum(m, jnp.max(jnp.max(s, axis=0), axis=0, keepdims=True))
        pb = jnp.exp2(s - m_new[None]).reshape(kc, tq).astype(BF16)
        rows = slice(c * kc, (c + 1) * kc)
        ot = jnp.exp2(m - m_new) * ot + jnp.dot(vg_ref[:, rows], pb, preferred_element_type=F32)
        m = m_new
        if c == n_chunks - 1:
            ot_ref[hh * HEAD_DIM:(hh + 1) * HEAD_DIM, :] = ot[:HEAD_DIM] / ot[HEAD_DIM:HEAD_DIM + 1]
    o_ref[...] = ot_ref[...].T.astype(BF16)


def _attention(q_t, k_t, v_t, *, q_tile0, n_q_tiles, key_blocks, tq, n_batch, name):
    n_keys = sum(c for c, _ in key_blocks)
    nkb = len(key_blocks)
    q_spec = pl.BlockSpec((KV_REP * HEAD_DIM, tq),
                          lambda b, g, i: (g, q_tile0 + b * n_q_tiles + i))
    k_specs = [pl.BlockSpec((KV_WIDTH, c), (lambda f: (lambda b, g, i: (0, f(b))))(f))
               for c, f in key_blocks]
    return pl.pallas_call(
        functools.partial(_attn_kernel, n_key_blocks=nkb, tq=tq),
        grid=(n_batch, N_KV_HEADS, n_q_tiles),
        in_specs=[q_spec] + k_specs + k_specs,
        out_specs=pl.BlockSpec((tq, KV_REP * HEAD_DIM),
                               lambda b, g, i: (b * n_q_tiles + i, g)),
        out_shape=jax.ShapeDtypeStruct((n_batch * n_q_tiles * tq, ATTN_WIDTH), BF16),
        scratch_shapes=[pltpu.VMEM((n_keys, KV_WIDTH), BF16),
                        pltpu.VMEM((HEAD_DIM + ONES_ROWS, n_keys), BF16),
                        pltpu.VMEM((KV_REP * HEAD_DIM, tq), F32)],
        compiler_params=_cparams(3),
        name=name,
    )(q_t, *([k_t] * nkb), *([v_t] * nkb))


def _fourier_kernel(c_ref, s_ref, x_ref, cc_ref, sc_ref, w_ref, o_ref, *, norm):
    x = x_ref[...]
    a = jnp.dot(c_ref[...], x, preferred_element_type=F32).astype(BF16)
    b = jnp.dot(s_ref[...], x, preferred_element_type=F32).astype(BF16)
    groups = [slice(grp * FOURIER_GROUP, (grp + 1) * FOURIER_GROUP) for grp in range(N_FOURIER_GROUPS)]
    cws, sws = _channel_maps(cc_ref[...], sc_ref[...], w_ref, norm)
    for grp, sl in enumerate(groups):
        o_ref[:, sl] = (jnp.dot(a[:, sl], cws[grp], preferred_element_type=F32)
                        - jnp.dot(b[:, sl], sws[grp], preferred_element_type=F32)).astype(BF16)


def _channel_maps(cc, sc, w_ref, norm):
    cws = [(jnp.dot(cc, w_ref[grp], preferred_element_type=F32) * norm).astype(BF16)
           for grp in range(N_FOURIER_GROUPS)]
    sws = [(jnp.dot(sc, w_ref[grp], preferred_element_type=F32) * norm).astype(BF16)
           for grp in range(N_FOURIER_GROUPS)]
    return cws, sws


def _fourier(f, cmat, smat, cc, sc, w_four, *, n, tr, x_block0, n_batch, name):
    n_row_tiles = n // tr
    return pl.pallas_call(
        functools.partial(_fourier_kernel, norm=float(1.0 / np.sqrt(n * FOURIER_GROUP))),
        grid=(n_row_tiles, n_batch),
        in_specs=[
            pl.BlockSpec((tr, n), lambda i, b: (i, 0)),
            pl.BlockSpec((tr, n), lambda i, b: (i, 0)),
            pl.BlockSpec((n, FOURIER_WIDTH), lambda i, b: (x_block0 + b, 0)),
            pl.BlockSpec(cc.shape, lambda i, b: (0, 0)),
            pl.BlockSpec(sc.shape, lambda i, b: (0, 0)),
            pl.BlockSpec(w_four.shape, lambda i, b: (0, 0, 0)),
        ],
        out_specs=pl.BlockSpec((tr, FOURIER_WIDTH), lambda i, b: (b * n_row_tiles + i, 0)),
        out_shape=jax.ShapeDtypeStruct((n_batch * n, FOURIER_WIDTH), BF16),
        compiler_params=_cparams(2),
        name=name,
    )(cmat, smat, f, cc, sc, w_four)


def _dft_half_tiles(n, tr):
    n_tiles = n // 2 // tr
    k = (np.arange(n_tiles, dtype=np.int64)[:, None] * tr
         + np.arange(tr + SYM_EXTRA, dtype=np.int64)[None, :])
    pos = np.arange(n, dtype=np.int64)
    ang = 2.0 * np.pi * ((k[:, :, None] * pos[None, None, :]) % n).astype(np.float64) / n
    rev = np.eye(tr, dtype=np.float32)[::-1]
    return jnp.asarray(np.cos(ang), BF16), jnp.asarray(np.sin(ang), BF16), jnp.asarray(rev, BF16)


def _fourier_sym_kernel(c_ref, s_ref, x_ref, cc_ref, sc_ref, w_ref, rev_ref, o_ref, cw_ref, sw_ref, *,
                        norm, tr, n_tiles):
    i = pl.program_id(1)

    @pl.when(jnp.logical_and(pl.program_id(0) == 0, i == 0))
    def _():
        cws, sws = _channel_maps(cc_ref[...], sc_ref[...], w_ref, norm)
        for grp in range(N_FOURIER_GROUPS):
            cw_ref[grp] = cws[grp]
            sw_ref[grp] = sws[grp]

    cws = [cw_ref[grp] for grp in range(N_FOURIER_GROUPS)]
    sws = [sw_ref[grp] for grp in range(N_FOURIER_GROUPS)]
    x = x_ref[...]
    a = jnp.dot(c_ref[0], x, preferred_element_type=F32).astype(BF16)
    b = jnp.dot(s_ref[0], x, preferred_element_type=F32).astype(BF16)
    groups = [slice(grp * FOURIER_GROUP, (grp + 1) * FOURIER_GROUP) for grp in range(N_FOURIER_GROUPS)]
    ps = [jnp.dot(a[:, sl], cws[grp], preferred_element_type=F32) for grp, sl in enumerate(groups)]
    qs = [jnp.dot(b[:, sl], sws[grp], preferred_element_type=F32) for grp, sl in enumerate(groups)]
    upper = jnp.concatenate([(p + q)[1:tr + 1] for p, q in zip(ps, qs)], axis=1).astype(BF16)
    hi_rows = pl.ds(pl.multiple_of((2 * n_tiles - 1 - i) * tr, tr), tr)
    o_ref[hi_rows, :] = jnp.dot(rev_ref[...], upper, preferred_element_type=F32).astype(BF16)
    lo_rows = pl.ds(pl.multiple_of(i * tr, tr), tr)
    for grp, sl in enumerate(groups):
        o_ref[lo_rows, sl] = (ps[grp] - qs[grp])[:tr].astype(BF16)


def _fourier_sym(f, c_tiles, s_tiles, rev, cc, sc, w_four, *, n, tr, x_block0, n_batch, name):
    n_tiles = n // 2 // tr
    ext = tr + SYM_EXTRA
    return pl.pallas_call(
        functools.partial(_fourier_sym_kernel, norm=float(1.0 / np.sqrt(n * FOURIER_GROUP)), tr=tr,
                          n_tiles=n_tiles),
        grid=(n_batch, n_tiles),
        in_specs=[
            pl.BlockSpec((1, ext, n), lambda b, i: (i, 0, 0)),
            pl.BlockSpec((1, ext, n), lambda b, i: (i, 0, 0)),
            pl.BlockSpec((n, FOURIER_WIDTH), lambda b, i: (x_block0 + b, 0)),
            pl.BlockSpec(cc.shape, lambda b, i: (0, 0)),
            pl.BlockSpec(sc.shape, lambda b, i: (0, 0)),
            pl.BlockSpec(w_four.shape, lambda b, i: (0, 0, 0)),
            pl.BlockSpec(rev.shape, lambda b, i: (0, 0)),
        ],
        out_specs=pl.BlockSpec((n, FOURIER_WIDTH), lambda b, i: (b, 0)),
        out_shape=jax.ShapeDtypeStruct((n_batch * n, FOURIER_WIDTH), BF16),
        scratch_shapes=[pltpu.VMEM(w_four.shape, BF16), pltpu.VMEM(w_four.shape, BF16)],
        compiler_params=_cparams(2),
        name=name,
    )(c_tiles, s_tiles, f, cc, sc, w_four, rev)


def _postmix_kernel(*refs, n_x, n_a, n_f, tile0, n_ctx_tiles, tiles_per_batch, with_router):
    x_refs, refs = refs[:n_x], refs[n_x:]
    a_refs, refs = refs[:n_a], refs[n_a:]
    f_refs, refs = refs[:n_f], refs[n_f:]
    mod_ref, g_ref, wo_ref = refs[:3]
    wob_ref = refs[-1]
    if with_router:
        wr_ref, br_ref, x1_ref, h_ref, rt_ref, rtt_ref, cnt_ref = refs[3:-1]
    else:
        x1_ref, h_ref = refs[3:-1]

    @pl.when(pl.program_id(0) == 0)
    def _():
        wob_ref[...] = wo_ref[...].astype(BF16)

    i = pl.program_id(0) + tile0
    row = _mod_row(i, n_ctx_tiles, tiles_per_batch)
    is_ctx = i < n_ctx_tiles
    a = _load_tokens(a_refs, is_ctx)
    f = _load_tokens(f_refs, is_ctx)
    mix = (jnp.dot(a, wob_ref[:ATTN_WIDTH, :], preferred_element_type=F32)
           + jnp.dot(f, wob_ref[ATTN_WIDTH:, :], preferred_element_type=F32))
    x1 = _load_tokens(x_refs, is_ctx) + _mod_vec(mod_ref, row, 2) * mix
    x1_ref[...] = x1
    h = _norm_modulate(x1, g_ref[...], _mod_vec(mod_ref, row, 3), _mod_vec(mod_ref, row, 4))
    hb = h.astype(BF16)
    if with_router:
        h_ref[...] = _pack_bf16_pairs(hb)
        logits_t = lax.dot_general(wr_ref[...], hb, (((1,), (1,)), ((), ())),
                                   preferred_element_type=F32) + br_ref[...]
        _top2_route(logits_t, rt_ref, rtt_ref, cnt_ref, first=pl.program_id(0) == 0)
    else:
        h_ref[...] = hb


def _top2_route(logits_t, rt_ref, rtt_ref, cnt_ref, first):
    ne, tm = logits_t.shape
    row = lax.broadcasted_iota(jnp.int32, (ne, tm), 0)
    neg = jnp.float32(-jnp.inf)
    m1 = jnp.max(logits_t, axis=0, keepdims=True)
    i1 = jnp.min(jnp.where(logits_t == m1, row, ne), axis=0, keepdims=True)
    rest = jnp.where(row == i1, neg, logits_t)
    m2 = jnp.max(rest, axis=0, keepdims=True)
    i2 = jnp.min(jnp.where(rest == m2, row, ne), axis=0, keepdims=True)
    e = jnp.exp(m2 - m1)
    g1 = 1.0 / (1.0 + e)
    g2 = e / (1.0 + e)
    pick1 = row == i1
    pick2 = row == i2
    picked = jnp.logical_or(pick1, pick2).astype(F32)

    @pl.when(first)
    def _():
        cnt_ref[...] = jnp.zeros_like(cnt_ref)

    blk = LOGIT_LANES
    r_i = lax.broadcasted_iota(jnp.int32, (blk, blk), 0)
    c_i = lax.broadcasted_iota(jnp.int32, (blk, blk), 1)
    upper = jnp.where(r_i < c_i, 1.0, 0.0).astype(BF16)
    run = cnt_ref[:, 0:1]
    parts = []
    for j in range(tm // blk):
        pj = picked[:, j * blk:(j + 1) * blk]
        parts.append(jnp.dot(pj.astype(BF16), upper, preferred_element_type=F32) + run)
        run = run + jnp.sum(pj, axis=1, keepdims=True)
    before = jnp.concatenate(parts, axis=1)
    cnt_ref[...] = jnp.broadcast_to(run, cnt_ref.shape)
    r1 = jnp.sum(jnp.where(pick1, before, 0.0), axis=0, keepdims=True)
    r2 = jnp.sum(jnp.where(pick2, before, 0.0), axis=0, keepdims=True)
    rec = jnp.concatenate([i1.astype(F32), i2.astype(F32), g1, g2, r1, r2, jnp.zeros((2, tm), F32)], axis=0)
    rtt_ref[...] = rec
    rt_ref[...] = jnp.concatenate([rec, jnp.zeros((LOGIT_LANES - 8, tm), F32)], axis=0).T


def _postmix(x, attn, four, mod_l, g_ffn, w_o, layer, router, *, n_ctx, seq, t, tile0):
    d = D_MODEL
    tm = TM_MIX
    nct = n_ctx // tm
    t = t - tile0 * tm
    full = lambda shape: pl.BlockSpec(shape, lambda i: (0,) * len(shape))
    tile = lambda w: pl.BlockSpec((tm, w), lambda i: (i, 0))
    x_specs, x_arrays = _token_specs(x, tm, nct, tile0)
    a_specs, a_arrays = _token_specs(attn, tm, nct, tile0)
    f_specs, f_arrays = _token_specs(four, tm, nct, tile0)
    in_specs = x_specs + a_specs + f_specs + [full(mod_l.shape), full((1, d)), _layer_spec(w_o, layer)]
    if router is None:
        out_specs = [tile(d), tile(d)]
        out_shape = [jax.ShapeDtypeStruct((t, d), F32), jax.ShapeDtypeStruct((t, d), BF16)]
    else:
        out_specs = [tile(d), tile(d // 2)]
        out_shape = [jax.ShapeDtypeStruct((t, d), F32), jax.ShapeDtypeStruct((t, d // 2), jnp.uint32)]
    args = x_arrays + a_arrays + f_arrays + [mod_l, g_ffn, w_o]
    if router is not None:
        in_specs += [full(router[0].shape), full(router[1].shape)]
        out_specs += [tile(LOGIT_LANES), pl.BlockSpec((8, tm), lambda i: (0, i)),
                      pl.BlockSpec((EXPERT_ROWS, LOGIT_LANES), lambda i: (0, 0))]
        out_shape += [jax.ShapeDtypeStruct((t, LOGIT_LANES), F32), jax.ShapeDtypeStruct((8, t), F32),
                      jax.ShapeDtypeStruct((EXPERT_ROWS, LOGIT_LANES), F32)]
        args += list(router)
    return pl.pallas_call(
        functools.partial(_postmix_kernel, n_x=len(x_arrays), n_a=len(a_arrays), n_f=len(f_arrays),
                          tile0=tile0, n_ctx_tiles=nct, tiles_per_batch=seq // tm,
                          with_router=router is not None),
        grid=(t // tm,),
        in_specs=in_specs,
        out_specs=out_specs,
        out_shape=out_shape,
        scratch_shapes=[pltpu.VMEM((d, d), BF16)],
        compiler_params=_cparams(1),
        name="postmix_wo_norm",
    )(*args)


def _silu_mul(gate, up):
    return (gate / (1.0 + jnp.exp(-gate))) * up


def _swiglu_chunks(xs, n_chunks, wg_of, wu_of, wd_of, acc_refs):
    if not isinstance(xs, (list, tuple)):
        xs, acc_refs = [xs], [acc_refs]
    items = [(c, r) for c in range(n_chunks) for r in range(len(xs))]
    fetched = {}

    def weight(getter, c):
        if (getter, c) not in fetched:
            fetched[(getter, c)] = getter(c)
        return fetched[(getter, c)]

    def gate_up(c, r):
        return (jnp.dot(xs[r], weight(wg_of, c), preferred_element_type=F32),
                jnp.dot(xs[r], weight(wu_of, c), preferred_element_type=F32))

    gate, up = gate_up(*items[0])
    for idx, (c, r) in enumerate(items):
        act = _silu_mul(gate, up).astype(BF16)
        if idx + 1 < len(items):
            gate, up = gate_up(*items[idx + 1])
        acc_refs[r][...] += jnp.dot(act, weight(wd_of, c), preferred_element_type=F32)


def _dense_ffn_kernel(h_ref, x1_ref, mod_ref, wg_ref, wu_ref, wd_ref, o_ref, acc_ref, *,
                      tile0, n_ctx_tiles, tiles_per_batch):
    i = pl.program_id(0) + tile0
    row = _mod_row(i, n_ctx_tiles, tiles_per_batch)
    acc_ref[...] = jnp.zeros_like(acc_ref)
    sub = lambda c: slice(c * FF_CHUNK, (c + 1) * FF_CHUNK)
    _swiglu_chunks(h_ref[...], wg_ref.shape[1] // FF_CHUNK, lambda c: wg_ref[:, sub(c)],
                   lambda c: wu_ref[:, sub(c)], lambda c: wd_ref[sub(c), :], acc_ref)
    o_ref[...] = x1_ref[...] + _mod_vec(mod_ref, row, 5) * acc_ref[...]


def _dense_ffn(h, x1, mod_l, wg, wu, wd, layer, *, n_ctx, seq, tile0):
    t, d = x1.shape
    tm = TM_DENSE
    full = lambda shape: pl.BlockSpec(shape, lambda i: (0,) * len(shape))
    tile = lambda: pl.BlockSpec((tm, d), lambda i: (i, 0))
    return pl.pallas_call(
        functools.partial(_dense_ffn_kernel, tile0=tile0 * TM_MIX // tm, n_ctx_tiles=n_ctx // tm,
                          tiles_per_batch=seq // tm),
        grid=(t // tm,),
        in_specs=[tile(), tile(), full(mod_l.shape), _layer_spec(wg, layer), _layer_spec(wu, layer),
                  _layer_spec(wd, layer)],
        out_specs=tile(),
        out_shape=jax.ShapeDtypeStruct((t, d), F32),
        scratch_shapes=[pltpu.VMEM((tm, d), F32)],
        compiler_params=_cparams(1),
        name="dense_swiglu",
    )(h, x1, mod_l, wg, wu, wd)


def _moe_ffn_kernel(be_ref, nv_ref, x_ref, wg_ref, wu_ref, wd_ref, o_ref, xb_ref, acc_ref, wgb_ref, wub_ref,
                    wdb_ref):
    i = pl.program_id(0)
    j = pl.program_id(1)
    n_valid = nv_ref[i]

    @pl.when(j == 0)
    def _():
        x = _unpack_bf16_pairs(x_ref[...])
        rows = lax.broadcasted_iota(jnp.int32, x.shape, 0)
        xb_ref[...] = jnp.where(rows < n_valid, x, jnp.zeros_like(x))
        acc_ref[...] = jnp.zeros_like(acc_ref)

    n_sub = TF_MOE // FF_CHUNK
    sub = lambda c: slice(c * FF_CHUNK, (c + 1) * FF_CHUNK)
    rows_needed = ((n_valid + TAIL_ROWS - 1) // TAIL_ROWS) * TAIL_ROWS
    n_full = rows_needed // TM_PASS
    n_tail = (rows_needed - n_full * TM_PASS) // TAIL_ROWS

    def cast_chunk(src_ref, dst_ref, idx):
        w = src_ref[(0, 0) + idx].astype(BF16)
        dst_ref[idx] = w
        return w

    cast_wg = lambda c: cast_chunk(wg_ref, wgb_ref, (slice(None), sub(c)))
    cast_wu = lambda c: cast_chunk(wu_ref, wub_ref, (slice(None), sub(c)))
    cast_wd = lambda c: cast_chunk(wd_ref, wdb_ref, (sub(c), slice(None)))
    read_wg = lambda c: wgb_ref[:, sub(c)]
    read_wu = lambda c: wub_ref[:, sub(c)]
    read_wd = lambda c: wdb_ref[sub(c), :]

    @pl.when(jnp.logical_and(n_full == 0, n_tail > 0))
    def _():
        for c in range(n_sub):
            cast_wg(c), cast_wu(c), cast_wd(c)

    for n_groups in range(1, TM_MOE // TM_PASS + 1):
        groups = [slice(r * TM_PASS, (r + 1) * TM_PASS) for r in range(n_groups)]

        @pl.when(n_full == n_groups)
        def _():
            _swiglu_chunks([xb_ref[rows, :] for rows in groups], n_sub, cast_wg, cast_wu, cast_wd,
                           [acc_ref.at[rows, :] for rows in groups])

    @pl.when(n_tail > 0)
    def _():
        def group(gi, carry):
            rows = pl.ds(pl.multiple_of(n_full * TM_PASS + gi * TAIL_ROWS, TAIL_ROWS), TAIL_ROWS)
            _swiglu_chunks(xb_ref[rows, :], n_sub, read_wg, read_wu, read_wd, acc_ref.at[rows, :])
            return carry

        lax.fori_loop(0, n_tail, group, 0)

    @pl.when(j == pl.num_programs(1) - 1)
    def _():
        o_ref[...] = _pack_bf16_pairs(acc_ref[...].astype(BF16))


def _moe_ffn(xs, block_expert, n_valid, wg, wu, wd, layer):
    n_rows = xs.shape[0]
    d = D_MODEL
    tm, tf = TM_MOE, TF_MOE
    n_blocks = n_rows // tm
    n_ff = wg.shape[3] // tf

    def ff_idx(j, nv, i):
        return jnp.where(nv[i] > 0, j, n_ff - 1)

    grid_spec = pltpu.PrefetchScalarGridSpec(
        num_scalar_prefetch=2,
        grid=(n_blocks, n_ff),
        in_specs=[
            pl.BlockSpec((tm, d // 2), lambda i, j, be, nv: (i, 0)),
            pl.BlockSpec((1, 1, d, tf), lambda i, j, be, nv: (layer, be[i], 0, ff_idx(j, nv, i))),
            pl.BlockSpec((1, 1, d, tf), lambda i, j, be, nv: (layer, be[i], 0, ff_idx(j, nv, i))),
            pl.BlockSpec((1, 1, tf, d), lambda i, j, be, nv: (layer, be[i], ff_idx(j, nv, i), 0)),
        ],
        out_specs=pl.BlockSpec((tm, d // 2), lambda i, j, be, nv: (i, 0)),
        scratch_shapes=[pltpu.VMEM((tm, d), BF16), pltpu.VMEM((tm, d), F32), pltpu.VMEM((d, tf), BF16),
                        pltpu.VMEM((d, tf), BF16), pltpu.VMEM((tf, d), BF16)],
    )
    return pl.pallas_call(
        _moe_ffn_kernel,
        grid_spec=grid_spec,
        out_shape=jax.ShapeDtypeStruct((n_rows, d // 2), jnp.uint32),
        compiler_params=_cparams(2),
        name="expert_swiglu",
    )(block_expert, n_valid, xs, wg, wu, wd)


def _sc_mesh():
    return plsc.VectorSubcoreMesh(core_axis_name="c", subcore_axis_name="s")


def _sc_params():
    return pltpu.CompilerParams(use_tc_tiling_on_sc=True)


def _sc_dispatch(h_packed, dest, n_rows):
    t, w = h_packed.shape
    win = SC_DISPATCH_ROWS
    n_win = t // win
    idx = [dest[k].reshape(n_win, 1, win) for k in range(TOP_K)]

    @functools.partial(
        pl.kernel, out_type=jax.ShapeDtypeStruct((n_rows, w), h_packed.dtype), mesh=_sc_mesh(),
        scratch_types=[], compiler_params=_sc_params(), name="expert_dispatch_scatter")
    def run(x_hbm, i0_hbm, i1_hbm, o_hbm):
        def body(x_vmem, i0_vmem, i1_vmem):
            pltpu.sync_copy(x_vmem, o_hbm.at[i0_vmem.at[0, 0]])
            pltpu.sync_copy(x_vmem, o_hbm.at[i1_vmem.at[0, 0]])

        idx_spec = pl.BlockSpec((1, 1, win), lambda i: (i, 0, 0))
        pltpu.emit_pipeline(
            body, grid=(n_win,),
            in_specs=[pl.BlockSpec((win, w), lambda i: (i, 0)), idx_spec, idx_spec],
            out_specs=[], core_axis_name=("c", "s"), dimension_semantics=(pltpu.PARALLEL,),
        )(x_hbm, i0_hbm, i1_hbm)

    return run(h_packed, *idx)


def _sc_gather(ys, idx_flat):
    w = ys.shape[1]
    n = idx_flat.shape[0]
    win = SC_GATHER_ROWS
    n_win = n // win

    @functools.partial(
        pl.kernel, out_type=jax.ShapeDtypeStruct((n, w), ys.dtype), mesh=_sc_mesh(),
        scratch_types=[], compiler_params=_sc_params(), name="expert_combine_gather")
    def run(y_hbm, i_hbm, o_hbm):
        def body(i_vmem, o_vmem):
            pltpu.sync_copy(y_hbm.at[i_vmem.at[0, 0]], o_vmem)

        pltpu.emit_pipeline(
            body, grid=(n_win,),
            in_specs=[pl.BlockSpec((1, 1, win), lambda i: (i, 0, 0))],
            out_specs=[pl.BlockSpec((win, w), lambda i: (i, 0))],
            core_axis_name=("c", "s"), dimension_semantics=(pltpu.PARALLEL,),
        )(i_hbm, o_hbm)

    return run(ys, idx_flat.reshape(n_win, 1, win))


def _moe_combine_kernel(x1_ref, y0_ref, y1_ref, g_ref, mod_ref, o_ref, *, tile0, n_ctx_tiles,
                        tiles_per_batch):
    i = pl.program_id(0) + tile0
    row = _mod_row(i, n_ctx_tiles, tiles_per_batch)
    gates = g_ref[...]
    y0 = _unpack_bf16_pairs(y0_ref[...]).astype(F32)
    y1 = _unpack_bf16_pairs(y1_ref[...]).astype(F32)
    y = y0 * gates[:, 2:3] + y1 * gates[:, 3:4]
    o_ref[...] = x1_ref[...] + _mod_vec(mod_ref, row, 5) * y


def _moe_combine(x1, y_sel, gates, mod_l, *, n_ctx, seq, tile0):
    t, d = x1.shape
    tm = TM_MIX
    n_tiles = t // tm
    tile = lambda w: pl.BlockSpec((tm, w), lambda i: (i, 0))
    return pl.pallas_call(
        functools.partial(_moe_combine_kernel, tile0=tile0, n_ctx_tiles=n_ctx // tm,
                          tiles_per_batch=seq // tm),
        grid=(n_tiles,),
        in_specs=[tile(d), tile(d // 2), pl.BlockSpec((tm, d // 2), lambda i: (n_tiles + i, 0)),
                  tile(LOGIT_LANES),
                  pl.BlockSpec(mod_l.shape, lambda i: (0, 0))],
        out_specs=tile(d),
        out_shape=jax.ShapeDtypeStruct((t, d), F32),
        compiler_params=_cparams(1),
        name="expert_combine",
    )(x1, y_sel, y_sel, gates, mod_l)


def _route(route_t, counts):
    t = route_t.shape[1]
    expert = route_t[0:TOP_K].astype(jnp.int32)
    rank = route_t[4:4 + TOP_K].astype(jnp.int32)
    counts = counts[:N_EXPERTS, 0].astype(jnp.int32)
    n_assign = t * TOP_K
    padded = ((counts + TM_MOE - 1) // TM_MOE) * TM_MOE
    pend = jnp.cumsum(padded)
    pstart = pend - padded
    dest = rank
    for e in range(N_EXPERTS):
        dest = dest + jnp.where(expert == e, pstart[e], 0)
    n_blocks = -(-n_assign // TM_MOE) + N_EXPERTS
    block_start = jnp.arange(n_blocks, dtype=jnp.int32) * TM_MOE
    block_expert = jnp.minimum(jnp.sum(pend[None, :] <= block_start[:, None], axis=1),
                               N_EXPERTS - 1).astype(jnp.int32)
    n_valid = jnp.clip(counts[block_expert] - (block_start - pstart[block_expert]), 0, TM_MOE)
    n_valid = jnp.where(block_start < pend[-1], n_valid, 0).astype(jnp.int32)
    return dest.astype(jnp.int32), block_expert, n_valid, n_blocks * TM_MOE


def kernel(x, c, ctx, c_ctx, w_mod, b_mod, g_mix, g_ffn, g_q, g_k, w_in, w_four, w_o,
           w_gate_dense, w_up_dense, w_down_dense, w_router, b_router,
           w_gate_moe, w_up_moe, w_down_moe):
    b, s, d = x.shape
    n_ctx_len = ctx.shape[1]
    n_ctx = b * n_ctx_len
    t = n_ctx + b * s
    assert d == D_MODEL and b + 1 <= MOD_ROWS
    assert n_ctx % TM_DENSE == 0 and s % TM_DENSE == 0 and n_ctx_len % 128 == 0 and n_ctx % s == 0

    cvec = jnp.zeros((MOD_ROWS, d), F32).at[0].set(c_ctx).at[1:b + 1].set(c)
    mod = _modulation(cvec, w_mod, b_mod)

    cos_t, sin_t = _rope_tables_t(s, TM_MIX)
    c_lat, s_lat, rev_lat = _dft_half_tiles(s, TR_FOUR)
    c_ctx_m, s_ctx_m = _dft_mats(n_ctx_len)
    c_grp, s_grp = _dft_mats(FOURIER_GROUP)

    xa = (ctx.reshape(n_ctx, d), x.reshape(b * s, d))

    wg_dense, wu_dense, wd_dense = (w.astype(BF16) for w in (w_gate_dense, w_up_dense, w_down_dense))

    for l in range(DEPTH):
        mod_l = mod[l]
        gq = jnp.broadcast_to((g_q[l] * (LOG2_E * HEAD_DIM ** -0.5))[:, None], (HEAD_DIM, TM_MIX))
        gk = jnp.broadcast_to(g_k[l][:, None], (HEAD_DIM, TM_MIX))
        last = l == DEPTH - 1
        tile0 = n_ctx // TM_MIX if last else 0
        q_t, k_t, v_t, f = _premix(xa, mod_l, g_mix[l].reshape(1, d), w_in, l, gq, gk, cos_t, sin_t,
                                   n_ctx=n_ctx, seq=s, t=t)

        n_qt = s // TQ_ATTN
        lat_keys = [(n_ctx_len, lambda bb: bb), (s, lambda bb: n_ctx // s + bb)]
        attn_lat = _attention(q_t, k_t, v_t, q_tile0=n_ctx // TQ_ATTN, n_q_tiles=n_qt,
                              key_blocks=lat_keys, tq=TQ_ATTN, n_batch=b, name="attention_latent")
        attn_ctx = None if last else _attention(
            q_t, k_t, v_t, q_tile0=0, n_q_tiles=1, key_blocks=[(n_ctx_len, lambda bb: bb)],
            tq=n_ctx_len, n_batch=b, name="attention_context")

        wfour = w_four[l].astype(BF16)
        four_lat = _fourier_sym(f, c_lat, s_lat, rev_lat, c_grp, s_grp, wfour, n=s, tr=TR_FOUR,
                                x_block0=n_ctx // s, n_batch=b, name="fourier_latent")
        four_ctx = None if last else _fourier(
            f, c_ctx_m, s_ctx_m, c_grp, s_grp, wfour, n=n_ctx_len, tr=n_ctx_len, x_block0=0, n_batch=b,
            name="fourier_context")

        is_moe = l % 2 == 1
        li = l // 2
        router = None
        if is_moe:
            wr = jnp.zeros((EXPERT_ROWS, d), BF16).at[:N_EXPERTS].set(w_router[li].T.astype(BF16))
            br = jnp.full((EXPERT_ROWS,), -jnp.inf, F32).at[:N_EXPERTS].set(b_router[li])
            br = jnp.broadcast_to(br[:, None], (EXPERT_ROWS, TM_MIX))
            router = (wr, br)
        res = _postmix(xa, (attn_ctx, attn_lat), (four_ctx, four_lat), mod_l, g_ffn[l].reshape(1, d),
                       w_o, l, router, n_ctx=n_ctx, seq=s, t=t, tile0=tile0)
        t_l = t - tile0 * TM_MIX
        if not is_moe:
            x1, h2 = res
            xa = _dense_ffn(h2, x1, mod_l, wg_dense, wu_dense, wd_dense, li, n_ctx=n_ctx, seq=s, tile0=tile0)
        else:
            x1, h2, route, route_t, counts = res
            dest, block_expert, n_valid, n_rows = _route(route_t, counts)
            xs = _sc_dispatch(h2, dest, n_rows)
            ys = _moe_ffn(xs, block_expert, n_valid, w_gate_moe, w_up_moe, w_down_moe, li)
            y_sel = _sc_gather(ys, dest.reshape(-1))
            xa = _moe_combine(x1, y_sel, route, mod_l, n_ctx=n_ctx, seq=s, tile0=tile0)

    return xa.reshape(b, s, d)
```

```python
import functools

import numpy as np
import jax
import jax.numpy as jnp
from jax import lax
from jax.experimental import pallas as pl
from jax.experimental.pallas import tpu as pltpu
from jax.experimental.pallas import tpu_sc as plsc

D_MODEL = 1024
DEPTH = 4
GRID_W = 64
HEAD_DIM = 64
ATTN_WIDTH = 512
N_Q_HEADS = 8
N_KV_HEADS = 2
KV_REP = 4
KV_WIDTH = 128
FOURIER_WIDTH = 512
N_FOURIER_GROUPS = 4
FOURIER_GROUP = 128
ROT_PER_AXIS = 32
ROPE_THETA = 10000.0
N_EXPERTS = 8
TOP_K = 2
EPS = 1e-6

QKV_WIDTH = ATTN_WIDTH + 2 * KV_WIDTH
MOD_ROWS = 16
LOGIT_LANES = 128
EXPERT_ROWS = 16

TM_MIX = 1024
TQ_ATTN = 512
KEY_CHUNK = 256
ONES_ROWS = 16
SCORE_LOOKAHEAD = 2
LOG2_E = 1.4426950408889634
TR_FOUR = 512
SYM_EXTRA = 16
TM_DENSE = 1024
FF_CHUNK = 256
TM_MOE = 2048
TM_PASS = 1024
TAIL_ROWS = 256
TF_MOE = 512
SC_DISPATCH_ROWS = 64
SC_GATHER_ROWS = 64
V7X_VMEM_BYTES = 64 * 1024 * 1024
VMEM_LIMIT = V7X_VMEM_BYTES - 8 * 1024 * 1024

F32 = jnp.float32
BF16 = jnp.bfloat16


def _cparams(n_axes):
    return pltpu.CompilerParams(
        dimension_semantics=("arbitrary",) * n_axes, vmem_limit_bytes=VMEM_LIMIT)


def _rope_tables_t(s, tm):
    n_rows = s // GRID_W
    rows = np.repeat(np.arange(n_rows), GRID_W).astype(np.float64)
    cols = np.tile(np.arange(GRID_W), n_rows).astype(np.float64)
    inv_freq = (ROPE_THETA ** (-np.arange(0, ROT_PER_AXIS, 2, dtype=np.float32) / ROT_PER_AXIS)
                ).astype(np.float32).astype(np.float64)
    ang_r = (rows[None, :].astype(np.float32) * inv_freq[:, None].astype(np.float32)).astype(np.float64)
    ang_c = (cols[None, :].astype(np.float32) * inv_freq[:, None].astype(np.float32)).astype(np.float64)
    cos = np.concatenate([np.cos(ang_r), np.cos(ang_r), np.cos(ang_c), np.cos(ang_c)], axis=0)
    sin = np.concatenate([-np.sin(ang_r), np.sin(ang_r), -np.sin(ang_c), np.sin(ang_c)], axis=0)
    cos = np.concatenate([np.ones((HEAD_DIM, tm)), cos], axis=1)
    sin = np.concatenate([np.zeros((HEAD_DIM, tm)), sin], axis=1)
    return jnp.asarray(cos, F32), jnp.asarray(sin, F32)


def _dft_mats(n):
    k = np.arange(n, dtype=np.int64)
    ang = 2.0 * np.pi * ((k[:, None] * k[None, :]) % n).astype(np.float64) / n
    return jnp.asarray(np.cos(ang), BF16), jnp.asarray(np.sin(ang), BF16)


def _mod_kernel(c_ref, w_ref, b_ref, o_ref):
    c = c_ref[...]
    s = (c / (1.0 + jnp.exp(-c))).astype(BF16)
    w = w_ref[0].astype(BF16)
    o_ref[0] = jnp.dot(s, w, preferred_element_type=F32) + b_ref[0]


def _modulation(cvec, w_mod, b_mod):
    depth, d, n = w_mod.shape
    tn = 1536
    return pl.pallas_call(
        _mod_kernel,
        grid=(depth, n // tn),
        in_specs=[
            pl.BlockSpec((MOD_ROWS, d), lambda l, j: (0, 0)),
            pl.BlockSpec((1, d, tn), lambda l, j: (l, 0, j)),
            pl.BlockSpec((1, 1, tn), lambda l, j: (l, 0, j)),
        ],
        out_specs=pl.BlockSpec((1, MOD_ROWS, tn), lambda l, j: (l, 0, j)),
        out_shape=jax.ShapeDtypeStruct((depth, MOD_ROWS, n), F32),
        compiler_params=_cparams(2),
        name="adaln_vectors",
    )(cvec, w_mod, b_mod.reshape(depth, 1, n))


def _mod_row(i, n_ctx_tiles, tiles_per_batch):
    lat = jnp.maximum(i - n_ctx_tiles, 0)
    return jnp.where(i < n_ctx_tiles, 0, lat // tiles_per_batch + 1)


def _mod_vec(mod_ref, row, comp):
    return mod_ref[pl.ds(row, 1), comp * D_MODEL:(comp + 1) * D_MODEL]


def _pack_bf16_pairs(xb):
    n = xb.shape[1] // 2
    bits = lax.bitcast_convert_type(xb.astype(F32), jnp.uint32)
    return (bits[:, :n] >> 16) | (bits[:, n:] & jnp.uint32(0xFFFF0000))


def _unpack_bf16_pairs(w):
    lo = lax.bitcast_convert_type(w << 16, F32)
    hi = lax.bitcast_convert_type(w & jnp.uint32(0xFFFF0000), F32)
    return jnp.concatenate([lo, hi], axis=1).astype(BF16)


def _token_specs(x, tm, n_ctx_tiles, tile0):
    if not isinstance(x, tuple):
        return [pl.BlockSpec((tm, x.shape[1]), lambda i: (i + tile0, 0))], [x]
    ctx, lat = x
    specs, arrays = [], []
    if ctx is not None:
        specs.append(pl.BlockSpec((tm, ctx.shape[1]), lambda i: (jnp.minimum(i + tile0, n_ctx_tiles - 1), 0)))
        arrays.append(ctx)
    else:
        assert tile0 >= n_ctx_tiles
    specs.append(pl.BlockSpec((tm, lat.shape[1]), lambda i: (jnp.maximum(i + tile0 - n_ctx_tiles, 0), 0)))
    arrays.append(lat)
    return specs, arrays


def _layer_spec(w, layer):
    return pl.BlockSpec((None,) + w.shape[1:], lambda i: (layer, 0, 0), pipeline_mode=pl.Buffered(1))


def _load_tokens(refs, is_ctx):
    if len(refs) == 2:
        return jnp.where(is_ctx, refs[0][...], refs[1][...])
    return refs[0][...]


def _norm_modulate(x, g, shift, scale):
    ms = jnp.mean(x * x, axis=-1, keepdims=True)
    y = x * lax.rsqrt(ms + EPS) * g
    return y * (1.0 + scale) + shift


def _premix_kernel(*refs, n_x, n_ctx_tiles, tiles_per_batch):
    x_refs = refs[:n_x]
    (mod_ref, g_ref, w_ref, gq_ref, gk_ref, cos_ref, sin_ref,
     q_ref, k_ref, v_ref, f_ref, wt_ref, wf_ref) = refs[n_x:]
    i = pl.program_id(0)

    @pl.when(i == 0)
    def _():
        wt_ref[...] = w_ref[:, :QKV_WIDTH].T.astype(BF16)
        wf_ref[...] = w_ref[:, QKV_WIDTH:].astype(BF16)

    row = _mod_row(i, n_ctx_tiles, tiles_per_batch)
    x = _load_tokens(x_refs, i < n_ctx_tiles)
    h = _norm_modulate(x, g_ref[...], _mod_vec(mod_ref, row, 0), _mod_vec(mod_ref, row, 1))
    hb = h.astype(BF16)
    f_ref[...] = jnp.dot(hb, wf_ref[...], preferred_element_type=F32).astype(BF16)
    pt = lax.dot_general(wt_ref[...], hb, (((1,), (1,)), ((), ())), preferred_element_type=F32)
    v_ref[...] = pt[ATTN_WIDTH + KV_WIDTH:, :].astype(BF16)
    cos = cos_ref[...]
    sin = sin_ref[...]

    def norm_rope(xh, gain):
        ms = jnp.mean(xh * xh, axis=0, keepdims=True)
        y = xh * lax.rsqrt(ms + EPS) * gain
        half = ROT_PER_AXIS // 2
        swapped = jnp.concatenate(
            [y[half:2 * half], y[0:half], y[3 * half:4 * half], y[2 * half:3 * half]], axis=0)
        return y * cos + swapped * sin

    gq = gq_ref[...]
    gk = gk_ref[...]
    for hh in range(N_Q_HEADS):
        q_ref[hh * HEAD_DIM:(hh + 1) * HEAD_DIM, :] = norm_rope(
            pt[hh * HEAD_DIM:(hh + 1) * HEAD_DIM, :], gq).astype(BF16)
    for hh in range(N_KV_HEADS):
        lo = ATTN_WIDTH + hh * HEAD_DIM
        k_ref[hh * HEAD_DIM:(hh + 1) * HEAD_DIM, :] = norm_rope(pt[lo:lo + HEAD_DIM, :], gk).astype(BF16)


def _premix(x, mod_l, g_mix, w_in, layer, gq, gk, cos_t, sin_t, *, n_ctx, seq, t):
    d = D_MODEL
    tm = TM_MIX
    n_ctx_tiles = n_ctx // tm
    tpb = seq // tm

    def tab_idx(i):
        lat = jnp.maximum(i - n_ctx_tiles, 0)
        return (0, jnp.where(i < n_ctx_tiles, 0, lax.rem(lat, tpb) + 1))

    full = lambda shape: pl.BlockSpec(shape, lambda i: (0,) * len(shape))
    x_specs, x_arrays = _token_specs(x, tm, n_ctx_tiles, 0)
    return pl.pallas_call(
        functools.partial(_premix_kernel, n_x=len(x_arrays), n_ctx_tiles=n_ctx_tiles, tiles_per_batch=tpb),
        grid=(t // tm,),
        in_specs=x_specs + [
            full(mod_l.shape),
            full((1, d)),
            _layer_spec(w_in, layer),
            full(gq.shape),
            full(gk.shape),
            pl.BlockSpec((HEAD_DIM, tm), tab_idx),
            pl.BlockSpec((HEAD_DIM, tm), tab_idx),
        ],
        out_specs=[
            pl.BlockSpec((ATTN_WIDTH, tm), lambda i: (0, i)),
            pl.BlockSpec((KV_WIDTH, tm), lambda i: (0, i)),
            pl.BlockSpec((KV_WIDTH, tm), lambda i: (0, i)),
            pl.BlockSpec((tm, FOURIER_WIDTH), lambda i: (i, 0)),
        ],
        out_shape=[
            jax.ShapeDtypeStruct((ATTN_WIDTH, t), BF16),
            jax.ShapeDtypeStruct((KV_WIDTH, t), BF16),
            jax.ShapeDtypeStruct((KV_WIDTH, t), BF16),
            jax.ShapeDtypeStruct((t, FOURIER_WIDTH), BF16),
        ],
        scratch_shapes=[pltpu.VMEM((QKV_WIDTH, d), BF16), pltpu.VMEM((d, FOURIER_WIDTH), BF16)],
        compiler_params=_cparams(1),
        name="premix_project",
    )(*x_arrays, mod_l, g_mix, w_in, gq, gk, cos_t, sin_t)


def _attn_kernel(*refs, n_key_blocks, tq):
    q_ref = refs[0]
    k_refs = refs[1:1 + n_key_blocks]
    v_refs = refs[1 + n_key_blocks:1 + 2 * n_key_blocks]
    o_ref = refs[1 + 2 * n_key_blocks]
    kall_ref, vg_ref, ot_ref = refs[2 + 2 * n_key_blocks:]
    g = pl.program_id(1)
    qt = pl.program_id(2)
    n_keys = kall_ref.shape[0]
    kc = KEY_CHUNK if n_keys % KEY_CHUNK == 0 else n_keys
    n_chunks = n_keys // kc
    slabs = kc // 8

    @pl.when(jnp.logical_and(g == 0, qt == 0))
    def _():
        off = 0
        for kr in k_refs:
            n = kr.shape[1]
            kall_ref[off:off + n, :] = kr[...].astype(F32).T.astype(BF16)
            off += n

    @pl.when(qt == 0)
    def _():
        g_rows = pl.ds(pl.multiple_of(g * HEAD_DIM, HEAD_DIM), HEAD_DIM)
        off = 0
        for vr in v_refs:
            n = vr.shape[1]
            vg_ref[0:HEAD_DIM, off:off + n] = vr[g_rows, :]
            off += n
        vg_ref[HEAD_DIM:, :] = jnp.ones((ONES_ROWS, n_keys), BF16)

    row_group = lax.broadcasted_iota(jnp.int32, (KV_WIDTH, tq), 0) // HEAD_DIM

    def masked_q(hh):
        qh = q_ref[hh * HEAD_DIM:(hh + 1) * HEAD_DIM, :]
        q2 = jnp.concatenate([qh, qh], axis=0)
        return jnp.where(row_group == g, q2, jnp.zeros_like(q2))

    q2s = [masked_q(hh) for hh in range(KV_REP)]
    items = [(hh, c) for hh in range(KV_REP) for c in range(n_chunks)]

    def score(item):
        hh, c = item
        return jnp.dot(kall_ref[c * kc:(c + 1) * kc, :], q2s[hh], preferred_element_type=F32)

    pending = [score(it) for it in items[:SCORE_LOOKAHEAD]]
    m = ot = None
    for i, (hh, c) in enumerate(items):
        if c == 0:
            m = jnp.full((1, tq), -jnp.inf, F32)
            ot = jnp.zeros((HEAD_DIM + ONES_ROWS, tq), F32)
        s = pending.pop(0).reshape(slabs, 8, tq)
        if i + SCORE_LOOKAHEAD < len(items):
            pending.append(score(items[i + SCORE_LOOKAHEAD]))
        m_new = jnp.maximum(m, jnp.max(jnp.max(s, axis=0), axis=0, keepdims=True))
        pb = jnp.exp2(s - m_new[None]).reshape(kc, tq).astype(BF16)
        rows = slice(c * kc, (c + 1) * kc)
        ot = jnp.exp2(m - m_new) * ot + jnp.dot(vg_ref[:, rows], pb, preferred_element_type=F32)
        m = m_new
        if c == n_chunks - 1:
            ot_ref[hh * HEAD_DIM:(hh + 1) * HEAD_DIM, :] = ot[:HEAD_DIM] / ot[HEAD_DIM:HEAD_DIM + 1]
    o_ref[...] = ot_ref[...].T.astype(BF16)


def _attention(q_t, k_t, v_t, *, q_tile0, n_q_tiles, key_blocks, tq, n_batch, name):
    n_keys = sum(c for c, _ in key_blocks)
    nkb = len(key_blocks)
    q_spec = pl.BlockSpec((KV_REP * HEAD_DIM, tq),
                          lambda b, g, i: (g, q_tile0 + b * n_q_tiles + i))
    k_specs = [pl.BlockSpec((KV_WIDTH, c), (lambda f: (lambda b, g, i: (0, f(b))))(f))
               for c, f in key_blocks]
    return pl.pallas_call(
        functools.partial(_attn_kernel, n_key_blocks=nkb, tq=tq),
        grid=(n_batch, N_KV_HEADS, n_q_tiles),
        in_specs=[q_spec] + k_specs + k_specs,
        out_specs=pl.BlockSpec((tq, KV_REP * HEAD_DIM),
                               lambda b, g, i: (b * n_q_tiles + i, g)),
        out_shape=jax.ShapeDtypeStruct((n_batch * n_q_tiles * tq, ATTN_WIDTH), BF16),
        scratch_shapes=[pltpu.VMEM((n_keys, KV_WIDTH), BF16),
                        pltpu.VMEM((HEAD_DIM + ONES_ROWS, n_keys), BF16),
                        pltpu.VMEM((KV_REP * HEAD_DIM, tq), F32)],
        compiler_params=_cparams(3),
        name=name,
    )(q_t, *([k_t] * nkb), *([v_t] * nkb))


def _fourier_kernel(c_ref, s_ref, x_ref, cc_ref, sc_ref, w_ref, o_ref, *, norm):
    x = x_ref[...]
    a = jnp.dot(c_ref[...], x, preferred_element_type=F32).astype(BF16)
    b = jnp.dot(s_ref[...], x, preferred_element_type=F32).astype(BF16)
    groups = [slice(grp * FOURIER_GROUP, (grp + 1) * FOURIER_GROUP) for grp in range(N_FOURIER_GROUPS)]
    cws, sws = _channel_maps(cc_ref[...], sc_ref[...], w_ref, norm)
    for grp, sl in enumerate(groups):
        o_ref[:, sl] = (jnp.dot(a[:, sl], cws[grp], preferred_element_type=F32)
                        - jnp.dot(b[:, sl], sws[grp], preferred_element_type=F32)).astype(BF16)


def _channel_maps(cc, sc, w_ref, norm):
    cws = [(jnp.dot(cc, w_ref[grp], preferred_element_type=F32) * norm).astype(BF16)
           for grp in range(N_FOURIER_GROUPS)]
    sws = [(jnp.dot(sc, w_ref[grp], preferred_element_type=F32) * norm).astype(BF16)
           for grp in range(N_FOURIER_GROUPS)]
    return cws, sws


def _fourier(f, cmat, smat, cc, sc, w_four, *, n, tr, x_block0, n_batch, name):
    n_row_tiles = n // tr
    return pl.pallas_call(
        functools.partial(_fourier_kernel, norm=float(1.0 / np.sqrt(n * FOURIER_GROUP))),
        grid=(n_row_tiles, n_batch),
        in_specs=[
            pl.BlockSpec((tr, n), lambda i, b: (i, 0)),
            pl.BlockSpec((tr, n), lambda i, b: (i, 0)),
            pl.BlockSpec((n, FOURIER_WIDTH), lambda i, b: (x_block0 + b, 0)),
            pl.BlockSpec(cc.shape, lambda i, b: (0, 0)),
            pl.BlockSpec(sc.shape, lambda i, b: (0, 0)),
            pl.BlockSpec(w_four.shape, lambda i, b: (0, 0, 0)),
        ],
        out_specs=pl.BlockSpec((tr, FOURIER_WIDTH), lambda i, b: (b * n_row_tiles + i, 0)),
        out_shape=jax.ShapeDtypeStruct((n_batch * n, FOURIER_WIDTH), BF16),
        compiler_params=_cparams(2),
        name=name,
    )(cmat, smat, f, cc, sc, w_four)


def _dft_half_tiles(n, tr):
    n_tiles = n // 2 // tr
    k = (np.arange(n_tiles, dtype=np.int64)[:, None] * tr
         + np.arange(tr + SYM_EXTRA, dtype=np.int64)[None, :])
    pos = np.arange(n, dtype=np.int64)
    ang = 2.0 * np.pi * ((k[:, :, None] * pos[None, None, :]) % n).astype(np.float64) / n
    rev = np.eye(tr, dtype=np.float32)[::-1]
    return jnp.asarray(np.cos(ang), BF16), jnp.asarray(np.sin(ang), BF16), jnp.asarray(rev, BF16)


def _fourier_sym_kernel(c_ref, s_ref, x_ref, cc_ref, sc_ref, w_ref, rev_ref, o_ref, cw_ref, sw_ref, *,
                        norm, tr, n_tiles):
    i = pl.program_id(1)

    @pl.when(jnp.logical_and(pl.program_id(0) == 0, i == 0))
    def _():
        cws, sws = _channel_maps(cc_ref[...], sc_ref[...], w_ref, norm)
        for grp in range(N_FOURIER_GROUPS):
            cw_ref[grp] = cws[grp]
            sw_ref[grp] = sws[grp]

    cws = [cw_ref[grp] for grp in range(N_FOURIER_GROUPS)]
    sws = [sw_ref[grp] for grp in range(N_FOURIER_GROUPS)]
    x = x_ref[...]
    a = jnp.dot(c_ref[0], x, preferred_element_type=F32).astype(BF16)
    b = jnp.dot(s_ref[0], x, preferred_element_type=F32).astype(BF16)
    groups = [slice(grp * FOURIER_GROUP, (grp + 1) * FOURIER_GROUP) for grp in range(N_FOURIER_GROUPS)]
    ps = [jnp.dot(a[:, sl], cws[grp], preferred_element_type=F32) for grp, sl in enumerate(groups)]
    qs = [jnp.dot(b[:, sl], sws[grp], preferred_element_type=F32) for grp, sl in enumerate(groups)]
    upper = jnp.concatenate([(p + q)[1:tr + 1] for p, q in zip(ps, qs)], axis=1).astype(BF16)
    hi_rows = pl.ds(pl.multiple_of((2 * n_tiles - 1 - i) * tr, tr), tr)
    o_ref[hi_rows, :] = jnp.dot(rev_ref[...], upper, preferred_element_type=F32).astype(BF16)
    lo_rows = pl.ds(pl.multiple_of(i * tr, tr), tr)
    for grp, sl in enumerate(groups):
        o_ref[lo_rows, sl] = (ps[grp] - qs[grp])[:tr].astype(BF16)


def _fourier_sym(f, c_tiles, s_tiles, rev, cc, sc, w_four, *, n, tr, x_block0, n_batch, name):
    n_tiles = n // 2 // tr
    ext = tr + SYM_EXTRA
    return pl.pallas_call(
        functools.partial(_fourier_sym_kernel, norm=float(1.0 / np.sqrt(n * FOURIER_GROUP)), tr=tr,
                          n_tiles=n_tiles),
        grid=(n_batch, n_tiles),
        in_specs=[
            pl.BlockSpec((1, ext, n), lambda b, i: (i, 0, 0)),
            pl.BlockSpec((1, ext, n), lambda b, i: (i, 0, 0)),
            pl.BlockSpec((n, FOURIER_WIDTH), lambda b, i: (x_block0 + b, 0)),
            pl.BlockSpec(cc.shape, lambda b, i: (0, 0)),
            pl.BlockSpec(sc.shape, lambda b, i: (0, 0)),
            pl.BlockSpec(w_four.shape, lambda b, i: (0, 0, 0)),
            pl.BlockSpec(rev.shape, lambda b, i: (0, 0)),
        ],
        out_specs=pl.BlockSpec((n, FOURIER_WIDTH), lambda b, i: (b, 0)),
        out_shape=jax.ShapeDtypeStruct((n_batch * n, FOURIER_WIDTH), BF16),
        scratch_shapes=[pltpu.VMEM(w_four.shape, BF16), pltpu.VMEM(w_four.shape, BF16)],
        compiler_params=_cparams(2),
        name=name,
    )(c_tiles, s_tiles, f, cc, sc, w_four, rev)


def _postmix_kernel(*refs, n_x, n_a, n_f, tile0, n_ctx_tiles, tiles_per_batch, with_router):
    x_refs, refs = refs[:n_x], refs[n_x:]
    a_refs, refs = refs[:n_a], refs[n_a:]
    f_refs, refs = refs[:n_f], refs[n_f:]
    mod_ref, g_ref, wo_ref = refs[:3]
    wob_ref = refs[-1]
    if with_router:
        wr_ref, br_ref, x1_ref, h_ref, rt_ref, rtt_ref, cnt_ref = refs[3:-1]
    else:
        x1_ref, h_ref = refs[3:-1]

    @pl.when(pl.program_id(0) == 0)
    def _():
        wob_ref[...] = wo_ref[...].astype(BF16)

    i = pl.program_id(0) + tile0
    row = _mod_row(i, n_ctx_tiles, tiles_per_batch)
    is_ctx = i < n_ctx_tiles
    a = _load_tokens(a_refs, is_ctx)
    f = _load_tokens(f_refs, is_ctx)
    mix = (jnp.dot(a, wob_ref[:ATTN_WIDTH, :], preferred_element_type=F32)
           + jnp.dot(f, wob_ref[ATTN_WIDTH:, :], preferred_element_type=F32))
    x1 = _load_tokens(x_refs, is_ctx) + _mod_vec(mod_ref, row, 2) * mix
    x1_ref[...] = x1
    h = _norm_modulate(x1, g_ref[...], _mod_vec(mod_ref, row, 3), _mod_vec(mod_ref, row, 4))
    hb = h.astype(BF16)
    if with_router:
        h_ref[...] = _pack_bf16_pairs(hb)
        logits_t = lax.dot_general(wr_ref[...], hb, (((1,), (1,)), ((), ())),
                                   preferred_element_type=F32) + br_ref[...]
        _top2_route(logits_t, rt_ref, rtt_ref, cnt_ref, first=pl.program_id(0) == 0)
    else:
        h_ref[...] = hb


def _top2_route(logits_t, rt_ref, rtt_ref, cnt_ref, first):
    ne, tm = logits_t.shape
    row = lax.broadcasted_iota(jnp.int32, (ne, tm), 0)
    neg = jnp.float32(-jnp.inf)
    m1 = jnp.max(logits_t, axis=0, keepdims=True)
    i1 = jnp.min(jnp.where(logits_t == m1, row, ne), axis=0, keepdims=True)
    rest = jnp.where(row == i1, neg, logits_t)
    m2 = jnp.max(rest, axis=0, keepdims=True)
    i2 = jnp.min(jnp.where(rest == m2, row, ne), axis=0, keepdims=True)
    e = jnp.exp(m2 - m1)
    g1 = 1.0 / (1.0 + e)
    g2 = e / (1.0 + e)
    pick1 = row == i1
    pick2 = row == i2
    picked = jnp.logical_or(pick1, pick2).astype(F32)

    @pl.when(first)
    def _():
        cnt_ref[...] = jnp.zeros_like(cnt_ref)

    blk = LOGIT_LANES
    r_i = lax.broadcasted_iota(jnp.int32, (blk, blk), 0)
    c_i = lax.broadcasted_iota(jnp.int32, (blk, blk), 1)
    upper = jnp.where(r_i < c_i, 1.0, 0.0).astype(BF16)
    run = cnt_ref[:, 0:1]
    parts = []
    for j in range(tm // blk):
        pj = picked[:, j * blk:(j + 1) * blk]
        parts.append(jnp.dot(pj.astype(BF16), upper, preferred_element_type=F32) + run)
        run = run + jnp.sum(pj, axis=1, keepdims=True)
    before = jnp.concatenate(parts, axis=1)
    cnt_ref[...] = jnp.broadcast_to(run, cnt_ref.shape)
    r1 = jnp.sum(jnp.where(pick1, before, 0.0), axis=0, keepdims=True)
    r2 = jnp.sum(jnp.where(pick2, before, 0.0), axis=0, keepdims=True)
    rec = jnp.concatenate([i1.astype(F32), i2.astype(F32), g1, g2, r1, r2, jnp.zeros((2, tm), F32)], axis=0)
    rtt_ref[...] = rec
    rt_ref[...] = jnp.concatenate([rec, jnp.zeros((LOGIT_LANES - 8, tm), F32)], axis=0).T


def _postmix(x, attn, four, mod_l, g_ffn, w_o, layer, router, *, n_ctx, seq, t, tile0):
    d = D_MODEL
    tm = TM_MIX
    nct = n_ctx // tm
    t = t - tile0 * tm
    full = lambda shape: pl.BlockSpec(shape, lambda i: (0,) * len(shape))
    tile = lambda w: pl.BlockSpec((tm, w), lambda i: (i, 0))
    x_specs, x_arrays = _token_specs(x, tm, nct, tile0)
    a_specs, a_arrays = _token_specs(attn, tm, nct, tile0)
    f_specs, f_arrays = _token_specs(four, tm, nct, tile0)
    in_specs = x_specs + a_specs + f_specs + [full(mod_l.shape), full((1, d)), _layer_spec(w_o, layer)]
    if router is None:
        out_specs = [tile(d), tile(d)]
        out_shape = [jax.ShapeDtypeStruct((t, d), F32), jax.ShapeDtypeStruct((t, d), BF16)]
    else:
        out_specs = [tile(d), tile(d // 2)]
        out_shape = [jax.ShapeDtypeStruct((t, d), F32), jax.ShapeDtypeStruct((t, d // 2), jnp.uint32)]
    args = x_arrays + a_arrays + f_arrays + [mod_l, g_ffn, w_o]
    if router is not None:
        in_specs += [full(router[0].shape), full(router[1].shape)]
        out_specs += [tile(LOGIT_LANES), pl.BlockSpec((8, tm), lambda i: (0, i)),
                      pl.BlockSpec((EXPERT_ROWS, LOGIT_LANES), lambda i: (0, 0))]
        out_shape += [jax.ShapeDtypeStruct((t, LOGIT_LANES), F32), jax.ShapeDtypeStruct((8, t), F32),
                      jax.ShapeDtypeStruct((EXPERT_ROWS, LOGIT_LANES), F32)]
        args += list(router)
    return pl.pallas_call(
        functools.partial(_postmix_kernel, n_x=len(x_arrays), n_a=len(a_arrays), n_f=len(f_arrays),
                          tile0=tile0, n_ctx_tiles=nct, tiles_per_batch=seq // tm,
                          with_router=router is not None),
        grid=(t // tm,),
        in_specs=in_specs,
        out_specs=out_specs,
        out_shape=out_shape,
        scratch_shapes=[pltpu.VMEM((d, d), BF16)],
        compiler_params=_cparams(1),
        name="postmix_wo_norm",
    )(*args)


def _silu_mul(gate, up):
    return (gate / (1.0 + jnp.exp(-gate))) * up


def _swiglu_chunks(xs, n_chunks, wg_of, wu_of, wd_of, acc_refs):
    if not isinstance(xs, (list, tuple)):
        xs, acc_refs = [xs], [acc_refs]
    items = [(c, r) for c in range(n_chunks) for r in range(len(xs))]
    fetched = {}

    def weight(getter, c):
        if (getter, c) not in fetched:
            fetched[(getter, c)] = getter(c)
        return fetched[(getter, c)]

    def gate_up(c, r):
        return (jnp.dot(xs[r], weight(wg_of, c), preferred_element_type=F32),
                jnp.dot(xs[r], weight(wu_of, c), preferred_element_type=F32))

    gate, up = gate_up(*items[0])
    for idx, (c, r) in enumerate(items):
        act = _silu_mul(gate, up).astype(BF16)
        if idx + 1 < len(items):
            gate, up = gate_up(*items[idx + 1])
        acc_refs[r][...] += jnp.dot(act, weight(wd_of, c), preferred_element_type=F32)


def _dense_ffn_kernel(h_ref, x1_ref, mod_ref, wg_ref, wu_ref, wd_ref, o_ref, acc_ref, *,
                      tile0, n_ctx_tiles, tiles_per_batch):
    i = pl.program_id(0) + tile0
    row = _mod_row(i, n_ctx_tiles, tiles_per_batch)
    acc_ref[...] = jnp.zeros_like(acc_ref)
    sub = lambda c: slice(c * FF_CHUNK, (c + 1) * FF_CHUNK)
    _swiglu_chunks(h_ref[...], wg_ref.shape[1] // FF_CHUNK, lambda c: wg_ref[:, sub(c)],
                   lambda c: wu_ref[:, sub(c)], lambda c: wd_ref[sub(c), :], acc_ref)
    o_ref[...] = x1_ref[...] + _mod_vec(mod_ref, row, 5) * acc_ref[...]


def _dense_ffn(h, x1, mod_l, wg, wu, wd, layer, *, n_ctx, seq, tile0):
    t, d = x1.shape
    tm = TM_DENSE
    full = lambda shape: pl.BlockSpec(shape, lambda i: (0,) * len(shape))
    tile = lambda: pl.BlockSpec((tm, d), lambda i: (i, 0))
    return pl.pallas_call(
        functools.partial(_dense_ffn_kernel, tile0=tile0 * TM_MIX // tm, n_ctx_tiles=n_ctx // tm,
                          tiles_per_batch=seq // tm),
        grid=(t // tm,),
        in_specs=[tile(), tile(), full(mod_l.shape), _layer_spec(wg, layer), _layer_spec(wu, layer),
                  _layer_spec(wd, layer)],
        out_specs=tile(),
        out_shape=jax.ShapeDtypeStruct((t, d), F32),
        scratch_shapes=[pltpu.VMEM((tm, d), F32)],
        compiler_params=_cparams(1),
        name="dense_swiglu",
    )(h, x1, mod_l, wg, wu, wd)


def _moe_ffn_kernel(be_ref, nv_ref, x_ref, wg_ref, wu_ref, wd_ref, o_ref, xb_ref, acc_ref, wgb_ref, wub_ref,
                    wdb_ref):
    i = pl.program_id(0)
    j = pl.program_id(1)
    n_valid = nv_ref[i]

    @pl.when(j == 0)
    def _():
        x = _unpack_bf16_pairs(x_ref[...])
        rows = lax.broadcasted_iota(jnp.int32, x.shape, 0)
        xb_ref[...] = jnp.where(rows < n_valid, x, jnp.zeros_like(x))
        acc_ref[...] = jnp.zeros_like(acc_ref)

    n_sub = TF_MOE // FF_CHUNK
    sub = lambda c: slice(c * FF_CHUNK, (c + 1) * FF_CHUNK)
    rows_needed = ((n_valid + TAIL_ROWS - 1) // TAIL_ROWS) * TAIL_ROWS
    n_full = rows_needed // TM_PASS
    n_tail = (rows_needed - n_full * TM_PASS) // TAIL_ROWS

    def cast_chunk(src_ref, dst_ref, idx):
        w = src_ref[(0, 0) + idx].astype(BF16)
        dst_ref[idx] = w
        return w

    cast_wg = lambda c: cast_chunk(wg_ref, wgb_ref, (slice(None), sub(c)))
    cast_wu = lambda c: cast_chunk(wu_ref, wub_ref, (slice(None), sub(c)))
    cast_wd = lambda c: cast_chunk(wd_ref, wdb_ref, (sub(c), slice(None)))
    read_wg = lambda c: wgb_ref[:, sub(c)]
    read_wu = lambda c: wub_ref[:, sub(c)]
    read_wd = lambda c: wdb_ref[sub(c), :]

    @pl.when(jnp.logical_and(n_full == 0, n_tail > 0))
    def _():
        for c in range(n_sub):
            cast_wg(c), cast_wu(c), cast_wd(c)

    for n_groups in range(1, TM_MOE // TM_PASS + 1):
        groups = [slice(r * TM_PASS, (r + 1) * TM_PASS) for r in range(n_groups)]

        @pl.when(n_full == n_groups)
        def _():
            _swiglu_chunks([xb_ref[rows, :] for rows in groups], n_sub, cast_wg, cast_wu, cast_wd,
                           [acc_ref.at[rows, :] for rows in groups])

    @pl.when(n_tail > 0)
    def _():
        def group(gi, carry):
            rows = pl.ds(pl.multiple_of(n_full * TM_PASS + gi * TAIL_ROWS, TAIL_ROWS), TAIL_ROWS)
            _swiglu_chunks(xb_ref[rows, :], n_sub, read_wg, read_wu, read_wd, acc_ref.at[rows, :])
            return carry

        lax.fori_loop(0, n_tail, group, 0)

    @pl.when(j == pl.num_programs(1) - 1)
    def _():
        o_ref[...] = _pack_bf16_pairs(acc_ref[...].astype(BF16))


def _moe_ffn(xs, block_expert, n_valid, wg, wu, wd, layer):
    n_rows = xs.shape[0]
    d = D_MODEL
    tm, tf = TM_MOE, TF_MOE
    n_blocks = n_rows // tm
    n_ff = wg.shape[3] // tf

    def ff_idx(j, nv, i):
        return jnp.where(nv[i] > 0, j, n_ff - 1)

    grid_spec = pltpu.PrefetchScalarGridSpec(
        num_scalar_prefetch=2,
        grid=(n_blocks, n_ff),
        in_specs=[
            pl.BlockSpec((tm, d // 2), lambda i, j, be, nv: (i, 0)),
            pl.BlockSpec((1, 1, d, tf), lambda i, j, be, nv: (layer, be[i], 0, ff_idx(j, nv, i))),
            pl.BlockSpec((1, 1, d, tf), lambda i, j, be, nv: (layer, be[i], 0, ff_idx(j, nv, i))),
            pl.BlockSpec((1, 1, tf, d), lambda i, j, be, nv: (layer, be[i], ff_idx(j, nv, i), 0)),
        ],
        out_specs=pl.BlockSpec((tm, d // 2), lambda i, j, be, nv: (i, 0)),
        scratch_shapes=[pltpu.VMEM((tm, d), BF16), pltpu.VMEM((tm, d), F32), pltpu.VMEM((d, tf), BF16),
                        pltpu.VMEM((d, tf), BF16), pltpu.VMEM((tf, d), BF16)],
    )
    return pl.pallas_call(
        _moe_ffn_kernel,
        grid_spec=grid_spec,
        out_shape=jax.ShapeDtypeStruct((n_rows, d // 2), jnp.uint32),
        compiler_params=_cparams(2),
        name="expert_swiglu",
    )(block_expert, n_valid, xs, wg, wu, wd)


def _sc_mesh():
    return plsc.VectorSubcoreMesh(core_axis_name="c", subcore_axis_name="s")


def _sc_params():
    return pltpu.CompilerParams(use_tc_tiling_on_sc=True)


def _sc_dispatch(h_packed, dest, n_rows):
    t, w = h_packed.shape
    win = SC_DISPATCH_ROWS
    n_win = t // win
    idx = [dest[k].reshape(n_win, 1, win) for k in range(TOP_K)]

    @functools.partial(
        pl.kernel, out_type=jax.ShapeDtypeStruct((n_rows, w), h_packed.dtype), mesh=_sc_mesh(),
        scratch_types=[], compiler_params=_sc_params(), name="expert_dispatch_scatter")
    def run(x_hbm, i0_hbm, i1_hbm, o_hbm):
        def body(x_vmem, i0_vmem, i1_vmem):
            pltpu.sync_copy(x_vmem, o_hbm.at[i0_vmem.at[0, 0]])
            pltpu.sync_copy(x_vmem, o_hbm.at[i1_vmem.at[0, 0]])

        idx_spec = pl.BlockSpec((1, 1, win), lambda i: (i, 0, 0))
        pltpu.emit_pipeline(
            body, grid=(n_win,),
            in_specs=[pl.BlockSpec((win, w), lambda i: (i, 0)), idx_spec, idx_spec],
            out_specs=[], core_axis_name=("c", "s"), dimension_semantics=(pltpu.PARALLEL,),
        )(x_hbm, i0_hbm, i1_hbm)

    return run(h_packed, *idx)


def _sc_gather(ys, idx_flat):
    w = ys.shape[1]
    n = idx_flat.shape[0]
    win = SC_GATHER_ROWS
    n_win = n // win

    @functools.partial(
        pl.kernel, out_type=jax.ShapeDtypeStruct((n, w), ys.dtype), mesh=_sc_mesh(),
        scratch_types=[], compiler_params=_sc_params(), name="expert_combine_gather")
    def run(y_hbm, i_hbm, o_hbm):
        def body(i_vmem, o_vmem):
            pltpu.sync_copy(y_hbm.at[i_vmem.at[0, 0]], o_vmem)

        pltpu.emit_pipeline(
            body, grid=(n_win,),
            in_specs=[pl.BlockSpec((1, 1, win), lambda i: (i, 0, 0))],
            out_specs=[pl.BlockSpec((win, w), lambda i: (i, 0))],
            core_axis_name=("c", "s"), dimension_semantics=(pltpu.PARALLEL,),
        )(i_hbm, o_hbm)

    return run(ys, idx_flat.reshape(n_win, 1, win))


def _moe_combine_kernel(x1_ref, y0_ref, y1_ref, g_ref, mod_ref, o_ref, *, tile0, n_ctx_tiles,
                        tiles_per_batch):
    i = pl.program_id(0) + tile0
    row = _mod_row(i, n_ctx_tiles, tiles_per_batch)
    gates = g_ref[...]
    y0 = _unpack_bf16_pairs(y0_ref[...]).astype(F32)
    y1 = _unpack_bf16_pairs(y1_ref[...]).astype(F32)
    y = y0 * gates[:, 2:3] + y1 * gates[:, 3:4]
    o_ref[...] = x1_ref[...] + _mod_vec(mod_ref, row, 5) * y


def _moe_combine(x1, y_sel, gates, mod_l, *, n_ctx, seq, tile0):
    t, d = x1.shape
    tm = TM_MIX
    n_tiles = t // tm
    tile = lambda w: pl.BlockSpec((tm, w), lambda i: (i, 0))
    return pl.pallas_call(
        functools.partial(_moe_combine_kernel, tile0=tile0, n_ctx_tiles=n_ctx // tm,
                          tiles_per_batch=seq // tm),
        grid=(n_tiles,),
        in_specs=[tile(d), tile(d // 2), pl.BlockSpec((tm, d // 2), lambda i: (n_tiles + i, 0)),
                  tile(LOGIT_LANES),
                  pl.BlockSpec(mod_l.shape, lambda i: (0, 0))],
        out_specs=tile(d),
        out_shape=jax.ShapeDtypeStruct((t, d), F32),
        compiler_params=_cparams(1),
        name="expert_combine",
    )(x1, y_sel, y_sel, gates, mod_l)


def _route(route_t, counts):
    t = route_t.shape[1]
    expert = route_t[0:TOP_K].astype(jnp.int32)
    rank = route_t[4:4 + TOP_K].astype(jnp.int32)
    counts = counts[:N_EXPERTS, 0].astype(jnp.int32)
    n_assign = t * TOP_K
    blocks_e = (counts + TM_MOE - 1) // TM_MOE
    per_block = -(-counts // jnp.maximum(blocks_e, 1))
    per_block = jnp.maximum(((per_block + TAIL_ROWS - 1) // TAIL_ROWS) * TAIL_ROWS, TAIL_ROWS)
    padded = blocks_e * TM_MOE
    pend = jnp.cumsum(padded)
    pstart = pend - padded
    start_of = jnp.zeros_like(rank)
    fill_of = jnp.ones_like(rank)
    for e in range(N_EXPERTS):
        start_of = start_of + jnp.where(expert == e, pstart[e], 0)
        fill_of = jnp.where(expert == e, per_block[e], fill_of)
    blk = rank // fill_of
    dest = start_of + blk * TM_MOE + (rank - blk * fill_of)
    n_blocks = -(-n_assign // TM_MOE) + N_EXPERTS
    block_start = jnp.arange(n_blocks, dtype=jnp.int32) * TM_MOE
    block_expert = jnp.minimum(jnp.sum(pend[None, :] <= block_start[:, None], axis=1),
                               N_EXPERTS - 1).astype(jnp.int32)
    local_blk = (block_start - pstart[block_expert]) // TM_MOE
    n_valid = jnp.clip(counts[block_expert] - local_blk * per_block[block_expert], 0, per_block[block_expert])
    n_valid = jnp.where(block_start < pend[-1], n_valid, 0).astype(jnp.int32)
    return dest.astype(jnp.int32), block_expert, n_valid, n_blocks * TM_MOE


def kernel(x, c, ctx, c_ctx, w_mod, b_mod, g_mix, g_ffn, g_q, g_k, w_in, w_four, w_o,
           w_gate_dense, w_up_dense, w_down_dense, w_router, b_router,
           w_gate_moe, w_up_moe, w_down_moe):
    b, s, d = x.shape
    n_ctx_len = ctx.shape[1]
    n_ctx = b * n_ctx_len
    t = n_ctx + b * s
    assert d == D_MODEL and b + 1 <= MOD_ROWS
    assert n_ctx % TM_DENSE == 0 and s % TM_DENSE == 0 and n_ctx_len % 128 == 0 and n_ctx % s == 0

    cvec = jnp.zeros((MOD_ROWS, d), F32).at[0].set(c_ctx).at[1:b + 1].set(c)
    mod = _modulation(cvec, w_mod, b_mod)

    cos_t, sin_t = _rope_tables_t(s, TM_MIX)
    c_lat, s_lat, rev_lat = _dft_half_tiles(s, TR_FOUR)
    c_ctx_m, s_ctx_m = _dft_mats(n_ctx_len)
    c_grp, s_grp = _dft_mats(FOURIER_GROUP)

    xa = (ctx.reshape(n_ctx, d), x.reshape(b * s, d))

    wg_dense, wu_dense, wd_dense = (w.astype(BF16) for w in (w_gate_dense, w_up_dense, w_down_dense))

    for l in range(DEPTH):
        mod_l = mod[l]
        gq = jnp.broadcast_to((g_q[l] * (LOG2_E * HEAD_DIM ** -0.5))[:, None], (HEAD_DIM, TM_MIX))
        gk = jnp.broadcast_to(g_k[l][:, None], (HEAD_DIM, TM_MIX))
        last = l == DEPTH - 1
        tile0 = n_ctx // TM_MIX if last else 0
        q_t, k_t, v_t, f = _premix(xa, mod_l, g_mix[l].reshape(1, d), w_in, l, gq, gk, cos_t, sin_t,
                                   n_ctx=n_ctx, seq=s, t=t)

        n_qt = s // TQ_ATTN
        lat_keys = [(n_ctx_len, lambda bb: bb), (s, lambda bb: n_ctx // s + bb)]
        attn_lat = _attention(q_t, k_t, v_t, q_tile0=n_ctx // TQ_ATTN, n_q_tiles=n_qt,
                              key_blocks=lat_keys, tq=TQ_ATTN, n_batch=b, name="attention_latent")
        attn_ctx = None if last else _attention(
            q_t, k_t, v_t, q_tile0=0, n_q_tiles=1, key_blocks=[(n_ctx_len, lambda bb: bb)],
            tq=n_ctx_len, n_batch=b, name="attention_context")

        wfour = w_four[l].astype(BF16)
        four_lat = _fourier_sym(f, c_lat, s_lat, rev_lat, c_grp, s_grp, wfour, n=s, tr=TR_FOUR,
                                x_block0=n_ctx // s, n_batch=b, name="fourier_latent")
        four_ctx = None if last else _fourier(
            f, c_ctx_m, s_ctx_m, c_grp, s_grp, wfour, n=n_ctx_len, tr=n_ctx_len, x_block0=0, n_batch=b,
            name="fourier_context")

        is_moe = l % 2 == 1
        li = l // 2
        router = None
        if is_moe:
            wr = jnp.zeros((EXPERT_ROWS, d), BF16).at[:N_EXPERTS].set(w_router[li].T.astype(BF16))
            br = jnp.full((EXPERT_ROWS,), -jnp.inf, F32).at[:N_EXPERTS].set(b_router[li])
            br = jnp.broadcast_to(br[:, None], (EXPERT_ROWS, TM_MIX))
            router = (wr, br)
        res = _postmix(xa, (attn_ctx, attn_lat), (four_ctx, four_lat), mod_l, g_ffn[l].reshape(1, d),
                       w_o, l, router, n_ctx=n_ctx, seq=s, t=t, tile0=tile0)
        t_l = t - tile0 * TM_MIX
        if not is_moe:
            x1, h2 = res
            xa = _dense_ffn(h2, x1, mod_l, wg_dense, wu_dense, wd_dense, li, n_ctx=n_ctx, seq=s, tile0=tile0)
        else:
            x1, h2, route, route_t, counts = res
            dest, block_expert, n_valid, n_rows = _route(route_t, counts)
            xs = _sc_dispatch(h2, dest, n_rows)
            ys = _moe_ffn(xs, block_expert, n_valid, w_gate_moe, w_up_moe, w_down_moe, li)
            y_sel = _sc_gather(ys, dest.reshape(-1))
            xa = _moe_combine(x1, y_sel, route, mod_l, n_ctx=n_ctx, seq=s, tile0=tile0)

    return xa.reshape(b, s, d)
```

```python
import functools

import numpy as np
import jax
import jax.numpy as jnp
from jax import lax
from jax.experimental import pallas as pl
from jax.experimental.pallas import tpu as pltpu
from jax.experimental.pallas import tpu_sc as plsc

D_MODEL = 1024
DEPTH = 4
GRID_W = 64
HEAD_DIM = 64
ATTN_WIDTH = 512
N_Q_HEADS = 8
N_KV_HEADS = 2
KV_REP = 4
KV_WIDTH = 128
FOURIER_WIDTH = 512
N_FOURIER_GROUPS = 4
FOURIER_GROUP = 128
ROT_PER_AXIS = 32
ROPE_THETA = 10000.0
N_EXPERTS = 8
TOP_K = 2
EPS = 1e-6

QKV_WIDTH = ATTN_WIDTH + 2 * KV_WIDTH
MOD_ROWS = 16
LOGIT_LANES = 128
EXPERT_ROWS = 16

TM_MIX = 1024
TQ_ATTN = 512
KEY_CHUNK = 256
ONES_ROWS = 16
SCORE_LOOKAHEAD = 2
LOG2_E = 1.4426950408889634
TR_FOUR = 512
SYM_EXTRA = 16
TM_DENSE = 1024
FF_CHUNK = 256
TM_MOE = 2048
TM_PASS = 1024
TAIL_ROWS = 256
TF_MOE = 512
SC_DISPATCH_ROWS = 64
SC_GATHER_ROWS = 64
V7X_VMEM_BYTES = 64 * 1024 * 1024
VMEM_LIMIT = V7X_VMEM_BYTES - 8 * 1024 * 1024

F32 = jnp.float32
BF16 = jnp.bfloat16


def _cparams(n_axes):
    return pltpu.CompilerParams(
        dimension_semantics=("arbitrary",) * n_axes, vmem_limit_bytes=VMEM_LIMIT)


def _rope_tables_t(s, tm):
    n_rows = s // GRID_W
    rows = np.repeat(np.arange(n_rows), GRID_W).astype(np.float64)
    cols = np.tile(np.arange(GRID_W), n_rows).astype(np.float64)
    inv_freq = (ROPE_THETA ** (-np.arange(0, ROT_PER_AXIS, 2, dtype=np.float32) / ROT_PER_AXIS)
                ).astype(np.float32).astype(np.float64)
    ang_r = (rows[None, :].astype(np.float32) * inv_freq[:, None].astype(np.float32)).astype(np.float64)
    ang_c = (cols[None, :].astype(np.float32) * inv_freq[:, None].astype(np.float32)).astype(np.float64)
    cos = np.concatenate([np.cos(ang_r), np.cos(ang_r), np.cos(ang_c), np.cos(ang_c)], axis=0)
    sin = np.concatenate([-np.sin(ang_r), np.sin(ang_r), -np.sin(ang_c), np.sin(ang_c)], axis=0)
    cos = np.concatenate([np.ones((HEAD_DIM, tm)), cos], axis=1)
    sin = np.concatenate([np.zeros((HEAD_DIM, tm)), sin], axis=1)
    return jnp.asarray(cos, F32), jnp.asarray(sin, F32)


def _dft_mats(n):
    k = np.arange(n, dtype=np.int64)
    ang = 2.0 * np.pi * ((k[:, None] * k[None, :]) % n).astype(np.float64) / n
    return jnp.asarray(np.cos(ang), BF16), jnp.asarray(np.sin(ang), BF16)


def _mod_kernel(c_ref, w_ref, b_ref, o_ref):
    c = c_ref[...]
    s = (c / (1.0 + jnp.exp(-c))).astype(BF16)
    w = w_ref[0].astype(BF16)
    o_ref[0] = jnp.dot(s, w, preferred_element_type=F32) + b_ref[0]


def _modulation(cvec, w_mod, b_mod):
    depth, d, n = w_mod.shape
    tn = 1536
    return pl.pallas_call(
        _mod_kernel,
        grid=(depth, n // tn),
        in_specs=[
            pl.BlockSpec((MOD_ROWS, d), lambda l, j: (0, 0)),
            pl.BlockSpec((1, d, tn), lambda l, j: (l, 0, j)),
            pl.BlockSpec((1, 1, tn), lambda l, j: (l, 0, j)),
        ],
        out_specs=pl.BlockSpec((1, MOD_ROWS, tn), lambda l, j: (l, 0, j)),
        out_shape=jax.ShapeDtypeStruct((depth, MOD_ROWS, n), F32),
        compiler_params=_cparams(2),
        name="adaln_vectors",
    )(cvec, w_mod, b_mod.reshape(depth, 1, n))


def _mod_row(i, n_ctx_tiles, tiles_per_batch):
    lat = jnp.maximum(i - n_ctx_tiles, 0)
    return jnp.where(i < n_ctx_tiles, 0, lat // tiles_per_batch + 1)


def _mod_vec(mod_ref, row, comp):
    return mod_ref[pl.ds(row, 1), comp * D_MODEL:(comp + 1) * D_MODEL]


def _pack_bf16_pairs(xb):
    n = xb.shape[1] // 2
    bits = lax.bitcast_convert_type(xb.astype(F32), jnp.uint32)
    return (bits[:, :n] >> 16) | (bits[:, n:] & jnp.uint32(0xFFFF0000))


def _unpack_bf16_pairs(w):
    lo = lax.bitcast_convert_type(w << 16, F32)
    hi = lax.bitcast_convert_type(w & jnp.uint32(0xFFFF0000), F32)
    return jnp.concatenate([lo, hi], axis=1).astype(BF16)


def _token_specs(x, tm, n_ctx_tiles, tile0):
    if not isinstance(x, tuple):
        return [pl.BlockSpec((tm, x.shape[1]), lambda i: (i + tile0, 0))], [x]
    ctx, lat = x
    specs, arrays = [], []
    if ctx is not None:
        specs.append(pl.BlockSpec((tm, ctx.shape[1]), lambda i: (jnp.minimum(i + tile0, n_ctx_tiles - 1), 0)))
        arrays.append(ctx)
    else:
        assert tile0 >= n_ctx_tiles
    specs.append(pl.BlockSpec((tm, lat.shape[1]), lambda i: (jnp.maximum(i + tile0 - n_ctx_tiles, 0), 0)))
    arrays.append(lat)
    return specs, arrays


def _layer_spec(w, layer):
    return pl.BlockSpec((None,) + w.shape[1:], lambda i: (layer, 0, 0), pipeline_mode=pl.Buffered(1))


def _load_tokens(refs, is_ctx):
    if len(refs) == 2:
        return jnp.where(is_ctx, refs[0][...], refs[1][...])
    return refs[0][...]


def _norm_modulate(x, g, shift, scale):
    ms = jnp.mean(x * x, axis=-1, keepdims=True)
    y = x * lax.rsqrt(ms + EPS) * g
    return y * (1.0 + scale) + shift


def _premix_kernel(*refs, n_x, n_ctx_tiles, tiles_per_batch):
    x_refs = refs[:n_x]
    (mod_ref, g_ref, w_ref, gq_ref, gk_ref, cos_ref, sin_ref,
     q_ref, k_ref, v_ref, f_ref, wt_ref, wf_ref) = refs[n_x:]
    i = pl.program_id(0)

    @pl.when(i == 0)
    def _():
        wt_ref[...] = w_ref[:, :QKV_WIDTH].T.astype(BF16)
        wf_ref[...] = w_ref[:, QKV_WIDTH:].astype(BF16)

    row = _mod_row(i, n_ctx_tiles, tiles_per_batch)
    x = _load_tokens(x_refs, i < n_ctx_tiles)
    h = _norm_modulate(x, g_ref[...], _mod_vec(mod_ref, row, 0), _mod_vec(mod_ref, row, 1))
    hb = h.astype(BF16)
    f_ref[...] = jnp.dot(hb, wf_ref[...], preferred_element_type=F32).astype(BF16)
    pt = lax.dot_general(wt_ref[...], hb, (((1,), (1,)), ((), ())), preferred_element_type=F32)
    v_ref[...] = pt[ATTN_WIDTH + KV_WIDTH:, :].astype(BF16)
    cos = cos_ref[...]
    sin = sin_ref[...]

    def norm_rope(xh, gain):
        ms = jnp.mean(xh * xh, axis=0, keepdims=True)
        y = xh * lax.rsqrt(ms + EPS) * gain
        half = ROT_PER_AXIS // 2
        swapped = jnp.concatenate(
            [y[half:2 * half], y[0:half], y[3 * half:4 * half], y[2 * half:3 * half]], axis=0)
        return y * cos + swapped * sin

    gq = gq_ref[...]
    gk = gk_ref[...]
    for hh in range(N_Q_HEADS):
        q_ref[hh * HEAD_DIM:(hh + 1) * HEAD_DIM, :] = norm_rope(
            pt[hh * HEAD_DIM:(hh + 1) * HEAD_DIM, :], gq).astype(BF16)
    for hh in range(N_KV_HEADS):
        lo = ATTN_WIDTH + hh * HEAD_DIM
        k_ref[hh * HEAD_DIM:(hh + 1) * HEAD_DIM, :] = norm_rope(pt[lo:lo + HEAD_DIM, :], gk).astype(BF16)


def _premix(x, mod_l, g_mix, w_in, layer, gq, gk, cos_t, sin_t, *, n_ctx, seq, t):
    d = D_MODEL
    tm = TM_MIX
    n_ctx_tiles = n_ctx // tm
    tpb = seq // tm

    def tab_idx(i):
        lat = jnp.maximum(i - n_ctx_tiles, 0)
        return (0, jnp.where(i < n_ctx_tiles, 0, lax.rem(lat, tpb) + 1))

    full = lambda shape: pl.BlockSpec(shape, lambda i: (0,) * len(shape))
    x_specs, x_arrays = _token_specs(x, tm, n_ctx_tiles, 0)
    return pl.pallas_call(
        functools.partial(_premix_kernel, n_x=len(x_arrays), n_ctx_tiles=n_ctx_tiles, tiles_per_batch=tpb),
        grid=(t // tm,),
        in_specs=x_specs + [
            full(mod_l.shape),
            full((1, d)),
            _layer_spec(w_in, layer),
            full(gq.shape),
            full(gk.shape),
            pl.BlockSpec((HEAD_DIM, tm), tab_idx),
            pl.BlockSpec((HEAD_DIM, tm), tab_idx),
        ],
        out_specs=[
            pl.BlockSpec((ATTN_WIDTH, tm), lambda i: (0, i)),
            pl.BlockSpec((KV_WIDTH, tm), lambda i: (0, i)),
            pl.BlockSpec((KV_WIDTH, tm), lambda i: (0, i)),
            pl.BlockSpec((tm, FOURIER_WIDTH), lambda i: (i, 0)),
        ],
        out_shape=[
            jax.ShapeDtypeStruct((ATTN_WIDTH, t), BF16),
            jax.ShapeDtypeStruct((KV_WIDTH, t), BF16),
            jax.ShapeDtypeStruct((KV_WIDTH, t), BF16),
            jax.ShapeDtypeStruct((t, FOURIER_WIDTH), BF16),
        ],
        scratch_shapes=[pltpu.VMEM((QKV_WIDTH, d), BF16), pltpu.VMEM((d, FOURIER_WIDTH), BF16)],
        compiler_params=_cparams(1),
        name="premix_project",
    )(*x_arrays, mod_l, g_mix, w_in, gq, gk, cos_t, sin_t)


def _attn_kernel(*refs, n_key_blocks, tq):
    q_ref = refs[0]
    k_refs = refs[1:1 + n_key_blocks]
    v_refs = refs[1 + n_key_blocks:1 + 2 * n_key_blocks]
    o_ref = refs[1 + 2 * n_key_blocks]
    kall_ref, vg_ref, ot_ref = refs[2 + 2 * n_key_blocks:]
    qt = pl.program_id(1)
    n_keys = kall_ref.shape[0]
    kc = KEY_CHUNK if n_keys % KEY_CHUNK == 0 else n_keys
    n_chunks = n_keys // kc
    slabs = kc // 8

    @pl.when(qt == 0)
    def _():
        off = 0
        for kr, vr in zip(k_refs, v_refs):
            n = kr.shape[1]
            kall_ref[off:off + n, :] = kr[...].astype(F32).T.astype(BF16)
            for g in range(N_KV_HEADS):
                vg_ref[g, 0:HEAD_DIM, off:off + n] = vr[g * HEAD_DIM:(g + 1) * HEAD_DIM, :]
            off += n
        for g in range(N_KV_HEADS):
            vg_ref[g, HEAD_DIM:, :] = jnp.ones((ONES_ROWS, n_keys), BF16)

    row_group = lax.broadcasted_iota(jnp.int32, (KV_WIDTH, tq), 0) // HEAD_DIM

    def masked_q(hh):
        qh = q_ref[hh * HEAD_DIM:(hh + 1) * HEAD_DIM, :]
        q2 = jnp.concatenate([qh, qh], axis=0)
        return jnp.where(row_group == hh // KV_REP, q2, jnp.zeros_like(q2))

    n_heads = N_KV_HEADS * KV_REP
    q2s = [masked_q(hh) for hh in range(n_heads)]
    items = [(hh, c) for hh in range(n_heads) for c in range(n_chunks)]

    def score(item):
        hh, c = item
        return jnp.dot(kall_ref[c * kc:(c + 1) * kc, :], q2s[hh], preferred_element_type=F32)

    pending = [score(it) for it in items[:SCORE_LOOKAHEAD]]
    m = ot = None
    for i, (hh, c) in enumerate(items):
        if c == 0:
            m = jnp.full((1, tq), -jnp.inf, F32)
            ot = jnp.zeros((HEAD_DIM + ONES_ROWS, tq), F32)
        s = pending.pop(0).reshape(slabs, 8, tq)
        if i + SCORE_LOOKAHEAD < len(items):
            pending.append(score(items[i + SCORE_LOOKAHEAD]))
        m_new = jnp.maximum(m, jnp.max(jnp.max(s, axis=0), axis=0, keepdims=True))
        pb = jnp.exp2(s - m_new[None]).reshape(kc, tq).astype(BF16)
        rows = slice(c * kc, (c + 1) * kc)
        ot = jnp.exp2(m - m_new) * ot + jnp.dot(vg_ref[hh // KV_REP, :, rows], pb, preferred_element_type=F32)
        m = m_new
        if c == n_chunks - 1:
            ot_ref[hh * HEAD_DIM:(hh + 1) * HEAD_DIM, :] = ot[:HEAD_DIM] / ot[HEAD_DIM:HEAD_DIM + 1]
    o_ref[...] = ot_ref[...].T.astype(BF16)


def _attention(q_t, k_t, v_t, *, q_tile0, n_q_tiles, key_blocks, tq, n_batch, name):
    n_keys = sum(c for c, _ in key_blocks)
    nkb = len(key_blocks)
    q_spec = pl.BlockSpec((ATTN_WIDTH, tq), lambda b, i: (0, q_tile0 + b * n_q_tiles + i))
    k_specs = [pl.BlockSpec((KV_WIDTH, c), (lambda f: (lambda b, i: (0, f(b))))(f))
               for c, f in key_blocks]
    return pl.pallas_call(
        functools.partial(_attn_kernel, n_key_blocks=nkb, tq=tq),
        grid=(n_batch, n_q_tiles),
        in_specs=[q_spec] + k_specs + k_specs,
        out_specs=pl.BlockSpec((tq, ATTN_WIDTH), lambda b, i: (b * n_q_tiles + i, 0)),
        out_shape=jax.ShapeDtypeStruct((n_batch * n_q_tiles * tq, ATTN_WIDTH), BF16),
        scratch_shapes=[pltpu.VMEM((n_keys, KV_WIDTH), BF16),
                        pltpu.VMEM((N_KV_HEADS, HEAD_DIM + ONES_ROWS, n_keys), BF16),
                        pltpu.VMEM((ATTN_WIDTH, tq), F32)],
        compiler_params=_cparams(2),
        name=name,
    )(q_t, *([k_t] * nkb), *([v_t] * nkb))


def _fourier_kernel(c_ref, s_ref, x_ref, cc_ref, sc_ref, w_ref, o_ref, *, norm):
    x = x_ref[...]
    a = jnp.dot(c_ref[...], x, preferred_element_type=F32).astype(BF16)
    b = jnp.dot(s_ref[...], x, preferred_element_type=F32).astype(BF16)
    groups = [slice(grp * FOURIER_GROUP, (grp + 1) * FOURIER_GROUP) for grp in range(N_FOURIER_GROUPS)]
    cws, sws = _channel_maps(cc_ref[...], sc_ref[...], w_ref, norm)
    for grp, sl in enumerate(groups):
        o_ref[:, sl] = (jnp.dot(a[:, sl], cws[grp], preferred_element_type=F32)
                        - jnp.dot(b[:, sl], sws[grp], preferred_element_type=F32)).astype(BF16)


def _channel_maps(cc, sc, w_ref, norm):
    cws = [(jnp.dot(cc, w_ref[grp], preferred_element_type=F32) * norm).astype(BF16)
           for grp in range(N_FOURIER_GROUPS)]
    sws = [(jnp.dot(sc, w_ref[grp], preferred_element_type=F32) * norm).astype(BF16)
           for grp in range(N_FOURIER_GROUPS)]
    return cws, sws


def _fourier(f, cmat, smat, cc, sc, w_four, *, n, tr, x_block0, n_batch, name):
    n_row_tiles = n // tr
    return pl.pallas_call(
        functools.partial(_fourier_kernel, norm=float(1.0 / np.sqrt(n * FOURIER_GROUP))),
        grid=(n_row_tiles, n_batch),
        in_specs=[
            pl.BlockSpec((tr, n), lambda i, b: (i, 0)),
            pl.BlockSpec((tr, n), lambda i, b: (i, 0)),
            pl.BlockSpec((n, FOURIER_WIDTH), lambda i, b: (x_block0 + b, 0)),
            pl.BlockSpec(cc.shape, lambda i, b: (0, 0)),
            pl.BlockSpec(sc.shape, lambda i, b: (0, 0)),
            pl.BlockSpec(w_four.shape, lambda i, b: (0, 0, 0)),
        ],
        out_specs=pl.BlockSpec((tr, FOURIER_WIDTH), lambda i, b: (b * n_row_tiles + i, 0)),
        out_shape=jax.ShapeDtypeStruct((n_batch * n, FOURIER_WIDTH), BF16),
        compiler_params=_cparams(2),
        name=name,
    )(cmat, smat, f, cc, sc, w_four)


def _dft_half_tiles(n, tr):
    n_tiles = n // 2 // tr
    k = (np.arange(n_tiles, dtype=np.int64)[:, None] * tr
         + np.arange(tr + SYM_EXTRA, dtype=np.int64)[None, :])
    pos = np.arange(n, dtype=np.int64)
    ang = 2.0 * np.pi * ((k[:, :, None] * pos[None, None, :]) % n).astype(np.float64) / n
    rev = np.eye(tr, dtype=np.float32)[::-1]
    return jnp.asarray(np.cos(ang), BF16), jnp.asarray(np.sin(ang), BF16), jnp.asarray(rev, BF16)


def _fourier_sym_kernel(c_ref, s_ref, x_ref, cc_ref, sc_ref, w_ref, rev_ref, o_ref, cw_ref, sw_ref, *,
                        norm, tr, n_tiles):
    i = pl.program_id(1)

    @pl.when(jnp.logical_and(pl.program_id(0) == 0, i == 0))
    def _():
        cws, sws = _channel_maps(cc_ref[...], sc_ref[...], w_ref, norm)
        for grp in range(N_FOURIER_GROUPS):
            cw_ref[grp] = cws[grp]
            sw_ref[grp] = sws[grp]

    cws = [cw_ref[grp] for grp in range(N_FOURIER_GROUPS)]
    sws = [sw_ref[grp] for grp in range(N_FOURIER_GROUPS)]
    x = x_ref[...]
    a = jnp.dot(c_ref[0], x, preferred_element_type=F32).astype(BF16)
    b = jnp.dot(s_ref[0], x, preferred_element_type=F32).astype(BF16)
    groups = [slice(grp * FOURIER_GROUP, (grp + 1) * FOURIER_GROUP) for grp in range(N_FOURIER_GROUPS)]
    ps = [jnp.dot(a[:, sl], cws[grp], preferred_element_type=F32) for grp, sl in enumerate(groups)]
    qs = [jnp.dot(b[:, sl], sws[grp], preferred_element_type=F32) for grp, sl in enumerate(groups)]
    upper = jnp.concatenate([(p + q)[1:tr + 1] for p, q in zip(ps, qs)], axis=1).astype(BF16)
    hi_rows = pl.ds(pl.multiple_of((2 * n_tiles - 1 - i) * tr, tr), tr)
    o_ref[hi_rows, :] = jnp.dot(rev_ref[...], upper, preferred_element_type=F32).astype(BF16)
    lo_rows = pl.ds(pl.multiple_of(i * tr, tr), tr)
    for grp, sl in enumerate(groups):
        o_ref[lo_rows, sl] = (ps[grp] - qs[grp])[:tr].astype(BF16)


def _fourier_sym(f, c_tiles, s_tiles, rev, cc, sc, w_four, *, n, tr, x_block0, n_batch, name):
    n_tiles = n // 2 // tr
    ext = tr + SYM_EXTRA
    return pl.pallas_call(
        functools.partial(_fourier_sym_kernel, norm=float(1.0 / np.sqrt(n * FOURIER_GROUP)), tr=tr,
                          n_tiles=n_tiles),
        grid=(n_batch, n_tiles),
        in_specs=[
            pl.BlockSpec((1, ext, n), lambda b, i: (i, 0, 0)),
            pl.BlockSpec((1, ext, n), lambda b, i: (i, 0, 0)),
            pl.BlockSpec((n, FOURIER_WIDTH), lambda b, i: (x_block0 + b, 0)),
            pl.BlockSpec(cc.shape, lambda b, i: (0, 0)),
            pl.BlockSpec(sc.shape, lambda b, i: (0, 0)),
            pl.BlockSpec(w_four.shape, lambda b, i: (0, 0, 0)),
            pl.BlockSpec(rev.shape, lambda b, i: (0, 0)),
        ],
        out_specs=pl.BlockSpec((n, FOURIER_WIDTH), lambda b, i: (b, 0)),
        out_shape=jax.ShapeDtypeStruct((n_batch * n, FOURIER_WIDTH), BF16),
        scratch_shapes=[pltpu.VMEM(w_four.shape, BF16), pltpu.VMEM(w_four.shape, BF16)],
        compiler_params=_cparams(2),
        name=name,
    )(c_tiles, s_tiles, f, cc, sc, w_four, rev)


def _postmix_kernel(*refs, n_x, n_a, n_f, tile0, n_ctx_tiles, tiles_per_batch, with_router):
    x_refs, refs = refs[:n_x], refs[n_x:]
    a_refs, refs = refs[:n_a], refs[n_a:]
    f_refs, refs = refs[:n_f], refs[n_f:]
    mod_ref, g_ref, wo_ref = refs[:3]
    wob_ref = refs[-1]
    if with_router:
        wr_ref, br_ref, x1_ref, h_ref, rt_ref, rtt_ref, cnt_ref = refs[3:-1]
    else:
        x1_ref, h_ref = refs[3:-1]

    @pl.when(pl.program_id(0) == 0)
    def _():
        wob_ref[...] = wo_ref[...].astype(BF16)

    i = pl.program_id(0) + tile0
    row = _mod_row(i, n_ctx_tiles, tiles_per_batch)
    is_ctx = i < n_ctx_tiles
    a = _load_tokens(a_refs, is_ctx)
    f = _load_tokens(f_refs, is_ctx)
    mix = (jnp.dot(a, wob_ref[:ATTN_WIDTH, :], preferred_element_type=F32)
           + jnp.dot(f, wob_ref[ATTN_WIDTH:, :], preferred_element_type=F32))
    x1 = _load_tokens(x_refs, is_ctx) + _mod_vec(mod_ref, row, 2) * mix
    x1_ref[...] = x1
    h = _norm_modulate(x1, g_ref[...], _mod_vec(mod_ref, row, 3), _mod_vec(mod_ref, row, 4))
    hb = h.astype(BF16)
    if with_router:
        h_ref[...] = _pack_bf16_pairs(hb)
        logits_t = lax.dot_general(wr_ref[...], hb, (((1,), (1,)), ((), ())),
                                   preferred_element_type=F32) + br_ref[...]
        _top2_route(logits_t, rt_ref, rtt_ref, cnt_ref, first=pl.program_id(0) == 0)
    else:
        h_ref[...] = hb


def _top2_route(logits_t, rt_ref, rtt_ref, cnt_ref, first):
    ne, tm = logits_t.shape
    row = lax.broadcasted_iota(jnp.int32, (ne, tm), 0)
    neg = jnp.float32(-jnp.inf)
    m1 = jnp.max(logits_t, axis=0, keepdims=True)
    i1 = jnp.min(jnp.where(logits_t == m1, row, ne), axis=0, keepdims=True)
    rest = jnp.where(row == i1, neg, logits_t)
    m2 = jnp.max(rest, axis=0, keepdims=True)
    i2 = jnp.min(jnp.where(rest == m2, row, ne), axis=0, keepdims=True)
    e = jnp.exp(m2 - m1)
    g1 = 1.0 / (1.0 + e)
    g2 = e / (1.0 + e)
    pick1 = row == i1
    pick2 = row == i2
    picked = jnp.logical_or(pick1, pick2).astype(F32)

    @pl.when(first)
    def _():
        cnt_ref[...] = jnp.zeros_like(cnt_ref)

    blk = LOGIT_LANES
    r_i = lax.broadcasted_iota(jnp.int32, (blk, blk), 0)
    c_i = lax.broadcasted_iota(jnp.int32, (blk, blk), 1)
    upper = jnp.where(r_i < c_i, 1.0, 0.0).astype(BF16)
    run = cnt_ref[:, 0:1]
    parts = []
    for j in range(tm // blk):
        pj = picked[:, j * blk:(j + 1) * blk]
        parts.append(jnp.dot(pj.astype(BF16), upper, preferred_element_type=F32) + run)
        run = run + jnp.sum(pj, axis=1, keepdims=True)
    before = jnp.concatenate(parts, axis=1)
    cnt_ref[...] = jnp.broadcast_to(run, cnt_ref.shape)
    r1 = jnp.sum(jnp.where(pick1, before, 0.0), axis=0, keepdims=True)
    r2 = jnp.sum(jnp.where(pick2, before, 0.0), axis=0, keepdims=True)
    rec = jnp.concatenate([i1.astype(F32), i2.astype(F32), g1, g2, r1, r2, jnp.zeros((2, tm), F32)], axis=0)
    rtt_ref[...] = rec
    rt_ref[...] = jnp.concatenate([rec, jnp.zeros((LOGIT_LANES - 8, tm), F32)], axis=0).T


def _postmix(x, attn, four, mod_l, g_ffn, w_o, layer, router, *, n_ctx, seq, t, tile0):
    d = D_MODEL
    tm = TM_MIX
    nct = n_ctx // tm
    t = t - tile0 * tm
    full = lambda shape: pl.BlockSpec(shape, lambda i: (0,) * len(shape))
    tile = lambda w: pl.BlockSpec((tm, w), lambda i: (i, 0))
    x_specs, x_arrays = _token_specs(x, tm, nct, tile0)
    a_specs, a_arrays = _token_specs(attn, tm, nct, tile0)
    f_specs, f_arrays = _token_specs(four, tm, nct, tile0)
    in_specs = x_specs + a_specs + f_specs + [full(mod_l.shape), full((1, d)), _layer_spec(w_o, layer)]
    if router is None:
        out_specs = [tile(d), tile(d)]
        out_shape = [jax.ShapeDtypeStruct((t, d), F32), jax.ShapeDtypeStruct((t, d), BF16)]
    else:
        out_specs = [tile(d), tile(d // 2)]
        out_shape = [jax.ShapeDtypeStruct((t, d), F32), jax.ShapeDtypeStruct((t, d // 2), jnp.uint32)]
    args = x_arrays + a_arrays + f_arrays + [mod_l, g_ffn, w_o]
    if router is not None:
        in_specs += [full(router[0].shape), full(router[1].shape)]
        out_specs += [tile(LOGIT_LANES), pl.BlockSpec((8, tm), lambda i: (0, i)),
                      pl.BlockSpec((EXPERT_ROWS, LOGIT_LANES), lambda i: (0, 0))]
        out_shape += [jax.ShapeDtypeStruct((t, LOGIT_LANES), F32), jax.ShapeDtypeStruct((8, t), F32),
                      jax.ShapeDtypeStruct((EXPERT_ROWS, LOGIT_LANES), F32)]
        args += list(router)
    return pl.pallas_call(
        functools.partial(_postmix_kernel, n_x=len(x_arrays), n_a=len(a_arrays), n_f=len(f_arrays),
                          tile0=tile0, n_ctx_tiles=nct, tiles_per_batch=seq // tm,
                          with_router=router is not None),
        grid=(t // tm,),
        in_specs=in_specs,
        out_specs=out_specs,
        out_shape=out_shape,
        scratch_shapes=[pltpu.VMEM((d, d), BF16)],
        compiler_params=_cparams(1),
        name="postmix_wo_norm",
    )(*args)


def _silu_mul(gate, up):
    return (gate / (1.0 + jnp.exp(-gate))) * up


def _swiglu_chunks(xs, n_chunks, wg_of, wu_of, wd_of, acc_refs):
    if not isinstance(xs, (list, tuple)):
        xs, acc_refs = [xs], [acc_refs]
    items = [(c, r) for c in range(n_chunks) for r in range(len(xs))]
    fetched = {}

    def weight(getter, c):
        if (getter, c) not in fetched:
            fetched[(getter, c)] = getter(c)
        return fetched[(getter, c)]

    def gate_up(c, r):
        return (jnp.dot(xs[r], weight(wg_of, c), preferred_element_type=F32),
                jnp.dot(xs[r], weight(wu_of, c), preferred_element_type=F32))

    gate, up = gate_up(*items[0])
    for idx, (c, r) in enumerate(items):
        act = _silu_mul(gate, up).astype(BF16)
        if idx + 1 < len(items):
            gate, up = gate_up(*items[idx + 1])
        acc_refs[r][...] += jnp.dot(act, weight(wd_of, c), preferred_element_type=F32)


def _dense_ffn_kernel(h_ref, x1_ref, mod_ref, wg_ref, wu_ref, wd_ref, o_ref, acc_ref, *,
                      tile0, n_ctx_tiles, tiles_per_batch):
    i = pl.program_id(0) + tile0
    row = _mod_row(i, n_ctx_tiles, tiles_per_batch)
    acc_ref[...] = jnp.zeros_like(acc_ref)
    sub = lambda c: slice(c * FF_CHUNK, (c + 1) * FF_CHUNK)
    _swiglu_chunks(h_ref[...], wg_ref.shape[1] // FF_CHUNK, lambda c: wg_ref[:, sub(c)],
                   lambda c: wu_ref[:, sub(c)], lambda c: wd_ref[sub(c), :], acc_ref)
    o_ref[...] = x1_ref[...] + _mod_vec(mod_ref, row, 5) * acc_ref[...]


def _dense_ffn(h, x1, mod_l, wg, wu, wd, layer, *, n_ctx, seq, tile0):
    t, d = x1.shape
    tm = TM_DENSE
    full = lambda shape: pl.BlockSpec(shape, lambda i: (0,) * len(shape))
    tile = lambda: pl.BlockSpec((tm, d), lambda i: (i, 0))
    return pl.pallas_call(
        functools.partial(_dense_ffn_kernel, tile0=tile0 * TM_MIX // tm, n_ctx_tiles=n_ctx // tm,
                          tiles_per_batch=seq // tm),
        grid=(t // tm,),
        in_specs=[tile(), tile(), full(mod_l.shape), _layer_spec(wg, layer), _layer_spec(wu, layer),
                  _layer_spec(wd, layer)],
        out_specs=tile(),
        out_shape=jax.ShapeDtypeStruct((t, d), F32),
        scratch_shapes=[pltpu.VMEM((tm, d), F32)],
        compiler_params=_cparams(1),
        name="dense_swiglu",
    )(h, x1, mod_l, wg, wu, wd)


def _moe_ffn_kernel(be_ref, nv_ref, x_ref, wg_ref, wu_ref, wd_ref, o_ref, xb_ref, acc_ref, wgb_ref, wub_ref,
                    wdb_ref):
    i = pl.program_id(0)
    j = pl.program_id(1)
    n_valid = nv_ref[i]

    @pl.when(j == 0)
    def _():
        x = _unpack_bf16_pairs(x_ref[...])
        rows = lax.broadcasted_iota(jnp.int32, x.shape, 0)
        xb_ref[...] = jnp.where(rows < n_valid, x, jnp.zeros_like(x))
        acc_ref[...] = jnp.zeros_like(acc_ref)

    n_sub = TF_MOE // FF_CHUNK
    sub = lambda c: slice(c * FF_CHUNK, (c + 1) * FF_CHUNK)
    rows_needed = ((n_valid + TAIL_ROWS - 1) // TAIL_ROWS) * TAIL_ROWS
    n_full = rows_needed // TM_PASS
    n_tail = (rows_needed - n_full * TM_PASS) // TAIL_ROWS

    def cast_chunk(src_ref, dst_ref, idx):
        w = src_ref[(0, 0) + idx].astype(BF16)
        dst_ref[idx] = w
        return w

    cast_wg = lambda c: cast_chunk(wg_ref, wgb_ref, (slice(None), sub(c)))
    cast_wu = lambda c: cast_chunk(wu_ref, wub_ref, (slice(None), sub(c)))
    cast_wd = lambda c: cast_chunk(wd_ref, wdb_ref, (sub(c), slice(None)))
    read_wg = lambda c: wgb_ref[:, sub(c)]
    read_wu = lambda c: wub_ref[:, sub(c)]
    read_wd = lambda c: wdb_ref[sub(c), :]

    @pl.when(jnp.logical_and(n_full == 0, n_tail > 0))
    def _():
        for c in range(n_sub):
            cast_wg(c), cast_wu(c), cast_wd(c)

    for n_groups in range(1, TM_MOE // TM_PASS + 1):
        groups = [slice(r * TM_PASS, (r + 1) * TM_PASS) for r in range(n_groups)]

        @pl.when(n_full == n_groups)
        def _():
            _swiglu_chunks([xb_ref[rows, :] for rows in groups], n_sub, cast_wg, cast_wu, cast_wd,
                           [acc_ref.at[rows, :] for rows in groups])

    @pl.when(n_tail > 0)
    def _():
        def group(gi, carry):
            rows = pl.ds(pl.multiple_of(n_full * TM_PASS + gi * TAIL_ROWS, TAIL_ROWS), TAIL_ROWS)
            _swiglu_chunks(xb_ref[rows, :], n_sub, read_wg, read_wu, read_wd, acc_ref.at[rows, :])
            return carry

        lax.fori_loop(0, n_tail, group, 0)

    @pl.when(j == pl.num_programs(1) - 1)
    def _():
        o_ref[...] = _pack_bf16_pairs(acc_ref[...].astype(BF16))


def _moe_ffn(xs, block_expert, n_valid, wg, wu, wd, layer):
    n_rows = xs.shape[0]
    d = D_MODEL
    tm, tf = TM_MOE, TF_MOE
    n_blocks = n_rows // tm
    n_ff = wg.shape[3] // tf

    def ff_idx(j, nv, i):
        return jnp.where(nv[i] > 0, j, n_ff - 1)

    grid_spec = pltpu.PrefetchScalarGridSpec(
        num_scalar_prefetch=2,
        grid=(n_blocks, n_ff),
        in_specs=[
            pl.BlockSpec((tm, d // 2), lambda i, j, be, nv: (i, 0)),
            pl.BlockSpec((1, 1, d, tf), lambda i, j, be, nv: (layer, be[i], 0, ff_idx(j, nv, i))),
            pl.BlockSpec((1, 1, d, tf), lambda i, j, be, nv: (layer, be[i], 0, ff_idx(j, nv, i))),
            pl.BlockSpec((1, 1, tf, d), lambda i, j, be, nv: (layer, be[i], ff_idx(j, nv, i), 0)),
        ],
        out_specs=pl.BlockSpec((tm, d // 2), lambda i, j, be, nv: (i, 0)),
        scratch_shapes=[pltpu.VMEM((tm, d), BF16), pltpu.VMEM((tm, d), F32), pltpu.VMEM((d, tf), BF16),
                        pltpu.VMEM((d, tf), BF16), pltpu.VMEM((tf, d), BF16)],
    )
    return pl.pallas_call(
        _moe_ffn_kernel,
        grid_spec=grid_spec,
        out_shape=jax.ShapeDtypeStruct((n_rows, d // 2), jnp.uint32),
        compiler_params=_cparams(2),
        name="expert_swiglu",
    )(block_expert, n_valid, xs, wg, wu, wd)


def _sc_mesh():
    return plsc.VectorSubcoreMesh(core_axis_name="c", subcore_axis_name="s")


def _sc_params():
    return pltpu.CompilerParams(use_tc_tiling_on_sc=True)


def _sc_dispatch(h_packed, dest, n_rows):
    t, w = h_packed.shape
    win = SC_DISPATCH_ROWS
    n_win = t // win
    idx = [dest[k].reshape(n_win, 1, win) for k in range(TOP_K)]

    @functools.partial(
        pl.kernel, out_type=jax.ShapeDtypeStruct((n_rows, w), h_packed.dtype), mesh=_sc_mesh(),
        scratch_types=[], compiler_params=_sc_params(), name="expert_dispatch_scatter")
    def run(x_hbm, i0_hbm, i1_hbm, o_hbm):
        def body(x_vmem, i0_vmem, i1_vmem):
            pltpu.sync_copy(x_vmem, o_hbm.at[i0_vmem.at[0, 0]])
            pltpu.sync_copy(x_vmem, o_hbm.at[i1_vmem.at[0, 0]])

        idx_spec = pl.BlockSpec((1, 1, win), lambda i: (i, 0, 0))
        pltpu.emit_pipeline(
            body, grid=(n_win,),
            in_specs=[pl.BlockSpec((win, w), lambda i: (i, 0)), idx_spec, idx_spec],
            out_specs=[], core_axis_name=("c", "s"), dimension_semantics=(pltpu.PARALLEL,),
        )(x_hbm, i0_hbm, i1_hbm)

    return run(h_packed, *idx)


def _sc_gather(ys, idx_flat):
    w = ys.shape[1]
    n = idx_flat.shape[0]
    win = SC_GATHER_ROWS
    n_win = n // win

    @functools.partial(
        pl.kernel, out_type=jax.ShapeDtypeStruct((n, w), ys.dtype), mesh=_sc_mesh(),
        scratch_types=[], compiler_params=_sc_params(), name="expert_combine_gather")
    def run(y_hbm, i_hbm, o_hbm):
        def body(i_vmem, o_vmem):
            pltpu.sync_copy(y_hbm.at[i_vmem.at[0, 0]], o_vmem)

        pltpu.emit_pipeline(
            body, grid=(n_win,),
            in_specs=[pl.BlockSpec((1, 1, win), lambda i: (i, 0, 0))],
            out_specs=[pl.BlockSpec((win, w), lambda i: (i, 0))],
            core_axis_name=("c", "s"), dimension_semantics=(pltpu.PARALLEL,),
        )(i_hbm, o_hbm)

    return run(ys, idx_flat.reshape(n_win, 1, win))


def _moe_combine_kernel(x1_ref, y0_ref, y1_ref, g_ref, mod_ref, o_ref, *, tile0, n_ctx_tiles,
                        tiles_per_batch):
    i = pl.program_id(0) + tile0
    row = _mod_row(i, n_ctx_tiles, tiles_per_batch)
    gates = g_ref[...]
    y0 = _unpack_bf16_pairs(y0_ref[...]).astype(F32)
    y1 = _unpack_bf16_pairs(y1_ref[...]).astype(F32)
    y = y0 * gates[:, 2:3] + y1 * gates[:, 3:4]
    o_ref[...] = x1_ref[...] + _mod_vec(mod_ref, row, 5) * y


def _moe_combine(x1, y_sel, gates, mod_l, *, n_ctx, seq, tile0):
    t, d = x1.shape
    tm = TM_MIX
    n_tiles = t // tm
    tile = lambda w: pl.BlockSpec((tm, w), lambda i: (i, 0))
    return pl.pallas_call(
        functools.partial(_moe_combine_kernel, tile0=tile0, n_ctx_tiles=n_ctx // tm,
                          tiles_per_batch=seq // tm),
        grid=(n_tiles,),
        in_specs=[tile(d), tile(d // 2), pl.BlockSpec((tm, d // 2), lambda i: (n_tiles + i, 0)),
                  tile(LOGIT_LANES),
                  pl.BlockSpec(mod_l.shape, lambda i: (0, 0))],
        out_specs=tile(d),
        out_shape=jax.ShapeDtypeStruct((t, d), F32),
        compiler_params=_cparams(1),
        name="expert_combine",
    )(x1, y_sel, y_sel, gates, mod_l)


def _route(route_t, counts):
    t = route_t.shape[1]
    expert = route_t[0:TOP_K].astype(jnp.int32)
    rank = route_t[4:4 + TOP_K].astype(jnp.int32)
    counts = counts[:N_EXPERTS, 0].astype(jnp.int32)
    n_assign = t * TOP_K
    padded = ((counts + TM_MOE - 1) // TM_MOE) * TM_MOE
    pend = jnp.cumsum(padded)
    pstart = pend - padded
    dest = rank
    for e in range(N_EXPERTS):
        dest = dest + jnp.where(expert == e, pstart[e], 0)
    n_blocks = -(-n_assign // TM_MOE) + N_EXPERTS
    block_start = jnp.arange(n_blocks, dtype=jnp.int32) * TM_MOE
    block_expert = jnp.minimum(jnp.sum(pend[None, :] <= block_start[:, None], axis=1),
                               N_EXPERTS - 1).astype(jnp.int32)
    n_valid = jnp.clip(counts[block_expert] - (block_start - pstart[block_expert]), 0, TM_MOE)
    n_valid = jnp.where(block_start < pend[-1], n_valid, 0).astype(jnp.int32)
    return dest.astype(jnp.int32), block_expert, n_valid, n_blocks * TM_MOE


def kernel(x, c, ctx, c_ctx, w_mod, b_mod, g_mix, g_ffn, g_q, g_k, w_in, w_four, w_o,
           w_gate_dense, w_up_dense, w_down_dense, w_router, b_router,
           w_gate_moe, w_up_moe, w_down_moe):
    b, s, d = x.shape
    n_ctx_len = ctx.shape[1]
    n_ctx = b * n_ctx_len
    t = n_ctx + b * s
    assert d == D_MODEL and b + 1 <= MOD_ROWS
    assert n_ctx % TM_DENSE == 0 and s % TM_DENSE == 0 and n_ctx_len % 128 == 0 and n_ctx % s == 0

    cvec = jnp.zeros((MOD_ROWS, d), F32).at[0].set(c_ctx).at[1:b + 1].set(c)
    mod = _modulation(cvec, w_mod, b_mod)

    cos_t, sin_t = _rope_tables_t(s, TM_MIX)
    c_lat, s_lat, rev_lat = _dft_half_tiles(s, TR_FOUR)
    c_ctx_m, s_ctx_m = _dft_mats(n_ctx_len)
    c_grp, s_grp = _dft_mats(FOURIER_GROUP)

    xa = (ctx.reshape(n_ctx, d), x.reshape(b * s, d))

    wg_dense, wu_dense, wd_dense = (w.astype(BF16) for w in (w_gate_dense, w_up_dense, w_down_dense))

    for l in range(DEPTH):
        mod_l = mod[l]
        gq = jnp.broadcast_to((g_q[l] * (LOG2_E * HEAD_DIM ** -0.5))[:, None], (HEAD_DIM, TM_MIX))
        gk = jnp.broadcast_to(g_k[l][:, None], (HEAD_DIM, TM_MIX))
        last = l == DEPTH - 1
        tile0 = n_ctx // TM_MIX if last else 0
        q_t, k_t, v_t, f = _premix(xa, mod_l, g_mix[l].reshape(1, d), w_in, l, gq, gk, cos_t, sin_t,
                                   n_ctx=n_ctx, seq=s, t=t)

        n_qt = s // TQ_ATTN
        lat_keys = [(n_ctx_len, lambda bb: bb), (s, lambda bb: n_ctx // s + bb)]
        attn_lat = _attention(q_t, k_t, v_t, q_tile0=n_ctx // TQ_ATTN, n_q_tiles=n_qt,
                              key_blocks=lat_keys, tq=TQ_ATTN, n_batch=b, name="attention_latent")
        attn_ctx = None if last else _attention(
            q_t, k_t, v_t, q_tile0=0, n_q_tiles=1, key_blocks=[(n_ctx_len, lambda bb: bb)],
            tq=n_ctx_len, n_batch=b, name="attention_context")

        wfour = w_four[l].astype(BF16)
        four_lat = _fourier_sym(f, c_lat, s_lat, rev_lat, c_grp, s_grp, wfour, n=s, tr=TR_FOUR,
                                x_block0=n_ctx // s, n_batch=b, name="fourier_latent")
        four_ctx = None if last else _fourier(
            f, c_ctx_m, s_ctx_m, c_grp, s_grp, wfour, n=n_ctx_len, tr=n_ctx_len, x_block0=0, n_batch=b,
            name="fourier_context")

        is_moe = l % 2 == 1
        li = l // 2
        router = None
        if is_moe:
            wr = jnp.zeros((EXPERT_ROWS, d), BF16).at[:N_EXPERTS].set(w_router[li].T.astype(BF16))
            br = jnp.full((EXPERT_ROWS,), -jnp.inf, F32).at[:N_EXPERTS].set(b_router[li])
            br = jnp.broadcast_to(br[:, None], (EXPERT_ROWS, TM_MIX))
            router = (wr, br)
        res = _postmix(xa, (attn_ctx, attn_lat), (four_ctx, four_lat), mod_l, g_ffn[l].reshape(1, d),
                       w_o, l, router, n_ctx=n_ctx, seq=s, t=t, tile0=tile0)
        t_l = t - tile0 * TM_MIX
        if not is_moe:
            x1, h2 = res
            xa = _dense_ffn(h2, x1, mod_l, wg_dense, wu_dense, wd_dense, li, n_ctx=n_ctx, seq=s, tile0=tile0)
        else:
            x1, h2, route, route_t, counts = res
            dest, block_expert, n_valid, n_rows = _route(route_t, counts)
            xs = _sc_dispatch(h2, dest, n_rows)
            ys = _moe_ffn(xs, block_expert, n_valid, w_gate_moe, w_up_moe, w_down_moe, li)
            y_sel = _sc_gather(ys, dest.reshape(-1))
            xa = _moe_combine(x1, y_sel, route, mod_l, n_ctx=n_ctx, seq=s, tile0=tile0)

    return xa.reshape(b, s, d)
```
